```python
import jax, jax.numpy as jnp
from jax import lax
import numpy as np

D_MODEL = 1024
BATCH = 1
SEQ = 16384
DEPTH = 2
DEC_BATCH = 32
DEC_SEQ = 1
PAST_LEN = 16384
PAGE_SIZE = 128

MIX = D_MODEL
SSM_INNER = MIX // 2
SSM_HEAD_DIM = 64
SSM_HEADS = SSM_INNER // SSM_HEAD_DIM
SSM_GROUPS = 2
SSM_STATE = 128
CONV_W = 4
CONV_DIM = SSM_INNER + 2 * SSM_GROUPS * SSM_STATE
SSD_CHUNK = 128
POOL_DIM = MIX // 4
POOL_WINDOWS = (2, 4, 8, 16)
POOL_NG = len(POOL_WINDOWS)
POOL_GROUP = POOL_DIM // POOL_NG
POOL_HIST = max(POOL_WINDOWS) - 1
ATT_DIM = MIX - SSM_INNER - POOL_DIM
ATT_HEAD_DIM = 64
ATT_HEADS = ATT_DIM // ATT_HEAD_DIM
MOBA_BLOCK = 256
MOBA_TOP_K = 3
Q_BLOCK = 128
MEM_LEN = 256
MEM_HEADS = 4
MEM_HEAD_DIM = D_MODEL // MEM_HEADS
D_FF = -(-(8 * D_MODEL) // (3 * 256)) * 256
RMS_EPS = 1e-6
IN_COLS = SSM_INNER + CONV_DIM + SSM_HEADS + POOL_DIM + 3 * ATT_DIM

kernel_name = 'hymba_ssd_pool_moba_decoder_step'


def rmsnorm(x, g):
    xf = x.astype(jnp.float32)
    y = xf * lax.rsqrt(jnp.mean(xf * xf, axis=-1, keepdims=True) + RMS_EPS)
    return (y * g.astype(jnp.float32)).astype(x.dtype)


def causal_conv(xbc, prev, w, b):
    t = xbc.shape[1]
    ext = jnp.concatenate([prev.astype(xbc.dtype), xbc], axis=1)
    out = b
    for j in range(CONV_W):
        out = out + ext[:, j:j + t] * w[j]
    return jax.nn.silu(out), ext[:, t:]


def ssd_scan(x, dt, a, b, c, s0):
    f32 = jnp.float32
    bsz, t, nh, hp = x.shape
    n = b.shape[-1]
    cl = min(SSD_CHUNK, t)
    nc = t // cl
    xdt = (x.astype(f32) * dt[..., None]).reshape(bsz, nc, cl, nh, hp)
    bc = b.astype(f32).reshape(bsz, nc, cl, nh, n)
    cc = c.astype(f32).reshape(bsz, nc, cl, nh, n)
    a_cum = jnp.cumsum((dt * a).reshape(bsz, nc, cl, nh), axis=2)
    causal = jnp.tril(jnp.ones((cl, cl), dtype=bool))
    seg = a_cum[:, :, :, None, :] - a_cum[:, :, None, :, :]
    decay = jnp.exp(jnp.where(causal[None, None, :, :, None], seg, -jnp.inf))
    scores = jnp.einsum('bclhn,bcshn->bclsh', cc, bc) * decay
    y_diag = jnp.einsum('bclsh,bcshp->bclhp', scores, xdt)
    to_end = jnp.exp(a_cum[:, :, -1:, :] - a_cum)
    chunk_states = jnp.einsum('bclhn,bclh,bclhp->bchpn', bc, to_end, xdt)
    chunk_decay = jnp.exp(a_cum[:, :, -1, :])

    def step(s, inp):
        st, dc = inp
        return s * dc[:, :, None, None] + st, s

    s_final, s_enter = lax.scan(step, s0.astype(f32),
                                (jnp.moveaxis(chunk_states, 1, 0), jnp.moveaxis(chunk_decay, 1, 0)))
    s_enter = jnp.moveaxis(s_enter, 0, 1)
    y_off = jnp.einsum('bclhn,bchpn,bclh->bclhp', cc, s_enter, jnp.exp(a_cum))
    return (y_diag + y_off).reshape(bsz, t, nh, hp), s_final


def pool_mix(u, prev, pos0, w, scale):
    f32 = jnp.float32
    bsz, t, _ = u.shape
    ext = jnp.concatenate([prev.astype(u.dtype), u], axis=1)
    cs = jnp.pad(jnp.cumsum(ext.astype(f32), axis=1), ((0, 0), (1, 0), (0, 0)))
    pos = pos0 + jnp.arange(t)
    end = cs[:, POOL_HIST + 1:]
    parts = []
    for g, win in enumerate(POOL_WINDOWS):
        sl = slice(g * POOL_GROUP, (g + 1) * POOL_GROUP)
        start = cs[:, POOL_HIST + 1 - win:POOL_HIST + 1 - win + t, sl]
        cnt = jnp.minimum(win, pos + 1).astype(f32)
        parts.append((end[..., sl] - start) / cnt[None, :, None])
    pooled = (jnp.concatenate(parts, axis=-1) - u.astype(f32)).reshape(bsz, t, POOL_NG, POOL_GROUP)
    y = jnp.einsum('btgc,gcd->btgd', pooled, w.astype(f32)).reshape(bsz, t, POOL_DIM)
    y = y * scale.astype(f32)
    return y.astype(u.dtype), ext[:, t:]


def moba_attention(q, k, v, pos0):
    f32 = jnp.float32
    bsz, tq, nh, hd = q.shape
    length = k.shape[1]
    nb = -(-length // MOBA_BLOCK)
    pad = nb * MOBA_BLOCK - length
    kp = jnp.pad(k, ((0, 0), (0, pad), (0, 0), (0, 0))).reshape(bsz, nb, MOBA_BLOCK, nh, hd)
    vp = jnp.pad(v, ((0, 0), (0, pad), (0, 0), (0, 0))).reshape(bsz, nb, MOBA_BLOCK, nh, hd)
    k_mean = jnp.mean(kp.astype(f32), axis=2)
    kk = min(MOBA_TOP_K, nb)
    qb = min(Q_BLOCK, tq)
    nqc = tq // qb
    q_c = jnp.moveaxis(q.reshape(bsz, nqc, qb, nh, hd), 1, 0)
    pos_c = (pos0 + jnp.arange(tq, dtype=jnp.int32)).reshape(nqc, qb)
    b_ix = jnp.arange(bsz)[:, None, None, None]
    h_ix = jnp.arange(nh)[None, None, :, None]
    blk_ix = jnp.arange(nb)
    off = jnp.arange(MOBA_BLOCK)
    scale = hd ** -0.5

    def one_block(args):
        qi, pos = args
        own = pos // MOBA_BLOCK
        gate = jnp.einsum('bqhd,bnhd->bqhn', qi.astype(f32), k_mean)
        past = blk_ix[None, :] < own[:, None]
        gate = jnp.where(past[None, :, None, :], gate, -jnp.inf)
        top_val, top_idx = lax.top_k(gate, kk)
        sel_ok = jnp.isfinite(top_val)
        own_idx = jnp.broadcast_to(own[None, :, None, None], (bsz, qb, nh, 1))
        idx = jnp.concatenate([top_idx.astype(jnp.int32), own_idx.astype(jnp.int32)], axis=-1)
        k_sel = kp[b_ix, idx, :, h_ix, :]
        v_sel = vp[b_ix, idx, :, h_ix, :]
        own_ok = (own[:, None] * MOBA_BLOCK + off[None, :]) <= pos[:, None]
        ok = jnp.concatenate([
            jnp.broadcast_to(sel_ok[..., None], (bsz, qb, nh, kk, MOBA_BLOCK)),
            jnp.broadcast_to(own_ok[None, :, None, None, :], (bsz, qb, nh, 1, MOBA_BLOCK))], axis=3)
        s = jnp.einsum('bqhd,bqhnkd->bqhnk', qi, k_sel, preferred_element_type=f32) * scale
        s = jnp.where(ok, s, -jnp.inf).reshape(bsz, qb, nh, (kk + 1) * MOBA_BLOCK)
        pr = jax.nn.softmax(s, axis=-1).reshape(bsz, qb, nh, kk + 1, MOBA_BLOCK)
        return jnp.einsum('bqhnk,bqhnkd->bqhd', pr.astype(v.dtype), v_sel)

    out = lax.map(one_block, (q_c, pos_c))
    return jnp.moveaxis(out, 0, 1).reshape(bsz, tq, nh * hd)


def memory_kv(mem, g, w):
    bsz, m, _ = mem.shape
    kv = (rmsnorm(mem, g) @ w).reshape(bsz, m, 2, MEM_HEADS, MEM_HEAD_DIM)
    return kv[:, :, 0], kv[:, :, 1]


def cross_attention(h, mk, mv, wq, wo):
    bsz, t, _ = h.shape
    q = (h @ wq).reshape(bsz, t, MEM_HEADS, MEM_HEAD_DIM)
    s = jnp.einsum('bthd,bmhd->bhtm', q, mk.astype(q.dtype), preferred_element_type=jnp.float32)
    p = jax.nn.softmax(s * (MEM_HEAD_DIM ** -0.5), axis=-1)
    o = jnp.einsum('bhtm,bmhd->bthd', p.astype(h.dtype), mv.astype(h.dtype)).reshape(bsz, t, MEM_HEADS * MEM_HEAD_DIM)
    return o @ wo


def swiglu(h, w_gu, w_down):
    g, u = jnp.split(h @ w_gu, 2, axis=-1)
    return (jax.nn.silu(g) * u) @ w_down


def mixer(h, l, conv_prev, ssm_prev, pool_prev, k_past, v_past, p):
    bsz, t, _ = h.shape
    pos0 = k_past.shape[1]
    proj = h @ p['w_in'][l]
    o_xbc = SSM_INNER
    o_dt = o_xbc + CONV_DIM
    o_pool = o_dt + SSM_HEADS
    o_q = o_pool + POOL_DIM
    o_k = o_q + ATT_DIM
    o_v = o_k + ATT_DIM
    z, xbc, dt_raw, u, q, k, v = jnp.split(proj, [o_xbc, o_dt, o_pool, o_q, o_k, o_v], axis=-1)
    xbc, conv_new = causal_conv(xbc, conv_prev, p['conv_w'][l], p['conv_b'][l])
    xs, bs, cs = jnp.split(xbc, [SSM_INNER, SSM_INNER + SSM_GROUPS * SSM_STATE], axis=-1)
    x_h = xs.reshape(bsz, t, SSM_HEADS, SSM_HEAD_DIM)
    rep = SSM_HEADS // SSM_GROUPS
    b_h = jnp.repeat(bs.reshape(bsz, t, SSM_GROUPS, SSM_STATE), rep, axis=2)
    c_h = jnp.repeat(cs.reshape(bsz, t, SSM_GROUPS, SSM_STATE), rep, axis=2)
    dt = jax.nn.softplus(dt_raw.astype(jnp.float32) + p['dt_bias'][l].astype(jnp.float32))
    a = -jnp.exp(p['a_log'][l].astype(jnp.float32))
    y, ssm_new = ssd_scan(x_h, dt, a, b_h, c_h, ssm_prev)
    y = y + p['d_skip'][l].astype(jnp.float32)[:, None] * x_h.astype(jnp.float32)
    y = y.reshape(bsz, t, SSM_INNER) * jax.nn.silu(z.astype(jnp.float32))
    y = rmsnorm(y.reshape(bsz, t, SSM_GROUPS, SSM_INNER // SSM_GROUPS),
                p['ssm_norm_g'][l].reshape(SSM_GROUPS, SSM_INNER // SSM_GROUPS))
    y = y.reshape(bsz, t, SSM_INNER).astype(h.dtype)
    pool_out, pool_new = pool_mix(u, pool_prev, pos0, p['pool_w'][l], p['pool_scale'][l])
    q = q.reshape(bsz, t, ATT_HEADS, ATT_HEAD_DIM)
    k = k.reshape(bsz, t, ATT_HEADS, ATT_HEAD_DIM)
    v = v.reshape(bsz, t, ATT_HEADS, ATT_HEAD_DIM)
    k_all = jnp.concatenate([k_past.astype(k.dtype), k], axis=1)
    v_all = jnp.concatenate([v_past.astype(v.dtype), v], axis=1)
    att = moba_attention(q, k_all, v_all, pos0)
    mixed = jnp.concatenate([y, pool_out, att], axis=-1)
    return mixed @ p['w_out'][l], conv_new, ssm_new.astype(h.dtype), pool_new, k, v


def layer(x, l, conv_prev, ssm_prev, pool_prev, k_past, v_past, mem_k, mem_v, p):
    h = rmsnorm(x, p['norm_mix_g'][l])
    mix, conv_new, ssm_new, pool_new, k_new, v_new = mixer(h, l, conv_prev, ssm_prev, pool_prev, k_past, v_past, p)
    x = x + mix
    h = rmsnorm(x, p['norm_cross_g'][l])
    x = x + cross_attention(h, mem_k, mem_v, p['w_mem_q'][l], p['w_mem_o'][l])
    h = rmsnorm(x, p['norm_ffn_g'][l])
    x = x + swiglu(h, p['w_gate_up'][l], p['w_down'][l])
    return x, conv_new, ssm_new, pool_new, k_new, v_new


def setup_inputs(seed: int = 0) -> dict:
    key = jax.random.key(seed)
    ks = jax.random.split(key, 40)
    f32 = jnp.float32
    n_pages = PAST_LEN // PAGE_SIZE
    n_phys = (DEC_BATCH * n_pages * 5) // 4
    nrm = lambda k, s, sc=1.0: jax.random.normal(k, s, f32) * sc
    perm = jax.random.permutation(ks[0], n_phys)[:DEC_BATCH * n_pages]
    page_table = perm.reshape(DEC_BATCH, n_pages).astype(jnp.int32)
    dt0 = jnp.exp(jax.random.uniform(ks[1], (DEPTH, SSM_HEADS), f32) * (np.log(0.1) - np.log(0.001)) + np.log(0.001))
    dt_bias = dt0 + jnp.log(-jnp.expm1(-dt0))
    a_log = jnp.log(jax.random.uniform(ks[2], (DEPTH, SSM_HEADS), f32, 1.0, 16.0))
    return {
        'x_prompt': nrm(ks[3], (BATCH, SEQ, D_MODEL)),
        'x_sample': nrm(ks[4], (DEC_BATCH, DEC_SEQ, D_MODEL)),
        'cache_moba_k': nrm(ks[5], (DEPTH, n_phys, PAGE_SIZE, ATT_HEADS, ATT_HEAD_DIM)),
        'cache_moba_v': nrm(ks[6], (DEPTH, n_phys, PAGE_SIZE, ATT_HEADS, ATT_HEAD_DIM)),
        'state_ssm': nrm(ks[7], (DEPTH, DEC_BATCH, SSM_HEADS, SSM_HEAD_DIM, SSM_STATE), 0.1),
        'state_conv': nrm(ks[8], (DEPTH, DEC_BATCH, CONV_W - 1, CONV_DIM)),
        'state_pool': nrm(ks[9], (DEPTH, DEC_BATCH, POOL_HIST, POOL_DIM)),
        'cache_mem_k': nrm(ks[10], (DEPTH, DEC_BATCH, MEM_LEN, MEM_HEADS, MEM_HEAD_DIM)),
        'cache_mem_v': nrm(ks[11], (DEPTH, DEC_BATCH, MEM_LEN, MEM_HEADS, MEM_HEAD_DIM)),
        'page_table': page_table,
        'mem_prompt': nrm(ks[12], (BATCH, MEM_LEN, D_MODEL)),
        'norm_mix_g': 1.0 + nrm(ks[13], (DEPTH, D_MODEL), 0.02),
        'w_in': nrm(ks[14], (DEPTH, D_MODEL, IN_COLS), D_MODEL ** -0.5),
        'conv_w': nrm(ks[15], (DEPTH, CONV_W, CONV_DIM), CONV_W ** -0.5),
        'conv_b': nrm(ks[16], (DEPTH, CONV_DIM), 0.01),
        'dt_bias': dt_bias,
        'a_log': a_log,
        'd_skip': 1.0 + nrm(ks[17], (DEPTH, SSM_HEADS), 0.02),
        'ssm_norm_g': 1.0 + nrm(ks[18], (DEPTH, SSM_INNER), 0.02),
        'pool_w': nrm(ks[19], (DEPTH, POOL_NG, POOL_GROUP, POOL_GROUP), POOL_GROUP ** -0.5),
        'pool_scale': 1.0 + nrm(ks[20], (DEPTH, POOL_DIM), 0.02),
        'w_out': nrm(ks[21], (DEPTH, MIX, D_MODEL), MIX ** -0.5),
        'norm_cross_g': 1.0 + nrm(ks[22], (DEPTH, D_MODEL), 0.02),
        'norm_mem_g': 1.0 + nrm(ks[23], (DEPTH, D_MODEL), 0.02),
        'w_mem_q': nrm(ks[24], (DEPTH, D_MODEL, MEM_HEADS * MEM_HEAD_DIM), D_MODEL ** -0.5),
        'w_mem_kv': nrm(ks[25], (DEPTH, D_MODEL, 2 * MEM_HEADS * MEM_HEAD_DIM), D_MODEL ** -0.5),
        'w_mem_o': nrm(ks[26], (DEPTH, MEM_HEADS * MEM_HEAD_DIM, D_MODEL), (MEM_HEADS * MEM_HEAD_DIM) ** -0.5),
        'norm_ffn_g': 1.0 + nrm(ks[27], (DEPTH, D_MODEL), 0.02),
        'w_gate_up': nrm(ks[28], (DEPTH, D_MODEL, 2 * D_FF), D_MODEL ** -0.5),
        'w_down': nrm(ks[29], (DEPTH, D_FF, D_MODEL), D_FF ** -0.5),
        'final_norm_g': 1.0 + nrm(ks[30], (D_MODEL,), 0.02),
    }


def reference(x_prompt, x_sample, cache_moba_k, cache_moba_v, state_ssm, state_conv, state_pool,
              cache_mem_k, cache_mem_v, page_table, mem_prompt, norm_mix_g, w_in, conv_w, conv_b,
              dt_bias, a_log, d_skip, ssm_norm_g, pool_w, pool_scale, w_out, norm_cross_g, norm_mem_g,
              w_mem_q, w_mem_kv, w_mem_o, norm_ffn_g, w_gate_up, w_down, final_norm_g):
    p = {'norm_mix_g': norm_mix_g, 'w_in': w_in, 'conv_w': conv_w, 'conv_b': conv_b,
         'dt_bias': dt_bias, 'a_log': a_log, 'd_skip': d_skip, 'ssm_norm_g': ssm_norm_g,
         'pool_w': pool_w, 'pool_scale': pool_scale, 'w_out': w_out, 'norm_cross_g': norm_cross_g,
         'w_mem_q': w_mem_q, 'w_mem_o': w_mem_o, 'norm_ffn_g': norm_ffn_g,
         'w_gate_up': w_gate_up, 'w_down': w_down}
    bp = x_prompt.shape[0]
    bs = x_sample.shape[0]
    past_len = page_table.shape[1] * PAGE_SIZE
    dt_ = x_prompt.dtype
    empty_kv = jnp.zeros((bp, 0, ATT_HEADS, ATT_HEAD_DIM), dt_)
    conv0 = jnp.zeros((bp, CONV_W - 1, CONV_DIM), dt_)
    ssm0 = jnp.zeros((bp, SSM_HEADS, SSM_HEAD_DIM, SSM_STATE), dt_)
    pool0 = jnp.zeros((bp, POOL_HIST, POOL_DIM), dt_)
    xp, xs = x_prompt, x_sample
    kp_l, vp_l, ks_l, vs_l = [], [], [], []
    sp_l, ss_l, cp_l, cs_l, pp_l, ps_l, mk_l, mv_l = [], [], [], [], [], [], [], []
    for l in range(DEPTH):
        mk, mv = memory_kv(mem_prompt, norm_mem_g[l], w_mem_kv[l])
        xp, c_new, s_new, pl_new, k_new, v_new = layer(xp, l, conv0, ssm0, pool0, empty_kv, empty_kv, mk, mv, p)
        kp_l.append(k_new); vp_l.append(v_new); cp_l.append(c_new); sp_l.append(s_new); pp_l.append(pl_new)
        mk_l.append(mk); mv_l.append(mv)
        k_past = cache_moba_k[l][page_table].reshape(bs, past_len, ATT_HEADS, ATT_HEAD_DIM)
        v_past = cache_moba_v[l][page_table].reshape(bs, past_len, ATT_HEADS, ATT_HEAD_DIM)
        xs, c_new, s_new, pl_new, k_new, v_new = layer(xs, l, state_conv[l], state_ssm[l], state_pool[l],
                                                       k_past, v_past, cache_mem_k[l], cache_mem_v[l], p)
        ks_l.append(k_new); vs_l.append(v_new); cs_l.append(c_new); ss_l.append(s_new); ps_l.append(pl_new)
    y_prompt = rmsnorm(xp, final_norm_g)
    y_sample = rmsnorm(xs, final_norm_g)
    new_moba_k_prompt = jnp.stack(kp_l)
    new_moba_v_prompt = jnp.stack(vp_l)
    new_moba_k_sample = jnp.stack(ks_l)
    new_moba_v_sample = jnp.stack(vs_l)
    new_ssm_prompt = jnp.stack(sp_l)
    new_ssm_sample = jnp.stack(ss_l)
    new_conv_prompt = jnp.stack(cp_l)
    new_conv_sample = jnp.stack(cs_l)
    new_pool_prompt = jnp.stack(pp_l)
    new_pool_sample = jnp.stack(ps_l)
    new_mem_k_prompt = jnp.stack(mk_l)
    new_mem_v_prompt = jnp.stack(mv_l)
    return (y_prompt, y_sample, new_moba_k_prompt, new_moba_v_prompt, new_moba_k_sample, new_moba_v_sample,
            new_ssm_prompt, new_ssm_sample, new_conv_prompt, new_conv_sample, new_pool_prompt, new_pool_sample,
            new_mem_k_prompt, new_mem_v_prompt)
```

```python
import functools

import jax
import jax.numpy as jnp
from jax import lax
from jax.experimental import pallas as pl
from jax.experimental.pallas import tpu as pltpu

f32 = jnp.float32
bf16 = jnp.bfloat16
HIGHEST = lax.Precision.HIGHEST

D_MODEL = 1024
SSM_INNER = 512
SSM_HEAD_DIM = 64
SSM_HEADS = 8
SSM_GROUPS = 2
SSM_STATE = 128
CONV_W = 4
CONV_DIM = SSM_INNER + 2 * SSM_GROUPS * SSM_STATE
SSD_CHUNK = 128
POOL_DIM = 256
POOL_WINDOWS = (2, 4, 8, 16)
POOL_GROUP = 64
POOL_HIST = 15
ATT_DIM = 256
ATT_HEAD_DIM = 64
ATT_HEADS = 4
MOBA_BLOCK = 256
MOBA_TOP_K = 3
PAGE_SIZE = 128
MEM_HEADS = 4
MEM_HEAD_DIM = 256
D_FF = 2816
RMS_EPS = 1e-6
LANES = 128
MASKED = -1e30
VMEM_LIMIT = 56 * 1024 * 1024


def _cparams(n_axes):
    return pltpu.CompilerParams(dimension_semantics=("arbitrary",) * n_axes,
                                vmem_limit_bytes=VMEM_LIMIT)


def _rms(x, g):
    ms = jnp.mean(x * x, axis=-1, keepdims=True)
    return x * lax.rsqrt(ms + RMS_EPS) * g


def _dot(a, b, **kw):
    return jnp.dot(a, b, preferred_element_type=f32, **kw)


def _dot_nt(a, b, **kw):
    return lax.dot_general(a, b, (((1,), (1,)), ((), ())), preferred_element_type=f32, **kw)


def _silu(x):
    return x * jax.nn.sigmoid(x)


def _softplus(x):
    return jnp.maximum(x, 0.0) + jnp.log1p(jnp.exp(-jnp.abs(x)))


def _full(shape):
    return pl.BlockSpec(shape, lambda *_: (0,) * len(shape))


def _norm_mm_kernel(*refs, n_w, out_w):
    x_ref, g_ref = refs[0], refs[1]
    w_refs = refs[2:2 + n_w]
    o_refs = refs[2 + n_w:]
    hb = _rms(x_ref[...], g_ref[...]).astype(bf16)
    res = [_dot(hb, w[...]) for w in w_refs]
    for o_ref, wi in zip(o_refs, out_w):
        o_ref[...] = res[wi].astype(o_ref.dtype)


def norm_matmul(x, g, ws, outs, tm):
    m, d = x.shape
    in_specs = [pl.BlockSpec((tm, d), lambda i: (i, 0)), _full((1, d))]
    in_specs += [_full(w.shape) for w in ws]
    out_shape = [jax.ShapeDtypeStruct((m, ws[wi].shape[1]), dt) for wi, dt in outs]
    out_specs = [pl.BlockSpec((tm, ws[wi].shape[1]), lambda i: (i, 0)) for wi, _ in outs]
    return pl.pallas_call(
        functools.partial(_norm_mm_kernel, n_w=len(ws), out_w=tuple(wi for wi, _ in outs)),
        grid=(m // tm,), in_specs=in_specs, out_specs=out_specs, out_shape=out_shape,
        compiler_params=_cparams(1), name="norm_matmul",
    )(x, g.reshape(1, d), *ws)


def _mm_res_kernel(*refs, n_a):
    res_ref = refs[0]
    a_refs = refs[1:1 + n_a]
    w_refs = refs[1 + n_a:1 + 2 * n_a]
    o_ref = refs[-1]
    acc = res_ref[...]
    for a, w in zip(a_refs, w_refs):
        acc = acc + _dot(a[...].astype(bf16), w[...])
    o_ref[...] = acc


def matmul_residual(res, a_list, w_list, tm):
    m, d = res.shape
    in_specs = [pl.BlockSpec((tm, d), lambda i: (i, 0))]
    in_specs += [pl.BlockSpec((tm, a.shape[1]), lambda i: (i, 0)) for a in a_list]
    in_specs += [_full(w.shape) for w in w_list]
    return pl.pallas_call(
        functools.partial(_mm_res_kernel, n_a=len(a_list)),
        grid=(m // tm,), in_specs=in_specs,
        out_specs=pl.BlockSpec((tm, d), lambda i: (i, 0)),
        out_shape=jax.ShapeDtypeStruct((m, d), f32),
        compiler_params=_cparams(1), name="matmul_residual",
    )(res, *a_list, *w_list)


def _cross_prompt_kernel(x_ref, g_ref, wq_ref, mk_ref, mv_ref, wo_ref, o_ref):
    x = x_ref[...]
    hb = _rms(x, g_ref[...]).astype(bf16)
    q = _dot(hb, wq_ref[...])
    acc = x
    for h in range(MEM_HEADS):
        sl = slice(h * MEM_HEAD_DIM, (h + 1) * MEM_HEAD_DIM)
        s = _dot_nt(q[:, sl].astype(bf16), mk_ref[:, sl]) * (MEM_HEAD_DIM ** -0.5)
        p = jnp.exp(s - jnp.max(s, axis=-1, keepdims=True))
        p = p / jnp.sum(p, axis=-1, keepdims=True)
        oh = _dot(p.astype(bf16), mv_ref[:, sl])
        acc = acc + _dot(oh.astype(bf16), wo_ref[sl, :])
    o_ref[...] = acc


def cross_prompt(x, g, wq, mkb, mvb, wo, tm):
    m, d = x.shape
    return pl.pallas_call(
        _cross_prompt_kernel, grid=(m // tm,),
        in_specs=[pl.BlockSpec((tm, d), lambda i: (i, 0)), _full((1, d)), _full(wq.shape),
                  _full(mkb.shape), _full(mvb.shape), _full(wo.shape)],
        out_specs=pl.BlockSpec((tm, d), lambda i: (i, 0)),
        out_shape=jax.ShapeDtypeStruct((m, d), f32),
        compiler_params=_cparams(1), name="cross_prompt",
    )(x, g.reshape(1, d), wq, mkb, mvb, wo)


def _cross_sample_kernel(q_ref, mk_ref, mv_ref, o_ref):
    q = q_ref[...]
    lane = lax.broadcasted_iota(jnp.int32, (8, D_MODEL), 1)
    row = lax.broadcasted_iota(jnp.int32, (8, D_MODEL), 0)
    qblk = jnp.where(lane // MEM_HEAD_DIM == row, q, 0.0).astype(bf16)
    s = _dot_nt(qblk, mk_ref[...].astype(bf16)) * (MEM_HEAD_DIM ** -0.5)
    p = jnp.exp(s - jnp.max(s, axis=-1, keepdims=True))
    p = p / jnp.sum(p, axis=-1, keepdims=True)
    o = _dot(p.astype(bf16), mv_ref[...].astype(bf16))
    o_ref[...] = jnp.sum(jnp.where(lane // MEM_HEAD_DIM == row, o, 0.0), axis=0, keepdims=True)


def cross_sample(q, mem_k, mem_v):
    b, d = q.shape
    mlen = mem_k.shape[1]
    out = pl.pallas_call(
        _cross_sample_kernel, grid=(b,),
        in_specs=[pl.BlockSpec((None, 1, d), lambda i: (i, 0, 0)),
                  pl.BlockSpec((None, mlen, d), lambda i: (i, 0, 0)),
                  pl.BlockSpec((None, mlen, d), lambda i: (i, 0, 0))],
        out_specs=pl.BlockSpec((None, 1, d), lambda i: (i, 0, 0)),
        out_shape=jax.ShapeDtypeStruct((b, 1, d), f32),
        compiler_params=_cparams(1), name="cross_sample",
    )(q.reshape(b, 1, d), mem_k.reshape(b, mlen, d), mem_v.reshape(b, mlen, d))
    return out.reshape(b, d)


def _swiglu_kernel(x_ref, g_ref, wg_ref, wu_ref, wd_ref, o_ref, h_ref, acc_ref):
    f = pl.program_id(1)

    @pl.when(f == 0)
    def _():
        h_ref[...] = _rms(x_ref[...], g_ref[...]).astype(bf16)
        acc_ref[...] = x_ref[...]

    hb = h_ref[...]
    a = _silu(_dot(hb, wg_ref[...])) * _dot(hb, wu_ref[...])
    acc_ref[...] += _dot(a.astype(bf16), wd_ref[...])

    @pl.when(f == pl.num_programs(1) - 1)
    def _():
        o_ref[...] = acc_ref[...]


def swiglu_block(x, g, w_gu, w_d, tm, tf):
    m, d = x.shape
    nf = D_FF // tf
    return pl.pallas_call(
        _swiglu_kernel, grid=(m // tm, nf),
        in_specs=[pl.BlockSpec((tm, d), lambda i, f: (i, 0)), _full((1, d)),
                  pl.BlockSpec((d, tf), lambda i, f: (0, f)),
                  pl.BlockSpec((d, tf), lambda i, f: (0, f + nf)),
                  pl.BlockSpec((tf, d), lambda i, f: (f, 0))],
        out_specs=pl.BlockSpec((tm, d), lambda i, f: (i, 0)),
        out_shape=jax.ShapeDtypeStruct((m, d), f32),
        scratch_shapes=[pltpu.VMEM((tm, d), bf16), pltpu.VMEM((tm, d), f32)],
        compiler_params=_cparams(2), name="swiglu_block",
    )(x, g.reshape(1, d), w_gu, w_gu, w_d)


def _final_norm_kernel(x_ref, g_ref, o_ref):
    o_ref[...] = _rms(x_ref[...], g_ref[...])


def final_norm(x, g, tm):
    m, d = x.shape
    return pl.pallas_call(
        _final_norm_kernel, grid=(m // tm,),
        in_specs=[pl.BlockSpec((tm, d), lambda i: (i, 0)), _full((1, d))],
        out_specs=pl.BlockSpec((tm, d), lambda i: (i, 0)),
        out_shape=jax.ShapeDtypeStruct((m, d), f32),
        compiler_params=_cparams(1), name="final_norm",
    )(x, g.reshape(1, d))


def _pool_select(w2, w4, w8, w16, cnt_of, lane):
    out = w16 / cnt_of(16)
    for win, acc in ((8, w8), (4, w4), (2, w2)):
        g = POOL_WINDOWS.index(win)
        out = jnp.where(lane < (g + 1) * POOL_GROUP, acc / cnt_of(win), out)
    return out


def _group_rmsnorm(y, g):
    half = SSM_INNER // SSM_GROUPS
    parts = [_rms(y[:, i * half:(i + 1) * half], g[:, i * half:(i + 1) * half])
             for i in range(SSM_GROUPS)]
    return jnp.concatenate(parts, axis=-1)


def _ssd_pool_prompt_kernel(z_ref, xbc_ref, dt_ref, u_ref, cw_ref, cb_ref, dtb_ref, a_ref, dsk_ref,
                            ng_ref, pw_ref, ps_ref,
                            y_ref, po_ref, st_ref,
                            ext_ref, pext_ref, xc_ref, s_ref, *, tm):
    i = pl.program_id(0)
    halo = 8
    phalo = 16

    @pl.when(i == 0)
    def _():
        ext_ref[0:halo, :] = jnp.zeros((halo, CONV_DIM), f32)
        pext_ref[0:phalo, :] = jnp.zeros((phalo, POOL_DIM), f32)
        s_ref[...] = jnp.zeros(s_ref.shape, f32)

    @pl.when(i > 0)
    def _():
        ext_ref[0:halo, :] = ext_ref[tm:tm + halo, :]
        pext_ref[0:phalo, :] = pext_ref[tm:tm + phalo, :]

    ext_ref[halo:halo + tm, :] = xbc_ref[...]
    pext_ref[phalo:phalo + tm, :] = u_ref[...]

    acc = jnp.broadcast_to(cb_ref[...], (tm, CONV_DIM))
    for j in range(CONV_W):
        off = halo - (CONV_W - 1) + j
        acc = acc + ext_ref[off:off + tm, :] * cw_ref[j:j + 1, :]
    xc_ref[...] = _silu(acc)

    u = u_ref[...]
    run = u
    sums = {}
    for back in range(1, max(POOL_WINDOWS)):
        run = run + pext_ref[phalo - back:phalo - back + tm, :]
        if back + 1 in POOL_WINDOWS:
            sums[back + 1] = run
    lane_p = lax.broadcasted_iota(jnp.int32, (tm, POOL_DIM), 1)
    pos1 = (lax.broadcasted_iota(jnp.int32, (tm, POOL_DIM), 0) + i * tm + 1).astype(f32)
    pooled = _pool_select(sums[2], sums[4], sums[8], sums[16],
                          lambda win: jnp.minimum(float(win), pos1), lane_p) - u
    po_ref[...] = (_dot(pooled.astype(bf16), pw_ref[...]) * ps_ref[...]).astype(po_ref.dtype)

    cl = SSD_CHUNK
    r_io = lax.broadcasted_iota(jnp.int32, (cl, cl), 0)
    c_io = lax.broadcasted_iota(jnp.int32, (cl, cl), 1)
    ltri = (c_io <= r_io).astype(f32)
    causal = c_io <= r_io
    first_half = c_io < SSM_HEAD_DIM
    n_pairs = SSM_HEADS // 2
    heads_per_group = SSM_HEADS // SSM_GROUPS

    def chunk(c, carry):
        r0 = pl.multiple_of(c * cl, cl)
        xc = xc_ref[pl.ds(r0, cl), :]
        dt = _softplus(dt_ref[pl.ds(r0, cl), :] + dtb_ref[...])
        a_cum = _dot(ltri, dt * a_ref[...], precision=HIGHEST)
        a_cum_t = a_cum.T
        dt_t = dt.T
        to_end_t = jnp.exp(a_cum_t[:, cl - 1:cl] - a_cum_t) * dt_t
        b_t = [xc[:, SSM_INNER + g * SSM_STATE:SSM_INNER + (g + 1) * SSM_STATE].T
               for g in range(SSM_GROUPS)]
        cm = [xc[:, SSM_INNER + SSM_GROUPS * SSM_STATE + g * SSM_STATE:
                 SSM_INNER + SSM_GROUPS * SSM_STATE + (g + 1) * SSM_STATE].astype(bf16)
              for g in range(SSM_GROUPS)]
        scores = [_dot(cm[g], b_t[g].astype(bf16)) for g in range(SSM_GROUPS)]
        y_pairs = []
        for k in range(n_pairs):
            g = (2 * k) // heads_per_group
            xs_pair = xc[:, k * LANES:(k + 1) * LANES]
            xs_b = xs_pair.astype(bf16)
            yd, cs, eb = [], [], []
            for h in (2 * k, 2 * k + 1):
                colb = jnp.broadcast_to(a_cum[:, h:h + 1], (cl, cl))
                seg = colb - a_cum_t[h:h + 1, :]
                decay = jnp.exp(jnp.where(causal, seg, -jnp.inf))
                mh = scores[g] * decay * dt_t[h:h + 1, :]
                yd.append(_dot(mh.astype(bf16), xs_b))
                cs.append(_dot((b_t[g] * to_end_t[h:h + 1, :]).astype(bf16), xs_b))
                eb.append(jnp.exp(colb))
            e_pair = jnp.where(first_half, eb[0], eb[1])
            s_old = s_ref[k]
            y_off = _dot(cm[g], s_old.astype(bf16)) * e_pair
            y_pairs.append(jnp.where(first_half, yd[0], yd[1]) + y_off
                           + dsk_ref[:, k * LANES:(k + 1) * LANES] * xs_pair)
            s_ref[k] = s_old * e_pair[cl - 1:cl, :] + jnp.where(first_half, cs[0], cs[1])
        y = jnp.concatenate(y_pairs, axis=-1) * _silu(z_ref[pl.ds(r0, cl), :])
        y_ref[pl.ds(r0, cl), :] = _group_rmsnorm(y, ng_ref[...]).astype(y_ref.dtype)
        return carry

    lax.fori_loop(0, tm // cl, chunk, 0)

    @pl.when(i == pl.num_programs(0) - 1)
    def _():
        for k in range(n_pairs):
            st = s_ref[k].T
            st_ref[2 * k] = st[0:SSM_HEAD_DIM, :]
            st_ref[2 * k + 1] = st[SSM_HEAD_DIM:2 * SSM_HEAD_DIM, :]


def ssd_pool_prompt(z, xbc, dt, u, prm, tm):
    t = z.shape[0]
    row = lambda c: pl.BlockSpec((tm, c), lambda i: (i, 0))
    consts = [prm['conv_w'], prm['conv_b'], prm['dt_bias'], prm['a'], prm['dsk'], prm['ssm_norm_g'],
              prm['pool_w'], prm['pool_scale']]
    return pl.pallas_call(
        functools.partial(_ssd_pool_prompt_kernel, tm=tm), grid=(t // tm,),
        in_specs=[row(SSM_INNER), row(CONV_DIM), row(LANES), row(POOL_DIM)] + [_full(c.shape) for c in consts],
        out_specs=[row(SSM_INNER), row(POOL_DIM), _full((SSM_HEADS, SSM_HEAD_DIM, SSM_STATE))],
        out_shape=[jax.ShapeDtypeStruct((t, SSM_INNER), bf16), jax.ShapeDtypeStruct((t, POOL_DIM), bf16),
                   jax.ShapeDtypeStruct((SSM_HEADS, SSM_HEAD_DIM, SSM_STATE), f32)],
        scratch_shapes=[pltpu.VMEM((tm + 8, CONV_DIM), f32), pltpu.VMEM((tm + 16, POOL_DIM), f32),
                        pltpu.VMEM((tm, CONV_DIM), f32),
                        pltpu.VMEM((SSM_HEADS // 2, SSM_STATE, 2 * SSM_HEAD_DIM), f32)],
        compiler_params=_cparams(1), name="ssd_pool_prompt",
    )(z, xbc, dt, u, *consts)


def _col_tile(row):
    return jnp.broadcast_to(row, (LANES, LANES)).T


def _ssd_pool_sample_kernel(z_ref, xbc_ref, dt_ref, u_ref, sc_ref, ss_ref, sp_ref,
                            cw_ref, cb_ref, dtb_ref, a_ref, dsk_ref, ng_ref, pw_ref, ps_ref, ex_ref,
                            y_ref, po_ref, cn_ref, sn_ref, pn_ref, *, pos0):
    xrow = xbc_ref[...]
    acc = cb_ref[...] + xrow * cw_ref[CONV_W - 1:CONV_W, :]
    for j in range(CONV_W - 1):
        acc = acc + sc_ref[j:j + 1, :] * cw_ref[j:j + 1, :]
        if j > 0:
            cn_ref[j - 1:j, :] = sc_ref[j:j + 1, :]
    cn_ref[CONV_W - 2:CONV_W - 1, :] = xrow
    xc = _silu(acc)

    dt = _softplus(dt_ref[...] + dtb_ref[...])
    dta = dt * a_ref[...]
    both = jnp.concatenate([jnp.broadcast_to(dt, (8, LANES)), jnp.broadcast_to(dta, (8, LANES))], axis=0)
    both_x = _dot(both, ex_ref[...], precision=HIGHEST)
    dtx = both_x[0:1, :]
    dec_x = jnp.exp(both_x[8:9, :])
    xs = xc[:, 0:SSM_INNER]
    xdt = xs * dtx
    heads_per_group = SSM_HEADS // SSM_GROUPS
    y_pairs = []
    for k in range(SSM_HEADS // 2):
        g = (2 * k) // heads_per_group
        b_row = xc[:, SSM_INNER + g * SSM_STATE:SSM_INNER + (g + 1) * SSM_STATE]
        c_row = xc[:, SSM_INNER + SSM_GROUPS * SSM_STATE + g * SSM_STATE:
                   SSM_INNER + SSM_GROUPS * SSM_STATE + (g + 1) * SSM_STATE]
        sl = slice(k * LANES, (k + 1) * LANES)
        s_old = jnp.concatenate([ss_ref[2 * k], ss_ref[2 * k + 1]], axis=0)
        s_new = s_old * _col_tile(dec_x[:, sl]) + _col_tile(xdt[:, sl]) * b_row
        sn_ref[2 * k] = s_new[0:SSM_HEAD_DIM, :]
        sn_ref[2 * k + 1] = s_new[SSM_HEAD_DIM:, :]
        y_k = _dot_nt(jnp.broadcast_to(c_row, (8, SSM_STATE)), s_new, precision=HIGHEST)[0:1, :]
        y_pairs.append(y_k + dsk_ref[:, sl] * xs[:, sl])
    y = jnp.concatenate(y_pairs, axis=-1) * _silu(z_ref[...])
    y_ref[...] = _group_rmsnorm(y, ng_ref[...]).astype(y_ref.dtype)

    u = u_ref[...]
    prev = sp_ref[...]
    rowi = lax.broadcasted_iota(jnp.int32, prev.shape, 0)
    tail = lambda win: u + jnp.sum(jnp.where(rowi >= POOL_HIST - (win - 1), prev, 0.0), axis=0, keepdims=True)
    lane_p = lax.broadcasted_iota(jnp.int32, (1, POOL_DIM), 1)
    pooled = _pool_select(tail(2), tail(4), tail(8), tail(16),
                          lambda win: float(min(win, pos0 + 1)), lane_p) - u
    po = _dot(jnp.broadcast_to(pooled, (8, POOL_DIM)).astype(bf16), pw_ref[...])[0:1, :] * ps_ref[...]
    po_ref[...] = po.astype(po_ref.dtype)
    pn_ref[0:POOL_HIST - 1, :] = sp_ref[1:POOL_HIST, :]
    pn_ref[POOL_HIST - 1:POOL_HIST, :] = u


def ssd_pool_sample(z, xbc, dt, u, st_conv, st_ssm, st_pool, prm, pos0):
    b = z.shape[0]
    per_seq = lambda *shape: pl.BlockSpec((None,) + shape, lambda i: (i,) + (0,) * len(shape))
    consts = [prm['conv_w'], prm['conv_b'], prm['dt_bias'], prm['a'], prm['dsk'], prm['ssm_norm_g'],
              prm['pool_w'], prm['pool_scale'], prm['expand']]
    outs = pl.pallas_call(
        functools.partial(_ssd_pool_sample_kernel, pos0=pos0), grid=(b,),
        in_specs=[per_seq(1, SSM_INNER), per_seq(1, CONV_DIM), per_seq(1, LANES), per_seq(1, POOL_DIM),
                  per_seq(CONV_W - 1, CONV_DIM), per_seq(SSM_HEADS, SSM_HEAD_DIM, SSM_STATE),
                  per_seq(POOL_HIST, POOL_DIM)] + [_full(c.shape) for c in consts],
        out_specs=[per_seq(1, SSM_INNER), per_seq(1, POOL_DIM), per_seq(CONV_W - 1, CONV_DIM),
                   per_seq(SSM_HEADS, SSM_HEAD_DIM, SSM_STATE), per_seq(POOL_HIST, POOL_DIM)],
        out_shape=[jax.ShapeDtypeStruct((b, 1, SSM_INNER), f32), jax.ShapeDtypeStruct((b, 1, POOL_DIM), f32),
                   jax.ShapeDtypeStruct((b, CONV_W - 1, CONV_DIM), f32),
                   jax.ShapeDtypeStruct((b, SSM_HEADS, SSM_HEAD_DIM, SSM_STATE), f32),
                   jax.ShapeDtypeStruct((b, POOL_HIST, POOL_DIM), f32)],
        compiler_params=_cparams(1), name="ssd_pool_sample",
    )(z.reshape(b, 1, -1), xbc.reshape(b, 1, -1), dt.reshape(b, 1, -1), u.reshape(b, 1, -1),
      st_conv, st_ssm, st_pool, *consts)
    y, po, cn, sn, pn = outs
    return y.reshape(b, -1), po.reshape(b, -1), cn, sn, pn


def _topk_blocks(gate, n_past):
    lane = lax.broadcasted_iota(jnp.int32, gate.shape, 1).astype(f32)
    g = jnp.where(lane < jnp.asarray(n_past, f32), gate, -jnp.inf)
    picks = []
    for _ in range(MOBA_TOP_K):
        m = jnp.max(g, axis=-1, keepdims=True)
        idx = jnp.min(jnp.where(g == m, lane, float(LANES)), axis=-1, keepdims=True)
        picks.append((idx, jnp.abs(m) < jnp.inf))
        g = jnp.where(lane == idx, -jnp.inf, g)
    return picks


def _moba_prompt_kernel(q_ref, k_ref, kb_ref, vb_ref, o_ref,
                        km_ref, qs_ref, nm_ref, acc_ref, m_ref, l_ref):
    i = pl.program_id(0)
    tq = MOBA_BLOCK
    scale = ATT_HEAD_DIM ** -0.5

    @pl.when(i == 0)
    def _():
        km_ref[...] = jnp.zeros(km_ref.shape, f32)

    q = q_ref[...]
    head_of_lane = lax.broadcasted_iota(jnp.int32, (tq, ATT_DIM), 1) // ATT_HEAD_DIM
    blk_lane = lax.broadcasted_iota(jnp.int32, (tq, LANES), 1).astype(f32)
    r_io = lax.broadcasted_iota(jnp.int32, (tq, tq), 0)
    c_io = lax.broadcasted_iota(jnp.int32, (tq, tq), 1)
    row0 = pl.multiple_of(i * tq, tq)
    k_own = kb_ref[pl.ds(row0, tq), :]
    v_own = vb_ref[pl.ds(row0, tq), :]
    km = km_ref[...]
    for h in range(ATT_HEADS):
        qh = jnp.where(head_of_lane == h, q, 0.0)
        gate = _dot_nt(qh, km, precision=HIGHEST)
        sel = jnp.zeros((tq, LANES), jnp.bool_)
        for idx, ok in _topk_blocks(gate, i):
            sel = sel | ((blk_lane == idx) & ok)
        nm_ref[h] = jnp.where(sel, 0.0, MASKED).astype(bf16)
        qs = (qh * scale).astype(bf16)
        qs_ref[h] = qs
        s = jnp.where(c_io <= r_io, _dot_nt(qs, k_own), MASKED)
        m = jnp.max(s, axis=-1, keepdims=True)
        p = jnp.exp(s - m)
        m_ref[h] = jnp.broadcast_to(m, (tq, LANES))
        l_ref[h] = jnp.broadcast_to(jnp.sum(p, axis=-1, keepdims=True), (tq, LANES))
        acc_ref[h] = _dot(p.astype(bf16), v_own)

    km_ref[pl.ds(i, 1), :] = jnp.mean(k_ref[...], axis=0, keepdims=True)

    def past_block(j, carry):
        r0 = pl.multiple_of(j * tq, tq)
        kj = kb_ref[pl.ds(r0, tq), :]
        vj = vb_ref[pl.ds(r0, tq), :]
        ej = (lax.broadcasted_iota(jnp.int32, (LANES, tq), 0) == j).astype(bf16)
        for h in range(ATT_HEADS):
            s = _dot_nt(qs_ref[h], kj) + _dot(nm_ref[h], ej)
            m_prev = m_ref[h][:, 0:1]
            m_new = jnp.maximum(m_prev, jnp.max(s, axis=-1, keepdims=True))
            alpha = jnp.exp(m_prev - m_new)
            p = jnp.exp(s - m_new)
            l_ref[h] = jnp.broadcast_to(alpha * l_ref[h][:, 0:1] + jnp.sum(p, axis=-1, keepdims=True),
                                        (tq, LANES))
            m_ref[h] = jnp.broadcast_to(m_new, (tq, LANES))
            acc_ref[h] = acc_ref[h] * alpha + _dot(p.astype(bf16), vj)
        return carry

    lax.fori_loop(0, i, past_block, 0)

    out = jnp.zeros((tq, ATT_DIM), f32)
    for h in range(ATT_HEADS):
        out = jnp.where(head_of_lane == h, acc_ref[h] / l_ref[h][:, 0:1], out)
    o_ref[...] = out.astype(o_ref.dtype)


def moba_prompt(q, k, kb, vb):
    t = q.shape[0]
    tq = MOBA_BLOCK
    return pl.pallas_call(
        _moba_prompt_kernel, grid=(t // tq,),
        in_specs=[pl.BlockSpec((tq, ATT_DIM), lambda i: (i, 0)), pl.BlockSpec((tq, ATT_DIM), lambda i: (i, 0)),
                  _full(kb.shape), _full(vb.shape)],
        out_specs=pl.BlockSpec((tq, ATT_DIM), lambda i: (i, 0)),
        out_shape=jax.ShapeDtypeStruct((t, ATT_DIM), bf16),
        scratch_shapes=[pltpu.VMEM((LANES, ATT_DIM), f32),
                        pltpu.VMEM((ATT_HEADS, tq, ATT_DIM), bf16),
                        pltpu.VMEM((ATT_HEADS, tq, LANES), bf16),
                        pltpu.VMEM((ATT_HEADS, tq, ATT_DIM), f32),
                        pltpu.VMEM((ATT_HEADS, tq, LANES), f32),
                        pltpu.VMEM((ATT_HEADS, tq, LANES), f32)],
        compiler_params=_cparams(1), name="moba_prompt",
    )(q, k, kb, vb)


K_CHUNK_PAGES = 16
PAGES_PER_BLOCK = MOBA_BLOCK // PAGE_SIZE


def _moba_decode_kernel(pt_ref, q_ref, kn_ref, vn_ref, kc_ref, vc_ref, o_ref,
                        kbuf, vbuf, s_ref, p_ref, km_ref, ksem, vsem, *, n_pages):
    b = pl.program_id(0)
    nb = pl.num_programs(0)
    n_chunks = n_pages // K_CHUNK_PAGES
    blocks_per_chunk = K_CHUNK_PAGES // PAGES_PER_BLOCK
    n_blocks = n_pages // PAGES_PER_BLOCK
    scale = ATT_HEAD_DIM ** -0.5

    def k_copy(seq, page_slot, slot, p):
        page = pt_ref[seq, page_slot]
        return pltpu.make_async_copy(kc_ref.at[page], kbuf.at[slot, p], ksem.at[slot])

    def start_chunk(seq, c, slot):
        for p in range(K_CHUNK_PAGES):
            k_copy(seq, c * K_CHUNK_PAGES + p, slot, p).start()

    def wait_chunk(seq, c, slot):
        for p in range(K_CHUNK_PAGES):
            k_copy(seq, c * K_CHUNK_PAGES + p, slot, p).wait()

    @pl.when(b == 0)
    def _():
        km_ref[...] = jnp.zeros(km_ref.shape, f32)
        start_chunk(0, 0, 0)

    q = q_ref[...]
    lane8 = lax.broadcasted_iota(jnp.int32, (8, ATT_DIM), 1)
    row8 = lax.broadcasted_iota(jnp.int32, (8, ATT_DIM), 0)
    own_head = lane8 // ATT_HEAD_DIM == row8
    qblk = jnp.where(own_head, q, 0.0)
    qblk_b = qblk.astype(bf16)

    for c in range(n_chunks):
        slot = c % 2
        if c + 1 < n_chunks:
            start_chunk(b, c + 1, 1 - slot)
        else:
            @pl.when(b + 1 < nb)
            def _():
                start_chunk(b + 1, 0, 1 - slot)
        wait_chunk(b, c, slot)
        for t in range(blocks_per_chunk):
            kblk = kbuf[slot, t * PAGES_PER_BLOCK:(t + 1) * PAGES_PER_BLOCK].reshape(MOBA_BLOCK, ATT_DIM)
            blk = c * blocks_per_chunk + t
            s_ref[blk] = _dot_nt(qblk_b, kblk.astype(bf16))
            km_ref[blk:blk + 1, :] = jnp.mean(kblk, axis=0, keepdims=True)

    gate = _dot_nt(qblk, km_ref[...], precision=HIGHEST)
    picks = _topk_blocks(gate, n_blocks)

    blk_ids, v_copies = {}, []
    for h in range(ATT_HEADS):
        for r in range(MOBA_TOP_K):
            blk = picks[r][0][h, 0].astype(jnp.int32)
            blk_ids[h, r] = blk
            for half in range(PAGES_PER_BLOCK):
                v_copies.append(pltpu.make_async_copy(
                    vc_ref.at[pt_ref[b, blk * PAGES_PER_BLOCK + half]],
                    vbuf.at[h * MOBA_TOP_K + r, half], vsem.at[0]))
    for cp in v_copies:
        cp.start()

    blk_io = lax.broadcasted_iota(jnp.int32, (n_blocks, 8, MOBA_BLOCK), 0).astype(f32)
    seen = jnp.zeros((n_blocks, 8, MOBA_BLOCK), jnp.bool_)
    for idx, ok in picks:
        seen = seen | ((blk_io == idx[None]) & ok[None])
    s_all = jnp.where(seen, s_ref[...] * scale, -jnp.inf)
    s_own = jnp.sum(qblk * kn_ref[...], axis=-1, keepdims=True) * scale
    m = jnp.maximum(jnp.max(jnp.max(s_all, axis=0), axis=-1, keepdims=True), s_own)
    p_all = jnp.exp(s_all - m[None])
    p_own = jnp.exp(s_own - m)
    l = jnp.sum(jnp.sum(p_all, axis=0), axis=-1, keepdims=True) + p_own
    p_ref[...] = p_all

    for cp in v_copies:
        cp.wait()

    o = p_own * vn_ref[...]
    for h in range(ATT_HEADS):
        for r in range(MOBA_TOP_K):
            pb = jnp.where(row8 == h, p_ref[blk_ids[h, r]], 0.0).astype(bf16)
            vblk = vbuf[h * MOBA_TOP_K + r].reshape(MOBA_BLOCK, ATT_DIM).astype(bf16)
            o = o + _dot(pb, vblk)
    o = o / l
    o_ref[...] = jnp.sum(jnp.where(own_head, o, 0.0), axis=0, keepdims=True).astype(o_ref.dtype)


def moba_decode(page_table, q, k_new, v_new, cache_k, cache_v):
    b = q.shape[0]
    n_pages = page_table.shape[1]
    n_phys = cache_k.shape[0]
    n_blocks = n_pages // PAGES_PER_BLOCK
    per_seq = pl.BlockSpec((None, 1, ATT_DIM), lambda i, pt: (i, 0, 0))
    grid_spec = pltpu.PrefetchScalarGridSpec(
        num_scalar_prefetch=1, grid=(b,),
        in_specs=[per_seq, per_seq, per_seq,
                  pl.BlockSpec(memory_space=pl.ANY), pl.BlockSpec(memory_space=pl.ANY)],
        out_specs=per_seq,
        scratch_shapes=[pltpu.VMEM((2, K_CHUNK_PAGES, PAGE_SIZE, ATT_DIM), f32),
                        pltpu.VMEM((ATT_HEADS * MOBA_TOP_K, PAGES_PER_BLOCK, PAGE_SIZE, ATT_DIM), f32),
                        pltpu.VMEM((n_blocks, 8, MOBA_BLOCK), f32),
                        pltpu.VMEM((n_blocks, 8, MOBA_BLOCK), f32),
                        pltpu.VMEM((LANES, ATT_DIM), f32),
                        pltpu.SemaphoreType.DMA((2,)),
                        pltpu.SemaphoreType.DMA((1,))])
    out = pl.pallas_call(
        functools.partial(_moba_decode_kernel, n_pages=n_pages),
        grid_spec=grid_spec,
        out_shape=jax.ShapeDtypeStruct((b, 1, ATT_DIM), f32),
        compiler_params=_cparams(1), name="moba_decode",
    )(page_table, q.reshape(b, 1, ATT_DIM), k_new.reshape(b, 1, ATT_DIM), v_new.reshape(b, 1, ATT_DIM),
      cache_k.reshape(n_phys, PAGE_SIZE, ATT_DIM), cache_v.reshape(n_phys, PAGE_SIZE, ATT_DIM))
    return out.reshape(b, ATT_DIM)


def _layer_params(l, w_in, conv_w, conv_b, dt_bias, a_log, d_skip, ssm_norm_g, pool_w, pool_scale, w_out):
    o_xbc = SSM_INNER
    o_dt = o_xbc + CONV_DIM
    o_pool = o_dt + SSM_HEADS
    o_q = o_pool + POOL_DIM
    o_k = o_q + ATT_DIM
    o_v = o_k + ATT_DIM
    wl = w_in[l]
    pad_heads = lambda v: jnp.pad(v.astype(f32), (0, LANES - SSM_HEADS)).reshape(1, LANES)
    w_dt = jnp.pad(wl[:, o_dt:o_pool], ((0, 0), (0, LANES - SSM_HEADS)))
    w_split = [wl[:, :o_xbc], wl[:, o_xbc:o_dt], w_dt, wl[:, o_pool:o_q], wl[:, o_q:o_k],
               wl[:, o_k:o_v], wl[:, o_v:]]
    pw = jnp.zeros((POOL_DIM, POOL_DIM), f32)
    for g in range(len(POOL_WINDOWS)):
        pw = pw.at[g * POOL_GROUP:(g + 1) * POOL_GROUP, g * POOL_GROUP:(g + 1) * POOL_GROUP].set(pool_w[l, g])
    expand = (jnp.arange(LANES)[:, None] == (jnp.arange(SSM_INNER)[None, :] // SSM_HEAD_DIM)).astype(f32)
    wo = w_out[l].astype(bf16)
    return {
        'w_in': [w.astype(bf16) for w in w_split],
        'conv_w': conv_w[l], 'conv_b': conv_b[l].reshape(1, CONV_DIM),
        'dt_bias': pad_heads(dt_bias[l]), 'a': pad_heads(-jnp.exp(a_log[l].astype(f32))),
        'dsk': jnp.repeat(d_skip[l].astype(f32), SSM_HEAD_DIM).reshape(1, SSM_INNER),
        'ssm_norm_g': ssm_norm_g[l].reshape(1, SSM_INNER),
        'pool_w': pw.astype(bf16), 'pool_scale': pool_scale[l].reshape(1, POOL_DIM),
        'expand': expand,
        'w_out': [wo[:SSM_INNER], wo[SSM_INNER:SSM_INNER + POOL_DIM], wo[SSM_INNER + POOL_DIM:]],
    }


IN_PROJ_OUTS = [(0, f32), (1, f32), (2, f32), (3, f32), (4, f32), (5, f32), (6, f32), (5, bf16), (6, bf16)]


def kernel(x_prompt, x_sample, cache_moba_k, cache_moba_v, state_ssm, state_conv, state_pool, cache_mem_k, cache_mem_v, page_table, mem_prompt, norm_mix_g, w_in, conv_w, conv_b, dt_bias, a_log, d_skip, ssm_norm_g, pool_w, pool_scale, w_out, norm_cross_g, norm_mem_g, w_mem_q, w_mem_kv, w_mem_o, norm_ffn_g, w_gate_up, w_down, final_norm_g):
    depth = w_in.shape[0]
    bp, t, d = x_prompt.shape
    bs = x_sample.shape[0]
    assert bp == 1 and x_sample.shape[1] == 1
    past_len = page_table.shape[1] * PAGE_SIZE
    mem_len = mem_prompt.shape[1]
    xp = x_prompt.reshape(t, d)
    xs = x_sample.reshape(bs, d)
    mem = mem_prompt.reshape(mem_len, d)
    tm_p, tm_s = 512, bs
    outs = {n: [] for n in ('kp', 'vp', 'ks', 'vs', 'sp', 'ss', 'cp', 'cs', 'pp', 'ps', 'mk', 'mv')}
    for l in range(depth):
        prm = _layer_params(l, w_in, conv_w, conv_b, dt_bias, a_log, d_skip, ssm_norm_g, pool_w, pool_scale, w_out)
        wq = w_mem_q[l].astype(bf16)
        wkv = w_mem_kv[l].astype(bf16)
        wo_mem = w_mem_o[l].astype(bf16)
        wgu = w_gate_up[l].astype(bf16)
        wdn = w_down[l].astype(bf16)
        half = MEM_HEADS * MEM_HEAD_DIM

        mk, mv, mkb, mvb = norm_matmul(mem, norm_mem_g[l], [wkv[:, :half], wkv[:, half:]],
                                       [(0, f32), (1, f32), (0, bf16), (1, bf16)], tm=mem_len)
        z, xbc, dtr, u, q, k, v, kb, vb = norm_matmul(xp, norm_mix_g[l], prm['w_in'], IN_PROJ_OUTS, tm=tm_p)
        y, po, s_new = ssd_pool_prompt(z, xbc, dtr, u, prm, tm=tm_p)
        att = moba_prompt(q, k, kb, vb)
        xp = matmul_residual(xp, [y, po, att], prm['w_out'], tm=tm_p)
        xp = cross_prompt(xp, norm_cross_g[l], wq, mkb, mvb, wo_mem, tm=tm_p)
        xp = swiglu_block(xp, norm_ffn_g[l], wgu, wdn, tm=1024, tf=256)
        outs['kp'].append(k.reshape(1, t, ATT_HEADS, ATT_HEAD_DIM))
        outs['vp'].append(v.reshape(1, t, ATT_HEADS, ATT_HEAD_DIM))
        outs['sp'].append(s_new[None])
        outs['cp'].append(xbc[t - (CONV_W - 1):][None])
        outs['pp'].append(u[t - POOL_HIST:][None])
        outs['mk'].append(mk.reshape(1, mem_len, MEM_HEADS, MEM_HEAD_DIM))
        outs['mv'].append(mv.reshape(1, mem_len, MEM_HEADS, MEM_HEAD_DIM))

        z, xbc, dtr, u, q, k, v, _, _ = norm_matmul(xs, norm_mix_g[l], prm['w_in'], IN_PROJ_OUTS, tm=tm_s)
        y, po, c_new, s_new, p_new = ssd_pool_sample(z, xbc, dtr, u, state_conv[l], state_ssm[l],
                                                     state_pool[l], prm, pos0=past_len)
        att = moba_decode(page_table, q, k, v, cache_moba_k[l], cache_moba_v[l])
        xs = matmul_residual(xs, [y, po, att], prm['w_out'], tm=tm_s)
        (qc,) = norm_matmul(xs, norm_cross_g[l], [wq], [(0, f32)], tm=tm_s)
        oc = cross_sample(qc, cache_mem_k[l], cache_mem_v[l])
        xs = matmul_residual(xs, [oc], [wo_mem], tm=tm_s)
        xs = swiglu_block(xs, norm_ffn_g[l], wgu, wdn, tm=tm_s, tf=256)
        outs['ks'].append(k.reshape(bs, 1, ATT_HEADS, ATT_HEAD_DIM))
        outs['vs'].append(v.reshape(bs, 1, ATT_HEADS, ATT_HEAD_DIM))
        outs['ss'].append(s_new)
        outs['cs'].append(c_new)
        outs['ps'].append(p_new)

    y_prompt = final_norm(xp, final_norm_g, tm=tm_p).reshape(1, t, d)
    y_sample = final_norm(xs, final_norm_g, tm=tm_s).reshape(bs, 1, d)
    st = lambda n: jnp.stack(outs[n])
    return (y_prompt, y_sample, st('kp'), st('vp'), st('ks'), st('vs'), st('sp'), st('ss'),
            st('cp'), st('cs'), st('pp'), st('ps'), st('mk'), st('mv'))
```

```python
import functools

import jax
import jax.numpy as jnp
from jax import lax
from jax.experimental import pallas as pl
from jax.experimental.pallas import tpu as pltpu

f32 = jnp.float32
bf16 = jnp.bfloat16
HIGHEST = lax.Precision.HIGHEST

D_MODEL = 1024
SSM_INNER = 512
SSM_HEAD_DIM = 64
SSM_HEADS = 8
SSM_GROUPS = 2
SSM_STATE = 128
CONV_W = 4
CONV_DIM = SSM_INNER + 2 * SSM_GROUPS * SSM_STATE
SSD_CHUNK = 128
POOL_DIM = 256
POOL_WINDOWS = (2, 4, 8, 16)
POOL_GROUP = 64
POOL_HIST = 15
ATT_DIM = 256
ATT_HEAD_DIM = 64
ATT_HEADS = 4
MOBA_BLOCK = 256
MOBA_TOP_K = 3
PAGE_SIZE = 128
MEM_HEADS = 4
MEM_HEAD_DIM = 256
D_FF = 2816
RMS_EPS = 1e-6
LANES = 128
MASKED = -1e30
VMEM_LIMIT = 56 * 1024 * 1024


def _cparams(n_axes):
    return pltpu.CompilerParams(dimension_semantics=("arbitrary",) * n_axes,
                                vmem_limit_bytes=VMEM_LIMIT)


def _rms(x, g):
    ms = jnp.mean(x * x, axis=-1, keepdims=True)
    return x * lax.rsqrt(ms + RMS_EPS) * g


def _dot(a, b, **kw):
    return jnp.dot(a, b, preferred_element_type=f32, **kw)


def _dot_nt(a, b, **kw):
    return lax.dot_general(a, b, (((1,), (1,)), ((), ())), preferred_element_type=f32, **kw)


def _silu(x):
    return x * jax.nn.sigmoid(x)


def _softplus(x):
    return jnp.maximum(x, 0.0) + jnp.log1p(jnp.exp(-jnp.abs(x)))


def _full(shape):
    return pl.BlockSpec(shape, lambda *_: (0,) * len(shape))


def _norm_mm_kernel(*refs, n_w, out_w):
    x_ref, g_ref = refs[0], refs[1]
    w_refs = refs[2:2 + n_w]
    o_refs = refs[2 + n_w:]
    hb = _rms(x_ref[...], g_ref[...]).astype(bf16)
    res = [_dot(hb, w[...]) for w in w_refs]
    for o_ref, wi in zip(o_refs, out_w):
        o_ref[...] = res[wi].astype(o_ref.dtype)


def norm_matmul(x, g, ws, outs, tm):
    m, d = x.shape
    in_specs = [pl.BlockSpec((tm, d), lambda i: (i, 0)), _full((1, d))]
    in_specs += [_full(w.shape) for w in ws]
    out_shape = [jax.ShapeDtypeStruct((m, ws[wi].shape[1]), dt) for wi, dt in outs]
    out_specs = [pl.BlockSpec((tm, ws[wi].shape[1]), lambda i: (i, 0)) for wi, _ in outs]
    return pl.pallas_call(
        functools.partial(_norm_mm_kernel, n_w=len(ws), out_w=tuple(wi for wi, _ in outs)),
        grid=(m // tm,), in_specs=in_specs, out_specs=out_specs, out_shape=out_shape,
        compiler_params=_cparams(1), name="norm_matmul",
    )(x, g.reshape(1, d), *ws)


def _mm_res_kernel(*refs, n_a):
    res_ref = refs[0]
    a_refs = refs[1:1 + n_a]
    w_refs = refs[1 + n_a:1 + 2 * n_a]
    o_ref = refs[-1]
    acc = res_ref[...]
    for a, w in zip(a_refs, w_refs):
        acc = acc + _dot(a[...].astype(bf16), w[...])
    o_ref[...] = acc


def matmul_residual(res, a_list, w_list, tm):
    m, d = res.shape
    in_specs = [pl.BlockSpec((tm, d), lambda i: (i, 0))]
    in_specs += [pl.BlockSpec((tm, a.shape[1]), lambda i: (i, 0)) for a in a_list]
    in_specs += [_full(w.shape) for w in w_list]
    return pl.pallas_call(
        functools.partial(_mm_res_kernel, n_a=len(a_list)),
        grid=(m // tm,), in_specs=in_specs,
        out_specs=pl.BlockSpec((tm, d), lambda i: (i, 0)),
        out_shape=jax.ShapeDtypeStruct((m, d), f32),
        compiler_params=_cparams(1), name="matmul_residual",
    )(res, *a_list, *w_list)


def _cross_prompt_kernel(x_ref, g_ref, wq_ref, mk_ref, mv_ref, wo_ref, o_ref):
    x = x_ref[...]
    hb = _rms(x, g_ref[...]).astype(bf16)
    q = _dot(hb, wq_ref[...])
    acc = x
    for h in range(MEM_HEADS):
        sl = slice(h * MEM_HEAD_DIM, (h + 1) * MEM_HEAD_DIM)
        s = _dot_nt(q[:, sl].astype(bf16), mk_ref[:, sl]) * (MEM_HEAD_DIM ** -0.5)
        p = jnp.exp(s - jnp.max(s, axis=-1, keepdims=True))
        p = p / jnp.sum(p, axis=-1, keepdims=True)
        oh = _dot(p.astype(bf16), mv_ref[:, sl])
        acc = acc + _dot(oh.astype(bf16), wo_ref[sl, :])
    o_ref[...] = acc


def cross_prompt(x, g, wq, mkb, mvb, wo, tm):
    m, d = x.shape
    return pl.pallas_call(
        _cross_prompt_kernel, grid=(m // tm,),
        in_specs=[pl.BlockSpec((tm, d), lambda i: (i, 0)), _full((1, d)), _full(wq.shape),
                  _full(mkb.shape), _full(mvb.shape), _full(wo.shape)],
        out_specs=pl.BlockSpec((tm, d), lambda i: (i, 0)),
        out_shape=jax.ShapeDtypeStruct((m, d), f32),
        compiler_params=_cparams(1), name="cross_prompt",
    )(x, g.reshape(1, d), wq, mkb, mvb, wo)


def _cross_sample_kernel(q_ref, mk_ref, mv_ref, o_ref):
    q = q_ref[...]
    lane = lax.broadcasted_iota(jnp.int32, (8, D_MODEL), 1)
    row = lax.broadcasted_iota(jnp.int32, (8, D_MODEL), 0)
    qblk = jnp.where(lane // MEM_HEAD_DIM == row, q, 0.0).astype(bf16)
    s = _dot_nt(qblk, mk_ref[...].astype(bf16)) * (MEM_HEAD_DIM ** -0.5)
    p = jnp.exp(s - jnp.max(s, axis=-1, keepdims=True))
    p = p / jnp.sum(p, axis=-1, keepdims=True)
    o = _dot(p.astype(bf16), mv_ref[...].astype(bf16))
    o_ref[...] = jnp.sum(jnp.where(lane // MEM_HEAD_DIM == row, o, 0.0), axis=0, keepdims=True)


def cross_sample(q, mem_k, mem_v):
    b, d = q.shape
    mlen = mem_k.shape[1]
    out = pl.pallas_call(
        _cross_sample_kernel, grid=(b,),
        in_specs=[pl.BlockSpec((None, 1, d), lambda i: (i, 0, 0)),
                  pl.BlockSpec((None, mlen, d), lambda i: (i, 0, 0)),
                  pl.BlockSpec((None, mlen, d), lambda i: (i, 0, 0))],
        out_specs=pl.BlockSpec((None, 1, d), lambda i: (i, 0, 0)),
        out_shape=jax.ShapeDtypeStruct((b, 1, d), f32),
        compiler_params=_cparams(1), name="cross_sample",
    )(q.reshape(b, 1, d), mem_k.reshape(b, mlen, d), mem_v.reshape(b, mlen, d))
    return out.reshape(b, d)


def _swiglu_kernel(x_ref, g_ref, wg_ref, wu_ref, wd_ref, o_ref, h_ref, acc_ref):
    f = pl.program_id(1)

    @pl.when(f == 0)
    def _():
        h_ref[...] = _rms(x_ref[...], g_ref[...]).astype(bf16)
        acc_ref[...] = x_ref[...]

    hb = h_ref[...]
    a = _silu(_dot(hb, wg_ref[...])) * _dot(hb, wu_ref[...])
    acc_ref[...] += _dot(a.astype(bf16), wd_ref[...])

    @pl.when(f == pl.num_programs(1) - 1)
    def _():
        o_ref[...] = acc_ref[...]


def swiglu_block(x, g, w_gu, w_d, tm, tf):
    m, d = x.shape
    nf = D_FF // tf
    return pl.pallas_call(
        _swiglu_kernel, grid=(m // tm, nf),
        in_specs=[pl.BlockSpec((tm, d), lambda i, f: (i, 0)), _full((1, d)),
                  pl.BlockSpec((d, tf), lambda i, f: (0, f)),
                  pl.BlockSpec((d, tf), lambda i, f: (0, f + nf)),
                  pl.BlockSpec((tf, d), lambda i, f: (f, 0))],
        out_specs=pl.BlockSpec((tm, d), lambda i, f: (i, 0)),
        out_shape=jax.ShapeDtypeStruct((m, d), f32),
        scratch_shapes=[pltpu.VMEM((tm, d), bf16), pltpu.VMEM((tm, d), f32)],
        compiler_params=_cparams(2), name="swiglu_block",
    )(x, g.reshape(1, d), w_gu, w_gu, w_d)


def _final_norm_kernel(x_ref, g_ref, o_ref):
    o_ref[...] = _rms(x_ref[...], g_ref[...])


def final_norm(x, g, tm):
    m, d = x.shape
    return pl.pallas_call(
        _final_norm_kernel, grid=(m // tm,),
        in_specs=[pl.BlockSpec((tm, d), lambda i: (i, 0)), _full((1, d))],
        out_specs=pl.BlockSpec((tm, d), lambda i: (i, 0)),
        out_shape=jax.ShapeDtypeStruct((m, d), f32),
        compiler_params=_cparams(1), name="final_norm",
    )(x, g.reshape(1, d))


def _pool_select(w2, w4, w8, w16, cnt_of, lane):
    out = w16 / cnt_of(16)
    for win, acc in ((8, w8), (4, w4), (2, w2)):
        g = POOL_WINDOWS.index(win)
        out = jnp.where(lane < (g + 1) * POOL_GROUP, acc / cnt_of(win), out)
    return out


def _group_rmsnorm(y, g):
    half = SSM_INNER // SSM_GROUPS
    parts = [_rms(y[:, i * half:(i + 1) * half], g[:, i * half:(i + 1) * half])
             for i in range(SSM_GROUPS)]
    return jnp.concatenate(parts, axis=-1)


def _ssd_pool_prompt_kernel(z_ref, xbc_ref, dt_ref, u_ref, cw_ref, cb_ref, dtb_ref, a_ref, dsk_ref,
                            ng_ref, pw_ref, ps_ref,
                            y_ref, po_ref, st_ref,
                            ext_ref, pext_ref, xc_ref, s_ref, *, tm):
    i = pl.program_id(0)
    halo = 8
    phalo = 16

    @pl.when(i == 0)
    def _():
        ext_ref[0:halo, :] = jnp.zeros((halo, CONV_DIM), f32)
        pext_ref[0:phalo, :] = jnp.zeros((phalo, POOL_DIM), f32)
        s_ref[...] = jnp.zeros(s_ref.shape, f32)

    @pl.when(i > 0)
    def _():
        ext_ref[0:halo, :] = ext_ref[tm:tm + halo, :]
        pext_ref[0:phalo, :] = pext_ref[tm:tm + phalo, :]

    ext_ref[halo:halo + tm, :] = xbc_ref[...]
    pext_ref[phalo:phalo + tm, :] = u_ref[...]

    acc = jnp.broadcast_to(cb_ref[...], (tm, CONV_DIM))
    for j in range(CONV_W):
        off = halo - (CONV_W - 1) + j
        acc = acc + ext_ref[off:off + tm, :] * cw_ref[j:j + 1, :]
    xc_ref[...] = _silu(acc)

    u = u_ref[...]
    run = u
    sums = {}
    for back in range(1, max(POOL_WINDOWS)):
        run = run + pext_ref[phalo - back:phalo - back + tm, :]
        if back + 1 in POOL_WINDOWS:
            sums[back + 1] = run
    lane_p = lax.broadcasted_iota(jnp.int32, (tm, POOL_DIM), 1)
    pos1 = (lax.broadcasted_iota(jnp.int32, (tm, POOL_DIM), 0) + i * tm + 1).astype(f32)
    pooled = _pool_select(sums[2], sums[4], sums[8], sums[16],
                          lambda win: jnp.minimum(float(win), pos1), lane_p) - u
    po_ref[...] = (_dot(pooled.astype(bf16), pw_ref[...]) * ps_ref[...]).astype(po_ref.dtype)

    cl = SSD_CHUNK
    r_io = lax.broadcasted_iota(jnp.int32, (cl, cl), 0)
    c_io = lax.broadcasted_iota(jnp.int32, (cl, cl), 1)
    ltri = (c_io <= r_io).astype(f32)
    causal = c_io <= r_io
    first_half = c_io < SSM_HEAD_DIM
    n_pairs = SSM_HEADS // 2
    heads_per_group = SSM_HEADS // SSM_GROUPS

    def chunk(c, carry):
        r0 = pl.multiple_of(c * cl, cl)
        xc = xc_ref[pl.ds(r0, cl), :]
        dt = _softplus(dt_ref[pl.ds(r0, cl), :] + dtb_ref[...])
        a_cum = _dot(ltri, dt * a_ref[...], precision=HIGHEST)
        a_cum_t = a_cum.T
        dt_t = dt.T
        to_end_t = jnp.exp(a_cum_t[:, cl - 1:cl] - a_cum_t) * dt_t
        b_t = [xc[:, SSM_INNER + g * SSM_STATE:SSM_INNER + (g + 1) * SSM_STATE].T
               for g in range(SSM_GROUPS)]
        cm = [xc[:, SSM_INNER + SSM_GROUPS * SSM_STATE + g * SSM_STATE:
                 SSM_INNER + SSM_GROUPS * SSM_STATE + (g + 1) * SSM_STATE].astype(bf16)
              for g in range(SSM_GROUPS)]
        scores = [_dot(cm[g], b_t[g].astype(bf16)) for g in range(SSM_GROUPS)]
        y_pairs = []
        for k in range(n_pairs):
            g = (2 * k) // heads_per_group
            xs_pair = xc[:, k * LANES:(k + 1) * LANES]
            xs_b = xs_pair.astype(bf16)
            yd, cs, eb = [], [], []
            for h in (2 * k, 2 * k + 1):
                colb = jnp.broadcast_to(a_cum[:, h:h + 1], (cl, cl))
                seg = colb - a_cum_t[h:h + 1, :]
                decay = jnp.exp(jnp.where(causal, seg, -jnp.inf))
                mh = scores[g] * decay * dt_t[h:h + 1, :]
                yd.append(_dot(mh.astype(bf16), xs_b))
                cs.append(_dot((b_t[g] * to_end_t[h:h + 1, :]).astype(bf16), xs_b))
                eb.append(jnp.exp(colb))
            e_pair = jnp.where(first_half, eb[0], eb[1])
            s_old = s_ref[k]
            y_off = _dot(cm[g], s_old.astype(bf16)) * e_pair
            y_pairs.append(jnp.where(first_half, yd[0], yd[1]) + y_off
                           + dsk_ref[:, k * LANES:(k + 1) * LANES] * xs_pair)
            s_ref[k] = s_old * e_pair[cl - 1:cl, :] + jnp.where(first_half, cs[0], cs[1])
        y = jnp.concatenate(y_pairs, axis=-1) * _silu(z_ref[pl.ds(r0, cl), :])
        y_ref[pl.ds(r0, cl), :] = _group_rmsnorm(y, ng_ref[...]).astype(y_ref.dtype)
        return carry

    lax.fori_loop(0, tm // cl, chunk, 0)

    @pl.when(i == pl.num_programs(0) - 1)
    def _():
        for k in range(n_pairs):
            st = s_ref[k].T
            st_ref[2 * k] = st[0:SSM_HEAD_DIM, :]
            st_ref[2 * k + 1] = st[SSM_HEAD_DIM:2 * SSM_HEAD_DIM, :]


def ssd_pool_prompt(z, xbc, dt, u, prm, tm):
    t = z.shape[0]
    row = lambda c: pl.BlockSpec((tm, c), lambda i: (i, 0))
    consts = [prm['conv_w'], prm['conv_b'], prm['dt_bias'], prm['a'], prm['dsk'], prm['ssm_norm_g'],
              prm['pool_w'], prm['pool_scale']]
    return pl.pallas_call(
        functools.partial(_ssd_pool_prompt_kernel, tm=tm), grid=(t // tm,),
        in_specs=[row(SSM_INNER), row(CONV_DIM), row(LANES), row(POOL_DIM)] + [_full(c.shape) for c in consts],
        out_specs=[row(SSM_INNER), row(POOL_DIM), _full((SSM_HEADS, SSM_HEAD_DIM, SSM_STATE))],
        out_shape=[jax.ShapeDtypeStruct((t, SSM_INNER), bf16), jax.ShapeDtypeStruct((t, POOL_DIM), bf16),
                   jax.ShapeDtypeStruct((SSM_HEADS, SSM_HEAD_DIM, SSM_STATE), f32)],
        scratch_shapes=[pltpu.VMEM((tm + 8, CONV_DIM), f32), pltpu.VMEM((tm + 16, POOL_DIM), f32),
                        pltpu.VMEM((tm, CONV_DIM), f32),
                        pltpu.VMEM((SSM_HEADS // 2, SSM_STATE, 2 * SSM_HEAD_DIM), f32)],
        compiler_params=_cparams(1), name="ssd_pool_prompt",
    )(z, xbc, dt, u, *consts)


def _col_tile(row):
    return jnp.broadcast_to(row, (LANES, LANES)).T


def _ssd_pool_sample_kernel(z_ref, xbc_ref, dt_ref, u_ref, sc_ref, ss_ref, sp_ref,
                            cw_ref, cb_ref, dtb_ref, a_ref, dsk_ref, ng_ref, pw_ref, ps_ref, ex_ref,
                            y_ref, po_ref, cn_ref, sn_ref, pn_ref, *, pos0):
    xrow = xbc_ref[...]
    acc = cb_ref[...] + xrow * cw_ref[CONV_W - 1:CONV_W, :]
    for j in range(CONV_W - 1):
        acc = acc + sc_ref[j:j + 1, :] * cw_ref[j:j + 1, :]
        if j > 0:
            cn_ref[j - 1:j, :] = sc_ref[j:j + 1, :]
    cn_ref[CONV_W - 2:CONV_W - 1, :] = xrow
    xc = _silu(acc)

    dt = _softplus(dt_ref[...] + dtb_ref[...])
    dta = dt * a_ref[...]
    both = jnp.concatenate([jnp.broadcast_to(dt, (8, LANES)), jnp.broadcast_to(dta, (8, LANES))], axis=0)
    both_x = _dot(both, ex_ref[...], precision=HIGHEST)
    dtx = both_x[0:1, :]
    dec_x = jnp.exp(both_x[8:9, :])
    xs = xc[:, 0:SSM_INNER]
    xdt = xs * dtx
    heads_per_group = SSM_HEADS // SSM_GROUPS
    y_pairs = []
    for k in range(SSM_HEADS // 2):
        g = (2 * k) // heads_per_group
        b_row = xc[:, SSM_INNER + g * SSM_STATE:SSM_INNER + (g + 1) * SSM_STATE]
        c_row = xc[:, SSM_INNER + SSM_GROUPS * SSM_STATE + g * SSM_STATE:
                   SSM_INNER + SSM_GROUPS * SSM_STATE + (g + 1) * SSM_STATE]
        sl = slice(k * LANES, (k + 1) * LANES)
        s_old = jnp.concatenate([ss_ref[2 * k], ss_ref[2 * k + 1]], axis=0)
        s_new = s_old * _col_tile(dec_x[:, sl]) + _col_tile(xdt[:, sl]) * b_row
        sn_ref[2 * k] = s_new[0:SSM_HEAD_DIM, :]
        sn_ref[2 * k + 1] = s_new[SSM_HEAD_DIM:, :]
        y_k = _dot_nt(jnp.broadcast_to(c_row, (8, SSM_STATE)), s_new, precision=HIGHEST)[0:1, :]
        y_pairs.append(y_k + dsk_ref[:, sl] * xs[:, sl])
    y = jnp.concatenate(y_pairs, axis=-1) * _silu(z_ref[...])
    y_ref[...] = _group_rmsnorm(y, ng_ref[...]).astype(y_ref.dtype)

    u = u_ref[...]
    prev = sp_ref[...]
    rowi = lax.broadcasted_iota(jnp.int32, prev.shape, 0)
    tail = lambda win: u + jnp.sum(jnp.where(rowi >= POOL_HIST - (win - 1), prev, 0.0), axis=0, keepdims=True)
    lane_p = lax.broadcasted_iota(jnp.int32, (1, POOL_DIM), 1)
    pooled = _pool_select(tail(2), tail(4), tail(8), tail(16),
                          lambda win: float(min(win, pos0 + 1)), lane_p) - u
    po = _dot(jnp.broadcast_to(pooled, (8, POOL_DIM)).astype(bf16), pw_ref[...])[0:1, :] * ps_ref[...]
    po_ref[...] = po.astype(po_ref.dtype)
    pn_ref[0:POOL_HIST - 1, :] = sp_ref[1:POOL_HIST, :]
    pn_ref[POOL_HIST - 1:POOL_HIST, :] = u


def ssd_pool_sample(z, xbc, dt, u, st_conv, st_ssm, st_pool, prm, pos0):
    b = z.shape[0]
    per_seq = lambda *shape: pl.BlockSpec((None,) + shape, lambda i: (i,) + (0,) * len(shape))
    consts = [prm['conv_w'], prm['conv_b'], prm['dt_bias'], prm['a'], prm['dsk'], prm['ssm_norm_g'],
              prm['pool_w'], prm['pool_scale'], prm['expand']]
    outs = pl.pallas_call(
        functools.partial(_ssd_pool_sample_kernel, pos0=pos0), grid=(b,),
        in_specs=[per_seq(1, SSM_INNER), per_seq(1, CONV_DIM), per_seq(1, LANES), per_seq(1, POOL_DIM),
                  per_seq(CONV_W - 1, CONV_DIM), per_seq(SSM_HEADS, SSM_HEAD_DIM, SSM_STATE),
                  per_seq(POOL_HIST, POOL_DIM)] + [_full(c.shape) for c in consts],
        out_specs=[per_seq(1, SSM_INNER), per_seq(1, POOL_DIM), per_seq(CONV_W - 1, CONV_DIM),
                   per_seq(SSM_HEADS, SSM_HEAD_DIM, SSM_STATE), per_seq(POOL_HIST, POOL_DIM)],
        out_shape=[jax.ShapeDtypeStruct((b, 1, SSM_INNER), f32), jax.ShapeDtypeStruct((b, 1, POOL_DIM), f32),
                   jax.ShapeDtypeStruct((b, CONV_W - 1, CONV_DIM), f32),
                   jax.ShapeDtypeStruct((b, SSM_HEADS, SSM_HEAD_DIM, SSM_STATE), f32),
                   jax.ShapeDtypeStruct((b, POOL_HIST, POOL_DIM), f32)],
        compiler_params=_cparams(1), name="ssd_pool_sample",
    )(z.reshape(b, 1, -1), xbc.reshape(b, 1, -1), dt.reshape(b, 1, -1), u.reshape(b, 1, -1),
      st_conv, st_ssm, st_pool, *consts)
    y, po, cn, sn, pn = outs
    return y.reshape(b, -1), po.reshape(b, -1), cn, sn, pn


def _topk_blocks(gate, n_past, axis):
    blk = lax.broadcasted_iota(jnp.int32, gate.shape, axis).astype(f32)
    g = jnp.where(blk < jnp.asarray(n_past, f32), gate, -jnp.inf)
    picks = []
    for _ in range(MOBA_TOP_K):
        m = jnp.max(g, axis=axis, keepdims=True)
        idx = jnp.min(jnp.where(g == m, blk, float(LANES)), axis=axis, keepdims=True)
        picks.append((idx, jnp.abs(m) < jnp.inf))
        g = jnp.where(blk == idx, -jnp.inf, g)
    return picks


def _moba_prompt_kernel(q_ref, k_ref, v_ref, o_ref,
                        km_ref, kb_ref, vt_ref, qst_ref, nmt_ref, acc_ref, sa_ref, sb_ref):
    i = pl.program_id(0)
    tq = MOBA_BLOCK
    scale = ATT_HEAD_DIM ** -0.5

    @pl.when(i == 0)
    def _():
        km_ref[...] = jnp.zeros(km_ref.shape, f32)

    k = k_ref[...]
    kb = k.astype(bf16)
    kb_ref[i] = kb
    vt = v_ref[...].T.astype(bf16)
    vt_ref[i] = vt
    qt = q_ref[...].T
    head_of_row = lax.broadcasted_iota(jnp.int32, (ATT_DIM, tq), 0) // ATT_HEAD_DIM
    blk_row = lax.broadcasted_iota(jnp.int32, (LANES, tq), 0).astype(f32)
    key_io = lax.broadcasted_iota(jnp.int32, (tq, tq), 0)
    qry_io = lax.broadcasted_iota(jnp.int32, (tq, tq), 1)
    km = km_ref[...]
    stats = []
    for h in range(ATT_HEADS):
        qth = jnp.where(head_of_row == h, qt, 0.0)
        gate = _dot(km, qth, precision=HIGHEST)
        sel = jnp.zeros((LANES, tq), jnp.bool_)
        for idx, ok in _topk_blocks(gate, i, axis=0):
            sel = sel | ((blk_row == idx) & ok)
        nmt_ref[h] = jnp.where(sel, 0.0, MASKED)
        qst = (qth * scale).astype(bf16)
        qst_ref[h] = qst
        st = jnp.where(key_io <= qry_io, _dot(kb, qst), MASKED)
        m = jnp.max(st, axis=0, keepdims=True)
        p = jnp.exp(st - m)
        stats += [m, jnp.sum(p, axis=0, keepdims=True)]
        acc_ref[h] = _dot(vt[h * ATT_HEAD_DIM:(h + 1) * ATT_HEAD_DIM, :], p.astype(bf16))

    km_ref[pl.ds(i, 1), :] = jnp.mean(k, axis=0, keepdims=True)

    def scores(j, dst):
        kj = kb_ref[j]
        for h in range(ATT_HEADS):
            dst[h] = _dot(kj, qst_ref[h])

    def absorb(j, src, carry):
        new = []
        for h in range(ATT_HEADS):
            m_prev, l_prev = carry[2 * h], carry[2 * h + 1]
            st = src[h] + nmt_ref[h, pl.ds(j, 1), :]
            m_new = jnp.maximum(m_prev, jnp.max(st, axis=0, keepdims=True))
            alpha = jnp.exp(m_prev - m_new)
            p = jnp.exp(st - m_new)
            new += [m_new, alpha * l_prev + jnp.sum(p, axis=0, keepdims=True)]
            acc_ref[h] = acc_ref[h] * alpha + _dot(vt_ref[j, pl.ds(h * ATT_HEAD_DIM, ATT_HEAD_DIM), :],
                                                   p.astype(bf16))
        return tuple(new)

    last = jnp.maximum(i - 1, 0)
    scores(0, sa_ref)

    def pair(t, carry):
        scores(jnp.minimum(2 * t + 1, last), sb_ref)
        carry = absorb(2 * t, sa_ref, carry)
        scores(jnp.minimum(2 * t + 2, last), sa_ref)
        return absorb(2 * t + 1, sb_ref, carry)

    stats = lax.fori_loop(0, i // 2, pair, tuple(stats))
    stats = lax.cond(i % 2 == 1, lambda c: absorb(i - 1, sa_ref, c), lambda c: c, stats)
    out_t = jnp.concatenate([acc_ref[h] / stats[2 * h + 1] for h in range(ATT_HEADS)], axis=0)
    o_ref[...] = out_t.T.astype(o_ref.dtype)


def moba_prompt(q, k, v):
    t = q.shape[0]
    tq = MOBA_BLOCK
    n_blk = t // tq
    tile = pl.BlockSpec((tq, ATT_DIM), lambda i: (i, 0))
    return pl.pallas_call(
        _moba_prompt_kernel, grid=(n_blk,),
        in_specs=[tile, tile, tile], out_specs=tile,
        out_shape=jax.ShapeDtypeStruct((t, ATT_DIM), bf16),
        scratch_shapes=[pltpu.VMEM((LANES, ATT_DIM), f32),
                        pltpu.VMEM((n_blk, tq, ATT_DIM), bf16),
                        pltpu.VMEM((n_blk, ATT_DIM, tq), bf16),
                        pltpu.VMEM((ATT_HEADS, ATT_DIM, tq), bf16),
                        pltpu.VMEM((ATT_HEADS, LANES, tq), f32),
                        pltpu.VMEM((ATT_HEADS, ATT_HEAD_DIM, tq), f32),
                        pltpu.VMEM((ATT_HEADS, tq, tq), f32),
                        pltpu.VMEM((ATT_HEADS, tq, tq), f32)],
        compiler_params=_cparams(1), name="moba_prompt",
    )(q, k, v)


K_CHUNK_PAGES = 16
PAGES_PER_BLOCK = MOBA_BLOCK // PAGE_SIZE


def _moba_decode_kernel(pt_ref, q_ref, q8_ref, kn8_ref, vn8_ref, kc_ref, vc_ref, o_ref,
                        kbuf, vbuf, s_ref, p_ref, ksem, vsem, *, layer, n_pages):
    b = pl.program_id(0)
    nb = pl.num_programs(0)
    n_chunks = n_pages // K_CHUNK_PAGES
    blocks_per_chunk = K_CHUNK_PAGES // PAGES_PER_BLOCK
    n_blocks = n_pages // PAGES_PER_BLOCK
    scale = ATT_HEAD_DIM ** -0.5

    def k_copy(seq, page_slot, slot, p):
        page = pt_ref[seq, page_slot]
        return pltpu.make_async_copy(kc_ref.at[layer, page], kbuf.at[slot, p], ksem.at[slot])

    def start_chunk(seq, c, slot):
        for p in range(K_CHUNK_PAGES):
            k_copy(seq, c * K_CHUNK_PAGES + p, slot, p).start()

    def wait_chunk(seq, c, slot):
        for p in range(K_CHUNK_PAGES):
            k_copy(seq, c * K_CHUNK_PAGES + p, slot, p).wait()

    @pl.when(b == 0)
    def _():
        start_chunk(0, 0, 0)

    q = q_ref[...]
    lane8 = lax.broadcasted_iota(jnp.int32, (8, ATT_DIM), 1)
    row8 = lax.broadcasted_iota(jnp.int32, (8, ATT_DIM), 0)
    qblk_b = jnp.where(lane8 // ATT_HEAD_DIM == row8, q, 0.0).astype(bf16)
    blk_lane = lax.broadcasted_iota(jnp.int32, (8, LANES), 1)

    gate = jnp.zeros((8, LANES), f32)
    for c in range(n_chunks):
        slot = c % 2
        if c + 1 < n_chunks:
            start_chunk(b, c + 1, 1 - slot)
        else:
            @pl.when(b + 1 < nb)
            def _():
                start_chunk(b + 1, 0, 1 - slot)
        wait_chunk(b, c, slot)
        for t in range(blocks_per_chunk):
            blk = c * blocks_per_chunk + t
            halves = [_dot(qblk_b, kbuf[slot, t * PAGES_PER_BLOCK + half].astype(bf16))
                      for half in range(PAGES_PER_BLOCK)]
            for half in range(PAGES_PER_BLOCK):
                s_ref[blk, :, half * PAGE_SIZE:(half + 1) * PAGE_SIZE] = halves[half]
            tot = jnp.sum(sum(halves), axis=-1, keepdims=True) * (1.0 / MOBA_BLOCK)
            gate = jnp.where(blk_lane == blk, tot, gate)

    picks = _topk_blocks(gate, n_blocks, axis=1)

    blk_ids, v_copies = {}, []
    for h in range(ATT_HEADS):
        for r in range(MOBA_TOP_K):
            blk = picks[r][0][h, 0].astype(jnp.int32)
            blk_ids[h, r] = blk
            for half in range(PAGES_PER_BLOCK):
                page = pt_ref[b, blk * PAGES_PER_BLOCK + half]
                v_copies.append(pltpu.make_async_copy(
                    vc_ref.at[layer, page, pl.ds(h * ATT_HEAD_DIM, ATT_HEAD_DIM), :],
                    vbuf.at[h * MOBA_TOP_K + r, half], vsem.at[0]))
    for cp in v_copies:
        cp.start()

    blk_io = lax.broadcasted_iota(jnp.int32, (n_blocks, 8, MOBA_BLOCK), 0).astype(f32)
    seen = jnp.zeros((n_blocks, 8, MOBA_BLOCK), jnp.bool_)
    for idx, ok in picks:
        seen = seen | ((blk_io == idx[None]) & ok[None])
    s_all = jnp.where(seen, s_ref[...] * scale, -jnp.inf)
    s_own = jnp.sum(q8_ref[...] * kn8_ref[...], axis=-1, keepdims=True) * scale
    m = jnp.maximum(jnp.max(jnp.max(s_all, axis=0), axis=-1, keepdims=True), s_own)
    p_all = jnp.exp(s_all - m[None])
    p_own = jnp.exp(s_own - m)
    l = jnp.sum(jnp.sum(p_all, axis=0), axis=-1, keepdims=True) + p_own
    p_ref[...] = p_all

    for cp in v_copies:
        cp.wait()

    row_hd = lax.broadcasted_iota(jnp.int32, (8, ATT_HEAD_DIM), 0)
    o = p_own * vn8_ref[...]
    for h in range(ATT_HEADS):
        oh = jnp.zeros((8, ATT_HEAD_DIM), f32)
        for r in range(MOBA_TOP_K):
            pb = p_ref[blk_ids[h, r]].astype(bf16)
            for half in range(PAGES_PER_BLOCK):
                oh = oh + _dot_nt(pb[:, half * PAGE_SIZE:(half + 1) * PAGE_SIZE],
                                  vbuf[h * MOBA_TOP_K + r, half].astype(bf16))
        o = o + jnp.where(row_hd == h, oh, 0.0)
    o_ref[...] = o / l


def moba_decode(page_table, q, k_new, v_new, cache_kt, cache_vt, layer):
    b = q.shape[0]
    n_pages = page_table.shape[1]
    n_blocks = n_pages // PAGES_PER_BLOCK
    assert n_blocks <= LANES and n_blocks >= MOBA_TOP_K
    heads8 = lambda a: jnp.pad(a.reshape(b, ATT_HEADS, ATT_HEAD_DIM), ((0, 0), (0, 8 - ATT_HEADS), (0, 0)))
    per_head = pl.BlockSpec((None, 8, ATT_HEAD_DIM), lambda i, pt: (i, 0, 0))
    grid_spec = pltpu.PrefetchScalarGridSpec(
        num_scalar_prefetch=1, grid=(b,),
        in_specs=[pl.BlockSpec((None, 1, ATT_DIM), lambda i, pt: (i, 0, 0)), per_head, per_head, per_head,
                  pl.BlockSpec(memory_space=pl.ANY), pl.BlockSpec(memory_space=pl.ANY)],
        out_specs=per_head,
        scratch_shapes=[pltpu.VMEM((2, K_CHUNK_PAGES, ATT_DIM, PAGE_SIZE), f32),
                        pltpu.VMEM((ATT_HEADS * MOBA_TOP_K, PAGES_PER_BLOCK, ATT_HEAD_DIM, PAGE_SIZE), f32),
                        pltpu.VMEM((n_blocks, 8, MOBA_BLOCK), f32),
                        pltpu.VMEM((n_blocks, 8, MOBA_BLOCK), f32),
                        pltpu.SemaphoreType.DMA((2,)),
                        pltpu.SemaphoreType.DMA((1,))])
    out = pl.pallas_call(
        functools.partial(_moba_decode_kernel, layer=layer, n_pages=n_pages),
        grid_spec=grid_spec,
        out_shape=jax.ShapeDtypeStruct((b, 8, ATT_HEAD_DIM), f32),
        compiler_params=_cparams(1), name="moba_decode",
    )(page_table, q.reshape(b, 1, ATT_DIM), heads8(q), heads8(k_new), heads8(v_new), cache_kt, cache_vt)
    return out[:, :ATT_HEADS].reshape(b, ATT_DIM)


def _layer_params(l, w_in, conv_w, conv_b, dt_bias, a_log, d_skip, ssm_norm_g, pool_w, pool_scale, w_out):
    o_xbc = SSM_INNER
    o_dt = o_xbc + CONV_DIM
    o_pool = o_dt + SSM_HEADS
    o_q = o_pool + POOL_DIM
    o_k = o_q + ATT_DIM
    o_v = o_k + ATT_DIM
    wl = w_in[l]
    pad_heads = lambda v: jnp.pad(v.astype(f32), (0, LANES - SSM_HEADS)).reshape(1, LANES)
    w_dt = jnp.pad(wl[:, o_dt:o_pool], ((0, 0), (0, LANES - SSM_HEADS)))
    w_split = [wl[:, :o_xbc], wl[:, o_xbc:o_dt], w_dt, wl[:, o_pool:o_q], wl[:, o_q:o_k],
               wl[:, o_k:o_v], wl[:, o_v:]]
    pw = jnp.zeros((POOL_DIM, POOL_DIM), f32)
    for g in range(len(POOL_WINDOWS)):
        pw = pw.at[g * POOL_GROUP:(g + 1) * POOL_GROUP, g * POOL_GROUP:(g + 1) * POOL_GROUP].set(pool_w[l, g])
    expand = (jnp.arange(LANES)[:, None] == (jnp.arange(SSM_INNER)[None, :] // SSM_HEAD_DIM)).astype(f32)
    wo = w_out[l].astype(bf16)
    return {
        'w_in': [w.astype(bf16) for w in w_split],
        'conv_w': conv_w[l], 'conv_b': conv_b[l].reshape(1, CONV_DIM),
        'dt_bias': pad_heads(dt_bias[l]), 'a': pad_heads(-jnp.exp(a_log[l].astype(f32))),
        'dsk': jnp.repeat(d_skip[l].astype(f32), SSM_HEAD_DIM).reshape(1, SSM_INNER),
        'ssm_norm_g': ssm_norm_g[l].reshape(1, SSM_INNER),
        'pool_w': pw.astype(bf16), 'pool_scale': pool_scale[l].reshape(1, POOL_DIM),
        'expand': expand,
        'w_out': [wo[:SSM_INNER], wo[SSM_INNER:SSM_INNER + POOL_DIM], wo[SSM_INNER + POOL_DIM:]],
    }


IN_PROJ_OUTS = [(i, f32) for i in range(7)]


def _pages_channel_major(cache):
    d, n_phys = cache.shape[:2]
    return jnp.transpose(cache, (0, 1, 3, 4, 2)).reshape(d, n_phys, ATT_DIM, PAGE_SIZE)


def kernel(x_prompt, x_sample, cache_moba_k, cache_moba_v, state_ssm, state_conv, state_pool, cache_mem_k, cache_mem_v, page_table, mem_prompt, norm_mix_g, w_in, conv_w, conv_b, dt_bias, a_log, d_skip, ssm_norm_g, pool_w, pool_scale, w_out, norm_cross_g, norm_mem_g, w_mem_q, w_mem_kv, w_mem_o, norm_ffn_g, w_gate_up, w_down, final_norm_g):
    depth = w_in.shape[0]
    bp, t, d = x_prompt.shape
    bs = x_sample.shape[0]
    assert bp == 1 and x_sample.shape[1] == 1
    past_len = page_table.shape[1] * PAGE_SIZE
    cache_kt = _pages_channel_major(cache_moba_k)
    cache_vt = _pages_channel_major(cache_moba_v)
    mem_len = mem_prompt.shape[1]
    xp = x_prompt.reshape(t, d)
    xs = x_sample.reshape(bs, d)
    mem = mem_prompt.reshape(mem_len, d)
    tm_p, tm_s = 512, bs
    outs = {n: [] for n in ('kp', 'vp', 'ks', 'vs', 'sp', 'ss', 'cp', 'cs', 'pp', 'ps', 'mk', 'mv')}
    for l in range(depth):
        prm = _layer_params(l, w_in, conv_w, conv_b, dt_bias, a_log, d_skip, ssm_norm_g, pool_w, pool_scale, w_out)
        wq = w_mem_q[l].astype(bf16)
        wkv = w_mem_kv[l].astype(bf16)
        wo_mem = w_mem_o[l].astype(bf16)
        wgu = w_gate_up[l].astype(bf16)
        wdn = w_down[l].astype(bf16)
        half = MEM_HEADS * MEM_HEAD_DIM

        mk, mv, mkb, mvb = norm_matmul(mem, norm_mem_g[l], [wkv[:, :half], wkv[:, half:]],
                                       [(0, f32), (1, f32), (0, bf16), (1, bf16)], tm=mem_len)
        z, xbc, dtr, u, q, k, v = norm_matmul(xp, norm_mix_g[l], prm['w_in'], IN_PROJ_OUTS, tm=tm_p)
        y, po, s_new = ssd_pool_prompt(z, xbc, dtr, u, prm, tm=tm_p)
        att = moba_prompt(q, k, v)
        xp = matmul_residual(xp, [y, po, att], prm['w_out'], tm=tm_p)
        xp = cross_prompt(xp, norm_cross_g[l], wq, mkb, mvb, wo_mem, tm=tm_p)
        xp = swiglu_block(xp, norm_ffn_g[l], wgu, wdn, tm=1024, tf=256)
        outs['kp'].append(k.reshape(1, t, ATT_HEADS, ATT_HEAD_DIM))
        outs['vp'].append(v.reshape(1, t, ATT_HEADS, ATT_HEAD_DIM))
        outs['sp'].append(s_new[None])
        outs['cp'].append(xbc[t - (CONV_W - 1):][None])
        outs['pp'].append(u[t - POOL_HIST:][None])
        outs['mk'].append(mk.reshape(1, mem_len, MEM_HEADS, MEM_HEAD_DIM))
        outs['mv'].append(mv.reshape(1, mem_len, MEM_HEADS, MEM_HEAD_DIM))

        z, xbc, dtr, u, q, k, v = norm_matmul(xs, norm_mix_g[l], prm['w_in'], IN_PROJ_OUTS, tm=tm_s)
        y, po, c_new, s_new, p_new = ssd_pool_sample(z, xbc, dtr, u, state_conv[l], state_ssm[l],
                                                     state_pool[l], prm, pos0=past_len)
        att = moba_decode(page_table, q, k, v, cache_kt, cache_vt, layer=l)
        xs = matmul_residual(xs, [y, po, att], prm['w_out'], tm=tm_s)
        (qc,) = norm_matmul(xs, norm_cross_g[l], [wq], [(0, f32)], tm=tm_s)
        oc = cross_sample(qc, cache_mem_k[l], cache_mem_v[l])
        xs = matmul_residual(xs, [oc], [wo_mem], tm=tm_s)
        xs = swiglu_block(xs, norm_ffn_g[l], wgu, wdn, tm=tm_s, tf=256)
        outs['ks'].append(k.reshape(bs, 1, ATT_HEADS, ATT_HEAD_DIM))
        outs['vs'].append(v.reshape(bs, 1, ATT_HEADS, ATT_HEAD_DIM))
        outs['ss'].append(s_new)
        outs['cs'].append(c_new)
        outs['ps'].append(p_new)

    y_prompt = final_norm(xp, final_norm_g, tm=tm_p).reshape(1, t, d)
    y_sample = final_norm(xs, final_norm_g, tm=tm_s).reshape(bs, 1, d)
    st = lambda n: jnp.stack(outs[n])
    return (y_prompt, y_sample, st('kp'), st('vp'), st('ks'), st('vs'), st('sp'), st('ss'),
            st('cp'), st('cs'), st('pp'), st('ps'), st('mk'), st('mv'))
```

```python
import functools

import jax
import jax.numpy as jnp
from jax import lax
from jax.experimental import pallas as pl
from jax.experimental.pallas import tpu as pltpu

f32 = jnp.float32
bf16 = jnp.bfloat16
HIGHEST = lax.Precision.HIGHEST

D_MODEL = 1024
SSM_INNER = 512
SSM_HEAD_DIM = 64
SSM_HEADS = 8
SSM_GROUPS = 2
SSM_STATE = 128
CONV_W = 4
CONV_DIM = SSM_INNER + 2 * SSM_GROUPS * SSM_STATE
SSD_CHUNK = 128
POOL_DIM = 256
POOL_WINDOWS = (2, 4, 8, 16)
POOL_GROUP = 64
POOL_HIST = 15
ATT_DIM = 256
ATT_HEAD_DIM = 64
ATT_HEADS = 4
MOBA_BLOCK = 256
MOBA_TOP_K = 3
PAGE_SIZE = 128
MEM_HEADS = 4
MEM_HEAD_DIM = 256
D_FF = 2816
RMS_EPS = 1e-6
LANES = 128
MASKED = -1e30
LOG2E = 1.4426950408889634
BF16_SUBLANES = 16
V_AUG_ROWS = ATT_HEAD_DIM + BF16_SUBLANES
VMEM_LIMIT = 56 * 1024 * 1024


def _cparams(n_axes):
    return pltpu.CompilerParams(dimension_semantics=("arbitrary",) * n_axes,
                                vmem_limit_bytes=VMEM_LIMIT)


def _rms(x, g):
    ms = jnp.mean(x * x, axis=-1, keepdims=True)
    return x * lax.rsqrt(ms + RMS_EPS) * g


def _dot(a, b, **kw):
    return jnp.dot(a, b, preferred_element_type=f32, **kw)


def _dot_nt(a, b, **kw):
    return lax.dot_general(a, b, (((1,), (1,)), ((), ())), preferred_element_type=f32, **kw)


def _silu(x):
    return x * jax.nn.sigmoid(x)


def _softplus(x):
    return jnp.maximum(x, 0.0) + jnp.log1p(jnp.exp(-jnp.abs(x)))


def _full(shape):
    return pl.BlockSpec(shape, lambda *_: (0,) * len(shape))


def _norm_mm_kernel(*refs, n_w, out_w):
    x_ref, g_ref = refs[0], refs[1]
    w_refs = refs[2:2 + n_w]
    o_refs = refs[2 + n_w:]
    hb = _rms(x_ref[...], g_ref[...]).astype(bf16)
    res = [_dot(hb, w[...]) for w in w_refs]
    for o_ref, wi in zip(o_refs, out_w):
        o_ref[...] = res[wi].astype(o_ref.dtype)


def norm_matmul(x, g, ws, outs, tm):
    m, d = x.shape
    in_specs = [pl.BlockSpec((tm, d), lambda i: (i, 0)), _full((1, d))]
    in_specs += [_full(w.shape) for w in ws]
    out_shape = [jax.ShapeDtypeStruct((m, ws[wi].shape[1]), dt) for wi, dt in outs]
    out_specs = [pl.BlockSpec((tm, ws[wi].shape[1]), lambda i: (i, 0)) for wi, _ in outs]
    return pl.pallas_call(
        functools.partial(_norm_mm_kernel, n_w=len(ws), out_w=tuple(wi for wi, _ in outs)),
        grid=(m // tm,), in_specs=in_specs, out_specs=out_specs, out_shape=out_shape,
        compiler_params=_cparams(1), name="norm_matmul",
    )(x, g.reshape(1, d), *ws)


def _mm_res_kernel(*refs, n_a):
    res_ref = refs[0]
    a_refs = refs[1:1 + n_a]
    w_refs = refs[1 + n_a:1 + 2 * n_a]
    o_ref = refs[-1]
    acc = res_ref[...]
    for a, w in zip(a_refs, w_refs):
        acc = acc + _dot(a[...].astype(bf16), w[...])
    o_ref[...] = acc


def matmul_residual(res, a_list, w_list, tm):
    m, d = res.shape
    in_specs = [pl.BlockSpec((tm, d), lambda i: (i, 0))]
    in_specs += [pl.BlockSpec((tm, a.shape[1]), lambda i: (i, 0)) for a in a_list]
    in_specs += [_full(w.shape) for w in w_list]
    return pl.pallas_call(
        functools.partial(_mm_res_kernel, n_a=len(a_list)),
        grid=(m // tm,), in_specs=in_specs,
        out_specs=pl.BlockSpec((tm, d), lambda i: (i, 0)),
        out_shape=jax.ShapeDtypeStruct((m, d), f32),
        compiler_params=_cparams(1), name="matmul_residual",
    )(res, *a_list, *w_list)


def _cross_prompt_kernel(x_ref, g_ref, wq_ref, mk_ref, mv_ref, wo_ref, o_ref):
    x = x_ref[...]
    hb = _rms(x, g_ref[...]).astype(bf16)
    q = _dot(hb, wq_ref[...])
    acc = x
    for h in range(MEM_HEADS):
        sl = slice(h * MEM_HEAD_DIM, (h + 1) * MEM_HEAD_DIM)
        s = _dot_nt(q[:, sl].astype(bf16), mk_ref[:, sl]) * (MEM_HEAD_DIM ** -0.5)
        p = jnp.exp(s - jnp.max(s, axis=-1, keepdims=True))
        p = p / jnp.sum(p, axis=-1, keepdims=True)
        oh = _dot(p.astype(bf16), mv_ref[:, sl])
        acc = acc + _dot(oh.astype(bf16), wo_ref[sl, :])
    o_ref[...] = acc


def cross_prompt(x, g, wq, mkb, mvb, wo, tm):
    m, d = x.shape
    return pl.pallas_call(
        _cross_prompt_kernel, grid=(m // tm,),
        in_specs=[pl.BlockSpec((tm, d), lambda i: (i, 0)), _full((1, d)), _full(wq.shape),
                  _full(mkb.shape), _full(mvb.shape), _full(wo.shape)],
        out_specs=pl.BlockSpec((tm, d), lambda i: (i, 0)),
        out_shape=jax.ShapeDtypeStruct((m, d), f32),
        compiler_params=_cparams(1), name="cross_prompt",
    )(x, g.reshape(1, d), wq, mkb, mvb, wo)


MEM_SPLIT = MEM_HEAD_DIM // LANES
MEM_ROWS = MEM_SPLIT * MEM_HEADS


def _split_channel_order():
    return jnp.arange(MEM_HEADS * MEM_HEAD_DIM).reshape(MEM_HEADS, MEM_SPLIT, LANES).transpose(1, 0, 2).reshape(-1)


def _mem_split_view(cache):
    d, b, m = cache.shape[:3]
    x = cache.reshape(d, b, m, MEM_HEADS, MEM_SPLIT, LANES)
    return jnp.transpose(x, (0, 1, 2, 4, 3, 5)).reshape(d, b, m, MEM_ROWS, LANES)


def _cross_sample_kernel(q_ref, mk_ref, mv_ref, o_ref):
    part = jnp.sum(mk_ref[...] * q_ref[...][None], axis=-1, keepdims=True)
    s = part
    for piece in range(1, MEM_SPLIT):
        s = s + jnp.roll(part, piece * MEM_HEADS, axis=1)
    s = s * (MEM_HEAD_DIM ** -0.5)
    p = jnp.exp(s - jnp.max(s, axis=0, keepdims=True))
    l = jnp.sum(p, axis=0)
    o_ref[...] = jnp.sum(p * mv_ref[...], axis=0) / l


def cross_sample(q_split, mem_k_split, mem_v_split, layer):
    b = q_split.shape[0]
    mlen = mem_k_split.shape[2]
    mem_spec = pl.BlockSpec((None, None, mlen, MEM_ROWS, LANES), lambda i: (layer, i, 0, 0, 0))
    row_spec = pl.BlockSpec((None, MEM_ROWS, LANES), lambda i: (i, 0, 0))
    out = pl.pallas_call(
        _cross_sample_kernel, grid=(b,),
        in_specs=[row_spec, mem_spec, mem_spec], out_specs=row_spec,
        out_shape=jax.ShapeDtypeStruct((b, MEM_ROWS, LANES), f32),
        compiler_params=_cparams(1), name="cross_sample",
    )(q_split.reshape(b, MEM_ROWS, LANES), mem_k_split, mem_v_split)
    return out.reshape(b, MEM_ROWS * LANES)


def _swiglu_kernel(x_ref, g_ref, wg_ref, wu_ref, wd_ref, o_ref, h_ref, acc_ref):
    f = pl.program_id(1)

    @pl.when(f == 0)
    def _():
        h_ref[...] = _rms(x_ref[...], g_ref[...]).astype(bf16)
        acc_ref[...] = x_ref[...]

    hb = h_ref[...]
    a = _silu(_dot(hb, wg_ref[...])) * _dot(hb, wu_ref[...])
    acc_ref[...] += _dot(a.astype(bf16), wd_ref[...])

    @pl.when(f == pl.num_programs(1) - 1)
    def _():
        o_ref[...] = acc_ref[...]


def swiglu_block(x, g, w_gu, w_d, tm, tf):
    m, d = x.shape
    nf = D_FF // tf
    return pl.pallas_call(
        _swiglu_kernel, grid=(m // tm, nf),
        in_specs=[pl.BlockSpec((tm, d), lambda i, f: (i, 0)), _full((1, d)),
                  pl.BlockSpec((d, tf), lambda i, f: (0, f)),
                  pl.BlockSpec((d, tf), lambda i, f: (0, f + nf)),
                  pl.BlockSpec((tf, d), lambda i, f: (f, 0))],
        out_specs=pl.BlockSpec((tm, d), lambda i, f: (i, 0)),
        out_shape=jax.ShapeDtypeStruct((m, d), f32),
        scratch_shapes=[pltpu.VMEM((tm, d), bf16), pltpu.VMEM((tm, d), f32)],
        compiler_params=_cparams(2), name="swiglu_block",
    )(x, g.reshape(1, d), w_gu, w_gu, w_d)


def _final_norm_kernel(x_ref, g_ref, o_ref):
    o_ref[...] = _rms(x_ref[...], g_ref[...])


def final_norm(x, g, tm):
    m, d = x.shape
    return pl.pallas_call(
        _final_norm_kernel, grid=(m // tm,),
        in_specs=[pl.BlockSpec((tm, d), lambda i: (i, 0)), _full((1, d))],
        out_specs=pl.BlockSpec((tm, d), lambda i: (i, 0)),
        out_shape=jax.ShapeDtypeStruct((m, d), f32),
        compiler_params=_cparams(1), name="final_norm",
    )(x, g.reshape(1, d))


def _pool_select(w2, w4, w8, w16, cnt_of, lane):
    out = w16 / cnt_of(16)
    for win, acc in ((8, w8), (4, w4), (2, w2)):
        g = POOL_WINDOWS.index(win)
        out = jnp.where(lane < (g + 1) * POOL_GROUP, acc / cnt_of(win), out)
    return out


def _group_rmsnorm(y, g):
    half = SSM_INNER // SSM_GROUPS
    parts = [_rms(y[:, i * half:(i + 1) * half], g[:, i * half:(i + 1) * half])
             for i in range(SSM_GROUPS)]
    return jnp.concatenate(parts, axis=-1)


def _ssd_pool_prompt_kernel(z_ref, xbc_ref, dt_ref, u_ref, cw_ref, cb_ref, dtb_ref, a_ref, dsk_ref,
                            ng_ref, pw_ref, ps_ref,
                            y_ref, po_ref, st_ref,
                            ext_ref, pext_ref, xc_ref, s_ref, *, tm):
    i = pl.program_id(0)
    halo = 8
    phalo = 16

    @pl.when(i == 0)
    def _():
        ext_ref[0:halo, :] = jnp.zeros((halo, CONV_DIM), f32)
        pext_ref[0:phalo, :] = jnp.zeros((phalo, POOL_DIM), f32)
        s_ref[...] = jnp.zeros(s_ref.shape, f32)

    @pl.when(i > 0)
    def _():
        ext_ref[0:halo, :] = ext_ref[tm:tm + halo, :]
        pext_ref[0:phalo, :] = pext_ref[tm:tm + phalo, :]

    ext_ref[halo:halo + tm, :] = xbc_ref[...]
    pext_ref[phalo:phalo + tm, :] = u_ref[...]

    acc = jnp.broadcast_to(cb_ref[...], (tm, CONV_DIM))
    for j in range(CONV_W):
        off = halo - (CONV_W - 1) + j
        acc = acc + ext_ref[off:off + tm, :] * cw_ref[j:j + 1, :]
    xc_ref[...] = _silu(acc)

    u = u_ref[...]
    run = u
    sums = {}
    for back in range(1, max(POOL_WINDOWS)):
        run = run + pext_ref[phalo - back:phalo - back + tm, :]
        if back + 1 in POOL_WINDOWS:
            sums[back + 1] = run
    lane_p = lax.broadcasted_iota(jnp.int32, (tm, POOL_DIM), 1)
    pos1 = (lax.broadcasted_iota(jnp.int32, (tm, POOL_DIM), 0) + i * tm + 1).astype(f32)
    pooled = _pool_select(sums[2], sums[4], sums[8], sums[16],
                          lambda win: jnp.minimum(float(win), pos1), lane_p) - u
    po_ref[...] = (_dot(pooled.astype(bf16), pw_ref[...]) * ps_ref[...]).astype(po_ref.dtype)

    cl = SSD_CHUNK
    r_io = lax.broadcasted_iota(jnp.int32, (cl, cl), 0)
    c_io = lax.broadcasted_iota(jnp.int32, (cl, cl), 1)
    ltri = (c_io <= r_io).astype(f32)
    causal = c_io <= r_io
    first_half = c_io < SSM_HEAD_DIM
    n_pairs = SSM_HEADS // 2
    heads_per_group = SSM_HEADS // SSM_GROUPS

    def chunk(c, carry):
        r0 = pl.multiple_of(c * cl, cl)
        xc = xc_ref[pl.ds(r0, cl), :]
        dt = _softplus(dt_ref[pl.ds(r0, cl), :] + dtb_ref[...])
        a_cum = _dot(ltri, dt * a_ref[...], precision=HIGHEST)
        a_cum_t = a_cum.T
        dt_t = dt.T
        to_end_t = jnp.exp(a_cum_t[:, cl - 1:cl] - a_cum_t) * dt_t
        b_t = [xc[:, SSM_INNER + g * SSM_STATE:SSM_INNER + (g + 1) * SSM_STATE].T
               for g in range(SSM_GROUPS)]
        cm = [xc[:, SSM_INNER + SSM_GROUPS * SSM_STATE + g * SSM_STATE:
                 SSM_INNER + SSM_GROUPS * SSM_STATE + (g + 1) * SSM_STATE].astype(bf16)
              for g in range(SSM_GROUPS)]
        scores = [_dot(cm[g], b_t[g].astype(bf16)) for g in range(SSM_GROUPS)]
        y_pairs = []
        for k in range(n_pairs):
            g = (2 * k) // heads_per_group
            xs_pair = xc[:, k * LANES:(k + 1) * LANES]
            xs_b = xs_pair.astype(bf16)
            yd, cs, eb = [], [], []
            for h in (2 * k, 2 * k + 1):
                colb = jnp.broadcast_to(a_cum[:, h:h + 1], (cl, cl))
                seg = colb - a_cum_t[h:h + 1, :]
                decay = jnp.exp(jnp.where(causal, seg, -jnp.inf))
                mh = scores[g] * decay * dt_t[h:h + 1, :]
                yd.append(_dot(mh.astype(bf16), xs_b))
                cs.append(_dot((b_t[g] * to_end_t[h:h + 1, :]).astype(bf16), xs_b))
                eb.append(jnp.exp(colb))
            e_pair = jnp.where(first_half, eb[0], eb[1])
            s_old = s_ref[k]
            y_off = _dot(cm[g], s_old.astype(bf16)) * e_pair
            y_pairs.append(jnp.where(first_half, yd[0], yd[1]) + y_off
                           + dsk_ref[:, k * LANES:(k + 1) * LANES] * xs_pair)
            s_ref[k] = s_old * e_pair[cl - 1:cl, :] + jnp.where(first_half, cs[0], cs[1])
        y = jnp.concatenate(y_pairs, axis=-1) * _silu(z_ref[pl.ds(r0, cl), :])
        y_ref[pl.ds(r0, cl), :] = _group_rmsnorm(y, ng_ref[...]).astype(y_ref.dtype)
        return carry

    lax.fori_loop(0, tm // cl, chunk, 0)

    @pl.when(i == pl.num_programs(0) - 1)
    def _():
        for k in range(n_pairs):
            st = s_ref[k].T
            st_ref[2 * k] = st[0:SSM_HEAD_DIM, :]
            st_ref[2 * k + 1] = st[SSM_HEAD_DIM:2 * SSM_HEAD_DIM, :]


def ssd_pool_prompt(z, xbc, dt, u, prm, tm):
    t = z.shape[0]
    row = lambda c: pl.BlockSpec((tm, c), lambda i: (i, 0))
    consts = [prm['conv_w'], prm['conv_b'], prm['dt_bias'], prm['a'], prm['dsk'], prm['ssm_norm_g'],
              prm['pool_w'], prm['pool_scale']]
    return pl.pallas_call(
        functools.partial(_ssd_pool_prompt_kernel, tm=tm), grid=(t // tm,),
        in_specs=[row(SSM_INNER), row(CONV_DIM), row(LANES), row(POOL_DIM)] + [_full(c.shape) for c in consts],
        out_specs=[row(SSM_INNER), row(POOL_DIM), _full((SSM_HEADS, SSM_HEAD_DIM, SSM_STATE))],
        out_shape=[jax.ShapeDtypeStruct((t, SSM_INNER), bf16), jax.ShapeDtypeStruct((t, POOL_DIM), bf16),
                   jax.ShapeDtypeStruct((SSM_HEADS, SSM_HEAD_DIM, SSM_STATE), f32)],
        scratch_shapes=[pltpu.VMEM((tm + 8, CONV_DIM), f32), pltpu.VMEM((tm + 16, POOL_DIM), f32),
                        pltpu.VMEM((tm, CONV_DIM), f32),
                        pltpu.VMEM((SSM_HEADS // 2, SSM_STATE, 2 * SSM_HEAD_DIM), f32)],
        compiler_params=_cparams(1), name="ssd_pool_prompt",
    )(z, xbc, dt, u, *consts)


def _col_tile(row):
    return jnp.broadcast_to(row, (LANES, LANES)).T


def _ssd_pool_sample_kernel(z_ref, xbc_ref, dt_ref, u_ref, sc_ref, ss_ref, sp_ref,
                            cw_ref, cb_ref, dtb_ref, a_ref, dsk_ref, ng_ref, pw_ref, ps_ref, ex_ref,
                            y_ref, po_ref, cn_ref, sn_ref, pn_ref, *, pos0):
    xrow = xbc_ref[...]
    acc = cb_ref[...] + xrow * cw_ref[CONV_W - 1:CONV_W, :]
    for j in range(CONV_W - 1):
        acc = acc + sc_ref[j:j + 1, :] * cw_ref[j:j + 1, :]
        if j > 0:
            cn_ref[j - 1:j, :] = sc_ref[j:j + 1, :]
    cn_ref[CONV_W - 2:CONV_W - 1, :] = xrow
    xc = _silu(acc)

    dt = _softplus(dt_ref[...] + dtb_ref[...])
    dta = dt * a_ref[...]
    both = jnp.concatenate([jnp.broadcast_to(dt, (8, LANES)), jnp.broadcast_to(dta, (8, LANES))], axis=0)
    both_x = _dot(both, ex_ref[...], precision=HIGHEST)
    dtx = both_x[0:1, :]
    dec_x = jnp.exp(both_x[8:9, :])
    xs = xc[:, 0:SSM_INNER]
    xdt = xs * dtx
    heads_per_group = SSM_HEADS // SSM_GROUPS
    y_pairs = []
    for k in range(SSM_HEADS // 2):
        g = (2 * k) // heads_per_group
        b_row = xc[:, SSM_INNER + g * SSM_STATE:SSM_INNER + (g + 1) * SSM_STATE]
        c_row = xc[:, SSM_INNER + SSM_GROUPS * SSM_STATE + g * SSM_STATE:
                   SSM_INNER + SSM_GROUPS * SSM_STATE + (g + 1) * SSM_STATE]
        sl = slice(k * LANES, (k + 1) * LANES)
        s_old = jnp.concatenate([ss_ref[2 * k], ss_ref[2 * k + 1]], axis=0)
        s_new = s_old * _col_tile(dec_x[:, sl]) + _col_tile(xdt[:, sl]) * b_row
        sn_ref[2 * k] = s_new[0:SSM_HEAD_DIM, :]
        sn_ref[2 * k + 1] = s_new[SSM_HEAD_DIM:, :]
        y_k = _dot_nt(jnp.broadcast_to(c_row, (8, SSM_STATE)), s_new, precision=HIGHEST)[0:1, :]
        y_pairs.append(y_k + dsk_ref[:, sl] * xs[:, sl])
    y = jnp.concatenate(y_pairs, axis=-1) * _silu(z_ref[...])
    y_ref[...] = _group_rmsnorm(y, ng_ref[...]).astype(y_ref.dtype)

    u = u_ref[...]
    prev = sp_ref[...]
    rowi = lax.broadcasted_iota(jnp.int32, prev.shape, 0)
    tail = lambda win: u + jnp.sum(jnp.where(rowi >= POOL_HIST - (win - 1), prev, 0.0), axis=0, keepdims=True)
    lane_p = lax.broadcasted_iota(jnp.int32, (1, POOL_DIM), 1)
    pooled = _pool_select(tail(2), tail(4), tail(8), tail(16),
                          lambda win: float(min(win, pos0 + 1)), lane_p) - u
    po = _dot(jnp.broadcast_to(pooled, (8, POOL_DIM)).astype(bf16), pw_ref[...])[0:1, :] * ps_ref[...]
    po_ref[...] = po.astype(po_ref.dtype)
    pn_ref[0:POOL_HIST - 1, :] = sp_ref[1:POOL_HIST, :]
    pn_ref[POOL_HIST - 1:POOL_HIST, :] = u


def ssd_pool_sample(z, xbc, dt, u, st_conv, st_ssm, st_pool, prm, pos0):
    b = z.shape[0]
    per_seq = lambda *shape: pl.BlockSpec((None,) + shape, lambda i: (i,) + (0,) * len(shape))
    consts = [prm['conv_w'], prm['conv_b'], prm['dt_bias'], prm['a'], prm['dsk'], prm['ssm_norm_g'],
              prm['pool_w'], prm['pool_scale'], prm['expand']]
    outs = pl.pallas_call(
        functools.partial(_ssd_pool_sample_kernel, pos0=pos0), grid=(b,),
        in_specs=[per_seq(1, SSM_INNER), per_seq(1, CONV_DIM), per_seq(1, LANES), per_seq(1, POOL_DIM),
                  per_seq(CONV_W - 1, CONV_DIM), per_seq(SSM_HEADS, SSM_HEAD_DIM, SSM_STATE),
                  per_seq(POOL_HIST, POOL_DIM)] + [_full(c.shape) for c in consts],
        out_specs=[per_seq(1, SSM_INNER), per_seq(1, POOL_DIM), per_seq(CONV_W - 1, CONV_DIM),
                   per_seq(SSM_HEADS, SSM_HEAD_DIM, SSM_STATE), per_seq(POOL_HIST, POOL_DIM)],
        out_shape=[jax.ShapeDtypeStruct((b, 1, SSM_INNER), f32), jax.ShapeDtypeStruct((b, 1, POOL_DIM), f32),
                   jax.ShapeDtypeStruct((b, CONV_W - 1, CONV_DIM), f32),
                   jax.ShapeDtypeStruct((b, SSM_HEADS, SSM_HEAD_DIM, SSM_STATE), f32),
                   jax.ShapeDtypeStruct((b, POOL_HIST, POOL_DIM), f32)],
        compiler_params=_cparams(1), name="ssd_pool_sample",
    )(z.reshape(b, 1, -1), xbc.reshape(b, 1, -1), dt.reshape(b, 1, -1), u.reshape(b, 1, -1),
      st_conv, st_ssm, st_pool, *consts)
    y, po, cn, sn, pn = outs
    return y.reshape(b, -1), po.reshape(b, -1), cn, sn, pn


def _topk_blocks(gate, n_past, axis):
    blk = lax.broadcasted_iota(jnp.int32, gate.shape, axis).astype(f32)
    g = jnp.where(blk < jnp.asarray(n_past, f32), gate, -jnp.inf)
    picks = []
    for _ in range(MOBA_TOP_K):
        m = jnp.max(g, axis=axis, keepdims=True)
        idx = jnp.min(jnp.where(g == m, blk, float(LANES)), axis=axis, keepdims=True)
        picks.append((idx, jnp.abs(m) < jnp.inf))
        g = jnp.where(blk == idx, -jnp.inf, g)
    return picks


def _moba_prompt_kernel(q_ref, k_ref, v_ref, o_ref,
                        km_ref, kb_ref, vt_ref, qst_ref, nmt_ref, acc_ref, sa_ref, sb_ref):
    i = pl.program_id(0)
    tq = MOBA_BLOCK
    scale = ATT_HEAD_DIM ** -0.5

    @pl.when(i == 0)
    def _():
        km_ref[...] = jnp.zeros(km_ref.shape, f32)

    k = k_ref[...]
    kb = k.astype(bf16)
    kb_ref[i] = kb
    vt = v_ref[...].T.astype(bf16)
    ones_rows = jnp.ones((V_AUG_ROWS - ATT_HEAD_DIM, tq), bf16)
    for h in range(ATT_HEADS):
        vt_ref[i, h] = jnp.concatenate([vt[h * ATT_HEAD_DIM:(h + 1) * ATT_HEAD_DIM, :], ones_rows], axis=0)
    qt = q_ref[...].T
    head_of_row = lax.broadcasted_iota(jnp.int32, (ATT_DIM, tq), 0) // ATT_HEAD_DIM
    blk_row = lax.broadcasted_iota(jnp.int32, (LANES, tq), 0).astype(f32)
    key_io = lax.broadcasted_iota(jnp.int32, (tq, tq), 0)
    qry_io = lax.broadcasted_iota(jnp.int32, (tq, tq), 1)
    km = km_ref[...]
    m_own = []
    for h in range(ATT_HEADS):
        qth = jnp.where(head_of_row == h, qt, 0.0)
        gate = _dot(km, qth, precision=HIGHEST)
        sel = jnp.zeros((LANES, tq), jnp.bool_)
        for idx, ok in _topk_blocks(gate, i, axis=0):
            sel = sel | ((blk_row == idx) & ok)
        nmt_ref[h] = jnp.where(sel, 0.0, MASKED)
        qst = (qth * (scale * LOG2E)).astype(bf16)
        qst_ref[h] = qst
        st = jnp.where(key_io <= qry_io, _dot(kb, qst), MASKED)
        m = jnp.max(st, axis=0, keepdims=True)
        m_own.append(m)
        acc_ref[h] = _dot(vt_ref[i, h], jnp.exp2(st - m).astype(bf16))

    km_ref[pl.ds(i, 1), :] = jnp.mean(k, axis=0, keepdims=True)

    def scores(j, dst):
        kj = kb_ref[j]
        for h in range(ATT_HEADS):
            dst[h] = _dot(kj, qst_ref[h])

    def absorb(j, src, carry):
        new = []
        for h in range(ATT_HEADS):
            m_prev = carry[h]
            raw = src[h]
            bias = nmt_ref[h, pl.ds(j, 1), :]
            m_new = jnp.maximum(m_prev, jnp.max(raw, axis=0, keepdims=True) + bias)
            alpha = jnp.exp2(m_prev - m_new)
            p = jnp.exp2(raw - (m_new - bias))
            new.append(m_new)
            acc_ref[h] = acc_ref[h] * alpha + _dot(vt_ref[j, h], p.astype(bf16))
        return tuple(new)

    last = jnp.maximum(i - 1, 0)
    scores(0, sa_ref)

    def pair(t, carry):
        scores(jnp.minimum(2 * t + 1, last), sb_ref)
        carry = absorb(2 * t, sa_ref, carry)
        scores(jnp.minimum(2 * t + 2, last), sa_ref)
        return absorb(2 * t + 1, sb_ref, carry)

    carry = lax.fori_loop(0, i // 2, pair, tuple(m_own))
    carry = lax.cond(i % 2 == 1, lambda c: absorb(i - 1, sa_ref, c), lambda c: c, carry)
    out_t = jnp.concatenate(
        [acc_ref[h, 0:ATT_HEAD_DIM, :] / acc_ref[h, ATT_HEAD_DIM:ATT_HEAD_DIM + 1, :] for h in range(ATT_HEADS)],
        axis=0)
    o_ref[...] = out_t.T.astype(o_ref.dtype)


def moba_prompt(q, k, v):
    t = q.shape[0]
    tq = MOBA_BLOCK
    n_blk = t // tq
    tile = pl.BlockSpec((tq, ATT_DIM), lambda i: (i, 0))
    return pl.pallas_call(
        _moba_prompt_kernel, grid=(n_blk,),
        in_specs=[tile, tile, tile], out_specs=tile,
        out_shape=jax.ShapeDtypeStruct((t, ATT_DIM), bf16),
        scratch_shapes=[pltpu.VMEM((LANES, ATT_DIM), f32),
                        pltpu.VMEM((n_blk, tq, ATT_DIM), bf16),
                        pltpu.VMEM((n_blk, ATT_HEADS, V_AUG_ROWS, tq), bf16),
                        pltpu.VMEM((ATT_HEADS, ATT_DIM, tq), bf16),
                        pltpu.VMEM((ATT_HEADS, LANES, tq), f32),
                        pltpu.VMEM((ATT_HEADS, V_AUG_ROWS, tq), f32),
                        pltpu.VMEM((ATT_HEADS, tq, tq), f32),
                        pltpu.VMEM((ATT_HEADS, tq, tq), f32)],
        compiler_params=_cparams(1), name="moba_prompt",
    )(q, k, v)


K_CHUNK_PAGES = 16
PAGES_PER_BLOCK = MOBA_BLOCK // PAGE_SIZE
CHUNKS_BEFORE_FINISH = 2


def _moba_decode_kernel(pt_ref, q_ref, q8_ref, kn8_ref, vn8_ref, kc_ref, vc_ref, o_ref,
                        kbuf, vbuf, s_ref, p_ref, gate_ref, stash_ref, ids_ref, ksem, vsem, *, layer, n_pages):
    b = pl.program_id(0)
    n_seq = pl.num_programs(0) - 1
    n_chunks = n_pages // K_CHUNK_PAGES
    blocks_per_chunk = K_CHUNK_PAGES // PAGES_PER_BLOCK
    n_blocks = n_pages // PAGES_PER_BLOCK
    scale = ATT_HEAD_DIM ** -0.5

    def k_copy(seq, page_slot, slot, p):
        page = pt_ref[seq, page_slot]
        return pltpu.make_async_copy(kc_ref.at[layer, page], kbuf.at[slot, p], ksem.at[slot])

    def start_chunk(seq, c, slot):
        for p in range(K_CHUNK_PAGES):
            k_copy(seq, c * K_CHUNK_PAGES + p, slot, p).start()

    def wait_chunk(seq, c, slot):
        for p in range(K_CHUNK_PAGES):
            k_copy(seq, c * K_CHUNK_PAGES + p, slot, p).wait()

    n_picks = ATT_HEADS * MOBA_TOP_K

    def v_copy(h, r, half, page):
        return pltpu.make_async_copy(vc_ref.at[layer, page, pl.ds(h * ATT_HEAD_DIM, ATT_HEAD_DIM), :],
                                     vbuf.at[h * MOBA_TOP_K + r, half], vsem.at[0])

    def stream(chunks):
        lane8 = lax.broadcasted_iota(jnp.int32, (8, ATT_DIM), 1)
        row8 = lax.broadcasted_iota(jnp.int32, (8, ATT_DIM), 0)
        qblk_b = jnp.where(lane8 // ATT_HEAD_DIM == row8, q_ref[...], 0.0).astype(bf16)
        blk_lane = lax.broadcasted_iota(jnp.int32, (8, LANES), 1)
        gate = gate_ref[...]
        for c in chunks:
            slot = c % 2
            if c + 1 < n_chunks:
                start_chunk(b, c + 1, 1 - slot)
            else:
                @pl.when(b + 1 < n_seq)
                def _():
                    start_chunk(b + 1, 0, 1 - slot)
            wait_chunk(b, c, slot)
            for t in range(blocks_per_chunk):
                blk = c * blocks_per_chunk + t
                halves = [_dot(qblk_b, kbuf[slot, t * PAGES_PER_BLOCK + half].astype(bf16))
                          for half in range(PAGES_PER_BLOCK)]
                for half in range(PAGES_PER_BLOCK):
                    s_ref[blk, :, half * PAGE_SIZE:(half + 1) * PAGE_SIZE] = halves[half]
                tot = jnp.sum(sum(halves), axis=-1, keepdims=True) * (1.0 / MOBA_BLOCK)
                gate = jnp.where(blk_lane == blk, tot, gate)
        gate_ref[...] = gate

    def choose_and_weigh():
        picks = _topk_blocks(gate_ref[...], n_blocks, axis=1)
        for h in range(ATT_HEADS):
            for r in range(MOBA_TOP_K):
                blk = picks[r][0][h, 0].astype(jnp.int32)
                ids_ref[h * MOBA_TOP_K + r] = blk
                for half in range(PAGES_PER_BLOCK):
                    page = pt_ref[b, blk * PAGES_PER_BLOCK + half]
                    ids_ref[n_picks + (h * MOBA_TOP_K + r) * PAGES_PER_BLOCK + half] = page
                    v_copy(h, r, half, page).start()
        blk_io = lax.broadcasted_iota(jnp.int32, (n_blocks, 8, MOBA_BLOCK), 0).astype(f32)
        seen = jnp.zeros((n_blocks, 8, MOBA_BLOCK), jnp.bool_)
        for idx, ok in picks:
            seen = seen | ((blk_io == idx[None]) & ok[None])
        s_all = jnp.where(seen, s_ref[...] * scale, -jnp.inf)
        s_own = jnp.sum(q8_ref[...] * kn8_ref[...], axis=-1, keepdims=True) * scale
        m = jnp.maximum(jnp.max(jnp.max(s_all, axis=0), axis=-1, keepdims=True), s_own)
        p_all = jnp.exp(s_all - m[None])
        p_own = jnp.exp(s_own - m)
        l = jnp.sum(jnp.sum(p_all, axis=0), axis=-1, keepdims=True) + p_own
        p_ref[...] = p_all
        stash_ref[:, 0:ATT_HEAD_DIM] = p_own * vn8_ref[...]
        stash_ref[:, ATT_HEAD_DIM:] = jnp.broadcast_to(l, (8, LANES - ATT_HEAD_DIM))

    def finish_previous():
        for h in range(ATT_HEADS):
            for r in range(MOBA_TOP_K):
                for half in range(PAGES_PER_BLOCK):
                    page = ids_ref[n_picks + (h * MOBA_TOP_K + r) * PAGES_PER_BLOCK + half]
                    v_copy(h, r, half, page).wait()
        row_hd = lax.broadcasted_iota(jnp.int32, (8, ATT_HEAD_DIM), 0)
        o = stash_ref[:, 0:ATT_HEAD_DIM]
        for h in range(ATT_HEADS):
            oh = jnp.zeros((8, ATT_HEAD_DIM), f32)
            for r in range(MOBA_TOP_K):
                pb = p_ref[ids_ref[h * MOBA_TOP_K + r]].astype(bf16)
                for half in range(PAGES_PER_BLOCK):
                    oh = oh + _dot_nt(pb[:, half * PAGE_SIZE:(half + 1) * PAGE_SIZE],
                                      vbuf[h * MOBA_TOP_K + r, half].astype(bf16))
            o = o + jnp.where(row_hd == h, oh, 0.0)
        o_ref[...] = o / stash_ref[:, ATT_HEAD_DIM:ATT_HEAD_DIM + 1]

    @pl.when(b == 0)
    def _():
        start_chunk(0, 0, 0)

    @pl.when(b < n_seq)
    def _():
        gate_ref[...] = jnp.zeros((8, LANES), f32)
        stream(range(0, CHUNKS_BEFORE_FINISH))

    @pl.when(b > 0)
    def _():
        finish_previous()

    @pl.when(b < n_seq)
    def _():
        stream(range(CHUNKS_BEFORE_FINISH, n_chunks))
        choose_and_weigh()


def moba_decode(page_table, q, k_new, v_new, cache_kt, cache_vt, layer):
    b = q.shape[0]
    n_pages = page_table.shape[1]
    n_blocks = n_pages // PAGES_PER_BLOCK
    assert n_blocks <= LANES and n_blocks >= MOBA_TOP_K
    heads8 = lambda a: jnp.pad(a.reshape(b, ATT_HEADS, ATT_HEAD_DIM), ((0, 0), (0, 8 - ATT_HEADS), (0, 0)))
    n_picks = ATT_HEADS * MOBA_TOP_K
    cur = lambda i, pt: (jnp.minimum(i, b - 1), 0, 0)
    prev = lambda i, pt: (jnp.maximum(i - 1, 0), 0, 0)
    per_head = pl.BlockSpec((None, 8, ATT_HEAD_DIM), cur)
    grid_spec = pltpu.PrefetchScalarGridSpec(
        num_scalar_prefetch=1, grid=(b + 1,),
        in_specs=[pl.BlockSpec((None, 1, ATT_DIM), cur), per_head, per_head, per_head,
                  pl.BlockSpec(memory_space=pl.ANY), pl.BlockSpec(memory_space=pl.ANY)],
        out_specs=pl.BlockSpec((None, 8, ATT_HEAD_DIM), prev),
        scratch_shapes=[pltpu.VMEM((2, K_CHUNK_PAGES, ATT_DIM, PAGE_SIZE), f32),
                        pltpu.VMEM((n_picks, PAGES_PER_BLOCK, ATT_HEAD_DIM, PAGE_SIZE), f32),
                        pltpu.VMEM((n_blocks, 8, MOBA_BLOCK), f32),
                        pltpu.VMEM((n_blocks, 8, MOBA_BLOCK), f32),
                        pltpu.VMEM((8, LANES), f32),
                        pltpu.VMEM((8, LANES), f32),
                        pltpu.SMEM((n_picks * (1 + PAGES_PER_BLOCK),), jnp.int32),
                        pltpu.SemaphoreType.DMA((2,)),
                        pltpu.SemaphoreType.DMA((1,))])
    out = pl.pallas_call(
        functools.partial(_moba_decode_kernel, layer=layer, n_pages=n_pages),
        grid_spec=grid_spec,
        out_shape=jax.ShapeDtypeStruct((b, 8, ATT_HEAD_DIM), f32),
        compiler_params=_cparams(1), name="moba_decode",
    )(page_table, q.reshape(b, 1, ATT_DIM), heads8(q), heads8(k_new), heads8(v_new), cache_kt, cache_vt)
    return out[:, :ATT_HEADS].reshape(b, ATT_DIM)


def _layer_params(l, w_in, conv_w, conv_b, dt_bias, a_log, d_skip, ssm_norm_g, pool_w, pool_scale, w_out):
    o_xbc = SSM_INNER
    o_dt = o_xbc + CONV_DIM
    o_pool = o_dt + SSM_HEADS
    o_q = o_pool + POOL_DIM
    o_k = o_q + ATT_DIM
    o_v = o_k + ATT_DIM
    wl = w_in[l]
    pad_heads = lambda v: jnp.pad(v.astype(f32), (0, LANES - SSM_HEADS)).reshape(1, LANES)
    w_dt = jnp.pad(wl[:, o_dt:o_pool], ((0, 0), (0, LANES - SSM_HEADS)))
    w_split = [wl[:, :o_xbc], wl[:, o_xbc:o_dt], w_dt, wl[:, o_pool:o_q], wl[:, o_q:o_k],
               wl[:, o_k:o_v], wl[:, o_v:]]
    pw = jnp.zeros((POOL_DIM, POOL_DIM), f32)
    for g in range(len(POOL_WINDOWS)):
        pw = pw.at[g * POOL_GROUP:(g + 1) * POOL_GROUP, g * POOL_GROUP:(g + 1) * POOL_GROUP].set(pool_w[l, g])
    expand = (jnp.arange(LANES)[:, None] == (jnp.arange(SSM_INNER)[None, :] // SSM_HEAD_DIM)).astype(f32)
    wo = w_out[l].astype(bf16)
    return {
        'w_in': [w.astype(bf16) for w in w_split],
        'conv_w': conv_w[l], 'conv_b': conv_b[l].reshape(1, CONV_DIM),
        'dt_bias': pad_heads(dt_bias[l]), 'a': pad_heads(-jnp.exp(a_log[l].astype(f32))),
        'dsk': jnp.repeat(d_skip[l].astype(f32), SSM_HEAD_DIM).reshape(1, SSM_INNER),
        'ssm_norm_g': ssm_norm_g[l].reshape(1, SSM_INNER),
        'pool_w': pw.astype(bf16), 'pool_scale': pool_scale[l].reshape(1, POOL_DIM),
        'expand': expand,
        'w_out': [wo[:SSM_INNER], wo[SSM_INNER:SSM_INNER + POOL_DIM], wo[SSM_INNER + POOL_DIM:]],
    }


IN_PROJ_OUTS = [(i, f32) for i in range(7)]


def _pages_channel_major(cache):
    d, n_phys = cache.shape[:2]
    return jnp.transpose(cache, (0, 1, 3, 4, 2)).reshape(d, n_phys, ATT_DIM, PAGE_SIZE)


def kernel(x_prompt, x_sample, cache_moba_k, cache_moba_v, state_ssm, state_conv, state_pool, cache_mem_k, cache_mem_v, page_table, mem_prompt, norm_mix_g, w_in, conv_w, conv_b, dt_bias, a_log, d_skip, ssm_norm_g, pool_w, pool_scale, w_out, norm_cross_g, norm_mem_g, w_mem_q, w_mem_kv, w_mem_o, norm_ffn_g, w_gate_up, w_down, final_norm_g):
    depth = w_in.shape[0]
    bp, t, d = x_prompt.shape
    bs = x_sample.shape[0]
    assert bp == 1 and x_sample.shape[1] == 1
    past_len = page_table.shape[1] * PAGE_SIZE
    cache_kt = _pages_channel_major(cache_moba_k)
    cache_vt = _pages_channel_major(cache_moba_v)
    mem_k_split = _mem_split_view(cache_mem_k)
    mem_v_split = _mem_split_view(cache_mem_v)
    split_order = _split_channel_order()
    mem_len = mem_prompt.shape[1]
    xp = x_prompt.reshape(t, d)
    xs = x_sample.reshape(bs, d)
    mem = mem_prompt.reshape(mem_len, d)
    tm_p, tm_s = 512, bs
    outs = {n: [] for n in ('kp', 'vp', 'ks', 'vs', 'sp', 'ss', 'cp', 'cs', 'pp', 'ps', 'mk', 'mv')}
    for l in range(depth):
        prm = _layer_params(l, w_in, conv_w, conv_b, dt_bias, a_log, d_skip, ssm_norm_g, pool_w, pool_scale, w_out)
        wq = w_mem_q[l].astype(bf16)
        wkv = w_mem_kv[l].astype(bf16)
        wo_mem = w_mem_o[l].astype(bf16)
        wgu = w_gate_up[l].astype(bf16)
        wdn = w_down[l].astype(bf16)
        half = MEM_HEADS * MEM_HEAD_DIM

        mk, mv, mkb, mvb = norm_matmul(mem, norm_mem_g[l], [wkv[:, :half], wkv[:, half:]],
                                       [(0, f32), (1, f32), (0, bf16), (1, bf16)], tm=mem_len)
        z, xbc, dtr, u, q, k, v = norm_matmul(xp, norm_mix_g[l], prm['w_in'], IN_PROJ_OUTS, tm=tm_p)
        y, po, s_new = ssd_pool_prompt(z, xbc, dtr, u, prm, tm=tm_p)
        att = moba_prompt(q, k, v)
        xp = matmul_residual(xp, [y, po, att], prm['w_out'], tm=tm_p)
        xp = cross_prompt(xp, norm_cross_g[l], wq, mkb, mvb, wo_mem, tm=tm_p)
        xp = swiglu_block(xp, norm_ffn_g[l], wgu, wdn, tm=1024, tf=256)
        outs['kp'].append(k.reshape(1, t, ATT_HEADS, ATT_HEAD_DIM))
        outs['vp'].append(v.reshape(1, t, ATT_HEADS, ATT_HEAD_DIM))
        outs['sp'].append(s_new[None])
        outs['cp'].append(xbc[t - (CONV_W - 1):][None])
        outs['pp'].append(u[t - POOL_HIST:][None])
        outs['mk'].append(mk.reshape(1, mem_len, MEM_HEADS, MEM_HEAD_DIM))
        outs['mv'].append(mv.reshape(1, mem_len, MEM_HEADS, MEM_HEAD_DIM))

        z, xbc, dtr, u, q, k, v = norm_matmul(xs, norm_mix_g[l], prm['w_in'], IN_PROJ_OUTS, tm=tm_s)
        y, po, c_new, s_new, p_new = ssd_pool_sample(z, xbc, dtr, u, state_conv[l], state_ssm[l],
                                                     state_pool[l], prm, pos0=past_len)
        att = moba_decode(page_table, q, k, v, cache_kt, cache_vt, layer=l)
        xs = matmul_residual(xs, [y, po, att], prm['w_out'], tm=tm_s)
        (qc,) = norm_matmul(xs, norm_cross_g[l], [wq[:, split_order]], [(0, f32)], tm=tm_s)
        oc = cross_sample(qc, mem_k_split, mem_v_split, layer=l)
        xs = matmul_residual(xs, [oc], [wo_mem[split_order, :]], tm=tm_s)
        xs = swiglu_block(xs, norm_ffn_g[l], wgu, wdn, tm=tm_s, tf=256)
        outs['ks'].append(k.reshape(bs, 1, ATT_HEADS, ATT_HEAD_DIM))
        outs['vs'].append(v.reshape(bs, 1, ATT_HEADS, ATT_HEAD_DIM))
        outs['ss'].append(s_new)
        outs['cs'].append(c_new)
        outs['ps'].append(p_new)

    y_prompt = final_norm(xp, final_norm_g, tm=tm_p).reshape(1, t, d)
    y_sample = final_norm(xs, final_norm_g, tm=tm_s).reshape(bs, 1, d)
    st = lambda n: jnp.stack(outs[n])
    return (y_prompt, y_sample, st('kp'), st('vp'), st('ks'), st('vs'), st('sp'), st('ss'),
            st('cp'), st('cs'), st('pp'), st('ps'), st('mk'), st('mv'))
```

```python
import functools

import jax
import jax.numpy as jnp
from jax import lax
from jax.experimental import pallas as pl
from jax.experimental.pallas import tpu as pltpu

f32 = jnp.float32
bf16 = jnp.bfloat16
HIGHEST = lax.Precision.HIGHEST

D_MODEL = 1024
SSM_INNER = 512
SSM_HEAD_DIM = 64
SSM_HEADS = 8
SSM_GROUPS = 2
SSM_STATE = 128
CONV_W = 4
CONV_DIM = SSM_INNER + 2 * SSM_GROUPS * SSM_STATE
SSD_CHUNK = 128
POOL_DIM = 256
POOL_WINDOWS = (2, 4, 8, 16)
POOL_GROUP = 64
POOL_HIST = 15
ATT_DIM = 256
ATT_HEAD_DIM = 64
ATT_HEADS = 4
MOBA_BLOCK = 256
MOBA_TOP_K = 3
PAGE_SIZE = 128
MEM_HEADS = 4
MEM_HEAD_DIM = 256
D_FF = 2816
RMS_EPS = 1e-6
LANES = 128
MASKED = -1e30
LOG2E = 1.4426950408889634
BF16_SUBLANES = 16
V_AUG_ROWS = ATT_HEAD_DIM + BF16_SUBLANES
VMEM_LIMIT = 56 * 1024 * 1024


def _cparams(n_axes):
    return pltpu.CompilerParams(dimension_semantics=("arbitrary",) * n_axes,
                                vmem_limit_bytes=VMEM_LIMIT)


def _rms(x, g):
    ms = jnp.mean(x * x, axis=-1, keepdims=True)
    return x * lax.rsqrt(ms + RMS_EPS) * g


def _dot(a, b, **kw):
    return jnp.dot(a, b, preferred_element_type=f32, **kw)


def _dot_nt(a, b, **kw):
    return lax.dot_general(a, b, (((1,), (1,)), ((), ())), preferred_element_type=f32, **kw)


def _silu(x):
    return x * jax.nn.sigmoid(x)


def _softplus(x):
    return jnp.maximum(x, 0.0) + jnp.log1p(jnp.exp(-jnp.abs(x)))


def _full(shape):
    return pl.BlockSpec(shape, lambda *_: (0,) * len(shape))


def _norm_mm_kernel(*refs, n_w, out_w):
    x_ref, g_ref = refs[0], refs[1]
    w_refs = refs[2:2 + n_w]
    o_refs = refs[2 + n_w:]
    hb = _rms(x_ref[...], g_ref[...]).astype(bf16)
    res = [_dot(hb, w[...]) for w in w_refs]
    for o_ref, wi in zip(o_refs, out_w):
        o_ref[...] = res[wi].astype(o_ref.dtype)


def norm_matmul(x, g, ws, outs, tm):
    m, d = x.shape
    in_specs = [pl.BlockSpec((tm, d), lambda i: (i, 0)), _full((1, d))]
    in_specs += [_full(w.shape) for w in ws]
    out_shape = [jax.ShapeDtypeStruct((m, ws[wi].shape[1]), dt) for wi, dt in outs]
    out_specs = [pl.BlockSpec((tm, ws[wi].shape[1]), lambda i: (i, 0)) for wi, _ in outs]
    return pl.pallas_call(
        functools.partial(_norm_mm_kernel, n_w=len(ws), out_w=tuple(wi for wi, _ in outs)),
        grid=(m // tm,), in_specs=in_specs, out_specs=out_specs, out_shape=out_shape,
        compiler_params=_cparams(1), name="norm_matmul",
    )(x, g.reshape(1, d), *ws)


def _mm_res_kernel(*refs, n_a):
    res_ref = refs[0]
    a_refs = refs[1:1 + n_a]
    w_refs = refs[1 + n_a:1 + 2 * n_a]
    o_ref = refs[-1]
    acc = res_ref[...]
    for a, w in zip(a_refs, w_refs):
        acc = acc + _dot(a[...].astype(bf16), w[...])
    o_ref[...] = acc


def matmul_residual(res, a_list, w_list, tm):
    m, d = res.shape
    in_specs = [pl.BlockSpec((tm, d), lambda i: (i, 0))]
    in_specs += [pl.BlockSpec((tm, a.shape[1]), lambda i: (i, 0)) for a in a_list]
    in_specs += [_full(w.shape) for w in w_list]
    return pl.pallas_call(
        functools.partial(_mm_res_kernel, n_a=len(a_list)),
        grid=(m // tm,), in_specs=in_specs,
        out_specs=pl.BlockSpec((tm, d), lambda i: (i, 0)),
        out_shape=jax.ShapeDtypeStruct((m, d), f32),
        compiler_params=_cparams(1), name="matmul_residual",
    )(res, *a_list, *w_list)


def _cross_prompt_kernel(*refs, n_mix):
    x_ref = refs[0]
    a_refs = refs[1:1 + n_mix]
    w_refs = refs[1 + n_mix:1 + 2 * n_mix]
    g_ref, wq_ref, mk_ref, mv_ref, wo_ref, o_ref = refs[1 + 2 * n_mix:]
    x = x_ref[...]
    for a, w in zip(a_refs, w_refs):
        x = x + _dot(a[...].astype(bf16), w[...])
    if n_mix:
        o_ref[...] = x
        x = o_ref[...]
    hb = _rms(x, g_ref[...]).astype(bf16)
    q = _dot(hb, wq_ref[...])
    acc = x
    for h in range(MEM_HEADS):
        sl = slice(h * MEM_HEAD_DIM, (h + 1) * MEM_HEAD_DIM)
        s = _dot_nt(q[:, sl].astype(bf16), mk_ref[:, sl]) * (MEM_HEAD_DIM ** -0.5)
        p = jnp.exp(s - jnp.max(s, axis=-1, keepdims=True))
        p = p / jnp.sum(p, axis=-1, keepdims=True)
        oh = _dot(p.astype(bf16), mv_ref[:, sl])
        acc = acc + _dot(oh.astype(bf16), wo_ref[sl, :])
    o_ref[...] = acc


def cross_prompt(x, mix_list, w_out_list, g, wq, mkb, mvb, wo, tm):
    m, d = x.shape
    row = lambda c: pl.BlockSpec((tm, c), lambda i: (i, 0))
    consts = [g.reshape(1, d), wq, mkb, mvb, wo]
    return pl.pallas_call(
        functools.partial(_cross_prompt_kernel, n_mix=len(mix_list)), grid=(m // tm,),
        in_specs=[row(d)] + [row(a.shape[1]) for a in mix_list] + [_full(w.shape) for w in w_out_list]
                 + [_full(c.shape) for c in consts],
        out_specs=row(d),
        out_shape=jax.ShapeDtypeStruct((m, d), f32),
        compiler_params=_cparams(1), name="cross_prompt",
    )(x, *mix_list, *w_out_list, *consts)


MEM_SPLIT = MEM_HEAD_DIM // LANES
MEM_ROWS = MEM_SPLIT * MEM_HEADS


def _split_channel_order():
    return jnp.arange(MEM_HEADS * MEM_HEAD_DIM).reshape(MEM_HEADS, MEM_SPLIT, LANES).transpose(1, 0, 2).reshape(-1)


def _mem_split_view(cache):
    d, b, m = cache.shape[:3]
    x = cache.reshape(d, b, m, MEM_HEADS, MEM_SPLIT, LANES)
    return jnp.transpose(x, (0, 1, 2, 4, 3, 5)).reshape(d, b, m, MEM_ROWS, LANES)


def _cross_sample_kernel(q_ref, mk_ref, mv_ref, o_ref):
    part = jnp.sum(mk_ref[...] * q_ref[...][None], axis=-1, keepdims=True)
    s = part
    for piece in range(1, MEM_SPLIT):
        s = s + jnp.roll(part, piece * MEM_HEADS, axis=1)
    s = s * (MEM_HEAD_DIM ** -0.5)
    p = jnp.exp(s - jnp.max(s, axis=0, keepdims=True))
    l = jnp.sum(p, axis=0)
    o_ref[...] = jnp.sum(p * mv_ref[...], axis=0) / l


def cross_sample(q_split, mem_k_split, mem_v_split, layer):
    b = q_split.shape[0]
    mlen = mem_k_split.shape[2]
    mem_spec = pl.BlockSpec((None, None, mlen, MEM_ROWS, LANES), lambda i: (layer, i, 0, 0, 0))
    row_spec = pl.BlockSpec((None, MEM_ROWS, LANES), lambda i: (i, 0, 0))
    out = pl.pallas_call(
        _cross_sample_kernel, grid=(b,),
        in_specs=[row_spec, mem_spec, mem_spec], out_specs=row_spec,
        out_shape=jax.ShapeDtypeStruct((b, MEM_ROWS, LANES), f32),
        compiler_params=_cparams(1), name="cross_sample",
    )(q_split.reshape(b, MEM_ROWS, LANES), mem_k_split, mem_v_split)
    return out.reshape(b, MEM_ROWS * LANES)


def _swiglu_kernel(x_ref, g_ref, wg_ref, wu_ref, wd_ref, gout_ref, o_ref, h_ref, acc_ref, *, norm_out):
    f = pl.program_id(1)

    @pl.when(f == 0)
    def _():
        h_ref[...] = _rms(x_ref[...], g_ref[...]).astype(bf16)
        acc_ref[...] = x_ref[...]

    hb = h_ref[...]
    a = _silu(_dot(hb, wg_ref[...])) * _dot(hb, wu_ref[...])
    acc_ref[...] += _dot(a.astype(bf16), wd_ref[...])

    @pl.when(f == pl.num_programs(1) - 1)
    def _():
        y = acc_ref[...]
        o_ref[...] = _rms(y, gout_ref[...]) if norm_out else y


def swiglu_block(x, g, w_gu, w_d, g_out, norm_out, tm, tf):
    m, d = x.shape
    nf = D_FF // tf
    return pl.pallas_call(
        functools.partial(_swiglu_kernel, norm_out=norm_out), grid=(m // tm, nf),
        in_specs=[pl.BlockSpec((tm, d), lambda i, f: (i, 0)), _full((1, d)),
                  pl.BlockSpec((d, tf), lambda i, f: (0, f)),
                  pl.BlockSpec((d, tf), lambda i, f: (0, f + nf)),
                  pl.BlockSpec((tf, d), lambda i, f: (f, 0)), _full((1, d))],
        out_specs=pl.BlockSpec((tm, d), lambda i, f: (i, 0)),
        out_shape=jax.ShapeDtypeStruct((m, d), f32),
        scratch_shapes=[pltpu.VMEM((tm, d), bf16), pltpu.VMEM((tm, d), f32)],
        compiler_params=_cparams(2), name="swiglu_block",
    )(x, g.reshape(1, d), w_gu, w_gu, w_d, g_out.reshape(1, d))


def _pool_select(w2, w4, w8, w16, cnt_of, lane):
    out = w16 / cnt_of(16)
    for win, acc in ((8, w8), (4, w4), (2, w2)):
        g = POOL_WINDOWS.index(win)
        out = jnp.where(lane < (g + 1) * POOL_GROUP, acc / cnt_of(win), out)
    return out


def _group_rmsnorm(y, g):
    half = SSM_INNER // SSM_GROUPS
    parts = [_rms(y[:, i * half:(i + 1) * half], g[:, i * half:(i + 1) * half])
             for i in range(SSM_GROUPS)]
    return jnp.concatenate(parts, axis=-1)


def _ssd_pool_prompt_kernel(z_ref, xbc_ref, dt_ref, u_ref, cw_ref, cb_ref, dtb_ref, a_ref, dsk_ref,
                            ng_ref, pw_ref, ps_ref,
                            y_ref, po_ref, st_ref,
                            ext_ref, pext_ref, xc_ref, s_ref, *, tm):
    i = pl.program_id(0)
    halo = 8
    phalo = 16

    @pl.when(i == 0)
    def _():
        ext_ref[0:halo, :] = jnp.zeros((halo, CONV_DIM), f32)
        pext_ref[0:phalo, :] = jnp.zeros((phalo, POOL_DIM), f32)
        s_ref[...] = jnp.zeros(s_ref.shape, f32)

    @pl.when(i > 0)
    def _():
        ext_ref[0:halo, :] = ext_ref[tm:tm + halo, :]
        pext_ref[0:phalo, :] = pext_ref[tm:tm + phalo, :]

    ext_ref[halo:halo + tm, :] = xbc_ref[...]
    pext_ref[phalo:phalo + tm, :] = u_ref[...]

    rp, lp = 64, 256
    for r0 in range(0, tm, rp):
        for c0 in range(0, CONV_DIM, lp):
            acc = jnp.broadcast_to(cb_ref[:, c0:c0 + lp], (rp, lp))
            for j in range(CONV_W):
                off = halo - (CONV_W - 1) + j + r0
                acc = acc + ext_ref[off:off + rp, c0:c0 + lp] * cw_ref[j:j + 1, c0:c0 + lp]
            xc_ref[r0:r0 + rp, c0:c0 + lp] = _silu(acc)

    lane_p = lax.broadcasted_iota(jnp.int32, (rp, POOL_DIM), 1)
    row_p = lax.broadcasted_iota(jnp.int32, (rp, POOL_DIM), 0)
    for r0 in range(0, tm, rp):
        u = u_ref[r0:r0 + rp, :]
        run = u
        sums = {}
        for back in range(1, max(POOL_WINDOWS)):
            run = run + pext_ref[phalo - back + r0:phalo - back + r0 + rp, :]
            if back + 1 in POOL_WINDOWS:
                sums[back + 1] = run
        pos1 = (row_p + (i * tm + r0 + 1)).astype(f32)
        pooled = _pool_select(sums[2], sums[4], sums[8], sums[16],
                              lambda win: jnp.minimum(float(win), pos1), lane_p) - u
        po_ref[r0:r0 + rp, :] = (_dot(pooled.astype(bf16), pw_ref[...]) * ps_ref[...]).astype(po_ref.dtype)

    cl = SSD_CHUNK
    r_io = lax.broadcasted_iota(jnp.int32, (cl, cl), 0)
    c_io = lax.broadcasted_iota(jnp.int32, (cl, cl), 1)
    ltri = (c_io <= r_io).astype(f32)
    causal = c_io <= r_io
    first_half = c_io < SSM_HEAD_DIM
    n_pairs = SSM_HEADS // 2
    heads_per_group = SSM_HEADS // SSM_GROUPS

    def chunk(c, carry):
        r0 = pl.multiple_of(c * cl, cl)
        xc = xc_ref[pl.ds(r0, cl), :]
        dt = _softplus(dt_ref[pl.ds(r0, cl), :] + dtb_ref[...])
        a_cum = _dot(ltri, dt * a_ref[...], precision=HIGHEST)
        a_cum_t = a_cum.T
        dt_t = dt.T
        to_end_t = jnp.exp(a_cum_t[:, cl - 1:cl] - a_cum_t) * dt_t
        b_t = [xc[:, SSM_INNER + g * SSM_STATE:SSM_INNER + (g + 1) * SSM_STATE].T
               for g in range(SSM_GROUPS)]
        cm = [xc[:, SSM_INNER + SSM_GROUPS * SSM_STATE + g * SSM_STATE:
                 SSM_INNER + SSM_GROUPS * SSM_STATE + (g + 1) * SSM_STATE].astype(bf16)
              for g in range(SSM_GROUPS)]
        scores = [_dot(cm[g], b_t[g].astype(bf16)) for g in range(SSM_GROUPS)]
        y_pairs = []
        for k in range(n_pairs):
            g = (2 * k) // heads_per_group
            xs_pair = xc[:, k * LANES:(k + 1) * LANES]
            xs_b = xs_pair.astype(bf16)
            yd, cs, eb = [], [], []
            for h in (2 * k, 2 * k + 1):
                colb = jnp.broadcast_to(a_cum[:, h:h + 1], (cl, cl))
                seg = colb - a_cum_t[h:h + 1, :]
                decay = jnp.exp(jnp.where(causal, seg, -jnp.inf))
                mh = scores[g] * decay * dt_t[h:h + 1, :]
                yd.append(_dot(mh.astype(bf16), xs_b))
                cs.append(_dot((b_t[g] * to_end_t[h:h + 1, :]).astype(bf16), xs_b))
                eb.append(jnp.exp(colb))
            e_pair = jnp.where(first_half, eb[0], eb[1])
            s_old = s_ref[k]
            y_off = _dot(cm[g], s_old.astype(bf16)) * e_pair
            y_pairs.append(jnp.where(first_half, yd[0], yd[1]) + y_off
                           + dsk_ref[:, k * LANES:(k + 1) * LANES] * xs_pair)
            s_ref[k] = s_old * e_pair[cl - 1:cl, :] + jnp.where(first_half, cs[0], cs[1])
        y = jnp.concatenate(y_pairs, axis=-1) * _silu(z_ref[pl.ds(r0, cl), :])
        y_ref[pl.ds(r0, cl), :] = _group_rmsnorm(y, ng_ref[...]).astype(y_ref.dtype)
        return carry

    lax.fori_loop(0, tm // cl, chunk, 0)

    @pl.when(i == pl.num_programs(0) - 1)
    def _():
        for k in range(n_pairs):
            st = s_ref[k].T
            st_ref[2 * k] = st[0:SSM_HEAD_DIM, :]
            st_ref[2 * k + 1] = st[SSM_HEAD_DIM:2 * SSM_HEAD_DIM, :]


def ssd_pool_prompt(z, xbc, dt, u, prm, tm):
    t = z.shape[0]
    row = lambda c: pl.BlockSpec((tm, c), lambda i: (i, 0))
    consts = [prm['conv_w'], prm['conv_b'], prm['dt_bias'], prm['a'], prm['dsk'], prm['ssm_norm_g'],
              prm['pool_w'], prm['pool_scale']]
    return pl.pallas_call(
        functools.partial(_ssd_pool_prompt_kernel, tm=tm), grid=(t // tm,),
        in_specs=[row(SSM_INNER), row(CONV_DIM), row(LANES), row(POOL_DIM)] + [_full(c.shape) for c in consts],
        out_specs=[row(SSM_INNER), row(POOL_DIM), _full((SSM_HEADS, SSM_HEAD_DIM, SSM_STATE))],
        out_shape=[jax.ShapeDtypeStruct((t, SSM_INNER), bf16), jax.ShapeDtypeStruct((t, POOL_DIM), bf16),
                   jax.ShapeDtypeStruct((SSM_HEADS, SSM_HEAD_DIM, SSM_STATE), f32)],
        scratch_shapes=[pltpu.VMEM((tm + 8, CONV_DIM), f32), pltpu.VMEM((tm + 16, POOL_DIM), f32),
                        pltpu.VMEM((tm, CONV_DIM), f32),
                        pltpu.VMEM((SSM_HEADS // 2, SSM_STATE, 2 * SSM_HEAD_DIM), f32)],
        compiler_params=_cparams(1), name="ssd_pool_prompt",
    )(z, xbc, dt, u, *consts)


def _col_tile(row):
    return jnp.broadcast_to(row, (LANES, LANES)).T


def _ssd_pool_sample_kernel(z_ref, xbc_ref, dt_ref, u_ref, sc_ref, ss_ref, sp_ref,
                            cw_ref, cb_ref, dtb_ref, a_ref, dsk_ref, ng_ref, pw_ref, ps_ref, ex_ref,
                            y_ref, po_ref, cn_ref, sn_ref, pn_ref, *, pos0):
    xrow = xbc_ref[...]
    acc = cb_ref[...] + xrow * cw_ref[CONV_W - 1:CONV_W, :]
    for j in range(CONV_W - 1):
        acc = acc + sc_ref[j:j + 1, :] * cw_ref[j:j + 1, :]
        if j > 0:
            cn_ref[j - 1:j, :] = sc_ref[j:j + 1, :]
    cn_ref[CONV_W - 2:CONV_W - 1, :] = xrow
    xc = _silu(acc)

    dt = _softplus(dt_ref[...] + dtb_ref[...])
    dta = dt * a_ref[...]
    both = jnp.concatenate([jnp.broadcast_to(dt, (8, LANES)), jnp.broadcast_to(dta, (8, LANES))], axis=0)
    both_x = _dot(both, ex_ref[...], precision=HIGHEST)
    dtx = both_x[0:1, :]
    dec_x = jnp.exp(both_x[8:9, :])
    xs = xc[:, 0:SSM_INNER]
    xdt = xs * dtx
    heads_per_group = SSM_HEADS // SSM_GROUPS
    y_pairs = []
    for k in range(SSM_HEADS // 2):
        g = (2 * k) // heads_per_group
        b_row = xc[:, SSM_INNER + g * SSM_STATE:SSM_INNER + (g + 1) * SSM_STATE]
        c_row = xc[:, SSM_INNER + SSM_GROUPS * SSM_STATE + g * SSM_STATE:
                   SSM_INNER + SSM_GROUPS * SSM_STATE + (g + 1) * SSM_STATE]
        sl = slice(k * LANES, (k + 1) * LANES)
        s_old = jnp.concatenate([ss_ref[2 * k], ss_ref[2 * k + 1]], axis=0)
        s_new = s_old * _col_tile(dec_x[:, sl]) + _col_tile(xdt[:, sl]) * b_row
        sn_ref[2 * k] = s_new[0:SSM_HEAD_DIM, :]
        sn_ref[2 * k + 1] = s_new[SSM_HEAD_DIM:, :]
        y_k = _dot_nt(jnp.broadcast_to(c_row, (8, SSM_STATE)), s_new, precision=HIGHEST)[0:1, :]
        y_pairs.append(y_k + dsk_ref[:, sl] * xs[:, sl])
    y = jnp.concatenate(y_pairs, axis=-1) * _silu(z_ref[...])
    y_ref[...] = _group_rmsnorm(y, ng_ref[...]).astype(y_ref.dtype)

    u = u_ref[...]
    prev = sp_ref[...]
    rowi = lax.broadcasted_iota(jnp.int32, prev.shape, 0)
    tail = lambda win: u + jnp.sum(jnp.where(rowi >= POOL_HIST - (win - 1), prev, 0.0), axis=0, keepdims=True)
    lane_p = lax.broadcasted_iota(jnp.int32, (1, POOL_DIM), 1)
    pooled = _pool_select(tail(2), tail(4), tail(8), tail(16),
                          lambda win: float(min(win, pos0 + 1)), lane_p) - u
    po = _dot(jnp.broadcast_to(pooled, (8, POOL_DIM)).astype(bf16), pw_ref[...])[0:1, :] * ps_ref[...]
    po_ref[...] = po.astype(po_ref.dtype)
    pn_ref[0:POOL_HIST - 1, :] = sp_ref[1:POOL_HIST, :]
    pn_ref[POOL_HIST - 1:POOL_HIST, :] = u


def ssd_pool_sample(z, xbc, dt, u, st_conv, st_ssm, st_pool, prm, pos0):
    b = z.shape[0]
    per_seq = lambda *shape: pl.BlockSpec((None,) + shape, lambda i: (i,) + (0,) * len(shape))
    consts = [prm['conv_w'], prm['conv_b'], prm['dt_bias'], prm['a'], prm['dsk'], prm['ssm_norm_g'],
              prm['pool_w'], prm['pool_scale'], prm['expand']]
    outs = pl.pallas_call(
        functools.partial(_ssd_pool_sample_kernel, pos0=pos0), grid=(b,),
        in_specs=[per_seq(1, SSM_INNER), per_seq(1, CONV_DIM), per_seq(1, LANES), per_seq(1, POOL_DIM),
                  per_seq(CONV_W - 1, CONV_DIM), per_seq(SSM_HEADS, SSM_HEAD_DIM, SSM_STATE),
                  per_seq(POOL_HIST, POOL_DIM)] + [_full(c.shape) for c in consts],
        out_specs=[per_seq(1, SSM_INNER), per_seq(1, POOL_DIM), per_seq(CONV_W - 1, CONV_DIM),
                   per_seq(SSM_HEADS, SSM_HEAD_DIM, SSM_STATE), per_seq(POOL_HIST, POOL_DIM)],
        out_shape=[jax.ShapeDtypeStruct((b, 1, SSM_INNER), f32), jax.ShapeDtypeStruct((b, 1, POOL_DIM), f32),
                   jax.ShapeDtypeStruct((b, CONV_W - 1, CONV_DIM), f32),
                   jax.ShapeDtypeStruct((b, SSM_HEADS, SSM_HEAD_DIM, SSM_STATE), f32),
                   jax.ShapeDtypeStruct((b, POOL_HIST, POOL_DIM), f32)],
        compiler_params=_cparams(1), name="ssd_pool_sample",
    )(z.reshape(b, 1, -1), xbc.reshape(b, 1, -1), dt.reshape(b, 1, -1), u.reshape(b, 1, -1),
      st_conv, st_ssm, st_pool, *consts)
    y, po, cn, sn, pn = outs
    return y.reshape(b, -1), po.reshape(b, -1), cn, sn, pn


def _topk_blocks(gate, n_past, axis):
    blk = lax.broadcasted_iota(jnp.int32, gate.shape, axis).astype(f32)
    g = jnp.where(blk < jnp.asarray(n_past, f32), gate, -jnp.inf)
    picks = []
    for _ in range(MOBA_TOP_K):
        m = jnp.max(g, axis=axis, keepdims=True)
        idx = jnp.min(jnp.where(g == m, blk, float(LANES)), axis=axis, keepdims=True)
        picks.append((idx, jnp.abs(m) < jnp.inf))
        g = jnp.where(blk == idx, -jnp.inf, g)
    return picks


def _moba_prompt_kernel(q_ref, k_ref, v_ref, o_ref,
                        km_ref, kb_ref, vt_ref, qst_ref, nmt_ref, acc_ref, sa_ref, sb_ref):
    i = pl.program_id(0)
    tq = MOBA_BLOCK
    scale = ATT_HEAD_DIM ** -0.5

    @pl.when(i == 0)
    def _():
        km_ref[...] = jnp.zeros(km_ref.shape, f32)

    k = k_ref[...]
    kb = k.astype(bf16)
    kb_ref[i] = kb
    vt = v_ref[...].T.astype(bf16)
    ones_rows = jnp.ones((V_AUG_ROWS - ATT_HEAD_DIM, tq), bf16)
    for h in range(ATT_HEADS):
        vt_ref[i, h] = jnp.concatenate([vt[h * ATT_HEAD_DIM:(h + 1) * ATT_HEAD_DIM, :], ones_rows], axis=0)
    qt = q_ref[...].T
    head_of_row = lax.broadcasted_iota(jnp.int32, (ATT_DIM, tq), 0) // ATT_HEAD_DIM
    blk_row = lax.broadcasted_iota(jnp.int32, (LANES, tq), 0).astype(f32)
    key_io = lax.broadcasted_iota(jnp.int32, (tq, tq), 0)
    qry_io = lax.broadcasted_iota(jnp.int32, (tq, tq), 1)
    km = km_ref[...]
    m_own = []
    for h in range(ATT_HEADS):
        qth = jnp.where(head_of_row == h, qt, 0.0)
        gate = _dot(km, qth, precision=HIGHEST)
        sel = jnp.zeros((LANES, tq), jnp.bool_)
        for idx, ok in _topk_blocks(gate, i, axis=0):
            sel = sel | ((blk_row == idx) & ok)
        nmt_ref[h] = jnp.where(sel, 0.0, MASKED)
        qst = (qth * (scale * LOG2E)).astype(bf16)
        qst_ref[h] = qst
        st = jnp.where(key_io <= qry_io, _dot(kb, qst), MASKED)
        m = jnp.max(st, axis=0, keepdims=True)
        m_own.append(m)
        acc_ref[h] = _dot(vt_ref[i, h], jnp.exp2(st - m).astype(bf16))

    km_ref[pl.ds(i, 1), :] = jnp.mean(k, axis=0, keepdims=True)

    def scores(j, dst):
        kj = kb_ref[j]
        for h in range(ATT_HEADS):
            dst[h] = _dot(kj, qst_ref[h])

    def absorb(j, src, carry):
        new = []
        for h in range(ATT_HEADS):
            m_prev = carry[h]
            raw = src[h]
            bias = nmt_ref[h, pl.ds(j, 1), :]
            m_new = jnp.maximum(m_prev, jnp.max(raw, axis=0, keepdims=True) + bias)
            alpha = jnp.exp2(m_prev - m_new)
            p = jnp.exp2(raw - (m_new - bias))
            new.append(m_new)
            acc_ref[h] = acc_ref[h] * alpha + _dot(vt_ref[j, h], p.astype(bf16))
        return tuple(new)

    last = jnp.maximum(i - 1, 0)
    scores(0, sa_ref)

    def pair(t, carry):
        scores(jnp.minimum(2 * t + 1, last), sb_ref)
        carry = absorb(2 * t, sa_ref, carry)
        scores(jnp.minimum(2 * t + 2, last), sa_ref)
        return absorb(2 * t + 1, sb_ref, carry)

    carry = lax.fori_loop(0, i // 2, pair, tuple(m_own))
    carry = lax.cond(i % 2 == 1, lambda c: absorb(i - 1, sa_ref, c), lambda c: c, carry)
    out_t = jnp.concatenate(
        [acc_ref[h, 0:ATT_HEAD_DIM, :] / acc_ref[h, ATT_HEAD_DIM:ATT_HEAD_DIM + 1, :] for h in range(ATT_HEADS)],
        axis=0)
    o_ref[...] = out_t.T.astype(o_ref.dtype)


def moba_prompt(q, k, v):
    t = q.shape[0]
    tq = MOBA_BLOCK
    n_blk = t // tq
    tile = pl.BlockSpec((tq, ATT_DIM), lambda i: (i, 0))
    return pl.pallas_call(
        _moba_prompt_kernel, grid=(n_blk,),
        in_specs=[tile, tile, tile], out_specs=tile,
        out_shape=jax.ShapeDtypeStruct((t, ATT_DIM), bf16),
        scratch_shapes=[pltpu.VMEM((LANES, ATT_DIM), f32),
                        pltpu.VMEM((n_blk, tq, ATT_DIM), bf16),
                        pltpu.VMEM((n_blk, ATT_HEADS, V_AUG_ROWS, tq), bf16),
                        pltpu.VMEM((ATT_HEADS, ATT_DIM, tq), bf16),
                        pltpu.VMEM((ATT_HEADS, LANES, tq), f32),
                        pltpu.VMEM((ATT_HEADS, V_AUG_ROWS, tq), f32),
                        pltpu.VMEM((ATT_HEADS, tq, tq), f32),
                        pltpu.VMEM((ATT_HEADS, tq, tq), f32)],
        compiler_params=_cparams(1), name="moba_prompt",
    )(q, k, v)


K_CHUNK_PAGES = 16
K_SLOTS = 4
PAGES_PER_BLOCK = MOBA_BLOCK // PAGE_SIZE
CHUNKS_BEFORE_FINISH = 2


def _moba_decode_kernel(pt_ref, q_ref, q8_ref, kn8_ref, vn8_ref, kc_ref, vc_ref, o_ref,
                        kbuf, vbuf, s_ref, p_ref, gate_ref, stash_ref, ids_ref, ksem, vsem, *, layer, n_pages):
    b = pl.program_id(0)
    n_seq = pl.num_programs(0) - 1
    n_chunks = n_pages // K_CHUNK_PAGES
    blocks_per_chunk = K_CHUNK_PAGES // PAGES_PER_BLOCK
    n_blocks = n_pages // PAGES_PER_BLOCK
    scale = ATT_HEAD_DIM ** -0.5

    def k_copy(seq, page_slot, slot, p):
        page = pt_ref[seq, page_slot]
        return pltpu.make_async_copy(kc_ref.at[layer, page], kbuf.at[slot, p], ksem.at[slot])

    def start_chunk(seq, c, slot):
        for p in range(K_CHUNK_PAGES):
            k_copy(seq, c * K_CHUNK_PAGES + p, slot, p).start()

    def wait_chunk(seq, c, slot):
        for p in range(K_CHUNK_PAGES):
            k_copy(seq, c * K_CHUNK_PAGES + p, slot, p).wait()

    n_picks = ATT_HEADS * MOBA_TOP_K

    def v_copy(h, r, half, page):
        return pltpu.make_async_copy(vc_ref.at[layer, page, pl.ds(h * ATT_HEAD_DIM, ATT_HEAD_DIM), :],
                                     vbuf.at[h * MOBA_TOP_K + r, half], vsem.at[0])

    def stream(chunks):
        lane8 = lax.broadcasted_iota(jnp.int32, (8, ATT_DIM), 1)
        row8 = lax.broadcasted_iota(jnp.int32, (8, ATT_DIM), 0)
        qblk_b = jnp.where(lane8 // ATT_HEAD_DIM == row8, q_ref[...], 0.0).astype(bf16)
        blk_lane = lax.broadcasted_iota(jnp.int32, (8, LANES), 1)
        gate = gate_ref[...]
        for c in chunks:
            slot = c % K_SLOTS
            ahead = c + K_SLOTS - 1
            if ahead < n_chunks:
                start_chunk(b, ahead, ahead % K_SLOTS)
            else:
                @pl.when(b + 1 < n_seq)
                def _():
                    start_chunk(b + 1, ahead - n_chunks, ahead % K_SLOTS)
            wait_chunk(b, c, slot)
            for t in range(blocks_per_chunk):
                blk = c * blocks_per_chunk + t
                halves = [_dot(qblk_b, kbuf[slot, t * PAGES_PER_BLOCK + half].astype(bf16))
                          for half in range(PAGES_PER_BLOCK)]
                for half in range(PAGES_PER_BLOCK):
                    s_ref[blk, :, half * PAGE_SIZE:(half + 1) * PAGE_SIZE] = halves[half]
                tot = jnp.sum(sum(halves), axis=-1, keepdims=True) * (1.0 / MOBA_BLOCK)
                gate = jnp.where(blk_lane == blk, tot, gate)
        gate_ref[...] = gate

    def choose_and_weigh():
        picks = _topk_blocks(gate_ref[...], n_blocks, axis=1)
        for h in range(ATT_HEADS):
            for r in range(MOBA_TOP_K):
                blk = picks[r][0][h, 0].astype(jnp.int32)
                ids_ref[h * MOBA_TOP_K + r] = blk
                for half in range(PAGES_PER_BLOCK):
                    page = pt_ref[b, blk * PAGES_PER_BLOCK + half]
                    ids_ref[n_picks + (h * MOBA_TOP_K + r) * PAGES_PER_BLOCK + half] = page
                    v_copy(h, r, half, page).start()
        blk_io = lax.broadcasted_iota(jnp.int32, (n_blocks, 8, MOBA_BLOCK), 0).astype(f32)
        seen = jnp.zeros((n_blocks, 8, MOBA_BLOCK), jnp.bool_)
        for idx, ok in picks:
            seen = seen | ((blk_io == idx[None]) & ok[None])
        s_all = jnp.where(seen, s_ref[...] * scale, -jnp.inf)
        s_own = jnp.sum(q8_ref[...] * kn8_ref[...], axis=-1, keepdims=True) * scale
        m = jnp.maximum(jnp.max(jnp.max(s_all, axis=0), axis=-1, keepdims=True), s_own)
        p_all = jnp.exp(s_all - m[None])
        p_own = jnp.exp(s_own - m)
        l = jnp.sum(jnp.sum(p_all, axis=0), axis=-1, keepdims=True) + p_own
        p_ref[...] = p_all
        stash_ref[:, 0:ATT_HEAD_DIM] = p_own * vn8_ref[...]
        stash_ref[:, ATT_HEAD_DIM:] = jnp.broadcast_to(l, (8, LANES - ATT_HEAD_DIM))

    def finish_previous():
        for h in range(ATT_HEADS):
            for r in range(MOBA_TOP_K):
                for half in range(PAGES_PER_BLOCK):
                    page = ids_ref[n_picks + (h * MOBA_TOP_K + r) * PAGES_PER_BLOCK + half]
                    v_copy(h, r, half, page).wait()
        row_hd = lax.broadcasted_iota(jnp.int32, (8, ATT_HEAD_DIM), 0)
        o = stash_ref[:, 0:ATT_HEAD_DIM]
        for h in range(ATT_HEADS):
            oh = jnp.zeros((8, ATT_HEAD_DIM), f32)
            for r in range(MOBA_TOP_K):
                pb = p_ref[ids_ref[h * MOBA_TOP_K + r]].astype(bf16)
                for half in range(PAGES_PER_BLOCK):
                    oh = oh + _dot_nt(pb[:, half * PAGE_SIZE:(half + 1) * PAGE_SIZE],
                                      vbuf[h * MOBA_TOP_K + r, half].astype(bf16))
            o = o + jnp.where(row_hd == h, oh, 0.0)
        o_ref[...] = o / stash_ref[:, ATT_HEAD_DIM:ATT_HEAD_DIM + 1]

    @pl.when(b == 0)
    def _():
        for c in range(K_SLOTS - 1):
            start_chunk(0, c, c)

    @pl.when(b < n_seq)
    def _():
        gate_ref[...] = jnp.zeros((8, LANES), f32)
        stream(range(0, CHUNKS_BEFORE_FINISH))

    @pl.when(b > 0)
    def _():
        finish_previous()

    @pl.when(b < n_seq)
    def _():
        stream(range(CHUNKS_BEFORE_FINISH, n_chunks))
        choose_and_weigh()


def moba_decode(page_table, q, k_new, v_new, cache_kt, cache_vt, layer):
    b = q.shape[0]
    n_pages = page_table.shape[1]
    n_blocks = n_pages // PAGES_PER_BLOCK
    assert n_blocks <= LANES and n_blocks >= MOBA_TOP_K
    heads8 = lambda a: jnp.pad(a.reshape(b, ATT_HEADS, ATT_HEAD_DIM), ((0, 0), (0, 8 - ATT_HEADS), (0, 0)))
    n_picks = ATT_HEADS * MOBA_TOP_K
    cur = lambda i, pt: (jnp.minimum(i, b - 1), 0, 0)
    prev = lambda i, pt: (jnp.maximum(i - 1, 0), 0, 0)
    per_head = pl.BlockSpec((None, 8, ATT_HEAD_DIM), cur)
    grid_spec = pltpu.PrefetchScalarGridSpec(
        num_scalar_prefetch=1, grid=(b + 1,),
        in_specs=[pl.BlockSpec((None, 1, ATT_DIM), cur), per_head, per_head, per_head,
                  pl.BlockSpec(memory_space=pl.ANY), pl.BlockSpec(memory_space=pl.ANY)],
        out_specs=pl.BlockSpec((None, 8, ATT_HEAD_DIM), prev),
        scratch_shapes=[pltpu.VMEM((K_SLOTS, K_CHUNK_PAGES, ATT_DIM, PAGE_SIZE), f32),
                        pltpu.VMEM((n_picks, PAGES_PER_BLOCK, ATT_HEAD_DIM, PAGE_SIZE), f32),
                        pltpu.VMEM((n_blocks, 8, MOBA_BLOCK), f32),
                        pltpu.VMEM((n_blocks, 8, MOBA_BLOCK), f32),
                        pltpu.VMEM((8, LANES), f32),
                        pltpu.VMEM((8, LANES), f32),
                        pltpu.SMEM((n_picks * (1 + PAGES_PER_BLOCK),), jnp.int32),
                        pltpu.SemaphoreType.DMA((K_SLOTS,)),
                        pltpu.SemaphoreType.DMA((1,))])
    out = pl.pallas_call(
        functools.partial(_moba_decode_kernel, layer=layer, n_pages=n_pages),
        grid_spec=grid_spec,
        out_shape=jax.ShapeDtypeStruct((b, 8, ATT_HEAD_DIM), f32),
        compiler_params=_cparams(1), name="moba_decode",
    )(page_table, q.reshape(b, 1, ATT_DIM), heads8(q), heads8(k_new), heads8(v_new), cache_kt, cache_vt)
    return out[:, :ATT_HEADS].reshape(b, ATT_DIM)


def _layer_params(l, w_in, conv_w, conv_b, dt_bias, a_log, d_skip, ssm_norm_g, pool_w, pool_scale, w_out):
    o_xbc = SSM_INNER
    o_dt = o_xbc + CONV_DIM
    o_pool = o_dt + SSM_HEADS
    o_q = o_pool + POOL_DIM
    o_k = o_q + ATT_DIM
    o_v = o_k + ATT_DIM
    wl = w_in[l]
    pad_heads = lambda v: jnp.pad(v.astype(f32), (0, LANES - SSM_HEADS)).reshape(1, LANES)
    w_dt = jnp.pad(wl[:, o_dt:o_pool], ((0, 0), (0, LANES - SSM_HEADS)))
    w_split = [wl[:, :o_xbc], wl[:, o_xbc:o_dt], w_dt, wl[:, o_pool:o_q], wl[:, o_q:o_k],
               wl[:, o_k:o_v], wl[:, o_v:]]
    pw = jnp.zeros((POOL_DIM, POOL_DIM), f32)
    for g in range(len(POOL_WINDOWS)):
        pw = pw.at[g * POOL_GROUP:(g + 1) * POOL_GROUP, g * POOL_GROUP:(g + 1) * POOL_GROUP].set(pool_w[l, g])
    expand = (jnp.arange(LANES)[:, None] == (jnp.arange(SSM_INNER)[None, :] // SSM_HEAD_DIM)).astype(f32)
    wo = w_out[l].astype(bf16)
    return {
        'w_in': [w.astype(bf16) for w in w_split],
        'conv_w': conv_w[l], 'conv_b': conv_b[l].reshape(1, CONV_DIM),
        'dt_bias': pad_heads(dt_bias[l]), 'a': pad_heads(-jnp.exp(a_log[l].astype(f32))),
        'dsk': jnp.repeat(d_skip[l].astype(f32), SSM_HEAD_DIM).reshape(1, SSM_INNER),
        'ssm_norm_g': ssm_norm_g[l].reshape(1, SSM_INNER),
        'pool_w': pw.astype(bf16), 'pool_scale': pool_scale[l].reshape(1, POOL_DIM),
        'expand': expand,
        'w_out': [wo[:SSM_INNER], wo[SSM_INNER:SSM_INNER + POOL_DIM], wo[SSM_INNER + POOL_DIM:]],
    }


IN_PROJ_OUTS = [(i, f32) for i in range(7)]


def _pages_channel_major(cache):
    d, n_phys = cache.shape[:2]
    return jnp.transpose(cache, (0, 1, 3, 4, 2)).reshape(d, n_phys, ATT_DIM, PAGE_SIZE)


def kernel(x_prompt, x_sample, cache_moba_k, cache_moba_v, state_ssm, state_conv, state_pool, cache_mem_k, cache_mem_v, page_table, mem_prompt, norm_mix_g, w_in, conv_w, conv_b, dt_bias, a_log, d_skip, ssm_norm_g, pool_w, pool_scale, w_out, norm_cross_g, norm_mem_g, w_mem_q, w_mem_kv, w_mem_o, norm_ffn_g, w_gate_up, w_down, final_norm_g):
    depth = w_in.shape[0]
    bp, t, d = x_prompt.shape
    bs = x_sample.shape[0]
    assert bp == 1 and x_sample.shape[1] == 1
    past_len = page_table.shape[1] * PAGE_SIZE
    cache_kt = _pages_channel_major(cache_moba_k)
    cache_vt = _pages_channel_major(cache_moba_v)
    mem_k_split = _mem_split_view(cache_mem_k)
    mem_v_split = _mem_split_view(cache_mem_v)
    split_order = _split_channel_order()
    mem_len = mem_prompt.shape[1]
    xp = x_prompt.reshape(t, d)
    xs = x_sample.reshape(bs, d)
    mem = mem_prompt.reshape(mem_len, d)
    tm_p, tm_s = 512, bs
    outs = {n: [] for n in ('kp', 'vp', 'ks', 'vs', 'sp', 'ss', 'cp', 'cs', 'pp', 'ps', 'mk', 'mv')}
    for l in range(depth):
        prm = _layer_params(l, w_in, conv_w, conv_b, dt_bias, a_log, d_skip, ssm_norm_g, pool_w, pool_scale, w_out)
        wq = w_mem_q[l].astype(bf16)
        wkv = w_mem_kv[l].astype(bf16)
        wo_mem = w_mem_o[l].astype(bf16)
        wgu = w_gate_up[l].astype(bf16)
        wdn = w_down[l].astype(bf16)
        half = MEM_HEADS * MEM_HEAD_DIM
        last_layer = l == depth - 1

        mk, mv, mkb, mvb = norm_matmul(mem, norm_mem_g[l], [wkv[:, :half], wkv[:, half:]],
                                       [(0, f32), (1, f32), (0, bf16), (1, bf16)], tm=mem_len)
        z, xbc, dtr, u, q, k, v = norm_matmul(xp, norm_mix_g[l], prm['w_in'], IN_PROJ_OUTS, tm=tm_p)
        y, po, s_new = ssd_pool_prompt(z, xbc, dtr, u, prm, tm=tm_p)
        att = moba_prompt(q, k, v)
        xp = cross_prompt(xp, [y, po, att], prm['w_out'], norm_cross_g[l], wq, mkb, mvb, wo_mem, tm=tm_p)
        xp = swiglu_block(xp, norm_ffn_g[l], wgu, wdn, final_norm_g, norm_out=last_layer, tm=512, tf=D_FF // 2)
        outs['kp'].append(k.reshape(1, t, ATT_HEADS, ATT_HEAD_DIM))
        outs['vp'].append(v.reshape(1, t, ATT_HEADS, ATT_HEAD_DIM))
        outs['sp'].append(s_new[None])
        outs['cp'].append(xbc[t - (CONV_W - 1):][None])
        outs['pp'].append(u[t - POOL_HIST:][None])
        outs['mk'].append(mk.reshape(1, mem_len, MEM_HEADS, MEM_HEAD_DIM))
        outs['mv'].append(mv.reshape(1, mem_len, MEM_HEADS, MEM_HEAD_DIM))

        z, xbc, dtr, u, q, k, v = norm_matmul(xs, norm_mix_g[l], prm['w_in'], IN_PROJ_OUTS, tm=tm_s)
        y, po, c_new, s_new, p_new = ssd_pool_sample(z, xbc, dtr, u, state_conv[l], state_ssm[l],
                                                     state_pool[l], prm, pos0=past_len)
        att = moba_decode(page_table, q, k, v, cache_kt, cache_vt, layer=l)
        xs = matmul_residual(xs, [y, po, att], prm['w_out'], tm=tm_s)
        (qc,) = norm_matmul(xs, norm_cross_g[l], [wq[:, split_order]], [(0, f32)], tm=tm_s)
        oc = cross_sample(qc, mem_k_split, mem_v_split, layer=l)
        xs = matmul_residual(xs, [oc], [wo_mem[split_order, :]], tm=tm_s)
        xs = swiglu_block(xs, norm_ffn_g[l], wgu, wdn, final_norm_g, norm_out=last_layer, tm=tm_s, tf=256)
        outs['ks'].append(k.reshape(bs, 1, ATT_HEADS, ATT_HEAD_DIM))
        outs['vs'].append(v.reshape(bs, 1, ATT_HEADS, ATT_HEAD_DIM))
        outs['ss'].append(s_new)
        outs['cs'].append(c_new)
        outs['ps'].append(p_new)

    y_prompt = xp.reshape(1, t, d)
    y_sample = xs.reshape(bs, 1, d)
    st = lambda n: jnp.stack(outs[n])
    return (y_prompt, y_sample, st('kp'), st('vp'), st('ks'), st('vs'), st('sp'), st('ss'),
            st('cp'), st('cs'), st('pp'), st('ps'), st('mk'), st('mv'))
```

```python
import functools

import jax
import jax.numpy as jnp
from jax import lax
from jax.experimental import pallas as pl
from jax.experimental.pallas import tpu as pltpu

f32 = jnp.float32
bf16 = jnp.bfloat16
HIGHEST = lax.Precision.HIGHEST

D_MODEL = 1024
SSM_INNER = 512
SSM_HEAD_DIM = 64
SSM_HEADS = 8
SSM_GROUPS = 2
SSM_STATE = 128
CONV_W = 4
CONV_DIM = SSM_INNER + 2 * SSM_GROUPS * SSM_STATE
SSD_CHUNK = 128
POOL_DIM = 256
POOL_WINDOWS = (2, 4, 8, 16)
POOL_GROUP = 64
POOL_HIST = 15
ATT_DIM = 256
ATT_HEAD_DIM = 64
ATT_HEADS = 4
MOBA_BLOCK = 256
MOBA_TOP_K = 3
PAGE_SIZE = 128
MEM_HEADS = 4
MEM_HEAD_DIM = 256
D_FF = 2816
RMS_EPS = 1e-6
LANES = 128
MASKED = -1e30
LOG2E = 1.4426950408889634
BF16_SUBLANES = 16
V_AUG_ROWS = ATT_HEAD_DIM + BF16_SUBLANES
VMEM_LIMIT = 56 * 1024 * 1024


def _cparams(n_axes):
    return pltpu.CompilerParams(dimension_semantics=("arbitrary",) * n_axes,
                                vmem_limit_bytes=VMEM_LIMIT)


def _rms(x, g):
    ms = jnp.mean(x * x, axis=-1, keepdims=True)
    return x * lax.rsqrt(ms + RMS_EPS) * g


def _dot(a, b, **kw):
    return jnp.dot(a, b, preferred_element_type=f32, **kw)


def _dot_nt(a, b, **kw):
    return lax.dot_general(a, b, (((1,), (1,)), ((), ())), preferred_element_type=f32, **kw)


def _silu(x):
    return x * jax.nn.sigmoid(x)


def _softplus(x):
    return jnp.maximum(x, 0.0) + jnp.log1p(jnp.exp(-jnp.abs(x)))


def _full(shape):
    return pl.BlockSpec(shape, lambda *_: (0,) * len(shape))


def _norm_mm_kernel(*refs, n_w, out_w):
    x_ref, g_ref = refs[0], refs[1]
    w_refs = refs[2:2 + n_w]
    o_refs = refs[2 + n_w:]
    hb = _rms(x_ref[...], g_ref[...]).astype(bf16)
    res = [_dot(hb, w[...]) for w in w_refs]
    for o_ref, wi in zip(o_refs, out_w):
        o_ref[...] = res[wi].astype(o_ref.dtype)


def norm_matmul(x, g, ws, outs, tm):
    m, d = x.shape
    in_specs = [pl.BlockSpec((tm, d), lambda i: (i, 0)), _full((1, d))]
    in_specs += [_full(w.shape) for w in ws]
    out_shape = [jax.ShapeDtypeStruct((m, ws[wi].shape[1]), dt) for wi, dt in outs]
    out_specs = [pl.BlockSpec((tm, ws[wi].shape[1]), lambda i: (i, 0)) for wi, _ in outs]
    return pl.pallas_call(
        functools.partial(_norm_mm_kernel, n_w=len(ws), out_w=tuple(wi for wi, _ in outs)),
        grid=(m // tm,), in_specs=in_specs, out_specs=out_specs, out_shape=out_shape,
        compiler_params=_cparams(1), name="norm_matmul",
    )(x, g.reshape(1, d), *ws)


def _mm_res_kernel(*refs, n_a):
    res_ref = refs[0]
    a_refs = refs[1:1 + n_a]
    w_refs = refs[1 + n_a:1 + 2 * n_a]
    o_ref = refs[-1]
    acc = res_ref[...]
    for a, w in zip(a_refs, w_refs):
        acc = acc + _dot(a[...].astype(bf16), w[...])
    o_ref[...] = acc


def matmul_residual(res, a_list, w_list, tm):
    m, d = res.shape
    in_specs = [pl.BlockSpec((tm, d), lambda i: (i, 0))]
    in_specs += [pl.BlockSpec((tm, a.shape[1]), lambda i: (i, 0)) for a in a_list]
    in_specs += [_full(w.shape) for w in w_list]
    return pl.pallas_call(
        functools.partial(_mm_res_kernel, n_a=len(a_list)),
        grid=(m // tm,), in_specs=in_specs,
        out_specs=pl.BlockSpec((tm, d), lambda i: (i, 0)),
        out_shape=jax.ShapeDtypeStruct((m, d), f32),
        compiler_params=_cparams(1), name="matmul_residual",
    )(res, *a_list, *w_list)


def _cross_prompt_kernel(*refs, n_mix):
    x_ref = refs[0]
    a_refs = refs[1:1 + n_mix]
    w_refs = refs[1 + n_mix:1 + 2 * n_mix]
    g_ref, wq_ref, mk_ref, mv_ref, wo_ref, o_ref = refs[1 + 2 * n_mix:]
    x = x_ref[...]
    for a, w in zip(a_refs, w_refs):
        x = x + _dot(a[...].astype(bf16), w[...])
    if n_mix:
        o_ref[...] = x
        x = o_ref[...]
    hb = _rms(x, g_ref[...]).astype(bf16)
    q = _dot(hb, wq_ref[...])
    acc = x
    for h in range(MEM_HEADS):
        sl = slice(h * MEM_HEAD_DIM, (h + 1) * MEM_HEAD_DIM)
        s = _dot_nt(q[:, sl].astype(bf16), mk_ref[:, sl]) * (MEM_HEAD_DIM ** -0.5)
        p = jnp.exp(s - jnp.max(s, axis=-1, keepdims=True))
        p = p / jnp.sum(p, axis=-1, keepdims=True)
        oh = _dot(p.astype(bf16), mv_ref[:, sl])
        acc = acc + _dot(oh.astype(bf16), wo_ref[sl, :])
    o_ref[...] = acc


def cross_prompt(x, mix_list, w_out_list, g, wq, mkb, mvb, wo, tm):
    m, d = x.shape
    row = lambda c: pl.BlockSpec((tm, c), lambda i: (i, 0))
    consts = [g.reshape(1, d), wq, mkb, mvb, wo]
    return pl.pallas_call(
        functools.partial(_cross_prompt_kernel, n_mix=len(mix_list)), grid=(m // tm,),
        in_specs=[row(d)] + [row(a.shape[1]) for a in mix_list] + [_full(w.shape) for w in w_out_list]
                 + [_full(c.shape) for c in consts],
        out_specs=row(d),
        out_shape=jax.ShapeDtypeStruct((m, d), f32),
        compiler_params=_cparams(1), name="cross_prompt",
    )(x, *mix_list, *w_out_list, *consts)


MEM_SPLIT = MEM_HEAD_DIM // LANES
MEM_ROWS = MEM_SPLIT * MEM_HEADS


def _split_channel_order():
    return jnp.arange(MEM_HEADS * MEM_HEAD_DIM).reshape(MEM_HEADS, MEM_SPLIT, LANES).transpose(1, 0, 2).reshape(-1)


def _mem_split_view(cache):
    d, b, m = cache.shape[:3]
    x = cache.reshape(d, b, m, MEM_HEADS, MEM_SPLIT, LANES)
    return jnp.transpose(x, (0, 1, 2, 4, 3, 5)).reshape(d, b, m, MEM_ROWS, LANES)


def _cross_sample_kernel(q_ref, mk_ref, mv_ref, o_ref):
    part = jnp.sum(mk_ref[...] * q_ref[...][None], axis=-1, keepdims=True)
    s = part
    for piece in range(1, MEM_SPLIT):
        s = s + jnp.roll(part, piece * MEM_HEADS, axis=1)
    s = s * (MEM_HEAD_DIM ** -0.5)
    p = jnp.exp(s - jnp.max(s, axis=0, keepdims=True))
    l = jnp.sum(p, axis=0)
    o_ref[...] = jnp.sum(p * mv_ref[...], axis=0) / l


def cross_sample(q_split, mem_k_split, mem_v_split, layer):
    b = q_split.shape[0]
    mlen = mem_k_split.shape[2]
    mem_spec = pl.BlockSpec((None, None, mlen, MEM_ROWS, LANES), lambda i: (layer, i, 0, 0, 0))
    row_spec = pl.BlockSpec((None, MEM_ROWS, LANES), lambda i: (i, 0, 0))
    out = pl.pallas_call(
        _cross_sample_kernel, grid=(b,),
        in_specs=[row_spec, mem_spec, mem_spec], out_specs=row_spec,
        out_shape=jax.ShapeDtypeStruct((b, MEM_ROWS, LANES), f32),
        compiler_params=_cparams(1), name="cross_sample",
    )(q_split.reshape(b, MEM_ROWS, LANES), mem_k_split, mem_v_split)
    return out.reshape(b, MEM_ROWS * LANES)


def _swiglu_kernel(x_ref, g_ref, wg_ref, wu_ref, wd_ref, gout_ref, o_ref, h_ref, acc_ref, *, norm_out):
    f = pl.program_id(1)

    @pl.when(f == 0)
    def _():
        h_ref[...] = _rms(x_ref[...], g_ref[...]).astype(bf16)
        acc_ref[...] = x_ref[...]

    hb = h_ref[...]
    a = _silu(_dot(hb, wg_ref[...])) * _dot(hb, wu_ref[...])
    acc_ref[...] += _dot(a.astype(bf16), wd_ref[...])

    @pl.when(f == pl.num_programs(1) - 1)
    def _():
        y = acc_ref[...]
        o_ref[...] = _rms(y, gout_ref[...]) if norm_out else y


def swiglu_block(x, g, w_gu, w_d, g_out, norm_out, tm, tf):
    m, d = x.shape
    nf = D_FF // tf
    return pl.pallas_call(
        functools.partial(_swiglu_kernel, norm_out=norm_out), grid=(m // tm, nf),
        in_specs=[pl.BlockSpec((tm, d), lambda i, f: (i, 0)), _full((1, d)),
                  pl.BlockSpec((d, tf), lambda i, f: (0, f)),
                  pl.BlockSpec((d, tf), lambda i, f: (0, f + nf)),
                  pl.BlockSpec((tf, d), lambda i, f: (f, 0)), _full((1, d))],
        out_specs=pl.BlockSpec((tm, d), lambda i, f: (i, 0)),
        out_shape=jax.ShapeDtypeStruct((m, d), f32),
        scratch_shapes=[pltpu.VMEM((tm, d), bf16), pltpu.VMEM((tm, d), f32)],
        compiler_params=_cparams(2), name="swiglu_block",
    )(x, g.reshape(1, d), w_gu, w_gu, w_d, g_out.reshape(1, d))


def _pool_select(w2, w4, w8, w16, cnt_of, lane):
    out = w16 / cnt_of(16)
    for win, acc in ((8, w8), (4, w4), (2, w2)):
        g = POOL_WINDOWS.index(win)
        out = jnp.where(lane < (g + 1) * POOL_GROUP, acc / cnt_of(win), out)
    return out


def _group_rmsnorm(y, g):
    half = SSM_INNER // SSM_GROUPS
    parts = [_rms(y[:, i * half:(i + 1) * half], g[:, i * half:(i + 1) * half])
             for i in range(SSM_GROUPS)]
    return jnp.concatenate(parts, axis=-1)


def _ssd_pool_prompt_kernel(z_ref, xbc_ref, dt_ref, u_ref, cw_ref, cb_ref, dtb_ref, a_ref, dsk_ref,
                            ng_ref, pw_ref, ps_ref,
                            y_ref, po_ref, st_ref,
                            ext_ref, pext_ref, xc_ref, s_ref, *, tm):
    i = pl.program_id(0)
    halo = 8
    phalo = 16

    @pl.when(i == 0)
    def _():
        ext_ref[0:halo, :] = jnp.zeros((halo, CONV_DIM), f32)
        pext_ref[0:phalo, :] = jnp.zeros((phalo, POOL_DIM), f32)
        s_ref[...] = jnp.zeros(s_ref.shape, f32)

    @pl.when(i > 0)
    def _():
        ext_ref[0:halo, :] = ext_ref[tm:tm + halo, :]
        pext_ref[0:phalo, :] = pext_ref[tm:tm + phalo, :]

    ext_ref[halo:halo + tm, :] = xbc_ref[...]
    pext_ref[phalo:phalo + tm, :] = u_ref[...]

    rp, lp = 64, 256
    for r0 in range(0, tm, rp):
        for c0 in range(0, CONV_DIM, lp):
            acc = jnp.broadcast_to(cb_ref[:, c0:c0 + lp], (rp, lp))
            for j in range(CONV_W):
                off = halo - (CONV_W - 1) + j + r0
                acc = acc + ext_ref[off:off + rp, c0:c0 + lp] * cw_ref[j:j + 1, c0:c0 + lp]
            xc_ref[r0:r0 + rp, c0:c0 + lp] = _silu(acc)

    lane_p = lax.broadcasted_iota(jnp.int32, (rp, POOL_DIM), 1)
    row_p = lax.broadcasted_iota(jnp.int32, (rp, POOL_DIM), 0)
    for r0 in range(0, tm, rp):
        u = u_ref[r0:r0 + rp, :]
        run = u
        sums = {}
        for back in range(1, max(POOL_WINDOWS)):
            run = run + pext_ref[phalo - back + r0:phalo - back + r0 + rp, :]
            if back + 1 in POOL_WINDOWS:
                sums[back + 1] = run
        pos1 = (row_p + (i * tm + r0 + 1)).astype(f32)
        pooled = _pool_select(sums[2], sums[4], sums[8], sums[16],
                              lambda win: jnp.minimum(float(win), pos1), lane_p) - u
        po_ref[r0:r0 + rp, :] = (_dot(pooled.astype(bf16), pw_ref[...]) * ps_ref[...]).astype(po_ref.dtype)

    cl = SSD_CHUNK
    r_io = lax.broadcasted_iota(jnp.int32, (cl, cl), 0)
    c_io = lax.broadcasted_iota(jnp.int32, (cl, cl), 1)
    ltri = (c_io <= r_io).astype(f32)
    causal = c_io <= r_io
    first_half = c_io < SSM_HEAD_DIM
    n_pairs = SSM_HEADS // 2
    heads_per_group = SSM_HEADS // SSM_GROUPS

    def chunk(c, carry):
        r0 = pl.multiple_of(c * cl, cl)
        xc = xc_ref[pl.ds(r0, cl), :]
        dt = _softplus(dt_ref[pl.ds(r0, cl), :] + dtb_ref[...])
        a_cum = _dot(ltri, dt * a_ref[...], precision=HIGHEST)
        a_cum_t = a_cum.T
        dt_t = dt.T
        to_end_t = jnp.exp(a_cum_t[:, cl - 1:cl] - a_cum_t) * dt_t
        b_t = [xc[:, SSM_INNER + g * SSM_STATE:SSM_INNER + (g + 1) * SSM_STATE].T
               for g in range(SSM_GROUPS)]
        cm = [xc[:, SSM_INNER + SSM_GROUPS * SSM_STATE + g * SSM_STATE:
                 SSM_INNER + SSM_GROUPS * SSM_STATE + (g + 1) * SSM_STATE].astype(bf16)
              for g in range(SSM_GROUPS)]
        scores = [_dot(cm[g], b_t[g].astype(bf16)) for g in range(SSM_GROUPS)]
        y_pairs = []
        for k in range(n_pairs):
            g = (2 * k) // heads_per_group
            xs_pair = xc[:, k * LANES:(k + 1) * LANES]
            xs_b = xs_pair.astype(bf16)
            yd, cs, eb = [], [], []
            for h in (2 * k, 2 * k + 1):
                colb = jnp.broadcast_to(a_cum[:, h:h + 1], (cl, cl))
                seg = colb - a_cum_t[h:h + 1, :]
                decay = jnp.exp(jnp.where(causal, seg, -jnp.inf))
                mh = scores[g] * decay * dt_t[h:h + 1, :]
                yd.append(_dot(mh.astype(bf16), xs_b))
                cs.append(_dot((b_t[g] * to_end_t[h:h + 1, :]).astype(bf16), xs_b))
                eb.append(jnp.exp(colb))
            e_pair = jnp.where(first_half, eb[0], eb[1])
            s_old = s_ref[k]
            y_off = _dot(cm[g], s_old.astype(bf16)) * e_pair
            y_pairs.append(jnp.where(first_half, yd[0], yd[1]) + y_off
                           + dsk_ref[:, k * LANES:(k + 1) * LANES] * xs_pair)
            s_ref[k] = s_old * e_pair[cl - 1:cl, :] + jnp.where(first_half, cs[0], cs[1])
        y = jnp.concatenate(y_pairs, axis=-1) * _silu(z_ref[pl.ds(r0, cl), :])
        y_ref[pl.ds(r0, cl), :] = _group_rmsnorm(y, ng_ref[...]).astype(y_ref.dtype)
        return carry

    lax.fori_loop(0, tm // cl, chunk, 0)

    @pl.when(i == pl.num_programs(0) - 1)
    def _():
        for k in range(n_pairs):
            st = s_ref[k].T
            st_ref[2 * k] = st[0:SSM_HEAD_DIM, :]
            st_ref[2 * k + 1] = st[SSM_HEAD_DIM:2 * SSM_HEAD_DIM, :]


def ssd_pool_prompt(z, xbc, dt, u, prm, tm):
    t = z.shape[0]
    row = lambda c: pl.BlockSpec((tm, c), lambda i: (i, 0))
    consts = [prm['conv_w'], prm['conv_b'], prm['dt_bias'], prm['a'], prm['dsk'], prm['ssm_norm_g'],
              prm['pool_w'], prm['pool_scale']]
    return pl.pallas_call(
        functools.partial(_ssd_pool_prompt_kernel, tm=tm), grid=(t // tm,),
        in_specs=[row(SSM_INNER), row(CONV_DIM), row(LANES), row(POOL_DIM)] + [_full(c.shape) for c in consts],
        out_specs=[row(SSM_INNER), row(POOL_DIM), _full((SSM_HEADS, SSM_HEAD_DIM, SSM_STATE))],
        out_shape=[jax.ShapeDtypeStruct((t, SSM_INNER), bf16), jax.ShapeDtypeStruct((t, POOL_DIM), bf16),
                   jax.ShapeDtypeStruct((SSM_HEADS, SSM_HEAD_DIM, SSM_STATE), f32)],
        scratch_shapes=[pltpu.VMEM((tm + 8, CONV_DIM), f32), pltpu.VMEM((tm + 16, POOL_DIM), f32),
                        pltpu.VMEM((tm, CONV_DIM), f32),
                        pltpu.VMEM((SSM_HEADS // 2, SSM_STATE, 2 * SSM_HEAD_DIM), f32)],
        compiler_params=_cparams(1), name="ssd_pool_prompt",
    )(z, xbc, dt, u, *consts)


def _col_tile(row):
    return jnp.broadcast_to(row, (LANES, LANES)).T


def _ssd_pool_sample_kernel(z_ref, xbc_ref, dt_ref, u_ref, sc_ref, ss_ref, sp_ref,
                            cw_ref, cb_ref, dtb_ref, a_ref, dsk_ref, ng_ref, pw_ref, ps_ref, ex_ref,
                            y_ref, po_ref, cn_ref, sn_ref, pn_ref, *, pos0):
    xrow = xbc_ref[...]
    acc = cb_ref[...] + xrow * cw_ref[CONV_W - 1:CONV_W, :]
    for j in range(CONV_W - 1):
        acc = acc + sc_ref[j:j + 1, :] * cw_ref[j:j + 1, :]
        if j > 0:
            cn_ref[j - 1:j, :] = sc_ref[j:j + 1, :]
    cn_ref[CONV_W - 2:CONV_W - 1, :] = xrow
    xc = _silu(acc)

    dt = _softplus(dt_ref[...] + dtb_ref[...])
    dta = dt * a_ref[...]
    both = jnp.concatenate([jnp.broadcast_to(dt, (8, LANES)), jnp.broadcast_to(dta, (8, LANES))], axis=0)
    both_x = _dot(both, ex_ref[...], precision=HIGHEST)
    dtx = both_x[0:1, :]
    dec_x = jnp.exp(both_x[8:9, :])
    xs = xc[:, 0:SSM_INNER]
    xdt = xs * dtx
    heads_per_group = SSM_HEADS // SSM_GROUPS
    y_pairs = []
    for k in range(SSM_HEADS // 2):
        g = (2 * k) // heads_per_group
        b_row = xc[:, SSM_INNER + g * SSM_STATE:SSM_INNER + (g + 1) * SSM_STATE]
        c_row = xc[:, SSM_INNER + SSM_GROUPS * SSM_STATE + g * SSM_STATE:
                   SSM_INNER + SSM_GROUPS * SSM_STATE + (g + 1) * SSM_STATE]
        sl = slice(k * LANES, (k + 1) * LANES)
        s_old = jnp.concatenate([ss_ref[2 * k], ss_ref[2 * k + 1]], axis=0)
        s_new = s_old * _col_tile(dec_x[:, sl]) + _col_tile(xdt[:, sl]) * b_row
        sn_ref[2 * k] = s_new[0:SSM_HEAD_DIM, :]
        sn_ref[2 * k + 1] = s_new[SSM_HEAD_DIM:, :]
        y_k = _dot_nt(jnp.broadcast_to(c_row, (8, SSM_STATE)), s_new, precision=HIGHEST)[0:1, :]
        y_pairs.append(y_k + dsk_ref[:, sl] * xs[:, sl])
    y = jnp.concatenate(y_pairs, axis=-1) * _silu(z_ref[...])
    y_ref[...] = _group_rmsnorm(y, ng_ref[...]).astype(y_ref.dtype)

    u = u_ref[...]
    prev = sp_ref[...]
    rowi = lax.broadcasted_iota(jnp.int32, prev.shape, 0)
    tail = lambda win: u + jnp.sum(jnp.where(rowi >= POOL_HIST - (win - 1), prev, 0.0), axis=0, keepdims=True)
    lane_p = lax.broadcasted_iota(jnp.int32, (1, POOL_DIM), 1)
    pooled = _pool_select(tail(2), tail(4), tail(8), tail(16),
                          lambda win: float(min(win, pos0 + 1)), lane_p) - u
    po = _dot(jnp.broadcast_to(pooled, (8, POOL_DIM)).astype(bf16), pw_ref[...])[0:1, :] * ps_ref[...]
    po_ref[...] = po.astype(po_ref.dtype)
    pn_ref[0:POOL_HIST - 1, :] = sp_ref[1:POOL_HIST, :]
    pn_ref[POOL_HIST - 1:POOL_HIST, :] = u


def ssd_pool_sample(z, xbc, dt, u, st_conv, st_ssm, st_pool, prm, pos0):
    b = z.shape[0]
    per_seq = lambda *shape: pl.BlockSpec((None,) + shape, lambda i: (i,) + (0,) * len(shape))
    consts = [prm['conv_w'], prm['conv_b'], prm['dt_bias'], prm['a'], prm['dsk'], prm['ssm_norm_g'],
              prm['pool_w'], prm['pool_scale'], prm['expand']]
    outs = pl.pallas_call(
        functools.partial(_ssd_pool_sample_kernel, pos0=pos0), grid=(b,),
        in_specs=[per_seq(1, SSM_INNER), per_seq(1, CONV_DIM), per_seq(1, LANES), per_seq(1, POOL_DIM),
                  per_seq(CONV_W - 1, CONV_DIM), per_seq(SSM_HEADS, SSM_HEAD_DIM, SSM_STATE),
                  per_seq(POOL_HIST, POOL_DIM)] + [_full(c.shape) for c in consts],
        out_specs=[per_seq(1, SSM_INNER), per_seq(1, POOL_DIM), per_seq(CONV_W - 1, CONV_DIM),
                   per_seq(SSM_HEADS, SSM_HEAD_DIM, SSM_STATE), per_seq(POOL_HIST, POOL_DIM)],
        out_shape=[jax.ShapeDtypeStruct((b, 1, SSM_INNER), f32), jax.ShapeDtypeStruct((b, 1, POOL_DIM), f32),
                   jax.ShapeDtypeStruct((b, CONV_W - 1, CONV_DIM), f32),
                   jax.ShapeDtypeStruct((b, SSM_HEADS, SSM_HEAD_DIM, SSM_STATE), f32),
                   jax.ShapeDtypeStruct((b, POOL_HIST, POOL_DIM), f32)],
        compiler_params=_cparams(1), name="ssd_pool_sample",
    )(z.reshape(b, 1, -1), xbc.reshape(b, 1, -1), dt.reshape(b, 1, -1), u.reshape(b, 1, -1),
      st_conv, st_ssm, st_pool, *consts)
    y, po, cn, sn, pn = outs
    return y.reshape(b, -1), po.reshape(b, -1), cn, sn, pn


def _topk_blocks(gate, n_past, axis):
    blk = lax.broadcasted_iota(jnp.int32, gate.shape, axis).astype(f32)
    g = jnp.where(blk < jnp.asarray(n_past, f32), gate, -jnp.inf)
    picks = []
    for _ in range(MOBA_TOP_K):
        m = jnp.max(g, axis=axis, keepdims=True)
        idx = jnp.min(jnp.where(g == m, blk, float(LANES)), axis=axis, keepdims=True)
        picks.append((idx, jnp.abs(m) < jnp.inf))
        g = jnp.where(blk == idx, -jnp.inf, g)
    return picks


def _moba_prompt_kernel(q_ref, k_ref, v_ref, o_ref,
                        km_ref, kb_ref, vt_ref, qst_ref, nmt_ref, acc_ref, sa_ref, sb_ref):
    i = pl.program_id(0)
    tq = MOBA_BLOCK
    scale = ATT_HEAD_DIM ** -0.5

    @pl.when(i == 0)
    def _():
        km_ref[...] = jnp.zeros(km_ref.shape, f32)

    k = k_ref[...]
    kb = k.astype(bf16)
    head = lambda h: slice(h * ATT_HEAD_DIM, (h + 1) * ATT_HEAD_DIM)
    for h in range(ATT_HEADS):
        kb_ref[i, h] = kb[:, head(h)]
    vt = v_ref[...].T.astype(bf16)
    ones_rows = jnp.ones((V_AUG_ROWS - ATT_HEAD_DIM, tq), bf16)
    for h in range(ATT_HEADS):
        vt_ref[i, h] = jnp.concatenate([vt[head(h), :], ones_rows], axis=0)
    qt = q_ref[...].T
    n_slots = km_ref.shape[0]
    blk_row = lax.broadcasted_iota(jnp.int32, (n_slots, tq), 0).astype(f32)
    key_io = lax.broadcasted_iota(jnp.int32, (tq, tq), 0)
    qry_io = lax.broadcasted_iota(jnp.int32, (tq, tq), 1)
    km = km_ref[...]
    m_own = []
    for h in range(ATT_HEADS):
        qth = qt[head(h), :]
        gate = _dot(km[:, head(h)], qth, precision=HIGHEST)
        sel = jnp.zeros((n_slots, tq), jnp.bool_)
        for idx, ok in _topk_blocks(gate, i, axis=0):
            sel = sel | ((blk_row == idx) & ok)
        nmt_ref[h] = jnp.where(sel, 0.0, MASKED)
        qst = (qth * (scale * LOG2E)).astype(bf16)
        qst_ref[h] = qst
        st = jnp.where(key_io <= qry_io, _dot(kb[:, head(h)], qst), MASKED)
        m = jnp.max(st, axis=0, keepdims=True)
        m_own.append(m)
        acc_ref[h] = _dot(vt_ref[i, h], jnp.exp2(st - m).astype(bf16))

    km_ref[pl.ds(i, 1), :] = jnp.mean(k, axis=0, keepdims=True)

    def scores(j, dst):
        for h in range(ATT_HEADS):
            dst[h] = _dot(kb_ref[j, h], qst_ref[h])

    def absorb(j, src, carry):
        new = []
        for h in range(ATT_HEADS):
            m_prev = carry[h]
            raw = src[h]
            bias = nmt_ref[h, pl.ds(j, 1), :]
            m_new = jnp.maximum(m_prev, jnp.max(raw, axis=0, keepdims=True) + bias)
            alpha = jnp.exp2(m_prev - m_new)
            p = jnp.exp2(raw - (m_new - bias))
            new.append(m_new)
            acc_ref[h] = acc_ref[h] * alpha + _dot(vt_ref[j, h], p.astype(bf16))
        return tuple(new)

    last = jnp.maximum(i - 1, 0)
    scores(0, sa_ref)

    def pair(t, carry):
        scores(jnp.minimum(2 * t + 1, last), sb_ref)
        carry = absorb(2 * t, sa_ref, carry)
        scores(jnp.minimum(2 * t + 2, last), sa_ref)
        return absorb(2 * t + 1, sb_ref, carry)

    carry = lax.fori_loop(0, i // 2, pair, tuple(m_own))
    carry = lax.cond(i % 2 == 1, lambda c: absorb(i - 1, sa_ref, c), lambda c: c, carry)
    out_t = jnp.concatenate(
        [acc_ref[h, 0:ATT_HEAD_DIM, :] / acc_ref[h, ATT_HEAD_DIM:ATT_HEAD_DIM + 1, :] for h in range(ATT_HEADS)],
        axis=0)
    o_ref[...] = out_t.T.astype(o_ref.dtype)


def moba_prompt(q, k, v):
    t = q.shape[0]
    tq = MOBA_BLOCK
    n_blk = t // tq
    n_slots = -(-n_blk // 8) * 8
    assert n_slots <= LANES
    tile = pl.BlockSpec((tq, ATT_DIM), lambda i: (i, 0))
    return pl.pallas_call(
        _moba_prompt_kernel, grid=(n_blk,),
        in_specs=[tile, tile, tile], out_specs=tile,
        out_shape=jax.ShapeDtypeStruct((t, ATT_DIM), bf16),
        scratch_shapes=[pltpu.VMEM((n_slots, ATT_DIM), f32),
                        pltpu.VMEM((n_blk, ATT_HEADS, tq, ATT_HEAD_DIM), bf16),
                        pltpu.VMEM((n_blk, ATT_HEADS, V_AUG_ROWS, tq), bf16),
                        pltpu.VMEM((ATT_HEADS, ATT_HEAD_DIM, tq), bf16),
                        pltpu.VMEM((ATT_HEADS, n_slots, tq), f32),
                        pltpu.VMEM((ATT_HEADS, V_AUG_ROWS, tq), f32),
                        pltpu.VMEM((ATT_HEADS, tq, tq), f32),
                        pltpu.VMEM((ATT_HEADS, tq, tq), f32)],
        compiler_params=_cparams(1), name="moba_prompt",
    )(q, k, v)


K_CHUNK_PAGES = 16
K_SLOTS = 4
PAGES_PER_BLOCK = MOBA_BLOCK // PAGE_SIZE
CHUNKS_BEFORE_FINISH = 2


def _moba_decode_kernel(pt_ref, q_ref, q8_ref, kn8_ref, vn8_ref, kc_ref, vc_ref, o_ref,
                        kbuf, vbuf, s_ref, p_ref, gate_ref, stash_ref, ids_ref, ksem, vsem, *, layer, n_pages):
    b = pl.program_id(0)
    n_seq = pl.num_programs(0) - 1
    n_chunks = n_pages // K_CHUNK_PAGES
    blocks_per_chunk = K_CHUNK_PAGES // PAGES_PER_BLOCK
    n_blocks = n_pages // PAGES_PER_BLOCK
    scale = ATT_HEAD_DIM ** -0.5

    def k_copy(seq, page_slot, slot, p):
        page = pt_ref[seq, page_slot]
        return pltpu.make_async_copy(kc_ref.at[layer, page], kbuf.at[slot, p], ksem.at[slot])

    def start_chunk(seq, c, slot):
        for p in range(K_CHUNK_PAGES):
            k_copy(seq, c * K_CHUNK_PAGES + p, slot, p).start()

    def wait_chunk(seq, c, slot):
        for p in range(K_CHUNK_PAGES):
            k_copy(seq, c * K_CHUNK_PAGES + p, slot, p).wait()

    n_picks = ATT_HEADS * MOBA_TOP_K

    def v_copy(h, r, half, page):
        return pltpu.make_async_copy(vc_ref.at[layer, page, pl.ds(h * ATT_HEAD_DIM, ATT_HEAD_DIM), :],
                                     vbuf.at[h * MOBA_TOP_K + r, half], vsem.at[0])

    def stream(chunks):
        lane8 = lax.broadcasted_iota(jnp.int32, (8, ATT_DIM), 1)
        row8 = lax.broadcasted_iota(jnp.int32, (8, ATT_DIM), 0)
        qblk_b = jnp.where(lane8 // ATT_HEAD_DIM == row8, q_ref[...], 0.0).astype(bf16)
        blk_lane = lax.broadcasted_iota(jnp.int32, (8, LANES), 1)
        gate = gate_ref[...]
        for c in chunks:
            slot = c % K_SLOTS
            ahead = c + K_SLOTS - 1
            if ahead < n_chunks:
                start_chunk(b, ahead, ahead % K_SLOTS)
            else:
                @pl.when(b + 1 < n_seq)
                def _():
                    start_chunk(b + 1, ahead - n_chunks, ahead % K_SLOTS)
            wait_chunk(b, c, slot)
            for t in range(blocks_per_chunk):
                blk = c * blocks_per_chunk + t
                halves = [_dot(qblk_b, kbuf[slot, t * PAGES_PER_BLOCK + half].astype(bf16))
                          for half in range(PAGES_PER_BLOCK)]
                for half in range(PAGES_PER_BLOCK):
                    s_ref[blk, :, half * PAGE_SIZE:(half + 1) * PAGE_SIZE] = halves[half]
                tot = jnp.sum(sum(halves), axis=-1, keepdims=True) * (1.0 / MOBA_BLOCK)
                gate = jnp.where(blk_lane == blk, tot, gate)
        gate_ref[...] = gate

    def choose_and_weigh():
        picks = _topk_blocks(gate_ref[...], n_blocks, axis=1)
        for h in range(ATT_HEADS):
            for r in range(MOBA_TOP_K):
                blk = picks[r][0][h, 0].astype(jnp.int32)
                ids_ref[h * MOBA_TOP_K + r] = blk
                for half in range(PAGES_PER_BLOCK):
                    page = pt_ref[b, blk * PAGES_PER_BLOCK + half]
                    ids_ref[n_picks + (h * MOBA_TOP_K + r) * PAGES_PER_BLOCK + half] = page
                    v_copy(h, r, half, page).start()
        blk_io = lax.broadcasted_iota(jnp.int32, (n_blocks, 8, MOBA_BLOCK), 0).astype(f32)
        seen = jnp.zeros((n_blocks, 8, MOBA_BLOCK), jnp.bool_)
        for idx, ok in picks:
            seen = seen | ((blk_io == idx[None]) & ok[None])
        s_all = jnp.where(seen, s_ref[...] * scale, -jnp.inf)
        s_own = jnp.sum(q8_ref[...] * kn8_ref[...], axis=-1, keepdims=True) * scale
        m = jnp.maximum(jnp.max(jnp.max(s_all, axis=0), axis=-1, keepdims=True), s_own)
        p_all = jnp.exp(s_all - m[None])
        p_own = jnp.exp(s_own - m)
        l = jnp.sum(jnp.sum(p_all, axis=0), axis=-1, keepdims=True) + p_own
        p_ref[...] = p_all
        stash_ref[:, 0:ATT_HEAD_DIM] = p_own * vn8_ref[...]
        stash_ref[:, ATT_HEAD_DIM:] = jnp.broadcast_to(l, (8, LANES - ATT_HEAD_DIM))

    def finish_previous():
        for h in range(ATT_HEADS):
            for r in range(MOBA_TOP_K):
                for half in range(PAGES_PER_BLOCK):
                    page = ids_ref[n_picks + (h * MOBA_TOP_K + r) * PAGES_PER_BLOCK + half]
                    v_copy(h, r, half, page).wait()
        row_hd = lax.broadcasted_iota(jnp.int32, (8, ATT_HEAD_DIM), 0)
        o = stash_ref[:, 0:ATT_HEAD_DIM]
        for h in range(ATT_HEADS):
            oh = jnp.zeros((8, ATT_HEAD_DIM), f32)
            for r in range(MOBA_TOP_K):
                pb = p_ref[ids_ref[h * MOBA_TOP_K + r]].astype(bf16)
                for half in range(PAGES_PER_BLOCK):
                    oh = oh + _dot_nt(pb[:, half * PAGE_SIZE:(half + 1) * PAGE_SIZE],
                                      vbuf[h * MOBA_TOP_K + r, half].astype(bf16))
            o = o + jnp.where(row_hd == h, oh, 0.0)
        o_ref[...] = o / stash_ref[:, ATT_HEAD_DIM:ATT_HEAD_DIM + 1]

    @pl.when(b == 0)
    def _():
        for c in range(K_SLOTS - 1):
            start_chunk(0, c, c)

    @pl.when(b < n_seq)
    def _():
        gate_ref[...] = jnp.zeros((8, LANES), f32)
        stream(range(0, CHUNKS_BEFORE_FINISH))

    @pl.when(b > 0)
    def _():
        finish_previous()

    @pl.when(b < n_seq)
    def _():
        stream(range(CHUNKS_BEFORE_FINISH, n_chunks))
        choose_and_weigh()


def moba_decode(page_table, q, k_new, v_new, cache_kt, cache_vt, layer):
    b = q.shape[0]
    n_pages = page_table.shape[1]
    n_blocks = n_pages // PAGES_PER_BLOCK
    assert n_blocks <= LANES and n_blocks >= MOBA_TOP_K
    heads8 = lambda a: jnp.pad(a.reshape(b, ATT_HEADS, ATT_HEAD_DIM), ((0, 0), (0, 8 - ATT_HEADS), (0, 0)))
    n_picks = ATT_HEADS * MOBA_TOP_K
    cur = lambda i, pt: (jnp.minimum(i, b - 1), 0, 0)
    prev = lambda i, pt: (jnp.maximum(i - 1, 0), 0, 0)
    per_head = pl.BlockSpec((None, 8, ATT_HEAD_DIM), cur)
    grid_spec = pltpu.PrefetchScalarGridSpec(
        num_scalar_prefetch=1, grid=(b + 1,),
        in_specs=[pl.BlockSpec((None, 1, ATT_DIM), cur), per_head, per_head, per_head,
                  pl.BlockSpec(memory_space=pl.ANY), pl.BlockSpec(memory_space=pl.ANY)],
        out_specs=pl.BlockSpec((None, 8, ATT_HEAD_DIM), prev),
        scratch_shapes=[pltpu.VMEM((K_SLOTS, K_CHUNK_PAGES, ATT_DIM, PAGE_SIZE), f32),
                        pltpu.VMEM((n_picks, PAGES_PER_BLOCK, ATT_HEAD_DIM, PAGE_SIZE), f32),
                        pltpu.VMEM((n_blocks, 8, MOBA_BLOCK), f32),
                        pltpu.VMEM((n_blocks, 8, MOBA_BLOCK), f32),
                        pltpu.VMEM((8, LANES), f32),
                        pltpu.VMEM((8, LANES), f32),
                        pltpu.SMEM((n_picks * (1 + PAGES_PER_BLOCK),), jnp.int32),
                        pltpu.SemaphoreType.DMA((K_SLOTS,)),
                        pltpu.SemaphoreType.DMA((1,))])
    out = pl.pallas_call(
        functools.partial(_moba_decode_kernel, layer=layer, n_pages=n_pages),
        grid_spec=grid_spec,
        out_shape=jax.ShapeDtypeStruct((b, 8, ATT_HEAD_DIM), f32),
        compiler_params=_cparams(1), name="moba_decode",
    )(page_table, q.reshape(b, 1, ATT_DIM), heads8(q), heads8(k_new), heads8(v_new), cache_kt, cache_vt)
    return out[:, :ATT_HEADS].reshape(b, ATT_DIM)


def _layer_params(l, w_in, conv_w, conv_b, dt_bias, a_log, d_skip, ssm_norm_g, pool_w, pool_scale, w_out):
    o_xbc = SSM_INNER
    o_dt = o_xbc + CONV_DIM
    o_pool = o_dt + SSM_HEADS
    o_q = o_pool + POOL_DIM
    o_k = o_q + ATT_DIM
    o_v = o_k + ATT_DIM
    wl = w_in[l]
    pad_heads = lambda v: jnp.pad(v.astype(f32), (0, LANES - SSM_HEADS)).reshape(1, LANES)
    w_dt = jnp.pad(wl[:, o_dt:o_pool], ((0, 0), (0, LANES - SSM_HEADS)))
    w_split = [wl[:, :o_xbc], wl[:, o_xbc:o_dt], w_dt, wl[:, o_pool:o_q], wl[:, o_q:o_k],
               wl[:, o_k:o_v], wl[:, o_v:]]
    pw = jnp.zeros((POOL_DIM, POOL_DIM), f32)
    for g in range(len(POOL_WINDOWS)):
        pw = pw.at[g * POOL_GROUP:(g + 1) * POOL_GROUP, g * POOL_GROUP:(g + 1) * POOL_GROUP].set(pool_w[l, g])
    expand = (jnp.arange(LANES)[:, None] == (jnp.arange(SSM_INNER)[None, :] // SSM_HEAD_DIM)).astype(f32)
    wo = w_out[l].astype(bf16)
    return {
        'w_in': [w.astype(bf16) for w in w_split],
        'conv_w': conv_w[l], 'conv_b': conv_b[l].reshape(1, CONV_DIM),
        'dt_bias': pad_heads(dt_bias[l]), 'a': pad_heads(-jnp.exp(a_log[l].astype(f32))),
        'dsk': jnp.repeat(d_skip[l].astype(f32), SSM_HEAD_DIM).reshape(1, SSM_INNER),
        'ssm_norm_g': ssm_norm_g[l].reshape(1, SSM_INNER),
        'pool_w': pw.astype(bf16), 'pool_scale': pool_scale[l].reshape(1, POOL_DIM),
        'expand': expand,
        'w_out': [wo[:SSM_INNER], wo[SSM_INNER:SSM_INNER + POOL_DIM], wo[SSM_INNER + POOL_DIM:]],
    }


IN_PROJ_OUTS = [(i, f32) for i in range(7)]


def _pages_channel_major(cache):
    d, n_phys = cache.shape[:2]
    return jnp.transpose(cache, (0, 1, 3, 4, 2)).reshape(d, n_phys, ATT_DIM, PAGE_SIZE)


def kernel(x_prompt, x_sample, cache_moba_k, cache_moba_v, state_ssm, state_conv, state_pool, cache_mem_k, cache_mem_v, page_table, mem_prompt, norm_mix_g, w_in, conv_w, conv_b, dt_bias, a_log, d_skip, ssm_norm_g, pool_w, pool_scale, w_out, norm_cross_g, norm_mem_g, w_mem_q, w_mem_kv, w_mem_o, norm_ffn_g, w_gate_up, w_down, final_norm_g):
    depth = w_in.shape[0]
    bp, t, d = x_prompt.shape
    bs = x_sample.shape[0]
    assert bp == 1 and x_sample.shape[1] == 1
    past_len = page_table.shape[1] * PAGE_SIZE
    cache_kt = _pages_channel_major(cache_moba_k)
    cache_vt = _pages_channel_major(cache_moba_v)
    mem_k_split = _mem_split_view(cache_mem_k)
    mem_v_split = _mem_split_view(cache_mem_v)
    split_order = _split_channel_order()
    mem_len = mem_prompt.shape[1]
    xp = x_prompt.reshape(t, d)
    xs = x_sample.reshape(bs, d)
    mem = mem_prompt.reshape(mem_len, d)
    tm_p, tm_s = 512, bs
    outs = {n: [] for n in ('kp', 'vp', 'ks', 'vs', 'sp', 'ss', 'cp', 'cs', 'pp', 'ps', 'mk', 'mv')}
    for l in range(depth):
        prm = _layer_params(l, w_in, conv_w, conv_b, dt_bias, a_log, d_skip, ssm_norm_g, pool_w, pool_scale, w_out)
        wq = w_mem_q[l].astype(bf16)
        wkv = w_mem_kv[l].astype(bf16)
        wo_mem = w_mem_o[l].astype(bf16)
        wgu = w_gate_up[l].astype(bf16)
        wdn = w_down[l].astype(bf16)
        half = MEM_HEADS * MEM_HEAD_DIM
        last_layer = l == depth - 1

        mk, mv, mkb, mvb = norm_matmul(mem, norm_mem_g[l], [wkv[:, :half], wkv[:, half:]],
                                       [(0, f32), (1, f32), (0, bf16), (1, bf16)], tm=mem_len)
        z, xbc, dtr, u, q, k, v = norm_matmul(xp, norm_mix_g[l], prm['w_in'], IN_PROJ_OUTS, tm=tm_p)
        y, po, s_new = ssd_pool_prompt(z, xbc, dtr, u, prm, tm=tm_p)
        att = moba_prompt(q, k, v)
        xp = cross_prompt(xp, [y, po, att], prm['w_out'], norm_cross_g[l], wq, mkb, mvb, wo_mem, tm=tm_p)
        xp = swiglu_block(xp, norm_ffn_g[l], wgu, wdn, final_norm_g, norm_out=last_layer, tm=512, tf=D_FF // 2)
        outs['kp'].append(k.reshape(1, t, ATT_HEADS, ATT_HEAD_DIM))
        outs['vp'].append(v.reshape(1, t, ATT_HEADS, ATT_HEAD_DIM))
        outs['sp'].append(s_new[None])
        outs['cp'].append(xbc[t - (CONV_W - 1):][None])
        outs['pp'].append(u[t - POOL_HIST:][None])
        outs['mk'].append(mk.reshape(1, mem_len, MEM_HEADS, MEM_HEAD_DIM))
        outs['mv'].append(mv.reshape(1, mem_len, MEM_HEADS, MEM_HEAD_DIM))

        z, xbc, dtr, u, q, k, v = norm_matmul(xs, norm_mix_g[l], prm['w_in'], IN_PROJ_OUTS, tm=tm_s)
        y, po, c_new, s_new, p_new = ssd_pool_sample(z, xbc, dtr, u, state_conv[l], state_ssm[l],
                                                     state_pool[l], prm, pos0=past_len)
        att = moba_decode(page_table, q, k, v, cache_kt, cache_vt, layer=l)
        xs = matmul_residual(xs, [y, po, att], prm['w_out'], tm=tm_s)
        (qc,) = norm_matmul(xs, norm_cross_g[l], [wq[:, split_order]], [(0, f32)], tm=tm_s)
        oc = cross_sample(qc, mem_k_split, mem_v_split, layer=l)
        xs = matmul_residual(xs, [oc], [wo_mem[split_order, :]], tm=tm_s)
        xs = swiglu_block(xs, norm_ffn_g[l], wgu, wdn, final_norm_g, norm_out=last_layer, tm=tm_s, tf=256)
        outs['ks'].append(k.reshape(bs, 1, ATT_HEADS, ATT_HEAD_DIM))
        outs['vs'].append(v.reshape(bs, 1, ATT_HEADS, ATT_HEAD_DIM))
        outs['ss'].append(s_new)
        outs['cs'].append(c_new)
        outs['ps'].append(p_new)

    y_prompt = xp.reshape(1, t, d)
    y_sample = xs.reshape(bs, 1, d)
    st = lambda n: jnp.stack(outs[n])
    return (y_prompt, y_sample, st('kp'), st('vp'), st('ks'), st('vs'), st('sp'), st('ss'),
            st('cp'), st('cs'), st('pp'), st('ps'), st('mk'), st('mv'))
```

```python
import functools

import jax
import jax.numpy as jnp
from jax import lax
from jax.experimental import pallas as pl
from jax.experimental.pallas import tpu as pltpu

f32 = jnp.float32
bf16 = jnp.bfloat16
HIGHEST = lax.Precision.HIGHEST

D_MODEL = 1024
SSM_INNER = 512
SSM_HEAD_DIM = 64
SSM_HEADS = 8
SSM_GROUPS = 2
SSM_STATE = 128
CONV_W = 4
CONV_DIM = SSM_INNER + 2 * SSM_GROUPS * SSM_STATE
SSD_CHUNK = 128
POOL_DIM = 256
POOL_WINDOWS = (2, 4, 8, 16)
POOL_GROUP = 64
POOL_HIST = 15
ATT_DIM = 256
ATT_HEAD_DIM = 64
ATT_HEADS = 4
MOBA_BLOCK = 256
MOBA_TOP_K = 3
PAGE_SIZE = 128
MEM_HEADS = 4
MEM_HEAD_DIM = 256
D_FF = 2816
RMS_EPS = 1e-6
LANES = 128
CONV_HALO = 8
POOL_HALO = 16
MASKED = -1e30
LOG2E = 1.4426950408889634
BF16_SUBLANES = 16
V_AUG_ROWS = ATT_HEAD_DIM + BF16_SUBLANES
VMEM_LIMIT = 56 * 1024 * 1024


def _cparams(n_axes):
    return pltpu.CompilerParams(dimension_semantics=("arbitrary",) * n_axes,
                                vmem_limit_bytes=VMEM_LIMIT)


def _rms(x, g):
    ms = jnp.mean(x * x, axis=-1, keepdims=True)
    return x * lax.rsqrt(ms + RMS_EPS) * g


def _dot(a, b, **kw):
    return jnp.dot(a, b, preferred_element_type=f32, **kw)


def _dot_nt(a, b, **kw):
    return lax.dot_general(a, b, (((1,), (1,)), ((), ())), preferred_element_type=f32, **kw)


def _silu(x):
    return x * jax.nn.sigmoid(x)


def _softplus(x):
    return jnp.maximum(x, 0.0) + jnp.log1p(jnp.exp(-jnp.abs(x)))


def _full(shape):
    return pl.BlockSpec(shape, lambda *_: (0,) * len(shape))


def _norm_mm_kernel(*refs, n_w, out_w):
    x_ref, g_ref = refs[0], refs[1]
    w_refs = refs[2:2 + n_w]
    o_refs = refs[2 + n_w:]
    hb = _rms(x_ref[...], g_ref[...]).astype(bf16)
    res = [_dot(hb, w[...]) for w in w_refs]
    for o_ref, wi in zip(o_refs, out_w):
        o_ref[...] = res[wi].astype(o_ref.dtype)


def norm_matmul(x, g, ws, outs, tm):
    m, d = x.shape
    in_specs = [pl.BlockSpec((tm, d), lambda i: (i, 0)), _full((1, d))]
    in_specs += [_full(w.shape) for w in ws]
    out_shape = [jax.ShapeDtypeStruct((m, ws[wi].shape[1]), dt) for wi, dt in outs]
    out_specs = [pl.BlockSpec((tm, ws[wi].shape[1]), lambda i: (i, 0)) for wi, _ in outs]
    return pl.pallas_call(
        functools.partial(_norm_mm_kernel, n_w=len(ws), out_w=tuple(wi for wi, _ in outs)),
        grid=(m // tm,), in_specs=in_specs, out_specs=out_specs, out_shape=out_shape,
        compiler_params=_cparams(1), name="norm_matmul",
    )(x, g.reshape(1, d), *ws)


def _mm_res_kernel(*refs, n_a):
    res_ref = refs[0]
    a_refs = refs[1:1 + n_a]
    w_refs = refs[1 + n_a:1 + 2 * n_a]
    o_ref = refs[-1]
    acc = res_ref[...]
    for a, w in zip(a_refs, w_refs):
        acc = acc + _dot(a[...].astype(bf16), w[...])
    o_ref[...] = acc


def matmul_residual(res, a_list, w_list, tm):
    m, d = res.shape
    in_specs = [pl.BlockSpec((tm, d), lambda i: (i, 0))]
    in_specs += [pl.BlockSpec((tm, a.shape[1]), lambda i: (i, 0)) for a in a_list]
    in_specs += [_full(w.shape) for w in w_list]
    return pl.pallas_call(
        functools.partial(_mm_res_kernel, n_a=len(a_list)),
        grid=(m // tm,), in_specs=in_specs,
        out_specs=pl.BlockSpec((tm, d), lambda i: (i, 0)),
        out_shape=jax.ShapeDtypeStruct((m, d), f32),
        compiler_params=_cparams(1), name="matmul_residual",
    )(res, *a_list, *w_list)


def _cross_prompt_kernel(*refs, n_mix):
    x_ref = refs[0]
    a_refs = refs[1:1 + n_mix]
    w_refs = refs[1 + n_mix:1 + 2 * n_mix]
    g_ref, wq_ref, mk_ref, mv_ref, wo_ref, o_ref = refs[1 + 2 * n_mix:]
    x = x_ref[...]
    for a, w in zip(a_refs, w_refs):
        x = x + _dot(a[...].astype(bf16), w[...])
    if n_mix:
        o_ref[...] = x
        x = o_ref[...]
    hb = _rms(x, g_ref[...]).astype(bf16)
    q = _dot(hb, wq_ref[...])
    acc = x
    for h in range(MEM_HEADS):
        sl = slice(h * MEM_HEAD_DIM, (h + 1) * MEM_HEAD_DIM)
        s = _dot_nt(q[:, sl].astype(bf16), mk_ref[:, sl]) * (MEM_HEAD_DIM ** -0.5)
        p = jnp.exp(s - jnp.max(s, axis=-1, keepdims=True))
        p = p / jnp.sum(p, axis=-1, keepdims=True)
        oh = _dot(p.astype(bf16), mv_ref[:, sl])
        acc = acc + _dot(oh.astype(bf16), wo_ref[sl, :])
    o_ref[...] = acc


def cross_prompt(x, mix_list, w_out_list, g, wq, mkb, mvb, wo, tm):
    m, d = x.shape
    row = lambda c: pl.BlockSpec((tm, c), lambda i: (i, 0))
    consts = [g.reshape(1, d), wq, mkb, mvb, wo]
    return pl.pallas_call(
        functools.partial(_cross_prompt_kernel, n_mix=len(mix_list)), grid=(m // tm,),
        in_specs=[row(d)] + [row(a.shape[1]) for a in mix_list] + [_full(w.shape) for w in w_out_list]
                 + [_full(c.shape) for c in consts],
        out_specs=row(d),
        out_shape=jax.ShapeDtypeStruct((m, d), f32),
        compiler_params=_cparams(1), name="cross_prompt",
    )(x, *mix_list, *w_out_list, *consts)


MEM_SPLIT = MEM_HEAD_DIM // LANES
MEM_ROWS = MEM_SPLIT * MEM_HEADS


def _split_channel_order():
    return jnp.arange(MEM_HEADS * MEM_HEAD_DIM).reshape(MEM_HEADS, MEM_SPLIT, LANES).transpose(1, 0, 2).reshape(-1)


def _mem_split_view(cache):
    d, b, m = cache.shape[:3]
    x = cache.reshape(d, b, m, MEM_HEADS, MEM_SPLIT, LANES)
    return jnp.transpose(x, (0, 1, 2, 4, 3, 5)).reshape(d, b, m, MEM_ROWS, LANES)


def _cross_sample_kernel(q_ref, mk_ref, mv_ref, o_ref):
    part = jnp.sum(mk_ref[...] * q_ref[...][None], axis=-1, keepdims=True)
    s = part
    for piece in range(1, MEM_SPLIT):
        s = s + jnp.roll(part, piece * MEM_HEADS, axis=1)
    s = s * (MEM_HEAD_DIM ** -0.5)
    p = jnp.exp(s - jnp.max(s, axis=0, keepdims=True))
    l = jnp.sum(p, axis=0)
    o_ref[...] = jnp.sum(p * mv_ref[...], axis=0) / l


def cross_sample(q_split, mem_k_split, mem_v_split, layer):
    b = q_split.shape[0]
    mlen = mem_k_split.shape[2]
    mem_spec = pl.BlockSpec((None, None, mlen, MEM_ROWS, LANES), lambda i: (layer, i, 0, 0, 0))
    row_spec = pl.BlockSpec((None, MEM_ROWS, LANES), lambda i: (i, 0, 0))
    out = pl.pallas_call(
        _cross_sample_kernel, grid=(b,),
        in_specs=[row_spec, mem_spec, mem_spec], out_specs=row_spec,
        out_shape=jax.ShapeDtypeStruct((b, MEM_ROWS, LANES), f32),
        compiler_params=_cparams(1), name="cross_sample",
    )(q_split.reshape(b, MEM_ROWS, LANES), mem_k_split, mem_v_split)
    return out.reshape(b, MEM_ROWS * LANES)


def _swiglu_kernel(x_ref, g_ref, wg_ref, wu_ref, wd_ref, gout_ref, o_ref, h_ref, acc_ref, *, norm_out):
    f = pl.program_id(1)

    @pl.when(f == 0)
    def _():
        h_ref[...] = _rms(x_ref[...], g_ref[...]).astype(bf16)
        acc_ref[...] = x_ref[...]

    hb = h_ref[...]
    a = _silu(_dot(hb, wg_ref[...])) * _dot(hb, wu_ref[...])
    acc_ref[...] += _dot(a.astype(bf16), wd_ref[...])

    @pl.when(f == pl.num_programs(1) - 1)
    def _():
        y = acc_ref[...]
        o_ref[...] = _rms(y, gout_ref[...]) if norm_out else y


def swiglu_block(x, g, w_gu, w_d, g_out, norm_out, tm, tf):
    m, d = x.shape
    nf = D_FF // tf
    return pl.pallas_call(
        functools.partial(_swiglu_kernel, norm_out=norm_out), grid=(m // tm, nf),
        in_specs=[pl.BlockSpec((tm, d), lambda i, f: (i, 0)), _full((1, d)),
                  pl.BlockSpec((d, tf), lambda i, f: (0, f)),
                  pl.BlockSpec((d, tf), lambda i, f: (0, f + nf)),
                  pl.BlockSpec((tf, d), lambda i, f: (f, 0)), _full((1, d))],
        out_specs=pl.BlockSpec((tm, d), lambda i, f: (i, 0)),
        out_shape=jax.ShapeDtypeStruct((m, d), f32),
        scratch_shapes=[pltpu.VMEM((tm, d), bf16), pltpu.VMEM((tm, d), f32)],
        compiler_params=_cparams(2), name="swiglu_block",
    )(x, g.reshape(1, d), w_gu, w_gu, w_d, g_out.reshape(1, d))


def _pool_select(w2, w4, w8, w16, cnt_of, lane):
    out = w16 / cnt_of(16)
    for win, acc in ((8, w8), (4, w4), (2, w2)):
        g = POOL_WINDOWS.index(win)
        out = jnp.where(lane < (g + 1) * POOL_GROUP, acc / cnt_of(win), out)
    return out


def _group_rmsnorm(y, g):
    half = SSM_INNER // SSM_GROUPS
    parts = [_rms(y[:, i * half:(i + 1) * half], g[:, i * half:(i + 1) * half])
             for i in range(SSM_GROUPS)]
    return jnp.concatenate(parts, axis=-1)


def _ssd_pool_prompt_kernel(z_ref, xbc_ref, dt_ref, u_ref, cw_ref, cb_ref, dtb_ref, a_ref, dsk_ref,
                            ng_ref, pw_ref, ps_ref,
                            y_ref, po_ref, st_ref,
                            ext_ref, pext_ref, xc_ref, s_ref, *, tm):
    i = pl.program_id(0)
    halo = CONV_HALO
    phalo = POOL_HALO

    @pl.when(i == 0)
    def _():
        ext_ref[0:halo, :] = jnp.zeros((halo, CONV_DIM), f32)
        pext_ref[0:phalo, :] = jnp.zeros((phalo, POOL_DIM), f32)
        s_ref[...] = jnp.zeros(s_ref.shape, f32)

    @pl.when(i > 0)
    def _():
        ext_ref[0:halo, :] = ext_ref[tm:tm + halo, :]
        pext_ref[0:phalo, :] = pext_ref[tm:tm + phalo, :]

    ext_ref[halo:halo + tm, :] = xbc_ref[...]
    pext_ref[phalo:phalo + tm, :] = u_ref[...]

    rp, lp = 64, 256
    for r0 in range(0, tm, rp):
        for c0 in range(0, CONV_DIM, lp):
            acc = jnp.broadcast_to(cb_ref[:, c0:c0 + lp], (rp, lp))
            for j in range(CONV_W):
                off = halo - (CONV_W - 1) + j + r0
                acc = acc + ext_ref[off:off + rp, c0:c0 + lp] * cw_ref[j:j + 1, c0:c0 + lp]
            xc_ref[r0:r0 + rp, c0:c0 + lp] = _silu(acc)

    lane_p = lax.broadcasted_iota(jnp.int32, (rp, POOL_DIM), 1)
    row_p = lax.broadcasted_iota(jnp.int32, (rp, POOL_DIM), 0)
    for r0 in range(0, tm, rp):
        u = pext_ref[phalo + r0:phalo + r0 + rp, :]
        run = u
        sums = {}
        for back in range(1, max(POOL_WINDOWS)):
            run = run + pext_ref[phalo - back + r0:phalo - back + r0 + rp, :]
            if back + 1 in POOL_WINDOWS:
                sums[back + 1] = run
        pos1 = (row_p + (i * tm + r0 + 1)).astype(f32)
        pooled = _pool_select(sums[2], sums[4], sums[8], sums[16],
                              lambda win: jnp.minimum(float(win), pos1), lane_p) - u
        po_ref[r0:r0 + rp, :] = (_dot(pooled.astype(bf16), pw_ref[...]) * ps_ref[...]).astype(po_ref.dtype)

    cl = SSD_CHUNK
    r_io = lax.broadcasted_iota(jnp.int32, (cl, cl), 0)
    c_io = lax.broadcasted_iota(jnp.int32, (cl, cl), 1)
    ltri = (c_io <= r_io).astype(f32)
    causal = c_io <= r_io
    first_half = c_io < SSM_HEAD_DIM
    n_pairs = SSM_HEADS // 2
    heads_per_group = SSM_HEADS // SSM_GROUPS

    def chunk(c, carry):
        r0 = pl.multiple_of(c * cl, cl)
        xc = xc_ref[pl.ds(r0, cl), :]
        dt = _softplus(dt_ref[pl.ds(r0, cl), :] + dtb_ref[...])
        a_cum = _dot(ltri, dt * a_ref[...], precision=HIGHEST)
        a_cum_t = a_cum.T
        dt_t = dt.T
        to_end_t = jnp.exp(a_cum_t[:, cl - 1:cl] - a_cum_t) * dt_t
        b_t = [xc[:, SSM_INNER + g * SSM_STATE:SSM_INNER + (g + 1) * SSM_STATE].T
               for g in range(SSM_GROUPS)]
        cm = [xc[:, SSM_INNER + SSM_GROUPS * SSM_STATE + g * SSM_STATE:
                 SSM_INNER + SSM_GROUPS * SSM_STATE + (g + 1) * SSM_STATE].astype(bf16)
              for g in range(SSM_GROUPS)]
        scores = [_dot(cm[g], b_t[g].astype(bf16)) for g in range(SSM_GROUPS)]
        y_pairs = []
        for k in range(n_pairs):
            g = (2 * k) // heads_per_group
            xs_pair = xc[:, k * LANES:(k + 1) * LANES]
            xs_b = xs_pair.astype(bf16)
            yd, cs, eb = [], [], []
            for h in (2 * k, 2 * k + 1):
                colb = jnp.broadcast_to(a_cum[:, h:h + 1], (cl, cl))
                seg = colb - a_cum_t[h:h + 1, :]
                decay = jnp.exp(jnp.where(causal, seg, -jnp.inf))
                mh = scores[g] * decay * dt_t[h:h + 1, :]
                yd.append(_dot(mh.astype(bf16), xs_b))
                cs.append(_dot((b_t[g] * to_end_t[h:h + 1, :]).astype(bf16), xs_b))
                eb.append(jnp.exp(colb))
            e_pair = jnp.where(first_half, eb[0], eb[1])
            s_old = s_ref[k]
            y_off = _dot(cm[g], s_old.astype(bf16)) * e_pair
            y_pairs.append(jnp.where(first_half, yd[0], yd[1]) + y_off
                           + dsk_ref[:, k * LANES:(k + 1) * LANES] * xs_pair)
            s_ref[k] = s_old * e_pair[cl - 1:cl, :] + jnp.where(first_half, cs[0], cs[1])
        y = jnp.concatenate(y_pairs, axis=-1) * _silu(z_ref[pl.ds(r0, cl), :])
        y_ref[pl.ds(r0, cl), :] = _group_rmsnorm(y, ng_ref[...]).astype(y_ref.dtype)
        return carry

    lax.fori_loop(0, tm // cl, chunk, 0, unroll=True)

    @pl.when(i == pl.num_programs(0) - 1)
    def _():
        for k in range(n_pairs):
            st = s_ref[k].T
            st_ref[2 * k] = st[0:SSM_HEAD_DIM, :]
            st_ref[2 * k + 1] = st[SSM_HEAD_DIM:2 * SSM_HEAD_DIM, :]


def ssd_pool_prompt(z, xbc, dt, u, prm, tm):
    t = z.shape[0]
    row = lambda c: pl.BlockSpec((tm, c), lambda i: (i, 0))
    consts = [prm['conv_w'], prm['conv_b'], prm['dt_bias'], prm['a'], prm['dsk'], prm['ssm_norm_g'],
              prm['pool_w'], prm['pool_scale']]
    return pl.pallas_call(
        functools.partial(_ssd_pool_prompt_kernel, tm=tm), grid=(t // tm,),
        in_specs=[row(SSM_INNER), row(CONV_DIM), row(LANES), row(POOL_DIM)] + [_full(c.shape) for c in consts],
        out_specs=[row(SSM_INNER), row(POOL_DIM), _full((SSM_HEADS, SSM_HEAD_DIM, SSM_STATE))],
        out_shape=[jax.ShapeDtypeStruct((t, SSM_INNER), bf16), jax.ShapeDtypeStruct((t, POOL_DIM), bf16),
                   jax.ShapeDtypeStruct((SSM_HEADS, SSM_HEAD_DIM, SSM_STATE), f32)],
        scratch_shapes=[pltpu.VMEM((tm + CONV_HALO, CONV_DIM), f32), pltpu.VMEM((tm + POOL_HALO, POOL_DIM), f32),
                        pltpu.VMEM((tm, CONV_DIM), f32),
                        pltpu.VMEM((SSM_HEADS // 2, SSM_STATE, 2 * SSM_HEAD_DIM), f32)],
        compiler_params=_cparams(1), name="ssd_pool_prompt",
    )(z, xbc, dt, u, *consts)


def _col_tile(row):
    return jnp.broadcast_to(row, (LANES, LANES)).T


def _ssd_pool_sample_kernel(z_ref, xbc_ref, dt_ref, u_ref, sc_ref, ss_ref, sp_ref,
                            cw_ref, cb_ref, dtb_ref, a_ref, dsk_ref, ng_ref, pw_ref, ps_ref, ex_ref,
                            y_ref, po_ref, cn_ref, sn_ref, pn_ref, *, pos0):
    xrow = xbc_ref[...]
    acc = cb_ref[...] + xrow * cw_ref[CONV_W - 1:CONV_W, :]
    for j in range(CONV_W - 1):
        acc = acc + sc_ref[j:j + 1, :] * cw_ref[j:j + 1, :]
        if j > 0:
            cn_ref[j - 1:j, :] = sc_ref[j:j + 1, :]
    cn_ref[CONV_W - 2:CONV_W - 1, :] = xrow
    xc = _silu(acc)

    dt = _softplus(dt_ref[...] + dtb_ref[...])
    dta = dt * a_ref[...]
    both = jnp.concatenate([jnp.broadcast_to(dt, (8, LANES)), jnp.broadcast_to(dta, (8, LANES))], axis=0)
    both_x = _dot(both, ex_ref[...], precision=HIGHEST)
    dtx = both_x[0:1, :]
    dec_x = jnp.exp(both_x[8:9, :])
    xs = xc[:, 0:SSM_INNER]
    xdt = xs * dtx
    heads_per_group = SSM_HEADS // SSM_GROUPS
    y_pairs = []
    for k in range(SSM_HEADS // 2):
        g = (2 * k) // heads_per_group
        b_row = xc[:, SSM_INNER + g * SSM_STATE:SSM_INNER + (g + 1) * SSM_STATE]
        c_row = xc[:, SSM_INNER + SSM_GROUPS * SSM_STATE + g * SSM_STATE:
                   SSM_INNER + SSM_GROUPS * SSM_STATE + (g + 1) * SSM_STATE]
        sl = slice(k * LANES, (k + 1) * LANES)
        s_old = jnp.concatenate([ss_ref[2 * k], ss_ref[2 * k + 1]], axis=0)
        s_new = s_old * _col_tile(dec_x[:, sl]) + _col_tile(xdt[:, sl]) * b_row
        sn_ref[2 * k] = s_new[0:SSM_HEAD_DIM, :]
        sn_ref[2 * k + 1] = s_new[SSM_HEAD_DIM:, :]
        y_k = _dot_nt(jnp.broadcast_to(c_row, (8, SSM_STATE)), s_new, precision=HIGHEST)[0:1, :]
        y_pairs.append(y_k + dsk_ref[:, sl] * xs[:, sl])
    y = jnp.concatenate(y_pairs, axis=-1) * _silu(z_ref[...])
    y_ref[...] = _group_rmsnorm(y, ng_ref[...]).astype(y_ref.dtype)

    u = u_ref[...]
    prev = sp_ref[...]
    rowi = lax.broadcasted_iota(jnp.int32, prev.shape, 0)
    tail = lambda win: u + jnp.sum(jnp.where(rowi >= POOL_HIST - (win - 1), prev, 0.0), axis=0, keepdims=True)
    lane_p = lax.broadcasted_iota(jnp.int32, (1, POOL_DIM), 1)
    pooled = _pool_select(tail(2), tail(4), tail(8), tail(16),
                          lambda win: float(min(win, pos0 + 1)), lane_p) - u
    po = _dot(jnp.broadcast_to(pooled, (8, POOL_DIM)).astype(bf16), pw_ref[...])[0:1, :] * ps_ref[...]
    po_ref[...] = po.astype(po_ref.dtype)
    pn_ref[0:POOL_HIST - 1, :] = sp_ref[1:POOL_HIST, :]
    pn_ref[POOL_HIST - 1:POOL_HIST, :] = u


def ssd_pool_sample(z, xbc, dt, u, st_conv, st_ssm, st_pool, prm, pos0):
    b = z.shape[0]
    per_seq = lambda *shape: pl.BlockSpec((None,) + shape, lambda i: (i,) + (0,) * len(shape))
    consts = [prm['conv_w'], prm['conv_b'], prm['dt_bias'], prm['a'], prm['dsk'], prm['ssm_norm_g'],
              prm['pool_w'], prm['pool_scale'], prm['expand']]
    outs = pl.pallas_call(
        functools.partial(_ssd_pool_sample_kernel, pos0=pos0), grid=(b,),
        in_specs=[per_seq(1, SSM_INNER), per_seq(1, CONV_DIM), per_seq(1, LANES), per_seq(1, POOL_DIM),
                  per_seq(CONV_W - 1, CONV_DIM), per_seq(SSM_HEADS, SSM_HEAD_DIM, SSM_STATE),
                  per_seq(POOL_HIST, POOL_DIM)] + [_full(c.shape) for c in consts],
        out_specs=[per_seq(1, SSM_INNER), per_seq(1, POOL_DIM), per_seq(CONV_W - 1, CONV_DIM),
                   per_seq(SSM_HEADS, SSM_HEAD_DIM, SSM_STATE), per_seq(POOL_HIST, POOL_DIM)],
        out_shape=[jax.ShapeDtypeStruct((b, 1, SSM_INNER), f32), jax.ShapeDtypeStruct((b, 1, POOL_DIM), f32),
                   jax.ShapeDtypeStruct((b, CONV_W - 1, CONV_DIM), f32),
                   jax.ShapeDtypeStruct((b, SSM_HEADS, SSM_HEAD_DIM, SSM_STATE), f32),
                   jax.ShapeDtypeStruct((b, POOL_HIST, POOL_DIM), f32)],
        compiler_params=_cparams(1), name="ssd_pool_sample",
    )(z.reshape(b, 1, -1), xbc.reshape(b, 1, -1), dt.reshape(b, 1, -1), u.reshape(b, 1, -1),
      st_conv, st_ssm, st_pool, *consts)
    y, po, cn, sn, pn = outs
    return y.reshape(b, -1), po.reshape(b, -1), cn, sn, pn


def _topk_blocks(gate, n_past, axis):
    blk = lax.broadcasted_iota(jnp.int32, gate.shape, axis).astype(f32)
    g = jnp.where(blk < jnp.asarray(n_past, f32), gate, -jnp.inf)
    picks = []
    for _ in range(MOBA_TOP_K):
        m = jnp.max(g, axis=axis, keepdims=True)
        idx = jnp.min(jnp.where(g == m, blk, float(LANES)), axis=axis, keepdims=True)
        picks.append((idx, jnp.abs(m) < jnp.inf))
        g = jnp.where(blk == idx, -jnp.inf, g)
    return picks


def _moba_prompt_kernel(q_ref, k_ref, v_ref, o_ref,
                        km_ref, kb_ref, vt_ref, qst_ref, nmt_ref, acc_ref, sa_ref, sb_ref):
    i = pl.program_id(0)
    tq = MOBA_BLOCK
    scale = ATT_HEAD_DIM ** -0.5

    @pl.when(i == 0)
    def _():
        km_ref[...] = jnp.zeros(km_ref.shape, f32)

    k = k_ref[...]
    kb = k.astype(bf16)
    head = lambda h: slice(h * ATT_HEAD_DIM, (h + 1) * ATT_HEAD_DIM)
    for h in range(ATT_HEADS):
        kb_ref[i, h] = kb[:, head(h)]
    vt = v_ref[...].T.astype(bf16)
    ones_rows = jnp.ones((V_AUG_ROWS - ATT_HEAD_DIM, tq), bf16)
    for h in range(ATT_HEADS):
        vt_ref[i, h] = jnp.concatenate([vt[head(h), :], ones_rows], axis=0)
    qt = q_ref[...].T
    n_slots = km_ref.shape[0]
    blk_row = lax.broadcasted_iota(jnp.int32, (n_slots, tq), 0).astype(f32)
    key_io = lax.broadcasted_iota(jnp.int32, (tq, tq), 0)
    qry_io = lax.broadcasted_iota(jnp.int32, (tq, tq), 1)
    km = km_ref[...]
    m_own = []
    for h in range(ATT_HEADS):
        qth = qt[head(h), :]
        gate = _dot(km[:, head(h)], qth, precision=HIGHEST)
        sel = jnp.zeros((n_slots, tq), jnp.bool_)
        for idx, ok in _topk_blocks(gate, i, axis=0):
            sel = sel | ((blk_row == idx) & ok)
        nmt_ref[h] = jnp.where(sel, 0.0, MASKED)
        qst = (qth * (scale * LOG2E)).astype(bf16)
        qst_ref[h] = qst
        st = jnp.where(key_io <= qry_io, _dot(kb[:, head(h)], qst), MASKED)
        m = jnp.max(st, axis=0, keepdims=True)
        m_own.append(m)
        acc_ref[h] = _dot(vt_ref[i, h], jnp.exp2(st - m).astype(bf16))

    km_ref[pl.ds(i, 1), :] = jnp.mean(k, axis=0, keepdims=True)

    def scores(j, dst):
        for h in range(ATT_HEADS):
            dst[h] = _dot(kb_ref[j, h], qst_ref[h])

    def absorb(j, src, carry):
        new = []
        for h in range(ATT_HEADS):
            m_prev = carry[h]
            raw = src[h]
            bias = nmt_ref[h, pl.ds(j, 1), :]
            m_new = jnp.maximum(m_prev, jnp.max(raw, axis=0, keepdims=True) + bias)
            alpha = jnp.exp2(m_prev - m_new)
            p = jnp.exp2(raw - (m_new - bias))
            new.append(m_new)
            acc_ref[h] = acc_ref[h] * alpha + _dot(vt_ref[j, h], p.astype(bf16))
        return tuple(new)

    last = jnp.maximum(i - 1, 0)
    scores(0, sa_ref)

    def pair(t, carry):
        scores(jnp.minimum(2 * t + 1, last), sb_ref)
        carry = absorb(2 * t, sa_ref, carry)
        scores(jnp.minimum(2 * t + 2, last), sa_ref)
        return absorb(2 * t + 1, sb_ref, carry)

    carry = lax.fori_loop(0, i // 2, pair, tuple(m_own))
    carry = lax.cond(i % 2 == 1, lambda c: absorb(i - 1, sa_ref, c), lambda c: c, carry)
    out_t = jnp.concatenate(
        [acc_ref[h, 0:ATT_HEAD_DIM, :] / acc_ref[h, ATT_HEAD_DIM:ATT_HEAD_DIM + 1, :] for h in range(ATT_HEADS)],
        axis=0)
    o_ref[...] = out_t.T.astype(o_ref.dtype)


def moba_prompt(q, k, v):
    t = q.shape[0]
    tq = MOBA_BLOCK
    n_blk = t // tq
    n_slots = -(-n_blk // 8) * 8
    assert n_slots <= LANES
    tile = pl.BlockSpec((tq, ATT_DIM), lambda i: (i, 0))
    return pl.pallas_call(
        _moba_prompt_kernel, grid=(n_blk,),
        in_specs=[tile, tile, tile], out_specs=tile,
        out_shape=jax.ShapeDtypeStruct((t, ATT_DIM), bf16),
        scratch_shapes=[pltpu.VMEM((n_slots, ATT_DIM), f32),
                        pltpu.VMEM((n_blk, ATT_HEADS, tq, ATT_HEAD_DIM), bf16),
                        pltpu.VMEM((n_blk, ATT_HEADS, V_AUG_ROWS, tq), bf16),
                        pltpu.VMEM((ATT_HEADS, ATT_HEAD_DIM, tq), bf16),
                        pltpu.VMEM((ATT_HEADS, n_slots, tq), f32),
                        pltpu.VMEM((ATT_HEADS, V_AUG_ROWS, tq), f32),
                        pltpu.VMEM((ATT_HEADS, tq, tq), f32),
                        pltpu.VMEM((ATT_HEADS, tq, tq), f32)],
        compiler_params=_cparams(1), name="moba_prompt",
    )(q, k, v)


K_CHUNK_PAGES = 16
K_SLOTS = 4
PAGES_PER_BLOCK = MOBA_BLOCK // PAGE_SIZE
CHUNKS_BEFORE_FINISH = 2


def _moba_decode_kernel(pt_ref, q_ref, q8_ref, kn8_ref, vn8_ref, kc_ref, vc_ref, o_ref,
                        kbuf, vbuf, s_ref, p_ref, gate_ref, stash_ref, ids_ref, ksem, vsem, *, layer, n_pages):
    b = pl.program_id(0)
    n_seq = pl.num_programs(0) - 1
    n_chunks = n_pages // K_CHUNK_PAGES
    blocks_per_chunk = K_CHUNK_PAGES // PAGES_PER_BLOCK
    n_blocks = n_pages // PAGES_PER_BLOCK
    scale = ATT_HEAD_DIM ** -0.5

    def k_copy(seq, page_slot, slot, p):
        page = pt_ref[seq, page_slot]
        return pltpu.make_async_copy(kc_ref.at[layer, page], kbuf.at[slot, p], ksem.at[slot])

    def start_chunk(seq, c, slot):
        for p in range(K_CHUNK_PAGES):
            k_copy(seq, c * K_CHUNK_PAGES + p, slot, p).start()

    def wait_chunk(seq, c, slot):
        for p in range(K_CHUNK_PAGES):
            k_copy(seq, c * K_CHUNK_PAGES + p, slot, p).wait()

    n_picks = ATT_HEADS * MOBA_TOP_K

    def v_copy(h, r, half, page):
        return pltpu.make_async_copy(vc_ref.at[layer, page, pl.ds(h * ATT_HEAD_DIM, ATT_HEAD_DIM), :],
                                     vbuf.at[h * MOBA_TOP_K + r, half], vsem.at[0])

    def stream(chunks):
        lane8 = lax.broadcasted_iota(jnp.int32, (8, ATT_DIM), 1)
        row8 = lax.broadcasted_iota(jnp.int32, (8, ATT_DIM), 0)
        qblk_b = jnp.where(lane8 // ATT_HEAD_DIM == row8, q_ref[...], 0.0).astype(bf16)
        blk_lane = lax.broadcasted_iota(jnp.int32, (8, LANES), 1)
        gate = gate_ref[...]
        for c in chunks:
            slot = c % K_SLOTS
            ahead = c + K_SLOTS - 1
            if ahead < n_chunks:
                start_chunk(b, ahead, ahead % K_SLOTS)
            else:
                @pl.when(b + 1 < n_seq)
                def _():
                    start_chunk(b + 1, ahead - n_chunks, ahead % K_SLOTS)
            wait_chunk(b, c, slot)
            for t in range(blocks_per_chunk):
                blk = c * blocks_per_chunk + t
                halves = [_dot(qblk_b, kbuf[slot, t * PAGES_PER_BLOCK + half].astype(bf16))
                          for half in range(PAGES_PER_BLOCK)]
                for half in range(PAGES_PER_BLOCK):
                    s_ref[blk, :, half * PAGE_SIZE:(half + 1) * PAGE_SIZE] = halves[half]
                tot = jnp.sum(sum(halves), axis=-1, keepdims=True) * (1.0 / MOBA_BLOCK)
                gate = jnp.where(blk_lane == blk, tot, gate)
        gate_ref[...] = gate

    def choose_and_weigh():
        picks = _topk_blocks(gate_ref[...], n_blocks, axis=1)
        for h in range(ATT_HEADS):
            for r in range(MOBA_TOP_K):
                blk = picks[r][0][h, 0].astype(jnp.int32)
                ids_ref[h * MOBA_TOP_K + r] = blk
                for half in range(PAGES_PER_BLOCK):
                    page = pt_ref[b, blk * PAGES_PER_BLOCK + half]
                    ids_ref[n_picks + (h * MOBA_TOP_K + r) * PAGES_PER_BLOCK + half] = page
                    v_copy(h, r, half, page).start()
        blk_io = lax.broadcasted_iota(jnp.int32, (n_blocks, 8, MOBA_BLOCK), 0).astype(f32)
        seen = jnp.zeros((n_blocks, 8, MOBA_BLOCK), jnp.bool_)
        for idx, ok in picks:
            seen = seen | ((blk_io == idx[None]) & ok[None])
        s_all = jnp.where(seen, s_ref[...] * scale, -jnp.inf)
        s_own = jnp.sum(q8_ref[...] * kn8_ref[...], axis=-1, keepdims=True) * scale
        m = jnp.maximum(jnp.max(jnp.max(s_all, axis=0), axis=-1, keepdims=True), s_own)
        p_all = jnp.exp(s_all - m[None])
        p_own = jnp.exp(s_own - m)
        l = jnp.sum(jnp.sum(p_all, axis=0), axis=-1, keepdims=True) + p_own
        p_ref[...] = p_all
        stash_ref[:, 0:ATT_HEAD_DIM] = p_own * vn8_ref[...]
        stash_ref[:, ATT_HEAD_DIM:] = jnp.broadcast_to(l, (8, LANES - ATT_HEAD_DIM))

    def finish_previous():
        for h in range(ATT_HEADS):
            for r in range(MOBA_TOP_K):
                for half in range(PAGES_PER_BLOCK):
                    page = ids_ref[n_picks + (h * MOBA_TOP_K + r) * PAGES_PER_BLOCK + half]
                    v_copy(h, r, half, page).wait()
        row_hd = lax.broadcasted_iota(jnp.int32, (8, ATT_HEAD_DIM), 0)
        o = stash_ref[:, 0:ATT_HEAD_DIM]
        for h in range(ATT_HEADS):
            oh = jnp.zeros((8, ATT_HEAD_DIM), f32)
            for r in range(MOBA_TOP_K):
                pb = p_ref[ids_ref[h * MOBA_TOP_K + r]].astype(bf16)
                for half in range(PAGES_PER_BLOCK):
                    oh = oh + _dot_nt(pb[:, half * PAGE_SIZE:(half + 1) * PAGE_SIZE],
                                      vbuf[h * MOBA_TOP_K + r, half].astype(bf16))
            o = o + jnp.where(row_hd == h, oh, 0.0)
        o_ref[...] = o / stash_ref[:, ATT_HEAD_DIM:ATT_HEAD_DIM + 1]

    @pl.when(b == 0)
    def _():
        for c in range(K_SLOTS - 1):
            start_chunk(0, c, c)

    @pl.when(b < n_seq)
    def _():
        gate_ref[...] = jnp.zeros((8, LANES), f32)
        stream(range(0, CHUNKS_BEFORE_FINISH))

    @pl.when(b > 0)
    def _():
        finish_previous()

    @pl.when(b < n_seq)
    def _():
        stream(range(CHUNKS_BEFORE_FINISH, n_chunks))
        choose_and_weigh()


def moba_decode(page_table, q, k_new, v_new, cache_kt, cache_vt, layer):
    b = q.shape[0]
    n_pages = page_table.shape[1]
    n_blocks = n_pages // PAGES_PER_BLOCK
    assert n_blocks <= LANES and n_blocks >= MOBA_TOP_K
    heads8 = lambda a: jnp.pad(a.reshape(b, ATT_HEADS, ATT_HEAD_DIM), ((0, 0), (0, 8 - ATT_HEADS), (0, 0)))
    n_picks = ATT_HEADS * MOBA_TOP_K
    cur = lambda i, pt: (jnp.minimum(i, b - 1), 0, 0)
    prev = lambda i, pt: (jnp.maximum(i - 1, 0), 0, 0)
    per_head = pl.BlockSpec((None, 8, ATT_HEAD_DIM), cur)
    grid_spec = pltpu.PrefetchScalarGridSpec(
        num_scalar_prefetch=1, grid=(b + 1,),
        in_specs=[pl.BlockSpec((None, 1, ATT_DIM), cur), per_head, per_head, per_head,
                  pl.BlockSpec(memory_space=pl.ANY), pl.BlockSpec(memory_space=pl.ANY)],
        out_specs=pl.BlockSpec((None, 8, ATT_HEAD_DIM), prev),
        scratch_shapes=[pltpu.VMEM((K_SLOTS, K_CHUNK_PAGES, ATT_DIM, PAGE_SIZE), f32),
                        pltpu.VMEM((n_picks, PAGES_PER_BLOCK, ATT_HEAD_DIM, PAGE_SIZE), f32),
                        pltpu.VMEM((n_blocks, 8, MOBA_BLOCK), f32),
                        pltpu.VMEM((n_blocks, 8, MOBA_BLOCK), f32),
                        pltpu.VMEM((8, LANES), f32),
                        pltpu.VMEM((8, LANES), f32),
                        pltpu.SMEM((n_picks * (1 + PAGES_PER_BLOCK),), jnp.int32),
                        pltpu.SemaphoreType.DMA((K_SLOTS,)),
                        pltpu.SemaphoreType.DMA((1,))])
    out = pl.pallas_call(
        functools.partial(_moba_decode_kernel, layer=layer, n_pages=n_pages),
        grid_spec=grid_spec,
        out_shape=jax.ShapeDtypeStruct((b, 8, ATT_HEAD_DIM), f32),
        compiler_params=_cparams(1), name="moba_decode",
    )(page_table, q.reshape(b, 1, ATT_DIM), heads8(q), heads8(k_new), heads8(v_new), cache_kt, cache_vt)
    return out[:, :ATT_HEADS].reshape(b, ATT_DIM)


def _layer_params(l, w_in, conv_w, conv_b, dt_bias, a_log, d_skip, ssm_norm_g, pool_w, pool_scale, w_out):
    o_xbc = SSM_INNER
    o_dt = o_xbc + CONV_DIM
    o_pool = o_dt + SSM_HEADS
    o_q = o_pool + POOL_DIM
    o_k = o_q + ATT_DIM
    o_v = o_k + ATT_DIM
    wl = w_in[l]
    pad_heads = lambda v: jnp.pad(v.astype(f32), (0, LANES - SSM_HEADS)).reshape(1, LANES)
    w_dt = jnp.pad(wl[:, o_dt:o_pool], ((0, 0), (0, LANES - SSM_HEADS)))
    w_split = [wl[:, :o_xbc], wl[:, o_xbc:o_dt], w_dt, wl[:, o_pool:o_q], wl[:, o_q:o_k],
               wl[:, o_k:o_v], wl[:, o_v:]]
    pw = jnp.zeros((POOL_DIM, POOL_DIM), f32)
    for g in range(len(POOL_WINDOWS)):
        pw = pw.at[g * POOL_GROUP:(g + 1) * POOL_GROUP, g * POOL_GROUP:(g + 1) * POOL_GROUP].set(pool_w[l, g])
    expand = (jnp.arange(LANES)[:, None] == (jnp.arange(SSM_INNER)[None, :] // SSM_HEAD_DIM)).astype(f32)
    wo = w_out[l].astype(bf16)
    return {
        'w_in': [w.astype(bf16) for w in w_split],
        'conv_w': conv_w[l], 'conv_b': conv_b[l].reshape(1, CONV_DIM),
        'dt_bias': pad_heads(dt_bias[l]), 'a': pad_heads(-jnp.exp(a_log[l].astype(f32))),
        'dsk': jnp.repeat(d_skip[l].astype(f32), SSM_HEAD_DIM).reshape(1, SSM_INNER),
        'ssm_norm_g': ssm_norm_g[l].reshape(1, SSM_INNER),
        'pool_w': pw.astype(bf16), 'pool_scale': pool_scale[l].reshape(1, POOL_DIM),
        'expand': expand,
        'w_out': [wo[:SSM_INNER], wo[SSM_INNER:SSM_INNER + POOL_DIM], wo[SSM_INNER + POOL_DIM:]],
    }


IN_PROJ_OUTS = [(i, f32) for i in range(7)]


def _pages_channel_major(cache):
    d, n_phys = cache.shape[:2]
    return jnp.transpose(cache, (0, 1, 3, 4, 2)).reshape(d, n_phys, ATT_DIM, PAGE_SIZE)


def kernel(x_prompt, x_sample, cache_moba_k, cache_moba_v, state_ssm, state_conv, state_pool, cache_mem_k, cache_mem_v, page_table, mem_prompt, norm_mix_g, w_in, conv_w, conv_b, dt_bias, a_log, d_skip, ssm_norm_g, pool_w, pool_scale, w_out, norm_cross_g, norm_mem_g, w_mem_q, w_mem_kv, w_mem_o, norm_ffn_g, w_gate_up, w_down, final_norm_g):
    depth = w_in.shape[0]
    bp, t, d = x_prompt.shape
    bs = x_sample.shape[0]
    assert bp == 1 and x_sample.shape[1] == 1
    past_len = page_table.shape[1] * PAGE_SIZE
    cache_kt = _pages_channel_major(cache_moba_k)
    cache_vt = _pages_channel_major(cache_moba_v)
    mem_k_split = _mem_split_view(cache_mem_k)
    mem_v_split = _mem_split_view(cache_mem_v)
    split_order = _split_channel_order()
    mem_len = mem_prompt.shape[1]
    xp = x_prompt.reshape(t, d)
    xs = x_sample.reshape(bs, d)
    mem = mem_prompt.reshape(mem_len, d)
    tm_p, tm_s = 512, bs
    outs = {n: [] for n in ('kp', 'vp', 'ks', 'vs', 'sp', 'ss', 'cp', 'cs', 'pp', 'ps', 'mk', 'mv')}
    for l in range(depth):
        prm = _layer_params(l, w_in, conv_w, conv_b, dt_bias, a_log, d_skip, ssm_norm_g, pool_w, pool_scale, w_out)
        wq = w_mem_q[l].astype(bf16)
        wkv = w_mem_kv[l].astype(bf16)
        wo_mem = w_mem_o[l].astype(bf16)
        wgu = w_gate_up[l].astype(bf16)
        wdn = w_down[l].astype(bf16)
        half = MEM_HEADS * MEM_HEAD_DIM
        last_layer = l == depth - 1

        mk, mv, mkb, mvb = norm_matmul(mem, norm_mem_g[l], [wkv[:, :half], wkv[:, half:]],
                                       [(0, f32), (1, f32), (0, bf16), (1, bf16)], tm=mem_len)
        z, xbc, dtr, u, q, k, v = norm_matmul(xp, norm_mix_g[l], prm['w_in'], IN_PROJ_OUTS, tm=tm_p)
        y, po, s_new = ssd_pool_prompt(z, xbc, dtr, u, prm, tm=tm_p)
        att = moba_prompt(q, k, v)
        xp = cross_prompt(xp, [y, po, att], prm['w_out'], norm_cross_g[l], wq, mkb, mvb, wo_mem, tm=tm_p)
        xp = swiglu_block(xp, norm_ffn_g[l], wgu, wdn, final_norm_g, norm_out=last_layer, tm=512, tf=D_FF // 2)
        outs['kp'].append(k.reshape(1, t, ATT_HEADS, ATT_HEAD_DIM))
        outs['vp'].append(v.reshape(1, t, ATT_HEADS, ATT_HEAD_DIM))
        outs['sp'].append(s_new[None])
        outs['cp'].append(xbc[t - (CONV_W - 1):][None])
        outs['pp'].append(u[t - POOL_HIST:][None])
        outs['mk'].append(mk.reshape(1, mem_len, MEM_HEADS, MEM_HEAD_DIM))
        outs['mv'].append(mv.reshape(1, mem_len, MEM_HEADS, MEM_HEAD_DIM))

        z, xbc, dtr, u, q, k, v = norm_matmul(xs, norm_mix_g[l], prm['w_in'], IN_PROJ_OUTS, tm=tm_s)
        y, po, c_new, s_new, p_new = ssd_pool_sample(z, xbc, dtr, u, state_conv[l], state_ssm[l],
                                                     state_pool[l], prm, pos0=past_len)
        att = moba_decode(page_table, q, k, v, cache_kt, cache_vt, layer=l)
        xs = matmul_residual(xs, [y, po, att], prm['w_out'], tm=tm_s)
        (qc,) = norm_matmul(xs, norm_cross_g[l], [wq[:, split_order]], [(0, f32)], tm=tm_s)
        oc = cross_sample(qc, mem_k_split, mem_v_split, layer=l)
        xs = matmul_residual(xs, [oc], [wo_mem[split_order, :]], tm=tm_s)
        xs = swiglu_block(xs, norm_ffn_g[l], wgu, wdn, final_norm_g, norm_out=last_layer, tm=tm_s, tf=256)
        outs['ks'].append(k.reshape(bs, 1, ATT_HEADS, ATT_HEAD_DIM))
        outs['vs'].append(v.reshape(bs, 1, ATT_HEADS, ATT_HEAD_DIM))
        outs['ss'].append(s_new)
        outs['cs'].append(c_new)
        outs['ps'].append(p_new)

    y_prompt = xp.reshape(1, t, d)
    y_sample = xs.reshape(bs, 1, d)
    st = lambda n: jnp.stack(outs[n])
    return (y_prompt, y_sample, st('kp'), st('vp'), st('ks'), st('vs'), st('sp'), st('ss'),
            st('cp'), st('cs'), st('pp'), st('ps'), st('mk'), st('mv'))
```

```python
import functools

import jax
import jax.numpy as jnp
from jax import lax
from jax.experimental import pallas as pl
from jax.experimental.pallas import tpu as pltpu

f32 = jnp.float32
bf16 = jnp.bfloat16
HIGHEST = lax.Precision.HIGHEST

D_MODEL = 1024
SSM_INNER = 512
SSM_HEAD_DIM = 64
SSM_HEADS = 8
SSM_GROUPS = 2
SSM_STATE = 128
CONV_W = 4
CONV_DIM = SSM_INNER + 2 * SSM_GROUPS * SSM_STATE
SSD_CHUNK = 128
POOL_DIM = 256
POOL_WINDOWS = (2, 4, 8, 16)
POOL_GROUP = 64
POOL_HIST = 15
ATT_DIM = 256
ATT_HEAD_DIM = 64
ATT_HEADS = 4
MOBA_BLOCK = 256
MOBA_TOP_K = 3
PAGE_SIZE = 128
MEM_HEADS = 4
MEM_HEAD_DIM = 256
D_FF = 2816
RMS_EPS = 1e-6
LANES = 128
CONV_HALO = 8
POOL_HALO = 16
MASKED = -1e30
LOG2E = 1.4426950408889634
BF16_SUBLANES = 16
V_AUG_ROWS = ATT_HEAD_DIM + BF16_SUBLANES
VMEM_LIMIT = 56 * 1024 * 1024


def _cparams(n_axes):
    return pltpu.CompilerParams(dimension_semantics=("arbitrary",) * n_axes,
                                vmem_limit_bytes=VMEM_LIMIT)


def _rms(x, g):
    ms = jnp.mean(x * x, axis=-1, keepdims=True)
    return x * lax.rsqrt(ms + RMS_EPS) * g


def _dot(a, b, **kw):
    return jnp.dot(a, b, preferred_element_type=f32, **kw)


def _dot_nt(a, b, **kw):
    return lax.dot_general(a, b, (((1,), (1,)), ((), ())), preferred_element_type=f32, **kw)


def _silu(x):
    return x * jax.nn.sigmoid(x)


def _softplus(x):
    return jnp.maximum(x, 0.0) + jnp.log1p(jnp.exp(-jnp.abs(x)))


def _full(shape):
    return pl.BlockSpec(shape, lambda *_: (0,) * len(shape))


def _norm_mm_kernel(*refs, n_w, out_w):
    x_ref, g_ref = refs[0], refs[1]
    w_refs = refs[2:2 + n_w]
    o_refs = refs[2 + n_w:]
    hb = _rms(x_ref[...], g_ref[...]).astype(bf16)
    res = [_dot(hb, w[...]) for w in w_refs]
    for o_ref, wi in zip(o_refs, out_w):
        o_ref[...] = res[wi].astype(o_ref.dtype)


def norm_matmul(x, g, ws, outs, tm):
    m, d = x.shape
    in_specs = [pl.BlockSpec((tm, d), lambda i: (i, 0)), _full((1, d))]
    in_specs += [_full(w.shape) for w in ws]
    out_shape = [jax.ShapeDtypeStruct((m, ws[wi].shape[1]), dt) for wi, dt in outs]
    out_specs = [pl.BlockSpec((tm, ws[wi].shape[1]), lambda i: (i, 0)) for wi, _ in outs]
    return pl.pallas_call(
        functools.partial(_norm_mm_kernel, n_w=len(ws), out_w=tuple(wi for wi, _ in outs)),
        grid=(m // tm,), in_specs=in_specs, out_specs=out_specs, out_shape=out_shape,
        compiler_params=_cparams(1), name="norm_matmul",
    )(x, g.reshape(1, d), *ws)


def _mm_res_kernel(*refs, n_a):
    res_ref = refs[0]
    a_refs = refs[1:1 + n_a]
    w_refs = refs[1 + n_a:1 + 2 * n_a]
    o_ref = refs[-1]
    acc = res_ref[...]
    for a, w in zip(a_refs, w_refs):
        acc = acc + _dot(a[...].astype(bf16), w[...])
    o_ref[...] = acc


def matmul_residual(res, a_list, w_list, tm):
    m, d = res.shape
    in_specs = [pl.BlockSpec((tm, d), lambda i: (i, 0))]
    in_specs += [pl.BlockSpec((tm, a.shape[1]), lambda i: (i, 0)) for a in a_list]
    in_specs += [_full(w.shape) for w in w_list]
    return pl.pallas_call(
        functools.partial(_mm_res_kernel, n_a=len(a_list)),
        grid=(m // tm,), in_specs=in_specs,
        out_specs=pl.BlockSpec((tm, d), lambda i: (i, 0)),
        out_shape=jax.ShapeDtypeStruct((m, d), f32),
        compiler_params=_cparams(1), name="matmul_residual",
    )(res, *a_list, *w_list)


def _prompt_tail_kernel(*refs, n_mix, norm_out):
    x_ref = refs[0]
    a_refs = refs[1:1 + n_mix]
    w_refs = refs[1 + n_mix:1 + 2 * n_mix]
    (g_ref, wq_ref, mk_ref, mv_ref, wo_ref,
     gf_ref, wg_ref, wu_ref, wd_ref, gout_ref, o_ref) = refs[1 + 2 * n_mix:]
    x = x_ref[...]
    for a, w in zip(a_refs, w_refs):
        x = x + _dot(a[...].astype(bf16), w[...])
    if n_mix:
        o_ref[...] = x
        x = o_ref[...]
    hb = _rms(x, g_ref[...]).astype(bf16)
    q = _dot(hb, wq_ref[...])
    acc = x
    for h in range(MEM_HEADS):
        sl = slice(h * MEM_HEAD_DIM, (h + 1) * MEM_HEAD_DIM)
        s = _dot_nt(q[:, sl].astype(bf16), mk_ref[:, sl]) * (MEM_HEAD_DIM ** -0.5)
        p = jnp.exp(s - jnp.max(s, axis=-1, keepdims=True))
        p = p / jnp.sum(p, axis=-1, keepdims=True)
        oh = _dot(p.astype(bf16), mv_ref[:, sl])
        acc = acc + _dot(oh.astype(bf16), wo_ref[sl, :])
    o_ref[...] = acc
    x2 = o_ref[...]
    hb = _rms(x2, gf_ref[...]).astype(bf16)
    act = _silu(_dot(hb, wg_ref[...])) * _dot(hb, wu_ref[...])
    y = x2 + _dot(act.astype(bf16), wd_ref[...])
    o_ref[...] = _rms(y, gout_ref[...]) if norm_out else y


def prompt_tail(x, mix_list, w_out_list, g, wq, mkb, mvb, wo, g_ffn, w_gu, w_d, g_out, norm_out, tm):
    m, d = x.shape
    row = lambda c: pl.BlockSpec((tm, c), lambda i: (i, 0))
    once = lambda shape, idx: pl.BlockSpec(shape, idx, pipeline_mode=pl.Buffered(1))
    resident = lambda a: once(a.shape, lambda i: (0,) * a.ndim)
    consts = [g.reshape(1, d), wq, mkb, mvb, wo, g_ffn.reshape(1, d)]
    tail = [g_out.reshape(1, d)]
    return pl.pallas_call(
        functools.partial(_prompt_tail_kernel, n_mix=len(mix_list), norm_out=norm_out), grid=(m // tm,),
        in_specs=[row(d)] + [row(a.shape[1]) for a in mix_list] + [resident(w) for w in w_out_list]
                 + [resident(c) for c in consts]
                 + [once((d, D_FF), lambda i: (0, 0)), once((d, D_FF), lambda i: (0, 1)), resident(w_d)]
                 + [resident(c) for c in tail],
        out_specs=row(d),
        out_shape=jax.ShapeDtypeStruct((m, d), f32),
        compiler_params=_cparams(1), name="prompt_tail",
    )(x, *mix_list, *w_out_list, *consts, w_gu, w_gu, w_d, *tail)


MEM_SPLIT = MEM_HEAD_DIM // LANES
MEM_ROWS = MEM_SPLIT * MEM_HEADS


def _split_channel_order():
    return jnp.arange(MEM_HEADS * MEM_HEAD_DIM).reshape(MEM_HEADS, MEM_SPLIT, LANES).transpose(1, 0, 2).reshape(-1)


def _mem_split_view(cache):
    d, b, m = cache.shape[:3]
    x = cache.reshape(d, b, m, MEM_HEADS, MEM_SPLIT, LANES)
    return jnp.transpose(x, (0, 1, 2, 4, 3, 5)).reshape(d, b, m, MEM_ROWS, LANES)


def _cross_sample_kernel(q_ref, mk_ref, mv_ref, o_ref):
    part = jnp.sum(mk_ref[...] * q_ref[...][None], axis=-1, keepdims=True)
    s = part
    for piece in range(1, MEM_SPLIT):
        s = s + jnp.roll(part, piece * MEM_HEADS, axis=1)
    s = s * (MEM_HEAD_DIM ** -0.5)
    p = jnp.exp(s - jnp.max(s, axis=0, keepdims=True))
    l = jnp.sum(p, axis=0)
    o_ref[...] = jnp.sum(p * mv_ref[...], axis=0) / l


def cross_sample(q_split, mem_k_split, mem_v_split, layer):
    b = q_split.shape[0]
    mlen = mem_k_split.shape[2]
    mem_spec = pl.BlockSpec((None, None, mlen, MEM_ROWS, LANES), lambda i: (layer, i, 0, 0, 0))
    row_spec = pl.BlockSpec((None, MEM_ROWS, LANES), lambda i: (i, 0, 0))
    out = pl.pallas_call(
        _cross_sample_kernel, grid=(b,),
        in_specs=[row_spec, mem_spec, mem_spec], out_specs=row_spec,
        out_shape=jax.ShapeDtypeStruct((b, MEM_ROWS, LANES), f32),
        compiler_params=_cparams(1), name="cross_sample",
    )(q_split.reshape(b, MEM_ROWS, LANES), mem_k_split, mem_v_split)
    return out.reshape(b, MEM_ROWS * LANES)


def _swiglu_kernel(x_ref, g_ref, wg_ref, wu_ref, wd_ref, gout_ref, o_ref, h_ref, acc_ref, *, norm_out):
    f = pl.program_id(1)

    @pl.when(f == 0)
    def _():
        h_ref[...] = _rms(x_ref[...], g_ref[...]).astype(bf16)
        acc_ref[...] = x_ref[...]

    hb = h_ref[...]
    a = _silu(_dot(hb, wg_ref[...])) * _dot(hb, wu_ref[...])
    acc_ref[...] += _dot(a.astype(bf16), wd_ref[...])

    @pl.when(f == pl.num_programs(1) - 1)
    def _():
        y = acc_ref[...]
        o_ref[...] = _rms(y, gout_ref[...]) if norm_out else y


def swiglu_block(x, g, w_gu, w_d, g_out, norm_out, tm, tf):
    m, d = x.shape
    nf = D_FF // tf
    mode = dict(pipeline_mode=pl.Buffered(1)) if nf == 1 else {}
    return pl.pallas_call(
        functools.partial(_swiglu_kernel, norm_out=norm_out), grid=(m // tm, nf),
        in_specs=[pl.BlockSpec((tm, d), lambda i, f: (i, 0)), _full((1, d)),
                  pl.BlockSpec((d, tf), lambda i, f: (0, f), **mode),
                  pl.BlockSpec((d, tf), lambda i, f: (0, f + nf), **mode),
                  pl.BlockSpec((tf, d), lambda i, f: (f, 0), **mode), _full((1, d))],
        out_specs=pl.BlockSpec((tm, d), lambda i, f: (i, 0)),
        out_shape=jax.ShapeDtypeStruct((m, d), f32),
        scratch_shapes=[pltpu.VMEM((tm, d), bf16), pltpu.VMEM((tm, d), f32)],
        compiler_params=_cparams(2), name="swiglu_block",
    )(x, g.reshape(1, d), w_gu, w_gu, w_d, g_out.reshape(1, d))


def _pool_select(w2, w4, w8, w16, cnt_of, lane):
    out = w16 / cnt_of(16)
    for win, acc in ((8, w8), (4, w4), (2, w2)):
        g = POOL_WINDOWS.index(win)
        out = jnp.where(lane < (g + 1) * POOL_GROUP, acc / cnt_of(win), out)
    return out


def _group_rmsnorm(y, g):
    half = SSM_INNER // SSM_GROUPS
    parts = [_rms(y[:, i * half:(i + 1) * half], g[:, i * half:(i + 1) * half])
             for i in range(SSM_GROUPS)]
    return jnp.concatenate(parts, axis=-1)


def _ssd_pool_prompt_kernel(z_ref, xbc_ref, dt_ref, u_ref, cw_ref, cb_ref, dtb_ref, a_ref, dsk_ref,
                            ng_ref, pw_ref, ps_ref,
                            y_ref, po_ref, st_ref,
                            ext_ref, pext_ref, xc_ref, s_ref, *, tm):
    i = pl.program_id(0)
    halo = CONV_HALO
    phalo = POOL_HALO

    @pl.when(i == 0)
    def _():
        ext_ref[0:halo, :] = jnp.zeros((halo, CONV_DIM), f32)
        pext_ref[0:phalo, :] = jnp.zeros((phalo, POOL_DIM), f32)
        s_ref[...] = jnp.zeros(s_ref.shape, f32)

    @pl.when(i > 0)
    def _():
        ext_ref[0:halo, :] = ext_ref[tm:tm + halo, :]
        pext_ref[0:phalo, :] = pext_ref[tm:tm + phalo, :]

    ext_ref[halo:halo + tm, :] = xbc_ref[...]
    pext_ref[phalo:phalo + tm, :] = u_ref[...]

    rp, lp = 64, 256
    for r0 in range(0, tm, rp):
        for c0 in range(0, CONV_DIM, lp):
            acc = jnp.broadcast_to(cb_ref[:, c0:c0 + lp], (rp, lp))
            for j in range(CONV_W):
                off = halo - (CONV_W - 1) + j + r0
                acc = acc + ext_ref[off:off + rp, c0:c0 + lp] * cw_ref[j:j + 1, c0:c0 + lp]
            xc_ref[r0:r0 + rp, c0:c0 + lp] = _silu(acc)

    lane_p = lax.broadcasted_iota(jnp.int32, (rp, POOL_DIM), 1)
    row_p = lax.broadcasted_iota(jnp.int32, (rp, POOL_DIM), 0)
    for r0 in range(0, tm, rp):
        u = pext_ref[phalo + r0:phalo + r0 + rp, :]
        run = u
        sums = {}
        for back in range(1, max(POOL_WINDOWS)):
            run = run + pext_ref[phalo - back + r0:phalo - back + r0 + rp, :]
            if back + 1 in POOL_WINDOWS:
                sums[back + 1] = run
        pos1 = (row_p + (i * tm + r0 + 1)).astype(f32)
        pooled = _pool_select(sums[2], sums[4], sums[8], sums[16],
                              lambda win: jnp.minimum(float(win), pos1), lane_p) - u
        po_ref[r0:r0 + rp, :] = (_dot(pooled.astype(bf16), pw_ref[...]) * ps_ref[...]).astype(po_ref.dtype)

    cl = SSD_CHUNK
    r_io = lax.broadcasted_iota(jnp.int32, (cl, cl), 0)
    c_io = lax.broadcasted_iota(jnp.int32, (cl, cl), 1)
    ltri = (c_io <= r_io).astype(f32)
    causal = c_io <= r_io
    first_half = c_io < SSM_HEAD_DIM
    n_pairs = SSM_HEADS // 2
    heads_per_group = SSM_HEADS // SSM_GROUPS

    def chunk(c, carry):
        r0 = pl.multiple_of(c * cl, cl)
        xc = xc_ref[pl.ds(r0, cl), :]
        dt = _softplus(dt_ref[pl.ds(r0, cl), :] + dtb_ref[...])
        a_cum = _dot(ltri, dt * a_ref[...], precision=HIGHEST)
        a_cum_t = a_cum.T
        dt_t = dt.T
        to_end_t = jnp.exp(a_cum_t[:, cl - 1:cl] - a_cum_t) * dt_t
        b_t = [xc[:, SSM_INNER + g * SSM_STATE:SSM_INNER + (g + 1) * SSM_STATE].T
               for g in range(SSM_GROUPS)]
        cm = [xc[:, SSM_INNER + SSM_GROUPS * SSM_STATE + g * SSM_STATE:
                 SSM_INNER + SSM_GROUPS * SSM_STATE + (g + 1) * SSM_STATE].astype(bf16)
              for g in range(SSM_GROUPS)]
        scores = [_dot(cm[g], b_t[g].astype(bf16)) for g in range(SSM_GROUPS)]
        y_pairs = []
        for k in range(n_pairs):
            g = (2 * k) // heads_per_group
            xs_pair = xc[:, k * LANES:(k + 1) * LANES]
            xs_b = xs_pair.astype(bf16)
            yd, cs, eb = [], [], []
            for h in (2 * k, 2 * k + 1):
                colb = jnp.broadcast_to(a_cum[:, h:h + 1], (cl, cl))
                seg = colb - a_cum_t[h:h + 1, :]
                decay = jnp.exp(jnp.where(causal, seg, -jnp.inf))
                mh = scores[g] * decay * dt_t[h:h + 1, :]
                yd.append(_dot(mh.astype(bf16), xs_b))
                cs.append(_dot((b_t[g] * to_end_t[h:h + 1, :]).astype(bf16), xs_b))
                eb.append(jnp.exp(colb))
            e_pair = jnp.where(first_half, eb[0], eb[1])
            s_old = s_ref[k]
            y_off = _dot(cm[g], s_old.astype(bf16)) * e_pair
            y_pairs.append(jnp.where(first_half, yd[0], yd[1]) + y_off
                           + dsk_ref[:, k * LANES:(k + 1) * LANES] * xs_pair)
            s_ref[k] = s_old * e_pair[cl - 1:cl, :] + jnp.where(first_half, cs[0], cs[1])
        y = jnp.concatenate(y_pairs, axis=-1) * _silu(z_ref[pl.ds(r0, cl), :])
        y_ref[pl.ds(r0, cl), :] = _group_rmsnorm(y, ng_ref[...]).astype(y_ref.dtype)
        return carry

    lax.fori_loop(0, tm // cl, chunk, 0, unroll=True)

    @pl.when(i == pl.num_programs(0) - 1)
    def _():
        for k in range(n_pairs):
            st = s_ref[k].T
            st_ref[2 * k] = st[0:SSM_HEAD_DIM, :]
            st_ref[2 * k + 1] = st[SSM_HEAD_DIM:2 * SSM_HEAD_DIM, :]


def ssd_pool_prompt(z, xbc, dt, u, prm, tm):
    t = z.shape[0]
    row = lambda c: pl.BlockSpec((tm, c), lambda i: (i, 0))
    consts = [prm['conv_w'], prm['conv_b'], prm['dt_bias'], prm['a'], prm['dsk'], prm['ssm_norm_g'],
              prm['pool_w'], prm['pool_scale']]
    return pl.pallas_call(
        functools.partial(_ssd_pool_prompt_kernel, tm=tm), grid=(t // tm,),
        in_specs=[row(SSM_INNER), row(CONV_DIM), row(LANES), row(POOL_DIM)] + [_full(c.shape) for c in consts],
        out_specs=[row(SSM_INNER), row(POOL_DIM), _full((SSM_HEADS, SSM_HEAD_DIM, SSM_STATE))],
        out_shape=[jax.ShapeDtypeStruct((t, SSM_INNER), bf16), jax.ShapeDtypeStruct((t, POOL_DIM), bf16),
                   jax.ShapeDtypeStruct((SSM_HEADS, SSM_HEAD_DIM, SSM_STATE), f32)],
        scratch_shapes=[pltpu.VMEM((tm + CONV_HALO, CONV_DIM), f32), pltpu.VMEM((tm + POOL_HALO, POOL_DIM), f32),
                        pltpu.VMEM((tm, CONV_DIM), f32),
                        pltpu.VMEM((SSM_HEADS // 2, SSM_STATE, 2 * SSM_HEAD_DIM), f32)],
        compiler_params=_cparams(1), name="ssd_pool_prompt",
    )(z, xbc, dt, u, *consts)


def _col_tile(row):
    return jnp.broadcast_to(row, (LANES, LANES)).T


def _ssd_pool_sample_kernel(z_ref, xbc_ref, dt_ref, u_ref, sc_ref, ss_ref, sp_ref,
                            cw_ref, cb_ref, dtb_ref, a_ref, dsk_ref, ng_ref, pw_ref, ps_ref, ex_ref,
                            y_ref, po_ref, cn_ref, sn_ref, pn_ref, *, pos0):
    xrow = xbc_ref[...]
    acc = cb_ref[...] + xrow * cw_ref[CONV_W - 1:CONV_W, :]
    for j in range(CONV_W - 1):
        acc = acc + sc_ref[j:j + 1, :] * cw_ref[j:j + 1, :]
        if j > 0:
            cn_ref[j - 1:j, :] = sc_ref[j:j + 1, :]
    cn_ref[CONV_W - 2:CONV_W - 1, :] = xrow
    xc = _silu(acc)

    dt = _softplus(dt_ref[...] + dtb_ref[...])
    dta = dt * a_ref[...]
    both = jnp.concatenate([jnp.broadcast_to(dt, (8, LANES)), jnp.broadcast_to(dta, (8, LANES))], axis=0)
    both_x = _dot(both, ex_ref[...], precision=HIGHEST)
    dtx = both_x[0:1, :]
    dec_x = jnp.exp(both_x[8:9, :])
    xs = xc[:, 0:SSM_INNER]
    xdt = xs * dtx
    heads_per_group = SSM_HEADS // SSM_GROUPS
    y_pairs = []
    for k in range(SSM_HEADS // 2):
        g = (2 * k) // heads_per_group
        b_row = xc[:, SSM_INNER + g * SSM_STATE:SSM_INNER + (g + 1) * SSM_STATE]
        c_row = xc[:, SSM_INNER + SSM_GROUPS * SSM_STATE + g * SSM_STATE:
                   SSM_INNER + SSM_GROUPS * SSM_STATE + (g + 1) * SSM_STATE]
        sl = slice(k * LANES, (k + 1) * LANES)
        s_old = jnp.concatenate([ss_ref[2 * k], ss_ref[2 * k + 1]], axis=0)
        s_new = s_old * _col_tile(dec_x[:, sl]) + _col_tile(xdt[:, sl]) * b_row
        sn_ref[2 * k] = s_new[0:SSM_HEAD_DIM, :]
        sn_ref[2 * k + 1] = s_new[SSM_HEAD_DIM:, :]
        y_k = _dot_nt(jnp.broadcast_to(c_row, (8, SSM_STATE)), s_new, precision=HIGHEST)[0:1, :]
        y_pairs.append(y_k + dsk_ref[:, sl] * xs[:, sl])
    y = jnp.concatenate(y_pairs, axis=-1) * _silu(z_ref[...])
    y_ref[...] = _group_rmsnorm(y, ng_ref[...]).astype(y_ref.dtype)

    u = u_ref[...]
    prev = sp_ref[...]
    rowi = lax.broadcasted_iota(jnp.int32, prev.shape, 0)
    tail = lambda win: u + jnp.sum(jnp.where(rowi >= POOL_HIST - (win - 1), prev, 0.0), axis=0, keepdims=True)
    lane_p = lax.broadcasted_iota(jnp.int32, (1, POOL_DIM), 1)
    pooled = _pool_select(tail(2), tail(4), tail(8), tail(16),
                          lambda win: float(min(win, pos0 + 1)), lane_p) - u
    po = _dot(jnp.broadcast_to(pooled, (8, POOL_DIM)).astype(bf16), pw_ref[...])[0:1, :] * ps_ref[...]
    po_ref[...] = po.astype(po_ref.dtype)
    pn_ref[0:POOL_HIST - 1, :] = sp_ref[1:POOL_HIST, :]
    pn_ref[POOL_HIST - 1:POOL_HIST, :] = u


def ssd_pool_sample(z, xbc, dt, u, st_conv, st_ssm, st_pool, prm, pos0):
    b = z.shape[0]
    per_seq = lambda *shape: pl.BlockSpec((None,) + shape, lambda i: (i,) + (0,) * len(shape))
    consts = [prm['conv_w'], prm['conv_b'], prm['dt_bias'], prm['a'], prm['dsk'], prm['ssm_norm_g'],
              prm['pool_w'], prm['pool_scale'], prm['expand']]
    outs = pl.pallas_call(
        functools.partial(_ssd_pool_sample_kernel, pos0=pos0), grid=(b,),
        in_specs=[per_seq(1, SSM_INNER), per_seq(1, CONV_DIM), per_seq(1, LANES), per_seq(1, POOL_DIM),
                  per_seq(CONV_W - 1, CONV_DIM), per_seq(SSM_HEADS, SSM_HEAD_DIM, SSM_STATE),
                  per_seq(POOL_HIST, POOL_DIM)] + [_full(c.shape) for c in consts],
        out_specs=[per_seq(1, SSM_INNER), per_seq(1, POOL_DIM), per_seq(CONV_W - 1, CONV_DIM),
                   per_seq(SSM_HEADS, SSM_HEAD_DIM, SSM_STATE), per_seq(POOL_HIST, POOL_DIM)],
        out_shape=[jax.ShapeDtypeStruct((b, 1, SSM_INNER), f32), jax.ShapeDtypeStruct((b, 1, POOL_DIM), f32),
                   jax.ShapeDtypeStruct((b, CONV_W - 1, CONV_DIM), f32),
                   jax.ShapeDtypeStruct((b, SSM_HEADS, SSM_HEAD_DIM, SSM_STATE), f32),
                   jax.ShapeDtypeStruct((b, POOL_HIST, POOL_DIM), f32)],
        compiler_params=_cparams(1), name="ssd_pool_sample",
    )(z.reshape(b, 1, -1), xbc.reshape(b, 1, -1), dt.reshape(b, 1, -1), u.reshape(b, 1, -1),
      st_conv, st_ssm, st_pool, *consts)
    y, po, cn, sn, pn = outs
    return y.reshape(b, -1), po.reshape(b, -1), cn, sn, pn


def _topk_blocks(gate, n_past, axis):
    blk = lax.broadcasted_iota(jnp.int32, gate.shape, axis).astype(f32)
    g = jnp.where(blk < jnp.asarray(n_past, f32), gate, -jnp.inf)
    picks = []
    for _ in range(MOBA_TOP_K):
        m = jnp.max(g, axis=axis, keepdims=True)
        idx = jnp.min(jnp.where(g == m, blk, float(LANES)), axis=axis, keepdims=True)
        picks.append((idx, jnp.abs(m) < jnp.inf))
        g = jnp.where(blk == idx, -jnp.inf, g)
    return picks


def _moba_prompt_kernel(q_ref, k_ref, v_ref, o_ref,
                        km_ref, kb_ref, vt_ref, qst_ref, nmt_ref, acc_ref, sa_ref, sb_ref):
    i = pl.program_id(0)
    tq = MOBA_BLOCK
    scale = ATT_HEAD_DIM ** -0.5

    @pl.when(i == 0)
    def _():
        km_ref[...] = jnp.zeros(km_ref.shape, f32)

    k = k_ref[...]
    kb = k.astype(bf16)
    head = lambda h: slice(h * ATT_HEAD_DIM, (h + 1) * ATT_HEAD_DIM)
    for h in range(ATT_HEADS):
        kb_ref[i, h] = kb[:, head(h)]
    vt = v_ref[...].T.astype(bf16)
    ones_rows = jnp.ones((V_AUG_ROWS - ATT_HEAD_DIM, tq), bf16)
    for h in range(ATT_HEADS):
        vt_ref[i, h] = jnp.concatenate([vt[head(h), :], ones_rows], axis=0)
    qt = q_ref[...].T
    n_slots = km_ref.shape[0]
    blk_row = lax.broadcasted_iota(jnp.int32, (n_slots, tq), 0).astype(f32)
    key_io = lax.broadcasted_iota(jnp.int32, (tq, tq), 0)
    qry_io = lax.broadcasted_iota(jnp.int32, (tq, tq), 1)
    km = km_ref[...]
    m_own = []
    for h in range(ATT_HEADS):
        qth = qt[head(h), :]
        gate = _dot(km[:, head(h)], qth, precision=HIGHEST)
        sel = jnp.zeros((n_slots, tq), jnp.bool_)
        for idx, ok in _topk_blocks(gate, i, axis=0):
            sel = sel | ((blk_row == idx) & ok)
        nmt_ref[h] = jnp.where(sel, 0.0, MASKED)
        qst = (qth * (scale * LOG2E)).astype(bf16)
        qst_ref[h] = qst
        st = jnp.where(key_io <= qry_io, _dot(kb[:, head(h)], qst), MASKED)
        m = jnp.max(st, axis=0, keepdims=True)
        m_own.append(m)
        acc_ref[h] = _dot(vt_ref[i, h], jnp.exp2(st - m).astype(bf16))

    km_ref[pl.ds(i, 1), :] = jnp.mean(k, axis=0, keepdims=True)

    def scores(j, dst):
        for h in range(ATT_HEADS):
            dst[h] = _dot(kb_ref[j, h], qst_ref[h])

    def absorb(j, src, carry):
        new = []
        for h in range(ATT_HEADS):
            m_prev = carry[h]
            raw = src[h]
            bias = nmt_ref[h, pl.ds(j, 1), :]
            m_new = jnp.maximum(m_prev, jnp.max(raw, axis=0, keepdims=True) + bias)
            alpha = jnp.exp2(m_prev - m_new)
            p = jnp.exp2(raw - (m_new - bias))
            new.append(m_new)
            acc_ref[h] = acc_ref[h] * alpha + _dot(vt_ref[j, h], p.astype(bf16))
        return tuple(new)

    last = jnp.maximum(i - 1, 0)
    scores(0, sa_ref)

    def pair(t, carry):
        scores(jnp.minimum(2 * t + 1, last), sb_ref)
        carry = absorb(2 * t, sa_ref, carry)
        scores(jnp.minimum(2 * t + 2, last), sa_ref)
        return absorb(2 * t + 1, sb_ref, carry)

    carry = lax.fori_loop(0, i // 2, pair, tuple(m_own))
    carry = lax.cond(i % 2 == 1, lambda c: absorb(i - 1, sa_ref, c), lambda c: c, carry)
    out_t = jnp.concatenate(
        [acc_ref[h, 0:ATT_HEAD_DIM, :] / acc_ref[h, ATT_HEAD_DIM:ATT_HEAD_DIM + 1, :] for h in range(ATT_HEADS)],
        axis=0)
    o_ref[...] = out_t.T.astype(o_ref.dtype)


def moba_prompt(q, k, v):
    t = q.shape[0]
    tq = MOBA_BLOCK
    n_blk = t // tq
    n_slots = -(-n_blk // 8) * 8
    assert n_slots <= LANES
    tile = pl.BlockSpec((tq, ATT_DIM), lambda i: (i, 0))
    return pl.pallas_call(
        _moba_prompt_kernel, grid=(n_blk,),
        in_specs=[tile, tile, tile], out_specs=tile,
        out_shape=jax.ShapeDtypeStruct((t, ATT_DIM), bf16),
        scratch_shapes=[pltpu.VMEM((n_slots, ATT_DIM), f32),
                        pltpu.VMEM((n_blk, ATT_HEADS, tq, ATT_HEAD_DIM), bf16),
                        pltpu.VMEM((n_blk, ATT_HEADS, V_AUG_ROWS, tq), bf16),
                        pltpu.VMEM((ATT_HEADS, ATT_HEAD_DIM, tq), bf16),
                        pltpu.VMEM((ATT_HEADS, n_slots, tq), f32),
                        pltpu.VMEM((ATT_HEADS, V_AUG_ROWS, tq), f32),
                        pltpu.VMEM((ATT_HEADS, tq, tq), f32),
                        pltpu.VMEM((ATT_HEADS, tq, tq), f32)],
        compiler_params=_cparams(1), name="moba_prompt",
    )(q, k, v)


K_CHUNK_PAGES = 16
K_SLOTS = 4
PAGES_PER_BLOCK = MOBA_BLOCK // PAGE_SIZE
CHUNKS_BEFORE_FINISH = 2


def _moba_decode_kernel(pt_ref, q_ref, q8_ref, kn8_ref, vn8_ref, kc_ref, vc_ref, o_ref,
                        kbuf, vbuf, s_ref, p_ref, gate_ref, stash_ref, ids_ref, ksem, vsem, *, layer, n_pages):
    b = pl.program_id(0)
    n_seq = pl.num_programs(0) - 1
    n_chunks = n_pages // K_CHUNK_PAGES
    blocks_per_chunk = K_CHUNK_PAGES // PAGES_PER_BLOCK
    n_blocks = n_pages // PAGES_PER_BLOCK
    scale = ATT_HEAD_DIM ** -0.5

    def k_copy(seq, page_slot, slot, p):
        page = pt_ref[seq, page_slot]
        return pltpu.make_async_copy(kc_ref.at[layer, page], kbuf.at[slot, p], ksem.at[slot])

    def start_chunk(seq, c, slot):
        for p in range(K_CHUNK_PAGES):
            k_copy(seq, c * K_CHUNK_PAGES + p, slot, p).start()

    def wait_chunk(seq, c, slot):
        for p in range(K_CHUNK_PAGES):
            k_copy(seq, c * K_CHUNK_PAGES + p, slot, p).wait()

    n_picks = ATT_HEADS * MOBA_TOP_K

    def v_copy(h, r, half, page):
        return pltpu.make_async_copy(vc_ref.at[layer, page, pl.ds(h * ATT_HEAD_DIM, ATT_HEAD_DIM), :],
                                     vbuf.at[h * MOBA_TOP_K + r, half], vsem.at[0])

    def stream(chunks):
        lane8 = lax.broadcasted_iota(jnp.int32, (8, ATT_DIM), 1)
        row8 = lax.broadcasted_iota(jnp.int32, (8, ATT_DIM), 0)
        qblk_b = jnp.where(lane8 // ATT_HEAD_DIM == row8, q_ref[...], 0.0).astype(bf16)
        blk_lane = lax.broadcasted_iota(jnp.int32, (8, LANES), 1)
        gate = gate_ref[...]
        for c in chunks:
            slot = c % K_SLOTS
            ahead = c + K_SLOTS - 1
            if ahead < n_chunks:
                start_chunk(b, ahead, ahead % K_SLOTS)
            else:
                @pl.when(b + 1 < n_seq)
                def _():
                    start_chunk(b + 1, ahead - n_chunks, ahead % K_SLOTS)
            wait_chunk(b, c, slot)
            for t in range(blocks_per_chunk):
                blk = c * blocks_per_chunk + t
                halves = [_dot(qblk_b, kbuf[slot, t * PAGES_PER_BLOCK + half].astype(bf16))
                          for half in range(PAGES_PER_BLOCK)]
                for half in range(PAGES_PER_BLOCK):
                    s_ref[blk, :, half * PAGE_SIZE:(half + 1) * PAGE_SIZE] = halves[half]
                tot = jnp.sum(sum(halves), axis=-1, keepdims=True) * (1.0 / MOBA_BLOCK)
                gate = jnp.where(blk_lane == blk, tot, gate)
        gate_ref[...] = gate

    def choose_and_weigh():
        picks = _topk_blocks(gate_ref[...], n_blocks, axis=1)
        for h in range(ATT_HEADS):
            for r in range(MOBA_TOP_K):
                blk = picks[r][0][h, 0].astype(jnp.int32)
                ids_ref[h * MOBA_TOP_K + r] = blk
                for half in range(PAGES_PER_BLOCK):
                    page = pt_ref[b, blk * PAGES_PER_BLOCK + half]
                    ids_ref[n_picks + (h * MOBA_TOP_K + r) * PAGES_PER_BLOCK + half] = page
                    v_copy(h, r, half, page).start()
        blk_io = lax.broadcasted_iota(jnp.int32, (n_blocks, 8, MOBA_BLOCK), 0).astype(f32)
        seen = jnp.zeros((n_blocks, 8, MOBA_BLOCK), jnp.bool_)
        for idx, ok in picks:
            seen = seen | ((blk_io == idx[None]) & ok[None])
        s_all = jnp.where(seen, s_ref[...] * scale, -jnp.inf)
        s_own = jnp.sum(q8_ref[...] * kn8_ref[...], axis=-1, keepdims=True) * scale
        m = jnp.maximum(jnp.max(jnp.max(s_all, axis=0), axis=-1, keepdims=True), s_own)
        p_all = jnp.exp(s_all - m[None])
        p_own = jnp.exp(s_own - m)
        l = jnp.sum(jnp.sum(p_all, axis=0), axis=-1, keepdims=True) + p_own
        p_ref[...] = p_all
        stash_ref[:, 0:ATT_HEAD_DIM] = p_own * vn8_ref[...]
        stash_ref[:, ATT_HEAD_DIM:] = jnp.broadcast_to(l, (8, LANES - ATT_HEAD_DIM))

    def finish_previous():
        for h in range(ATT_HEADS):
            for r in range(MOBA_TOP_K):
                for half in range(PAGES_PER_BLOCK):
                    page = ids_ref[n_picks + (h * MOBA_TOP_K + r) * PAGES_PER_BLOCK + half]
                    v_copy(h, r, half, page).wait()
        row_hd = lax.broadcasted_iota(jnp.int32, (8, ATT_HEAD_DIM), 0)
        o = stash_ref[:, 0:ATT_HEAD_DIM]
        for h in range(ATT_HEADS):
            oh = jnp.zeros((8, ATT_HEAD_DIM), f32)
            for r in range(MOBA_TOP_K):
                pb = p_ref[ids_ref[h * MOBA_TOP_K + r]].astype(bf16)
                for half in range(PAGES_PER_BLOCK):
                    oh = oh + _dot_nt(pb[:, half * PAGE_SIZE:(half + 1) * PAGE_SIZE],
                                      vbuf[h * MOBA_TOP_K + r, half].astype(bf16))
            o = o + jnp.where(row_hd == h, oh, 0.0)
        o_ref[...] = o / stash_ref[:, ATT_HEAD_DIM:ATT_HEAD_DIM + 1]

    @pl.when(b == 0)
    def _():
        for c in range(K_SLOTS - 1):
            start_chunk(0, c, c)

    @pl.when(b < n_seq)
    def _():
        gate_ref[...] = jnp.zeros((8, LANES), f32)
        stream(range(0, CHUNKS_BEFORE_FINISH))

    @pl.when(b > 0)
    def _():
        finish_previous()

    @pl.when(b < n_seq)
    def _():
        stream(range(CHUNKS_BEFORE_FINISH, n_chunks))
        choose_and_weigh()


def moba_decode(page_table, q, k_new, v_new, cache_kt, cache_vt, layer):
    b = q.shape[0]
    n_pages = page_table.shape[1]
    n_blocks = n_pages // PAGES_PER_BLOCK
    assert n_blocks <= LANES and n_blocks >= MOBA_TOP_K
    heads8 = lambda a: jnp.pad(a.reshape(b, ATT_HEADS, ATT_HEAD_DIM), ((0, 0), (0, 8 - ATT_HEADS), (0, 0)))
    n_picks = ATT_HEADS * MOBA_TOP_K
    cur = lambda i, pt: (jnp.minimum(i, b - 1), 0, 0)
    prev = lambda i, pt: (jnp.maximum(i - 1, 0), 0, 0)
    per_head = pl.BlockSpec((None, 8, ATT_HEAD_DIM), cur)
    grid_spec = pltpu.PrefetchScalarGridSpec(
        num_scalar_prefetch=1, grid=(b + 1,),
        in_specs=[pl.BlockSpec((None, 1, ATT_DIM), cur), per_head, per_head, per_head,
                  pl.BlockSpec(memory_space=pl.ANY), pl.BlockSpec(memory_space=pl.ANY)],
        out_specs=pl.BlockSpec((None, 8, ATT_HEAD_DIM), prev),
        scratch_shapes=[pltpu.VMEM((K_SLOTS, K_CHUNK_PAGES, ATT_DIM, PAGE_SIZE), f32),
                        pltpu.VMEM((n_picks, PAGES_PER_BLOCK, ATT_HEAD_DIM, PAGE_SIZE), f32),
                        pltpu.VMEM((n_blocks, 8, MOBA_BLOCK), f32),
                        pltpu.VMEM((n_blocks, 8, MOBA_BLOCK), f32),
                        pltpu.VMEM((8, LANES), f32),
                        pltpu.VMEM((8, LANES), f32),
                        pltpu.SMEM((n_picks * (1 + PAGES_PER_BLOCK),), jnp.int32),
                        pltpu.SemaphoreType.DMA((K_SLOTS,)),
                        pltpu.SemaphoreType.DMA((1,))])
    out = pl.pallas_call(
        functools.partial(_moba_decode_kernel, layer=layer, n_pages=n_pages),
        grid_spec=grid_spec,
        out_shape=jax.ShapeDtypeStruct((b, 8, ATT_HEAD_DIM), f32),
        compiler_params=_cparams(1), name="moba_decode",
    )(page_table, q.reshape(b, 1, ATT_DIM), heads8(q), heads8(k_new), heads8(v_new), cache_kt, cache_vt)
    return out[:, :ATT_HEADS].reshape(b, ATT_DIM)


def _layer_params(l, w_in, conv_w, conv_b, dt_bias, a_log, d_skip, ssm_norm_g, pool_w, pool_scale, w_out):
    o_xbc = SSM_INNER
    o_dt = o_xbc + CONV_DIM
    o_pool = o_dt + SSM_HEADS
    o_q = o_pool + POOL_DIM
    o_k = o_q + ATT_DIM
    o_v = o_k + ATT_DIM
    wl = w_in[l]
    pad_heads = lambda v: jnp.pad(v.astype(f32), (0, LANES - SSM_HEADS)).reshape(1, LANES)
    w_dt = jnp.pad(wl[:, o_dt:o_pool], ((0, 0), (0, LANES - SSM_HEADS)))
    w_split = [wl[:, :o_xbc], wl[:, o_xbc:o_dt], w_dt, wl[:, o_pool:o_q], wl[:, o_q:o_k],
               wl[:, o_k:o_v], wl[:, o_v:]]
    pw = jnp.zeros((POOL_DIM, POOL_DIM), f32)
    for g in range(len(POOL_WINDOWS)):
        pw = pw.at[g * POOL_GROUP:(g + 1) * POOL_GROUP, g * POOL_GROUP:(g + 1) * POOL_GROUP].set(pool_w[l, g])
    expand = (jnp.arange(LANES)[:, None] == (jnp.arange(SSM_INNER)[None, :] // SSM_HEAD_DIM)).astype(f32)
    wo = w_out[l].astype(bf16)
    return {
        'w_in': [w.astype(bf16) for w in w_split],
        'conv_w': conv_w[l], 'conv_b': conv_b[l].reshape(1, CONV_DIM),
        'dt_bias': pad_heads(dt_bias[l]), 'a': pad_heads(-jnp.exp(a_log[l].astype(f32))),
        'dsk': jnp.repeat(d_skip[l].astype(f32), SSM_HEAD_DIM).reshape(1, SSM_INNER),
        'ssm_norm_g': ssm_norm_g[l].reshape(1, SSM_INNER),
        'pool_w': pw.astype(bf16), 'pool_scale': pool_scale[l].reshape(1, POOL_DIM),
        'expand': expand,
        'w_out': [wo[:SSM_INNER], wo[SSM_INNER:SSM_INNER + POOL_DIM], wo[SSM_INNER + POOL_DIM:]],
    }


IN_PROJ_OUTS = [(i, f32) for i in range(7)]


def _pages_channel_major(cache):
    d, n_phys = cache.shape[:2]
    return jnp.transpose(cache, (0, 1, 3, 4, 2)).reshape(d, n_phys, ATT_DIM, PAGE_SIZE)


def kernel(x_prompt, x_sample, cache_moba_k, cache_moba_v, state_ssm, state_conv, state_pool, cache_mem_k, cache_mem_v, page_table, mem_prompt, norm_mix_g, w_in, conv_w, conv_b, dt_bias, a_log, d_skip, ssm_norm_g, pool_w, pool_scale, w_out, norm_cross_g, norm_mem_g, w_mem_q, w_mem_kv, w_mem_o, norm_ffn_g, w_gate_up, w_down, final_norm_g):
    depth = w_in.shape[0]
    bp, t, d = x_prompt.shape
    bs = x_sample.shape[0]
    assert bp == 1 and x_sample.shape[1] == 1
    past_len = page_table.shape[1] * PAGE_SIZE
    cache_kt = _pages_channel_major(cache_moba_k)
    cache_vt = _pages_channel_major(cache_moba_v)
    mem_k_split = _mem_split_view(cache_mem_k)
    mem_v_split = _mem_split_view(cache_mem_v)
    split_order = _split_channel_order()
    mem_len = mem_prompt.shape[1]
    xp = x_prompt.reshape(t, d)
    xs = x_sample.reshape(bs, d)
    mem = mem_prompt.reshape(mem_len, d)
    tm_p, tm_s = 512, bs
    outs = {n: [] for n in ('kp', 'vp', 'ks', 'vs', 'sp', 'ss', 'cp', 'cs', 'pp', 'ps', 'mk', 'mv')}
    for l in range(depth):
        prm = _layer_params(l, w_in, conv_w, conv_b, dt_bias, a_log, d_skip, ssm_norm_g, pool_w, pool_scale, w_out)
        wq = w_mem_q[l].astype(bf16)
        wkv = w_mem_kv[l].astype(bf16)
        wo_mem = w_mem_o[l].astype(bf16)
        wgu = w_gate_up[l].astype(bf16)
        wdn = w_down[l].astype(bf16)
        half = MEM_HEADS * MEM_HEAD_DIM
        last_layer = l == depth - 1

        mk, mv, mkb, mvb = norm_matmul(mem, norm_mem_g[l], [wkv[:, :half], wkv[:, half:]],
                                       [(0, f32), (1, f32), (0, bf16), (1, bf16)], tm=mem_len)
        z, xbc, dtr, u, q, k, v = norm_matmul(xp, norm_mix_g[l], prm['w_in'], IN_PROJ_OUTS, tm=tm_p)
        y, po, s_new = ssd_pool_prompt(z, xbc, dtr, u, prm, tm=tm_p)
        att = moba_prompt(q, k, v)
        xp = prompt_tail(xp, [y, po, att], prm['w_out'], norm_cross_g[l], wq, mkb, mvb, wo_mem,
                         norm_ffn_g[l], wgu, wdn, final_norm_g, norm_out=last_layer, tm=tm_p)
        outs['kp'].append(k.reshape(1, t, ATT_HEADS, ATT_HEAD_DIM))
        outs['vp'].append(v.reshape(1, t, ATT_HEADS, ATT_HEAD_DIM))
        outs['sp'].append(s_new[None])
        outs['cp'].append(xbc[t - (CONV_W - 1):][None])
        outs['pp'].append(u[t - POOL_HIST:][None])
        outs['mk'].append(mk.reshape(1, mem_len, MEM_HEADS, MEM_HEAD_DIM))
        outs['mv'].append(mv.reshape(1, mem_len, MEM_HEADS, MEM_HEAD_DIM))

        z, xbc, dtr, u, q, k, v = norm_matmul(xs, norm_mix_g[l], prm['w_in'], IN_PROJ_OUTS, tm=tm_s)
        y, po, c_new, s_new, p_new = ssd_pool_sample(z, xbc, dtr, u, state_conv[l], state_ssm[l],
                                                     state_pool[l], prm, pos0=past_len)
        att = moba_decode(page_table, q, k, v, cache_kt, cache_vt, layer=l)
        xs = matmul_residual(xs, [y, po, att], prm['w_out'], tm=tm_s)
        (qc,) = norm_matmul(xs, norm_cross_g[l], [wq[:, split_order]], [(0, f32)], tm=tm_s)
        oc = cross_sample(qc, mem_k_split, mem_v_split, layer=l)
        xs = matmul_residual(xs, [oc], [wo_mem[split_order, :]], tm=tm_s)
        xs = swiglu_block(xs, norm_ffn_g[l], wgu, wdn, final_norm_g, norm_out=last_layer, tm=tm_s, tf=D_FF)
        outs['ks'].append(k.reshape(bs, 1, ATT_HEADS, ATT_HEAD_DIM))
        outs['vs'].append(v.reshape(bs, 1, ATT_HEADS, ATT_HEAD_DIM))
        outs['ss'].append(s_new)
        outs['cs'].append(c_new)
        outs['ps'].append(p_new)

    y_prompt = xp.reshape(1, t, d)
    y_sample = xs.reshape(bs, 1, d)
    st = lambda n: jnp.stack(outs[n])
    return (y_prompt, y_sample, st('kp'), st('vp'), st('ks'), st('vs'), st('sp'), st('ss'),
            st('cp'), st('cs'), st('pp'), st('ps'), st('mk'), st('mv'))
```

```python
import functools

import jax
import jax.numpy as jnp
from jax import lax
from jax.experimental import pallas as pl
from jax.experimental.pallas import tpu as pltpu

f32 = jnp.float32
bf16 = jnp.bfloat16
HIGHEST = lax.Precision.HIGHEST

D_MODEL = 1024
SSM_INNER = 512
SSM_HEAD_DIM = 64
SSM_HEADS = 8
SSM_GROUPS = 2
SSM_STATE = 128
CONV_W = 4
CONV_DIM = SSM_INNER + 2 * SSM_GROUPS * SSM_STATE
SSD_CHUNK = 128
POOL_DIM = 256
POOL_WINDOWS = (2, 4, 8, 16)
POOL_GROUP = 64
POOL_HIST = 15
ATT_DIM = 256
ATT_HEAD_DIM = 64
ATT_HEADS = 4
MOBA_BLOCK = 256
MOBA_TOP_K = 3
PAGE_SIZE = 128
MEM_HEADS = 4
MEM_HEAD_DIM = 256
D_FF = 2816
RMS_EPS = 1e-6
LANES = 128
CONV_HALO = 8
POOL_HALO = 16
MASKED = -1e30
LOG2E = 1.4426950408889634
BF16_SUBLANES = 16
V_AUG_ROWS = ATT_HEAD_DIM + BF16_SUBLANES
VMEM_LIMIT = 56 * 1024 * 1024


def _cparams(n_axes):
    return pltpu.CompilerParams(dimension_semantics=("arbitrary",) * n_axes,
                                vmem_limit_bytes=VMEM_LIMIT)


def _rms(x, g):
    ms = jnp.mean(x * x, axis=-1, keepdims=True)
    return x * lax.rsqrt(ms + RMS_EPS) * g


def _dot(a, b, **kw):
    return jnp.dot(a, b, preferred_element_type=f32, **kw)


def _dot_nt(a, b, **kw):
    return lax.dot_general(a, b, (((1,), (1,)), ((), ())), preferred_element_type=f32, **kw)


def _silu(x):
    return x * jax.nn.sigmoid(x)


def _softplus(x):
    return jnp.maximum(x, 0.0) + jnp.log1p(jnp.exp(-jnp.abs(x)))


def _full(shape):
    return pl.BlockSpec(shape, lambda *_: (0,) * len(shape))


def _norm_mm_kernel(x_ref, g_ref, w_ref, *o_refs, groups):
    hb = _rms(x_ref[...], g_ref[...]).astype(bf16)
    res = {}
    for o_ref, (start, size) in zip(o_refs, groups):
        if (start, size) not in res:
            res[start, size] = _dot(hb, w_ref[:, start:start + size])
        o_ref[...] = res[start, size].astype(o_ref.dtype)


def norm_matmul(x, g, w, outs, tm):
    m, d = x.shape
    assert all(start % LANES == 0 and size % LANES == 0 for start, size, _ in outs)
    return pl.pallas_call(
        functools.partial(_norm_mm_kernel, groups=tuple((s, n) for s, n, _ in outs)),
        grid=(m // tm,),
        in_specs=[pl.BlockSpec((tm, d), lambda i: (i, 0)), _full((1, d)), _full(w.shape)],
        out_specs=[pl.BlockSpec((tm, n), lambda i: (i, 0)) for _, n, _ in outs],
        out_shape=[jax.ShapeDtypeStruct((m, n), dt) for _, n, dt in outs],
        compiler_params=_cparams(1), name="norm_matmul",
    )(x, g.reshape(1, d), w)


def _mm_res_kernel(*refs, n_a):
    res_ref = refs[0]
    a_refs = refs[1:1 + n_a]
    w_refs = refs[1 + n_a:1 + 2 * n_a]
    o_ref = refs[-1]
    acc = res_ref[...]
    for a, w in zip(a_refs, w_refs):
        acc = acc + _dot(a[...].astype(bf16), w[...])
    o_ref[...] = acc


def matmul_residual(res, a_list, w_list, tm):
    m, d = res.shape
    in_specs = [pl.BlockSpec((tm, d), lambda i: (i, 0))]
    in_specs += [pl.BlockSpec((tm, a.shape[1]), lambda i: (i, 0)) for a in a_list]
    in_specs += [_full(w.shape) for w in w_list]
    return pl.pallas_call(
        functools.partial(_mm_res_kernel, n_a=len(a_list)),
        grid=(m // tm,), in_specs=in_specs,
        out_specs=pl.BlockSpec((tm, d), lambda i: (i, 0)),
        out_shape=jax.ShapeDtypeStruct((m, d), f32),
        compiler_params=_cparams(1), name="matmul_residual",
    )(res, *a_list, *w_list)


def _prompt_tail_kernel(*refs, n_mix, norm_out):
    x_ref = refs[0]
    a_refs = refs[1:1 + n_mix]
    w_refs = refs[1 + n_mix:1 + 2 * n_mix]
    (g_ref, wq_ref, mk_ref, mv_ref, wo_ref,
     gf_ref, wg_ref, wu_ref, wd_ref, gout_ref, o_ref) = refs[1 + 2 * n_mix:]
    x = x_ref[...]
    for a, w in zip(a_refs, w_refs):
        x = x + _dot(a[...].astype(bf16), w[...])
    if n_mix:
        o_ref[...] = x
        x = o_ref[...]
    hb = _rms(x, g_ref[...]).astype(bf16)
    q = _dot(hb, wq_ref[...])
    acc = x
    for h in range(MEM_HEADS):
        sl = slice(h * MEM_HEAD_DIM, (h + 1) * MEM_HEAD_DIM)
        s = _dot_nt(q[:, sl].astype(bf16), mk_ref[:, sl]) * (MEM_HEAD_DIM ** -0.5)
        p = jnp.exp(s - jnp.max(s, axis=-1, keepdims=True))
        p = p / jnp.sum(p, axis=-1, keepdims=True)
        oh = _dot(p.astype(bf16), mv_ref[:, sl])
        acc = acc + _dot(oh.astype(bf16), wo_ref[sl, :])
    o_ref[...] = acc
    x2 = o_ref[...]
    hb = _rms(x2, gf_ref[...]).astype(bf16)
    act = _silu(_dot(hb, wg_ref[...])) * _dot(hb, wu_ref[...])
    y = x2 + _dot(act.astype(bf16), wd_ref[...])
    o_ref[...] = _rms(y, gout_ref[...]) if norm_out else y


def prompt_tail(x, mix_list, w_out_list, g, wq, mkb, mvb, wo, g_ffn, w_gu, w_d, g_out, norm_out, tm):
    m, d = x.shape
    row = lambda c: pl.BlockSpec((tm, c), lambda i: (i, 0))
    once = lambda shape, idx: pl.BlockSpec(shape, idx, pipeline_mode=pl.Buffered(1))
    resident = lambda a: once(a.shape, lambda i: (0,) * a.ndim)
    consts = [g.reshape(1, d), wq, mkb, mvb, wo, g_ffn.reshape(1, d)]
    tail = [g_out.reshape(1, d)]
    return pl.pallas_call(
        functools.partial(_prompt_tail_kernel, n_mix=len(mix_list), norm_out=norm_out), grid=(m // tm,),
        in_specs=[row(d)] + [row(a.shape[1]) for a in mix_list] + [resident(w) for w in w_out_list]
                 + [resident(c) for c in consts]
                 + [once((d, D_FF), lambda i: (0, 0)), once((d, D_FF), lambda i: (0, 1)), resident(w_d)]
                 + [resident(c) for c in tail],
        out_specs=row(d),
        out_shape=jax.ShapeDtypeStruct((m, d), f32),
        compiler_params=_cparams(1), name="prompt_tail",
    )(x, *mix_list, *w_out_list, *consts, w_gu, w_gu, w_d, *tail)


MEM_SPLIT = MEM_HEAD_DIM // LANES
MEM_ROWS = MEM_SPLIT * MEM_HEADS


def _split_channel_order():
    return jnp.arange(MEM_HEADS * MEM_HEAD_DIM).reshape(MEM_HEADS, MEM_SPLIT, LANES).transpose(1, 0, 2).reshape(-1)


def _mem_split_view(cache):
    d, b, m = cache.shape[:3]
    x = cache.reshape(d, b, m, MEM_HEADS, MEM_SPLIT, LANES)
    return jnp.transpose(x, (0, 1, 2, 4, 3, 5)).reshape(d, b, m, MEM_ROWS, LANES)


def _cross_sample_kernel(q_ref, mk_ref, mv_ref, o_ref):
    part = jnp.sum(mk_ref[...] * q_ref[...][None], axis=-1, keepdims=True)
    s = part
    for piece in range(1, MEM_SPLIT):
        s = s + jnp.roll(part, piece * MEM_HEADS, axis=1)
    s = s * (MEM_HEAD_DIM ** -0.5)
    p = jnp.exp(s - jnp.max(s, axis=0, keepdims=True))
    l = jnp.sum(p, axis=0)
    o_ref[...] = jnp.sum(p * mv_ref[...], axis=0) / l


def cross_sample(q_split, mem_k_split, mem_v_split, layer):
    b = q_split.shape[0]
    mlen = mem_k_split.shape[2]
    mem_spec = pl.BlockSpec((None, None, mlen, MEM_ROWS, LANES), lambda i: (layer, i, 0, 0, 0))
    row_spec = pl.BlockSpec((None, MEM_ROWS, LANES), lambda i: (i, 0, 0))
    out = pl.pallas_call(
        _cross_sample_kernel, grid=(b,),
        in_specs=[row_spec, mem_spec, mem_spec], out_specs=row_spec,
        out_shape=jax.ShapeDtypeStruct((b, MEM_ROWS, LANES), f32),
        compiler_params=_cparams(1), name="cross_sample",
    )(q_split.reshape(b, MEM_ROWS, LANES), mem_k_split, mem_v_split)
    return out.reshape(b, MEM_ROWS * LANES)


def _swiglu_kernel(x_ref, g_ref, wg_ref, wu_ref, wd_ref, gout_ref, o_ref, h_ref, acc_ref, *, norm_out):
    f = pl.program_id(1)

    @pl.when(f == 0)
    def _():
        h_ref[...] = _rms(x_ref[...], g_ref[...]).astype(bf16)
        acc_ref[...] = x_ref[...]

    hb = h_ref[...]
    a = _silu(_dot(hb, wg_ref[...])) * _dot(hb, wu_ref[...])
    acc_ref[...] += _dot(a.astype(bf16), wd_ref[...])

    @pl.when(f == pl.num_programs(1) - 1)
    def _():
        y = acc_ref[...]
        o_ref[...] = _rms(y, gout_ref[...]) if norm_out else y


def swiglu_block(x, g, w_gu, w_d, g_out, norm_out, tm, tf):
    m, d = x.shape
    nf = D_FF // tf
    mode = dict(pipeline_mode=pl.Buffered(1)) if nf == 1 else {}
    return pl.pallas_call(
        functools.partial(_swiglu_kernel, norm_out=norm_out), grid=(m // tm, nf),
        in_specs=[pl.BlockSpec((tm, d), lambda i, f: (i, 0)), _full((1, d)),
                  pl.BlockSpec((d, tf), lambda i, f: (0, f), **mode),
                  pl.BlockSpec((d, tf), lambda i, f: (0, f + nf), **mode),
                  pl.BlockSpec((tf, d), lambda i, f: (f, 0), **mode), _full((1, d))],
        out_specs=pl.BlockSpec((tm, d), lambda i, f: (i, 0)),
        out_shape=jax.ShapeDtypeStruct((m, d), f32),
        scratch_shapes=[pltpu.VMEM((tm, d), bf16), pltpu.VMEM((tm, d), f32)],
        compiler_params=_cparams(2), name="swiglu_block",
    )(x, g.reshape(1, d), w_gu, w_gu, w_d, g_out.reshape(1, d))


def _pool_select(w2, w4, w8, w16, cnt_of, lane):
    out = w16 / cnt_of(16)
    for win, acc in ((8, w8), (4, w4), (2, w2)):
        g = POOL_WINDOWS.index(win)
        out = jnp.where(lane < (g + 1) * POOL_GROUP, acc / cnt_of(win), out)
    return out


def _group_rmsnorm(y, g):
    half = SSM_INNER // SSM_GROUPS
    parts = [_rms(y[:, i * half:(i + 1) * half], g[:, i * half:(i + 1) * half])
             for i in range(SSM_GROUPS)]
    return jnp.concatenate(parts, axis=-1)


def _ssd_pool_prompt_kernel(z_ref, xbc_ref, dt_ref, u_ref, cw_ref, cb_ref, dtb_ref, a_ref, dsk_ref,
                            ng_ref, pw_ref, ps_ref,
                            y_ref, po_ref, st_ref,
                            ext_ref, pext_ref, xc_ref, s_ref, *, tm):
    i = pl.program_id(0)
    halo = CONV_HALO
    phalo = POOL_HALO

    @pl.when(i == 0)
    def _():
        ext_ref[0:halo, :] = jnp.zeros((halo, CONV_DIM), f32)
        pext_ref[0:phalo, :] = jnp.zeros((phalo, POOL_DIM), f32)
        s_ref[...] = jnp.zeros(s_ref.shape, f32)

    @pl.when(i > 0)
    def _():
        ext_ref[0:halo, :] = ext_ref[tm:tm + halo, :]
        pext_ref[0:phalo, :] = pext_ref[tm:tm + phalo, :]

    ext_ref[halo:halo + tm, :] = xbc_ref[...]
    pext_ref[phalo:phalo + tm, :] = u_ref[...]

    rp, lp = 64, 256
    for r0 in range(0, tm, rp):
        for c0 in range(0, CONV_DIM, lp):
            acc = jnp.broadcast_to(cb_ref[:, c0:c0 + lp], (rp, lp))
            for j in range(CONV_W):
                off = halo - (CONV_W - 1) + j + r0
                acc = acc + ext_ref[off:off + rp, c0:c0 + lp] * cw_ref[j:j + 1, c0:c0 + lp]
            xc_ref[r0:r0 + rp, c0:c0 + lp] = _silu(acc)

    lane_p = lax.broadcasted_iota(jnp.int32, (rp, POOL_DIM), 1)
    row_p = lax.broadcasted_iota(jnp.int32, (rp, POOL_DIM), 0)
    for r0 in range(0, tm, rp):
        u = pext_ref[phalo + r0:phalo + r0 + rp, :]
        run = u
        sums = {}
        for back in range(1, max(POOL_WINDOWS)):
            run = run + pext_ref[phalo - back + r0:phalo - back + r0 + rp, :]
            if back + 1 in POOL_WINDOWS:
                sums[back + 1] = run
        pos1 = (row_p + (i * tm + r0 + 1)).astype(f32)
        pooled = _pool_select(sums[2], sums[4], sums[8], sums[16],
                              lambda win: jnp.minimum(float(win), pos1), lane_p) - u
        po_ref[r0:r0 + rp, :] = (_dot(pooled.astype(bf16), pw_ref[...]) * ps_ref[...]).astype(po_ref.dtype)

    cl = SSD_CHUNK
    r_io = lax.broadcasted_iota(jnp.int32, (cl, cl), 0)
    c_io = lax.broadcasted_iota(jnp.int32, (cl, cl), 1)
    ltri = (c_io <= r_io).astype(f32)
    causal = c_io <= r_io
    first_half = c_io < SSM_HEAD_DIM
    n_pairs = SSM_HEADS // 2
    heads_per_group = SSM_HEADS // SSM_GROUPS

    def chunk(c, carry):
        r0 = pl.multiple_of(c * cl, cl)
        xc = xc_ref[pl.ds(r0, cl), :]
        dt = _softplus(dt_ref[pl.ds(r0, cl), :] + dtb_ref[...])
        a_cum = _dot(ltri, dt * a_ref[...], precision=HIGHEST)
        a_cum_t = a_cum.T
        dt_t = dt.T
        to_end_t = jnp.exp(a_cum_t[:, cl - 1:cl] - a_cum_t) * dt_t
        b_t = [xc[:, SSM_INNER + g * SSM_STATE:SSM_INNER + (g + 1) * SSM_STATE].T
               for g in range(SSM_GROUPS)]
        cm = [xc[:, SSM_INNER + SSM_GROUPS * SSM_STATE + g * SSM_STATE:
                 SSM_INNER + SSM_GROUPS * SSM_STATE + (g + 1) * SSM_STATE].astype(bf16)
              for g in range(SSM_GROUPS)]
        scores = [_dot(cm[g], b_t[g].astype(bf16)) for g in range(SSM_GROUPS)]
        y_pairs = []
        for k in range(n_pairs):
            g = (2 * k) // heads_per_group
            xs_pair = xc[:, k * LANES:(k + 1) * LANES]
            xs_b = xs_pair.astype(bf16)
            yd, cs, eb = [], [], []
            for h in (2 * k, 2 * k + 1):
                colb = jnp.broadcast_to(a_cum[:, h:h + 1], (cl, cl))
                seg = colb - a_cum_t[h:h + 1, :]
                decay = jnp.exp(jnp.where(causal, seg, -jnp.inf))
                mh = scores[g] * decay * dt_t[h:h + 1, :]
                yd.append(_dot(mh.astype(bf16), xs_b))
                cs.append(_dot((b_t[g] * to_end_t[h:h + 1, :]).astype(bf16), xs_b))
                eb.append(jnp.exp(colb))
            e_pair = jnp.where(first_half, eb[0], eb[1])
            s_old = s_ref[k]
            y_off = _dot(cm[g], s_old.astype(bf16)) * e_pair
            y_pairs.append(jnp.where(first_half, yd[0], yd[1]) + y_off
                           + dsk_ref[:, k * LANES:(k + 1) * LANES] * xs_pair)
            s_ref[k] = s_old * e_pair[cl - 1:cl, :] + jnp.where(first_half, cs[0], cs[1])
        y = jnp.concatenate(y_pairs, axis=-1) * _silu(z_ref[pl.ds(r0, cl), :])
        y_ref[pl.ds(r0, cl), :] = _group_rmsnorm(y, ng_ref[...]).astype(y_ref.dtype)
        return carry

    lax.fori_loop(0, tm // cl, chunk, 0, unroll=True)

    @pl.when(i == pl.num_programs(0) - 1)
    def _():
        for k in range(n_pairs):
            st = s_ref[k].T
            st_ref[2 * k] = st[0:SSM_HEAD_DIM, :]
            st_ref[2 * k + 1] = st[SSM_HEAD_DIM:2 * SSM_HEAD_DIM, :]


def ssd_pool_prompt(z, xbc, dt, u, prm, tm):
    t = z.shape[0]
    row = lambda c: pl.BlockSpec((tm, c), lambda i: (i, 0))
    consts = [prm['conv_w'], prm['conv_b'], prm['dt_bias'], prm['a'], prm['dsk'], prm['ssm_norm_g'],
              prm['pool_w'], prm['pool_scale']]
    return pl.pallas_call(
        functools.partial(_ssd_pool_prompt_kernel, tm=tm), grid=(t // tm,),
        in_specs=[row(SSM_INNER), row(CONV_DIM), row(LANES), row(POOL_DIM)] + [_full(c.shape) for c in consts],
        out_specs=[row(SSM_INNER), row(POOL_DIM), _full((SSM_HEADS, SSM_HEAD_DIM, SSM_STATE))],
        out_shape=[jax.ShapeDtypeStruct((t, SSM_INNER), bf16), jax.ShapeDtypeStruct((t, POOL_DIM), bf16),
                   jax.ShapeDtypeStruct((SSM_HEADS, SSM_HEAD_DIM, SSM_STATE), f32)],
        scratch_shapes=[pltpu.VMEM((tm + CONV_HALO, CONV_DIM), f32), pltpu.VMEM((tm + POOL_HALO, POOL_DIM), f32),
                        pltpu.VMEM((tm, CONV_DIM), f32),
                        pltpu.VMEM((SSM_HEADS // 2, SSM_STATE, 2 * SSM_HEAD_DIM), f32)],
        compiler_params=_cparams(1), name="ssd_pool_prompt",
    )(z, xbc, dt, u, *consts)


def _col_tile(row):
    return jnp.broadcast_to(row, (LANES, LANES)).T


def _ssd_pool_sample_kernel(z_ref, xbc_ref, dt_ref, u_ref, sc_ref, ss_ref, sp_ref,
                            cw_ref, cb_ref, dtb_ref, a_ref, dsk_ref, ng_ref, pw_ref, ps_ref, ex_ref,
                            y_ref, po_ref, cn_ref, sn_ref, pn_ref, *, pos0):
    xrow = xbc_ref[...]
    acc = cb_ref[...] + xrow * cw_ref[CONV_W - 1:CONV_W, :]
    for j in range(CONV_W - 1):
        acc = acc + sc_ref[j:j + 1, :] * cw_ref[j:j + 1, :]
        if j > 0:
            cn_ref[j - 1:j, :] = sc_ref[j:j + 1, :]
    cn_ref[CONV_W - 2:CONV_W - 1, :] = xrow
    xc = _silu(acc)

    dt = _softplus(dt_ref[...] + dtb_ref[...])
    dta = dt * a_ref[...]
    both = jnp.concatenate([jnp.broadcast_to(dt, (8, LANES)), jnp.broadcast_to(dta, (8, LANES))], axis=0)
    both_x = _dot(both, ex_ref[...], precision=HIGHEST)
    dtx = both_x[0:1, :]
    dec_x = jnp.exp(both_x[8:9, :])
    xs = xc[:, 0:SSM_INNER]
    xdt = xs * dtx
    heads_per_group = SSM_HEADS // SSM_GROUPS
    y_pairs = []
    for k in range(SSM_HEADS // 2):
        g = (2 * k) // heads_per_group
        b_row = xc[:, SSM_INNER + g * SSM_STATE:SSM_INNER + (g + 1) * SSM_STATE]
        c_row = xc[:, SSM_INNER + SSM_GROUPS * SSM_STATE + g * SSM_STATE:
                   SSM_INNER + SSM_GROUPS * SSM_STATE + (g + 1) * SSM_STATE]
        sl = slice(k * LANES, (k + 1) * LANES)
        s_old = jnp.concatenate([ss_ref[2 * k], ss_ref[2 * k + 1]], axis=0)
        s_new = s_old * _col_tile(dec_x[:, sl]) + _col_tile(xdt[:, sl]) * b_row
        sn_ref[2 * k] = s_new[0:SSM_HEAD_DIM, :]
        sn_ref[2 * k + 1] = s_new[SSM_HEAD_DIM:, :]
        y_k = _dot_nt(jnp.broadcast_to(c_row, (8, SSM_STATE)), s_new, precision=HIGHEST)[0:1, :]
        y_pairs.append(y_k + dsk_ref[:, sl] * xs[:, sl])
    y = jnp.concatenate(y_pairs, axis=-1) * _silu(z_ref[...])
    y_ref[...] = _group_rmsnorm(y, ng_ref[...]).astype(y_ref.dtype)

    u = u_ref[...]
    prev = sp_ref[...]
    rowi = lax.broadcasted_iota(jnp.int32, prev.shape, 0)
    tail = lambda win: u + jnp.sum(jnp.where(rowi >= POOL_HIST - (win - 1), prev, 0.0), axis=0, keepdims=True)
    lane_p = lax.broadcasted_iota(jnp.int32, (1, POOL_DIM), 1)
    pooled = _pool_select(tail(2), tail(4), tail(8), tail(16),
                          lambda win: float(min(win, pos0 + 1)), lane_p) - u
    po = _dot(jnp.broadcast_to(pooled, (8, POOL_DIM)).astype(bf16), pw_ref[...])[0:1, :] * ps_ref[...]
    po_ref[...] = po.astype(po_ref.dtype)
    pn_ref[0:POOL_HIST - 1, :] = sp_ref[1:POOL_HIST, :]
    pn_ref[POOL_HIST - 1:POOL_HIST, :] = u


def ssd_pool_sample(z, xbc, dt, u, st_conv, st_ssm, st_pool, prm, pos0):
    b = z.shape[0]
    per_seq = lambda *shape: pl.BlockSpec((None,) + shape, lambda i: (i,) + (0,) * len(shape))
    consts = [prm['conv_w'], prm['conv_b'], prm['dt_bias'], prm['a'], prm['dsk'], prm['ssm_norm_g'],
              prm['pool_w'], prm['pool_scale'], prm['expand']]
    outs = pl.pallas_call(
        functools.partial(_ssd_pool_sample_kernel, pos0=pos0), grid=(b,),
        in_specs=[per_seq(1, SSM_INNER), per_seq(1, CONV_DIM), per_seq(1, LANES), per_seq(1, POOL_DIM),
                  per_seq(CONV_W - 1, CONV_DIM), per_seq(SSM_HEADS, SSM_HEAD_DIM, SSM_STATE),
                  per_seq(POOL_HIST, POOL_DIM)] + [_full(c.shape) for c in consts],
        out_specs=[per_seq(1, SSM_INNER), per_seq(1, POOL_DIM), per_seq(CONV_W - 1, CONV_DIM),
                   per_seq(SSM_HEADS, SSM_HEAD_DIM, SSM_STATE), per_seq(POOL_HIST, POOL_DIM)],
        out_shape=[jax.ShapeDtypeStruct((b, 1, SSM_INNER), f32), jax.ShapeDtypeStruct((b, 1, POOL_DIM), f32),
                   jax.ShapeDtypeStruct((b, CONV_W - 1, CONV_DIM), f32),
                   jax.ShapeDtypeStruct((b, SSM_HEADS, SSM_HEAD_DIM, SSM_STATE), f32),
                   jax.ShapeDtypeStruct((b, POOL_HIST, POOL_DIM), f32)],
        compiler_params=_cparams(1), name="ssd_pool_sample",
    )(z.reshape(b, 1, -1), xbc.reshape(b, 1, -1), dt.reshape(b, 1, -1), u.reshape(b, 1, -1),
      st_conv, st_ssm, st_pool, *consts)
    y, po, cn, sn, pn = outs
    return y.reshape(b, -1), po.reshape(b, -1), cn, sn, pn


def _topk_blocks(gate, n_past, axis):
    blk = lax.broadcasted_iota(jnp.int32, gate.shape, axis).astype(f32)
    g = jnp.where(blk < jnp.asarray(n_past, f32), gate, -jnp.inf)
    picks = []
    for _ in range(MOBA_TOP_K):
        m = jnp.max(g, axis=axis, keepdims=True)
        idx = jnp.min(jnp.where(g == m, blk, float(LANES)), axis=axis, keepdims=True)
        picks.append((idx, jnp.abs(m) < jnp.inf))
        g = jnp.where(blk == idx, -jnp.inf, g)
    return picks


def _moba_prompt_kernel(q_ref, k_ref, v_ref, o_ref,
                        km_ref, kb_ref, vt_ref, qst_ref, nmt_ref, acc_ref, sa_ref, sb_ref):
    i = pl.program_id(0)
    tq = MOBA_BLOCK
    scale = ATT_HEAD_DIM ** -0.5

    @pl.when(i == 0)
    def _():
        km_ref[...] = jnp.zeros(km_ref.shape, f32)

    k = k_ref[...]
    kb = k.astype(bf16)
    head = lambda h: slice(h * ATT_HEAD_DIM, (h + 1) * ATT_HEAD_DIM)
    for h in range(ATT_HEADS):
        kb_ref[i, h] = kb[:, head(h)]
    vt = v_ref[...].T.astype(bf16)
    ones_rows = jnp.ones((V_AUG_ROWS - ATT_HEAD_DIM, tq), bf16)
    for h in range(ATT_HEADS):
        vt_ref[i, h] = jnp.concatenate([vt[head(h), :], ones_rows], axis=0)
    qt = q_ref[...].T
    n_slots = km_ref.shape[0]
    blk_row = lax.broadcasted_iota(jnp.int32, (n_slots, tq), 0).astype(f32)
    key_io = lax.broadcasted_iota(jnp.int32, (tq, tq), 0)
    qry_io = lax.broadcasted_iota(jnp.int32, (tq, tq), 1)
    km = km_ref[...]
    m_own = []
    for h in range(ATT_HEADS):
        qth = qt[head(h), :]
        gate = _dot(km[:, head(h)], qth, precision=HIGHEST)
        sel = jnp.zeros((n_slots, tq), jnp.bool_)
        for idx, ok in _topk_blocks(gate, i, axis=0):
            sel = sel | ((blk_row == idx) & ok)
        nmt_ref[h] = jnp.where(sel, 0.0, MASKED)
        qst = (qth * (scale * LOG2E)).astype(bf16)
        qst_ref[h] = qst
        st = jnp.where(key_io <= qry_io, _dot(kb[:, head(h)], qst), MASKED)
        m = jnp.max(st, axis=0, keepdims=True)
        m_own.append(m)
        acc_ref[h] = _dot(vt_ref[i, h], jnp.exp2(st - m).astype(bf16))

    km_ref[pl.ds(i, 1), :] = jnp.mean(k, axis=0, keepdims=True)

    def scores(j, dst):
        for h in range(ATT_HEADS):
            dst[h] = _dot(kb_ref[j, h], qst_ref[h])

    def absorb(j, src, carry):
        new = []
        for h in range(ATT_HEADS):
            m_prev = carry[h]
            raw = src[h]
            bias = nmt_ref[h, pl.ds(j, 1), :]
            m_new = jnp.maximum(m_prev, jnp.max(raw, axis=0, keepdims=True) + bias)
            alpha = jnp.exp2(m_prev - m_new)
            p = jnp.exp2(raw - (m_new - bias))
            new.append(m_new)
            acc_ref[h] = acc_ref[h] * alpha + _dot(vt_ref[j, h], p.astype(bf16))
        return tuple(new)

    last = jnp.maximum(i - 1, 0)
    scores(0, sa_ref)

    def pair(t, carry):
        scores(jnp.minimum(2 * t + 1, last), sb_ref)
        carry = absorb(2 * t, sa_ref, carry)
        scores(jnp.minimum(2 * t + 2, last), sa_ref)
        return absorb(2 * t + 1, sb_ref, carry)

    def pairs(first, count, c):
        for t in range(count):
            c = pair(first + t, c)
        return c

    n_octs = i // 8
    carry = lax.fori_loop(0, n_octs, lambda u, c: pairs(4 * u, 4, c), tuple(m_own))
    carry = lax.cond(i % 8 >= 4, lambda c: pairs(4 * n_octs, 2, c), lambda c: c, carry)
    carry = lax.cond(i % 4 >= 2, lambda c: pair(i // 4 * 2, c), lambda c: c, carry)
    carry = lax.cond(i % 2 == 1, lambda c: absorb(i - 1, sa_ref, c), lambda c: c, carry)
    out_t = jnp.concatenate(
        [acc_ref[h, 0:ATT_HEAD_DIM, :] / acc_ref[h, ATT_HEAD_DIM:ATT_HEAD_DIM + 1, :] for h in range(ATT_HEADS)],
        axis=0)
    o_ref[...] = out_t.T.astype(o_ref.dtype)


def moba_prompt(q, k, v):
    t = q.shape[0]
    tq = MOBA_BLOCK
    n_blk = t // tq
    n_slots = -(-n_blk // 8) * 8
    assert n_slots <= LANES
    tile = pl.BlockSpec((tq, ATT_DIM), lambda i: (i, 0))
    return pl.pallas_call(
        _moba_prompt_kernel, grid=(n_blk,),
        in_specs=[tile, tile, tile], out_specs=tile,
        out_shape=jax.ShapeDtypeStruct((t, ATT_DIM), bf16),
        scratch_shapes=[pltpu.VMEM((n_slots, ATT_DIM), f32),
                        pltpu.VMEM((n_blk, ATT_HEADS, tq, ATT_HEAD_DIM), bf16),
                        pltpu.VMEM((n_blk, ATT_HEADS, V_AUG_ROWS, tq), bf16),
                        pltpu.VMEM((ATT_HEADS, ATT_HEAD_DIM, tq), bf16),
                        pltpu.VMEM((ATT_HEADS, n_slots, tq), f32),
                        pltpu.VMEM((ATT_HEADS, V_AUG_ROWS, tq), f32),
                        pltpu.VMEM((ATT_HEADS, tq, tq), f32),
                        pltpu.VMEM((ATT_HEADS, tq, tq), f32)],
        compiler_params=_cparams(1), name="moba_prompt",
    )(q, k, v)


K_CHUNK_PAGES = 16
K_SLOTS = 4
PAGES_PER_BLOCK = MOBA_BLOCK // PAGE_SIZE
CHUNKS_BEFORE_FINISH = 2


def _moba_decode_kernel(pt_ref, q_ref, q8_ref, kn8_ref, vn8_ref, kc_ref, vc_ref, o_ref,
                        kbuf, vbuf, s_ref, p_ref, gate_ref, stash_ref, ids_ref, ksem, vsem, *, layer, n_pages):
    b = pl.program_id(0)
    n_seq = pl.num_programs(0) - 1
    n_chunks = n_pages // K_CHUNK_PAGES
    blocks_per_chunk = K_CHUNK_PAGES // PAGES_PER_BLOCK
    n_blocks = n_pages // PAGES_PER_BLOCK
    scale = ATT_HEAD_DIM ** -0.5

    def k_copy(seq, page_slot, slot, p):
        page = pt_ref[seq, page_slot]
        return pltpu.make_async_copy(kc_ref.at[layer, page], kbuf.at[slot, p], ksem.at[slot])

    def start_chunk(seq, c, slot):
        for p in range(K_CHUNK_PAGES):
            k_copy(seq, c * K_CHUNK_PAGES + p, slot, p).start()

    def wait_chunk(seq, c, slot):
        for p in range(K_CHUNK_PAGES):
            k_copy(seq, c * K_CHUNK_PAGES + p, slot, p).wait()

    n_picks = ATT_HEADS * MOBA_TOP_K

    def v_copy(h, r, half, page):
        return pltpu.make_async_copy(vc_ref.at[layer, page, pl.ds(h * ATT_HEAD_DIM, ATT_HEAD_DIM), :],
                                     vbuf.at[h * MOBA_TOP_K + r, half], vsem.at[0])

    def stream(chunks):
        lane8 = lax.broadcasted_iota(jnp.int32, (8, ATT_DIM), 1)
        row8 = lax.broadcasted_iota(jnp.int32, (8, ATT_DIM), 0)
        qblk_b = jnp.where(lane8 // ATT_HEAD_DIM == row8, q_ref[...], 0.0).astype(bf16)
        blk_lane = lax.broadcasted_iota(jnp.int32, (8, LANES), 1)
        gate = gate_ref[...]
        for c in chunks:
            slot = c % K_SLOTS
            ahead = c + K_SLOTS - 1
            if ahead < n_chunks:
                start_chunk(b, ahead, ahead % K_SLOTS)
            else:
                @pl.when(b + 1 < n_seq)
                def _():
                    start_chunk(b + 1, ahead - n_chunks, ahead % K_SLOTS)
            wait_chunk(b, c, slot)
            for t in range(blocks_per_chunk):
                blk = c * blocks_per_chunk + t
                halves = [_dot(qblk_b, kbuf[slot, t * PAGES_PER_BLOCK + half].astype(bf16))
                          for half in range(PAGES_PER_BLOCK)]
                for half in range(PAGES_PER_BLOCK):
                    s_ref[blk, :, half * PAGE_SIZE:(half + 1) * PAGE_SIZE] = halves[half]
                tot = jnp.sum(sum(halves), axis=-1, keepdims=True) * (1.0 / MOBA_BLOCK)
                gate = jnp.where(blk_lane == blk, tot, gate)
        gate_ref[...] = gate

    def choose_and_weigh():
        picks = _topk_blocks(gate_ref[...], n_blocks, axis=1)
        for h in range(ATT_HEADS):
            for r in range(MOBA_TOP_K):
                blk = picks[r][0][h, 0].astype(jnp.int32)
                ids_ref[h * MOBA_TOP_K + r] = blk
                for half in range(PAGES_PER_BLOCK):
                    page = pt_ref[b, blk * PAGES_PER_BLOCK + half]
                    ids_ref[n_picks + (h * MOBA_TOP_K + r) * PAGES_PER_BLOCK + half] = page
                    v_copy(h, r, half, page).start()
        blk_io = lax.broadcasted_iota(jnp.int32, (n_blocks, 8, MOBA_BLOCK), 0).astype(f32)
        seen = jnp.zeros((n_blocks, 8, MOBA_BLOCK), jnp.bool_)
        for idx, ok in picks:
            seen = seen | ((blk_io == idx[None]) & ok[None])
        s_all = jnp.where(seen, s_ref[...] * scale, -jnp.inf)
        s_own = jnp.sum(q8_ref[...] * kn8_ref[...], axis=-1, keepdims=True) * scale
        m = jnp.maximum(jnp.max(jnp.max(s_all, axis=0), axis=-1, keepdims=True), s_own)
        p_all = jnp.exp(s_all - m[None])
        p_own = jnp.exp(s_own - m)
        l = jnp.sum(jnp.sum(p_all, axis=0), axis=-1, keepdims=True) + p_own
        p_ref[...] = p_all
        stash_ref[:, 0:ATT_HEAD_DIM] = p_own * vn8_ref[...]
        stash_ref[:, ATT_HEAD_DIM:] = jnp.broadcast_to(l, (8, LANES - ATT_HEAD_DIM))

    def finish_previous():
        for h in range(ATT_HEADS):
            for r in range(MOBA_TOP_K):
                for half in range(PAGES_PER_BLOCK):
                    page = ids_ref[n_picks + (h * MOBA_TOP_K + r) * PAGES_PER_BLOCK + half]
                    v_copy(h, r, half, page).wait()
        row_hd = lax.broadcasted_iota(jnp.int32, (8, ATT_HEAD_DIM), 0)
        o = stash_ref[:, 0:ATT_HEAD_DIM]
        for h in range(ATT_HEADS):
            oh = jnp.zeros((8, ATT_HEAD_DIM), f32)
            for r in range(MOBA_TOP_K):
                pb = p_ref[ids_ref[h * MOBA_TOP_K + r]].astype(bf16)
                for half in range(PAGES_PER_BLOCK):
                    oh = oh + _dot_nt(pb[:, half * PAGE_SIZE:(half + 1) * PAGE_SIZE],
                                      vbuf[h * MOBA_TOP_K + r, half].astype(bf16))
            o = o + jnp.where(row_hd == h, oh, 0.0)
        o_ref[...] = o / stash_ref[:, ATT_HEAD_DIM:ATT_HEAD_DIM + 1]

    @pl.when(b == 0)
    def _():
        for c in range(K_SLOTS - 1):
            start_chunk(0, c, c)

    @pl.when(b < n_seq)
    def _():
        gate_ref[...] = jnp.zeros((8, LANES), f32)
        stream(range(0, CHUNKS_BEFORE_FINISH))

    @pl.when(b > 0)
    def _():
        finish_previous()

    @pl.when(b < n_seq)
    def _():
        stream(range(CHUNKS_BEFORE_FINISH, n_chunks))
        choose_and_weigh()


def moba_decode(page_table, q, k_new, v_new, cache_kt, cache_vt, layer):
    b = q.shape[0]
    n_pages = page_table.shape[1]
    n_blocks = n_pages // PAGES_PER_BLOCK
    assert n_blocks <= LANES and n_blocks >= MOBA_TOP_K
    heads8 = lambda a: jnp.pad(a.reshape(b, ATT_HEADS, ATT_HEAD_DIM), ((0, 0), (0, 8 - ATT_HEADS), (0, 0)))
    n_picks = ATT_HEADS * MOBA_TOP_K
    cur = lambda i, pt: (jnp.minimum(i, b - 1), 0, 0)
    prev = lambda i, pt: (jnp.maximum(i - 1, 0), 0, 0)
    per_head = pl.BlockSpec((None, 8, ATT_HEAD_DIM), cur)
    grid_spec = pltpu.PrefetchScalarGridSpec(
        num_scalar_prefetch=1, grid=(b + 1,),
        in_specs=[pl.BlockSpec((None, 1, ATT_DIM), cur), per_head, per_head, per_head,
                  pl.BlockSpec(memory_space=pl.ANY), pl.BlockSpec(memory_space=pl.ANY)],
        out_specs=pl.BlockSpec((None, 8, ATT_HEAD_DIM), prev),
        scratch_shapes=[pltpu.VMEM((K_SLOTS, K_CHUNK_PAGES, ATT_DIM, PAGE_SIZE), f32),
                        pltpu.VMEM((n_picks, PAGES_PER_BLOCK, ATT_HEAD_DIM, PAGE_SIZE), f32),
                        pltpu.VMEM((n_blocks, 8, MOBA_BLOCK), f32),
                        pltpu.VMEM((n_blocks, 8, MOBA_BLOCK), f32),
                        pltpu.VMEM((8, LANES), f32),
                        pltpu.VMEM((8, LANES), f32),
                        pltpu.SMEM((n_picks * (1 + PAGES_PER_BLOCK),), jnp.int32),
                        pltpu.SemaphoreType.DMA((K_SLOTS,)),
                        pltpu.SemaphoreType.DMA((1,))])
    out = pl.pallas_call(
        functools.partial(_moba_decode_kernel, layer=layer, n_pages=n_pages),
        grid_spec=grid_spec,
        out_shape=jax.ShapeDtypeStruct((b, 8, ATT_HEAD_DIM), f32),
        compiler_params=_cparams(1), name="moba_decode",
    )(page_table, q.reshape(b, 1, ATT_DIM), heads8(q), heads8(k_new), heads8(v_new), cache_kt, cache_vt)
    return out[:, :ATT_HEADS].reshape(b, ATT_DIM)


def _layer_params(l, w_in, conv_w, conv_b, dt_bias, a_log, d_skip, ssm_norm_g, pool_w, pool_scale, w_out):
    o_xbc = SSM_INNER
    o_dt = o_xbc + CONV_DIM
    o_pool = o_dt + SSM_HEADS
    o_q = o_pool + POOL_DIM
    o_k = o_q + ATT_DIM
    o_v = o_k + ATT_DIM
    wl = w_in[l]
    pad_heads = lambda v: jnp.pad(v.astype(f32), (0, LANES - SSM_HEADS)).reshape(1, LANES)
    w_cat = jnp.concatenate([wl[:, :o_pool], jnp.zeros((wl.shape[0], LANES - SSM_HEADS), wl.dtype),
                             wl[:, o_pool:]], axis=1).astype(bf16)
    pw = jnp.zeros((POOL_DIM, POOL_DIM), f32)
    for g in range(len(POOL_WINDOWS)):
        pw = pw.at[g * POOL_GROUP:(g + 1) * POOL_GROUP, g * POOL_GROUP:(g + 1) * POOL_GROUP].set(pool_w[l, g])
    expand = (jnp.arange(LANES)[:, None] == (jnp.arange(SSM_INNER)[None, :] // SSM_HEAD_DIM)).astype(f32)
    wo = w_out[l].astype(bf16)
    return {
        'w_in': w_cat,
        'conv_w': conv_w[l], 'conv_b': conv_b[l].reshape(1, CONV_DIM),
        'dt_bias': pad_heads(dt_bias[l]), 'a': pad_heads(-jnp.exp(a_log[l].astype(f32))),
        'dsk': jnp.repeat(d_skip[l].astype(f32), SSM_HEAD_DIM).reshape(1, SSM_INNER),
        'ssm_norm_g': ssm_norm_g[l].reshape(1, SSM_INNER),
        'pool_w': pw.astype(bf16), 'pool_scale': pool_scale[l].reshape(1, POOL_DIM),
        'expand': expand,
        'w_out': [wo[:SSM_INNER], wo[SSM_INNER:SSM_INNER + POOL_DIM], wo[SSM_INNER + POOL_DIM:]],
    }


def _in_proj_outs():
    sizes = [SSM_INNER, CONV_DIM, LANES, POOL_DIM, ATT_DIM, ATT_DIM, ATT_DIM]
    starts = [sum(sizes[:i]) for i in range(len(sizes))]
    return [(s, n, f32) for s, n in zip(starts, sizes)]


IN_PROJ_OUTS = _in_proj_outs()


def _pages_channel_major(cache):
    d, n_phys = cache.shape[:2]
    return jnp.transpose(cache, (0, 1, 3, 4, 2)).reshape(d, n_phys, ATT_DIM, PAGE_SIZE)


def kernel(x_prompt, x_sample, cache_moba_k, cache_moba_v, state_ssm, state_conv, state_pool, cache_mem_k, cache_mem_v, page_table, mem_prompt, norm_mix_g, w_in, conv_w, conv_b, dt_bias, a_log, d_skip, ssm_norm_g, pool_w, pool_scale, w_out, norm_cross_g, norm_mem_g, w_mem_q, w_mem_kv, w_mem_o, norm_ffn_g, w_gate_up, w_down, final_norm_g):
    depth = w_in.shape[0]
    bp, t, d = x_prompt.shape
    bs = x_sample.shape[0]
    assert bp == 1 and x_sample.shape[1] == 1
    past_len = page_table.shape[1] * PAGE_SIZE
    cache_kt = _pages_channel_major(cache_moba_k)
    cache_vt = _pages_channel_major(cache_moba_v)
    mem_k_split = _mem_split_view(cache_mem_k)
    mem_v_split = _mem_split_view(cache_mem_v)
    split_order = _split_channel_order()
    mem_len = mem_prompt.shape[1]
    xp = x_prompt.reshape(t, d)
    xs = x_sample.reshape(bs, d)
    mem = mem_prompt.reshape(mem_len, d)
    tm_p, tm_s = 512, bs
    outs = {n: [] for n in ('kp', 'vp', 'ks', 'vs', 'sp', 'ss', 'cp', 'cs', 'pp', 'ps', 'mk', 'mv')}
    for l in range(depth):
        prm = _layer_params(l, w_in, conv_w, conv_b, dt_bias, a_log, d_skip, ssm_norm_g, pool_w, pool_scale, w_out)
        wq = w_mem_q[l].astype(bf16)
        wkv = w_mem_kv[l].astype(bf16)
        wo_mem = w_mem_o[l].astype(bf16)
        wgu = w_gate_up[l].astype(bf16)
        wdn = w_down[l].astype(bf16)
        half = MEM_HEADS * MEM_HEAD_DIM
        last_layer = l == depth - 1

        mk, mv, mkb, mvb = norm_matmul(mem, norm_mem_g[l], wkv,
                                       [(0, half, f32), (half, half, f32), (0, half, bf16), (half, half, bf16)],
                                       tm=mem_len)
        z, xbc, dtr, u, q, k, v = norm_matmul(xp, norm_mix_g[l], prm['w_in'], IN_PROJ_OUTS, tm=tm_p)
        y, po, s_new = ssd_pool_prompt(z, xbc, dtr, u, prm, tm=tm_p)
        att = moba_prompt(q, k, v)
        xp = prompt_tail(xp, [y, po, att], prm['w_out'], norm_cross_g[l], wq, mkb, mvb, wo_mem,
                         norm_ffn_g[l], wgu, wdn, final_norm_g, norm_out=last_layer, tm=tm_p)
        outs['kp'].append(k.reshape(1, t, ATT_HEADS, ATT_HEAD_DIM))
        outs['vp'].append(v.reshape(1, t, ATT_HEADS, ATT_HEAD_DIM))
        outs['sp'].append(s_new[None])
        outs['cp'].append(xbc[t - (CONV_W - 1):][None])
        outs['pp'].append(u[t - POOL_HIST:][None])
        outs['mk'].append(mk.reshape(1, mem_len, MEM_HEADS, MEM_HEAD_DIM))
        outs['mv'].append(mv.reshape(1, mem_len, MEM_HEADS, MEM_HEAD_DIM))

        z, xbc, dtr, u, q, k, v = norm_matmul(xs, norm_mix_g[l], prm['w_in'], IN_PROJ_OUTS, tm=tm_s)
        y, po, c_new, s_new, p_new = ssd_pool_sample(z, xbc, dtr, u, state_conv[l], state_ssm[l],
                                                     state_pool[l], prm, pos0=past_len)
        att = moba_decode(page_table, q, k, v, cache_kt, cache_vt, layer=l)
        xs = matmul_residual(xs, [y, po, att], prm['w_out'], tm=tm_s)
        (qc,) = norm_matmul(xs, norm_cross_g[l], wq[:, split_order], [(0, d, f32)], tm=tm_s)
        oc = cross_sample(qc, mem_k_split, mem_v_split, layer=l)
        xs = matmul_residual(xs, [oc], [wo_mem[split_order, :]], tm=tm_s)
        xs = swiglu_block(xs, norm_ffn_g[l], wgu, wdn, final_norm_g, norm_out=last_layer, tm=tm_s, tf=D_FF)
        outs['ks'].append(k.reshape(bs, 1, ATT_HEADS, ATT_HEAD_DIM))
        outs['vs'].append(v.reshape(bs, 1, ATT_HEADS, ATT_HEAD_DIM))
        outs['ss'].append(s_new)
        outs['cs'].append(c_new)
        outs['ps'].append(p_new)

    y_prompt = xp.reshape(1, t, d)
    y_sample = xs.reshape(bs, 1, d)
    st = lambda n: jnp.stack(outs[n])
    return (y_prompt, y_sample, st('kp'), st('vp'), st('ks'), st('vs'), st('sp'), st('ss'),
            st('cp'), st('cs'), st('pp'), st('ps'), st('mk'), st('mv'))
```

```python
import functools

import jax
import jax.numpy as jnp
from jax import lax
from jax.experimental import pallas as pl
from jax.experimental.pallas import tpu as pltpu

f32 = jnp.float32
bf16 = jnp.bfloat16
HIGHEST = lax.Precision.HIGHEST

D_MODEL = 1024
SSM_INNER = 512
SSM_HEAD_DIM = 64
SSM_HEADS = 8
SSM_GROUPS = 2
SSM_STATE = 128
CONV_W = 4
CONV_DIM = SSM_INNER + 2 * SSM_GROUPS * SSM_STATE
SSD_CHUNK = 128
POOL_DIM = 256
POOL_WINDOWS = (2, 4, 8, 16)
POOL_GROUP = 64
POOL_HIST = 15
ATT_DIM = 256
ATT_HEAD_DIM = 64
ATT_HEADS = 4
MOBA_BLOCK = 256
MOBA_TOP_K = 3
PAGE_SIZE = 128
MEM_HEADS = 4
MEM_HEAD_DIM = 256
D_FF = 2816
RMS_EPS = 1e-6
LANES = 128
CONV_HALO = 8
POOL_HALO = 16
MASKED = -1e30
LOG2E = 1.4426950408889634
BF16_SUBLANES = 16
V_AUG_ROWS = ATT_HEAD_DIM + BF16_SUBLANES
MOBA_PAIRS_PER_TRIP = 4
VMEM_LIMIT = 56 * 1024 * 1024


def _cparams(n_axes):
    return pltpu.CompilerParams(dimension_semantics=("arbitrary",) * n_axes,
                                vmem_limit_bytes=VMEM_LIMIT)


def _rms(x, g):
    ms = jnp.mean(x * x, axis=-1, keepdims=True)
    return x * lax.rsqrt(ms + RMS_EPS) * g


def _dot(a, b, **kw):
    return jnp.dot(a, b, preferred_element_type=f32, **kw)


def _dot_nt(a, b, **kw):
    return lax.dot_general(a, b, (((1,), (1,)), ((), ())), preferred_element_type=f32, **kw)


def _silu(x):
    return x * jax.nn.sigmoid(x)


def _softplus(x):
    return jnp.maximum(x, 0.0) + jnp.log1p(jnp.exp(-jnp.abs(x)))


def _full(shape):
    return pl.BlockSpec(shape, lambda *_: (0,) * len(shape))


def _norm_mm_kernel(x_ref, g_ref, w_ref, *o_refs, groups):
    hb = _rms(x_ref[...], g_ref[...]).astype(bf16)
    res = {}
    for o_ref, (start, size) in zip(o_refs, groups):
        if (start, size) not in res:
            res[start, size] = _dot(hb, w_ref[:, start:start + size])
        o_ref[...] = res[start, size].astype(o_ref.dtype)


def norm_matmul(x, g, w, outs, tm):
    m, d = x.shape
    assert all(start % LANES == 0 and size % LANES == 0 for start, size, _ in outs)
    return pl.pallas_call(
        functools.partial(_norm_mm_kernel, groups=tuple((s, n) for s, n, _ in outs)),
        grid=(m // tm,),
        in_specs=[pl.BlockSpec((tm, d), lambda i: (i, 0)), _full((1, d)), _full(w.shape)],
        out_specs=[pl.BlockSpec((tm, n), lambda i: (i, 0)) for _, n, _ in outs],
        out_shape=[jax.ShapeDtypeStruct((m, n), dt) for _, n, dt in outs],
        compiler_params=_cparams(1), name="norm_matmul",
    )(x, g.reshape(1, d), w)


def _mm_res_kernel(*refs, n_a):
    res_ref = refs[0]
    a_refs = refs[1:1 + n_a]
    w_refs = refs[1 + n_a:1 + 2 * n_a]
    o_ref = refs[-1]
    acc = res_ref[...]
    for a, w in zip(a_refs, w_refs):
        acc = acc + _dot(a[...].astype(bf16), w[...])
    o_ref[...] = acc


def matmul_residual(res, a_list, w_list, tm):
    m, d = res.shape
    in_specs = [pl.BlockSpec((tm, d), lambda i: (i, 0))]
    in_specs += [pl.BlockSpec((tm, a.shape[1]), lambda i: (i, 0)) for a in a_list]
    in_specs += [_full(w.shape) for w in w_list]
    return pl.pallas_call(
        functools.partial(_mm_res_kernel, n_a=len(a_list)),
        grid=(m // tm,), in_specs=in_specs,
        out_specs=pl.BlockSpec((tm, d), lambda i: (i, 0)),
        out_shape=jax.ShapeDtypeStruct((m, d), f32),
        compiler_params=_cparams(1), name="matmul_residual",
    )(res, *a_list, *w_list)


def _prompt_tail_kernel(*refs, n_mix, norm_out):
    x_ref = refs[0]
    a_refs = refs[1:1 + n_mix]
    w_refs = refs[1 + n_mix:1 + 2 * n_mix]
    (g_ref, wq_ref, mk_ref, mv_ref, wo_ref,
     gf_ref, wg_ref, wu_ref, wd_ref, gout_ref, o_ref) = refs[1 + 2 * n_mix:]
    x = x_ref[...]
    for a, w in zip(a_refs, w_refs):
        x = x + _dot(a[...].astype(bf16), w[...])
    if n_mix:
        o_ref[...] = x
        x = o_ref[...]
    hb = _rms(x, g_ref[...]).astype(bf16)
    q = _dot(hb, wq_ref[...])
    acc = x
    for h in range(MEM_HEADS):
        sl = slice(h * MEM_HEAD_DIM, (h + 1) * MEM_HEAD_DIM)
        s = _dot_nt(q[:, sl].astype(bf16), mk_ref[:, sl]) * (MEM_HEAD_DIM ** -0.5)
        p = jnp.exp(s - jnp.max(s, axis=-1, keepdims=True))
        p = p / jnp.sum(p, axis=-1, keepdims=True)
        oh = _dot(p.astype(bf16), mv_ref[:, sl])
        acc = acc + _dot(oh.astype(bf16), wo_ref[sl, :])
    o_ref[...] = acc
    x2 = o_ref[...]
    hb = _rms(x2, gf_ref[...]).astype(bf16)
    act = _silu(_dot(hb, wg_ref[...])) * _dot(hb, wu_ref[...])
    y = x2 + _dot(act.astype(bf16), wd_ref[...])
    o_ref[...] = _rms(y, gout_ref[...]) if norm_out else y


def prompt_tail(x, mix_list, w_out_list, g, wq, mkb, mvb, wo, g_ffn, w_gu, w_d, g_out, norm_out, tm):
    m, d = x.shape
    row = lambda c: pl.BlockSpec((tm, c), lambda i: (i, 0))
    once = lambda shape, idx: pl.BlockSpec(shape, idx, pipeline_mode=pl.Buffered(1))
    resident = lambda a: once(a.shape, lambda i: (0,) * a.ndim)
    consts = [g.reshape(1, d), wq, mkb, mvb, wo, g_ffn.reshape(1, d)]
    tail = [g_out.reshape(1, d)]
    return pl.pallas_call(
        functools.partial(_prompt_tail_kernel, n_mix=len(mix_list), norm_out=norm_out), grid=(m // tm,),
        in_specs=[row(d)] + [row(a.shape[1]) for a in mix_list] + [resident(w) for w in w_out_list]
                 + [resident(c) for c in consts]
                 + [once((d, D_FF), lambda i: (0, 0)), once((d, D_FF), lambda i: (0, 1)), resident(w_d)]
                 + [resident(c) for c in tail],
        out_specs=row(d),
        out_shape=jax.ShapeDtypeStruct((m, d), f32),
        compiler_params=_cparams(1), name="prompt_tail",
    )(x, *mix_list, *w_out_list, *consts, w_gu, w_gu, w_d, *tail)


MEM_SPLIT = MEM_HEAD_DIM // LANES
MEM_ROWS = MEM_SPLIT * MEM_HEADS


def _split_channel_order(w, axis):
    shape = w.shape
    w = w.reshape(shape[:axis] + (MEM_HEADS, MEM_SPLIT, LANES) + shape[axis + 1:])
    return jnp.swapaxes(w, axis, axis + 1).reshape(shape)


def _mem_split_view(cache):
    d, b, m = cache.shape[:3]
    x = cache.reshape(d, b, m, MEM_HEADS, MEM_SPLIT, LANES)
    return jnp.transpose(x, (0, 1, 2, 4, 3, 5)).reshape(d, b, m, MEM_ROWS, LANES)


def _cross_sample_kernel(q_ref, mk_ref, mv_ref, o_ref):
    part = jnp.sum(mk_ref[...] * q_ref[...][None], axis=-1, keepdims=True)
    s = part
    for piece in range(1, MEM_SPLIT):
        s = s + jnp.roll(part, piece * MEM_HEADS, axis=1)
    s = s * (MEM_HEAD_DIM ** -0.5)
    p = jnp.exp(s - jnp.max(s, axis=0, keepdims=True))
    l = jnp.sum(p, axis=0)
    o_ref[...] = jnp.sum(p * mv_ref[...], axis=0) / l


def cross_sample(q_split, mem_k_split, mem_v_split, layer):
    b = q_split.shape[0]
    mlen = mem_k_split.shape[2]
    mem_spec = pl.BlockSpec((None, None, mlen, MEM_ROWS, LANES), lambda i: (layer, i, 0, 0, 0))
    row_spec = pl.BlockSpec((None, MEM_ROWS, LANES), lambda i: (i, 0, 0))
    out = pl.pallas_call(
        _cross_sample_kernel, grid=(b,),
        in_specs=[row_spec, mem_spec, mem_spec], out_specs=row_spec,
        out_shape=jax.ShapeDtypeStruct((b, MEM_ROWS, LANES), f32),
        compiler_params=_cparams(1), name="cross_sample",
    )(q_split.reshape(b, MEM_ROWS, LANES), mem_k_split, mem_v_split)
    return out.reshape(b, MEM_ROWS * LANES)


def _swiglu_kernel(x_ref, g_ref, wg_ref, wu_ref, wd_ref, gout_ref, o_ref, h_ref, acc_ref, *, norm_out):
    f = pl.program_id(1)

    @pl.when(f == 0)
    def _():
        h_ref[...] = _rms(x_ref[...], g_ref[...]).astype(bf16)
        acc_ref[...] = x_ref[...]

    hb = h_ref[...]
    a = _silu(_dot(hb, wg_ref[...])) * _dot(hb, wu_ref[...])
    acc_ref[...] += _dot(a.astype(bf16), wd_ref[...])

    @pl.when(f == pl.num_programs(1) - 1)
    def _():
        y = acc_ref[...]
        o_ref[...] = _rms(y, gout_ref[...]) if norm_out else y


def swiglu_block(x, g, w_gu, w_d, g_out, norm_out, tm, tf):
    m, d = x.shape
    nf = D_FF // tf
    mode = dict(pipeline_mode=pl.Buffered(1)) if nf == 1 else {}
    return pl.pallas_call(
        functools.partial(_swiglu_kernel, norm_out=norm_out), grid=(m // tm, nf),
        in_specs=[pl.BlockSpec((tm, d), lambda i, f: (i, 0)), _full((1, d)),
                  pl.BlockSpec((d, tf), lambda i, f: (0, f), **mode),
                  pl.BlockSpec((d, tf), lambda i, f: (0, f + nf), **mode),
                  pl.BlockSpec((tf, d), lambda i, f: (f, 0), **mode), _full((1, d))],
        out_specs=pl.BlockSpec((tm, d), lambda i, f: (i, 0)),
        out_shape=jax.ShapeDtypeStruct((m, d), f32),
        scratch_shapes=[pltpu.VMEM((tm, d), bf16), pltpu.VMEM((tm, d), f32)],
        compiler_params=_cparams(2), name="swiglu_block",
    )(x, g.reshape(1, d), w_gu, w_gu, w_d, g_out.reshape(1, d))


def _pool_select(w2, w4, w8, w16, cnt_of, lane):
    out = w16 / cnt_of(16)
    for win, acc in ((8, w8), (4, w4), (2, w2)):
        g = POOL_WINDOWS.index(win)
        out = jnp.where(lane < (g + 1) * POOL_GROUP, acc / cnt_of(win), out)
    return out


def _group_rmsnorm(y, g):
    half = SSM_INNER // SSM_GROUPS
    parts = [_rms(y[:, i * half:(i + 1) * half], g[:, i * half:(i + 1) * half])
             for i in range(SSM_GROUPS)]
    return jnp.concatenate(parts, axis=-1)


def _ssd_pool_prompt_kernel(z_ref, xbc_ref, dt_ref, u_ref, cw_ref, cb_ref, dtb_ref, a_ref, dsk_ref,
                            ng_ref, pw_ref, ps_ref,
                            y_ref, po_ref, st_ref,
                            ext_ref, pext_ref, xc_ref, s_ref, *, tm):
    i = pl.program_id(0)
    halo = CONV_HALO
    phalo = POOL_HALO

    @pl.when(i == 0)
    def _():
        ext_ref[0:halo, :] = jnp.zeros((halo, CONV_DIM), f32)
        pext_ref[0:phalo, :] = jnp.zeros((phalo, POOL_DIM), f32)
        s_ref[...] = jnp.zeros(s_ref.shape, f32)

    @pl.when(i > 0)
    def _():
        ext_ref[0:halo, :] = ext_ref[tm:tm + halo, :]
        pext_ref[0:phalo, :] = pext_ref[tm:tm + phalo, :]

    ext_ref[halo:halo + tm, :] = xbc_ref[...]
    pext_ref[phalo:phalo + tm, :] = u_ref[...]

    rp, lp = 64, 256
    for r0 in range(0, tm, rp):
        for c0 in range(0, CONV_DIM, lp):
            acc = jnp.broadcast_to(cb_ref[:, c0:c0 + lp], (rp, lp))
            for j in range(CONV_W):
                off = halo - (CONV_W - 1) + j + r0
                acc = acc + ext_ref[off:off + rp, c0:c0 + lp] * cw_ref[j:j + 1, c0:c0 + lp]
            xc_ref[r0:r0 + rp, c0:c0 + lp] = _silu(acc)

    lane_p = lax.broadcasted_iota(jnp.int32, (rp, POOL_DIM), 1)
    row_p = lax.broadcasted_iota(jnp.int32, (rp, POOL_DIM), 0)
    for r0 in range(0, tm, rp):
        run = pext_ref[r0:r0 + phalo + rp, :]
        u = run[phalo:]
        sums, win = {}, 1
        while win < max(POOL_WINDOWS):
            run = run + pltpu.roll(run, win, 0)
            win *= 2
            sums[win] = run[phalo:]
        pos1 = (row_p + (i * tm + r0 + 1)).astype(f32)
        pooled = _pool_select(sums[2], sums[4], sums[8], sums[16],
                              lambda win: jnp.minimum(float(win), pos1), lane_p) - u
        po_ref[r0:r0 + rp, :] = (_dot(pooled.astype(bf16), pw_ref[...]) * ps_ref[...]).astype(po_ref.dtype)

    cl = SSD_CHUNK
    r_io = lax.broadcasted_iota(jnp.int32, (cl, cl), 0)
    c_io = lax.broadcasted_iota(jnp.int32, (cl, cl), 1)
    ltri = (c_io <= r_io).astype(f32)
    causal = c_io <= r_io
    first_half = c_io < SSM_HEAD_DIM
    n_pairs = SSM_HEADS // 2
    heads_per_group = SSM_HEADS // SSM_GROUPS

    def chunk(c, carry):
        r0 = pl.multiple_of(c * cl, cl)
        xc = xc_ref[pl.ds(r0, cl), :]
        dt = _softplus(dt_ref[pl.ds(r0, cl), :] + dtb_ref[...])
        a_cum = _dot(ltri, dt * a_ref[...], precision=HIGHEST)
        a_cum_t = a_cum.T
        dt_t = dt.T
        to_end_t = jnp.exp(a_cum_t[:, cl - 1:cl] - a_cum_t) * dt_t
        b_t = [xc[:, SSM_INNER + g * SSM_STATE:SSM_INNER + (g + 1) * SSM_STATE].T
               for g in range(SSM_GROUPS)]
        cm = [xc[:, SSM_INNER + SSM_GROUPS * SSM_STATE + g * SSM_STATE:
                 SSM_INNER + SSM_GROUPS * SSM_STATE + (g + 1) * SSM_STATE].astype(bf16)
              for g in range(SSM_GROUPS)]
        scores = [_dot(cm[g], b_t[g].astype(bf16)) for g in range(SSM_GROUPS)]
        y_pairs = []
        for k in range(n_pairs):
            g = (2 * k) // heads_per_group
            xs_pair = xc[:, k * LANES:(k + 1) * LANES]
            xs_b = xs_pair.astype(bf16)
            yd, cs, eb = [], [], []
            for h in (2 * k, 2 * k + 1):
                colb = jnp.broadcast_to(a_cum[:, h:h + 1], (cl, cl))
                seg = colb - a_cum_t[h:h + 1, :]
                decay = jnp.exp(jnp.where(causal, seg, -jnp.inf))
                mh = scores[g] * decay * dt_t[h:h + 1, :]
                yd.append(_dot(mh.astype(bf16), xs_b))
                cs.append(_dot((b_t[g] * to_end_t[h:h + 1, :]).astype(bf16), xs_b))
                eb.append(jnp.exp(colb))
            e_pair = jnp.where(first_half, eb[0], eb[1])
            s_old = s_ref[k]
            y_off = _dot(cm[g], s_old.astype(bf16)) * e_pair
            y_pairs.append(jnp.where(first_half, yd[0], yd[1]) + y_off
                           + dsk_ref[:, k * LANES:(k + 1) * LANES] * xs_pair)
            s_ref[k] = s_old * e_pair[cl - 1:cl, :] + jnp.where(first_half, cs[0], cs[1])
        y = jnp.concatenate(y_pairs, axis=-1) * _silu(z_ref[pl.ds(r0, cl), :])
        y_ref[pl.ds(r0, cl), :] = _group_rmsnorm(y, ng_ref[...]).astype(y_ref.dtype)
        return carry

    lax.fori_loop(0, tm // cl, chunk, 0, unroll=True)

    @pl.when(i == pl.num_programs(0) - 1)
    def _():
        for k in range(n_pairs):
            st = s_ref[k].T
            st_ref[2 * k] = st[0:SSM_HEAD_DIM, :]
            st_ref[2 * k + 1] = st[SSM_HEAD_DIM:2 * SSM_HEAD_DIM, :]


def ssd_pool_prompt(z, xbc, dt, u, prm, tm):
    t = z.shape[0]
    assert POOL_WINDOWS == tuple(2 ** (n + 1) for n in range(len(POOL_WINDOWS)))
    row = lambda c: pl.BlockSpec((tm, c), lambda i: (i, 0))
    consts = [prm['conv_w'], prm['conv_b'], prm['dt_bias'], prm['a'], prm['dsk'], prm['ssm_norm_g'],
              prm['pool_w'], prm['pool_scale']]
    return pl.pallas_call(
        functools.partial(_ssd_pool_prompt_kernel, tm=tm), grid=(t // tm,),
        in_specs=[row(SSM_INNER), row(CONV_DIM), row(LANES), row(POOL_DIM)] + [_full(c.shape) for c in consts],
        out_specs=[row(SSM_INNER), row(POOL_DIM), _full((SSM_HEADS, SSM_HEAD_DIM, SSM_STATE))],
        out_shape=[jax.ShapeDtypeStruct((t, SSM_INNER), bf16), jax.ShapeDtypeStruct((t, POOL_DIM), bf16),
                   jax.ShapeDtypeStruct((SSM_HEADS, SSM_HEAD_DIM, SSM_STATE), f32)],
        scratch_shapes=[pltpu.VMEM((tm + CONV_HALO, CONV_DIM), f32), pltpu.VMEM((tm + POOL_HALO, POOL_DIM), f32),
                        pltpu.VMEM((tm, CONV_DIM), f32),
                        pltpu.VMEM((SSM_HEADS // 2, SSM_STATE, 2 * SSM_HEAD_DIM), f32)],
        compiler_params=_cparams(1), name="ssd_pool_prompt",
    )(z, xbc, dt, u, *consts)


def _col_tile(row):
    return jnp.broadcast_to(row, (LANES, LANES)).T


def _ssd_pool_sample_kernel(z_ref, xbc_ref, dt_ref, u_ref, sc_ref, ss_ref, sp_ref,
                            cw_ref, cb_ref, dtb_ref, a_ref, dsk_ref, ng_ref, pw_ref, ps_ref, ex_ref,
                            y_ref, po_ref, cn_ref, sn_ref, pn_ref, *, pos0):
    xrow = xbc_ref[...]
    acc = cb_ref[...] + xrow * cw_ref[CONV_W - 1:CONV_W, :]
    for j in range(CONV_W - 1):
        acc = acc + sc_ref[j:j + 1, :] * cw_ref[j:j + 1, :]
        if j > 0:
            cn_ref[j - 1:j, :] = sc_ref[j:j + 1, :]
    cn_ref[CONV_W - 2:CONV_W - 1, :] = xrow
    xc = _silu(acc)

    dt = _softplus(dt_ref[...] + dtb_ref[...])
    dta = dt * a_ref[...]
    both = jnp.concatenate([jnp.broadcast_to(dt, (8, LANES)), jnp.broadcast_to(dta, (8, LANES))], axis=0)
    both_x = _dot(both, ex_ref[...], precision=HIGHEST)
    dtx = both_x[0:1, :]
    dec_x = jnp.exp(both_x[8:9, :])
    xs = xc[:, 0:SSM_INNER]
    xdt = xs * dtx
    heads_per_group = SSM_HEADS // SSM_GROUPS
    y_pairs = []
    for k in range(SSM_HEADS // 2):
        g = (2 * k) // heads_per_group
        b_row = xc[:, SSM_INNER + g * SSM_STATE:SSM_INNER + (g + 1) * SSM_STATE]
        c_row = xc[:, SSM_INNER + SSM_GROUPS * SSM_STATE + g * SSM_STATE:
                   SSM_INNER + SSM_GROUPS * SSM_STATE + (g + 1) * SSM_STATE]
        sl = slice(k * LANES, (k + 1) * LANES)
        s_old = jnp.concatenate([ss_ref[2 * k], ss_ref[2 * k + 1]], axis=0)
        s_new = s_old * _col_tile(dec_x[:, sl]) + _col_tile(xdt[:, sl]) * b_row
        sn_ref[2 * k] = s_new[0:SSM_HEAD_DIM, :]
        sn_ref[2 * k + 1] = s_new[SSM_HEAD_DIM:, :]
        y_k = _dot_nt(jnp.broadcast_to(c_row, (8, SSM_STATE)), s_new, precision=HIGHEST)[0:1, :]
        y_pairs.append(y_k + dsk_ref[:, sl] * xs[:, sl])
    y = jnp.concatenate(y_pairs, axis=-1) * _silu(z_ref[...])
    y_ref[...] = _group_rmsnorm(y, ng_ref[...]).astype(y_ref.dtype)

    u = u_ref[...]
    prev = sp_ref[...]
    rowi = lax.broadcasted_iota(jnp.int32, prev.shape, 0)
    tail = lambda win: u + jnp.sum(jnp.where(rowi >= POOL_HIST - (win - 1), prev, 0.0), axis=0, keepdims=True)
    lane_p = lax.broadcasted_iota(jnp.int32, (1, POOL_DIM), 1)
    pooled = _pool_select(tail(2), tail(4), tail(8), tail(16),
                          lambda win: float(min(win, pos0 + 1)), lane_p) - u
    po = _dot(jnp.broadcast_to(pooled, (8, POOL_DIM)).astype(bf16), pw_ref[...])[0:1, :] * ps_ref[...]
    po_ref[...] = po.astype(po_ref.dtype)
    pn_ref[0:POOL_HIST - 1, :] = sp_ref[1:POOL_HIST, :]
    pn_ref[POOL_HIST - 1:POOL_HIST, :] = u


SAMPLE_SEQS_PER_STEP = 2
N_SAMPLE_SEQ_INPUTS, N_SAMPLE_SEQ_OUTPUTS = 7, 5


def _ssd_pool_sample_group_kernel(*refs, pos0):
    seq_in = refs[:N_SAMPLE_SEQ_INPUTS]
    consts = refs[N_SAMPLE_SEQ_INPUTS:len(refs) - N_SAMPLE_SEQ_OUTPUTS]
    seq_out = refs[len(refs) - N_SAMPLE_SEQ_OUTPUTS:]
    for s in range(SAMPLE_SEQS_PER_STEP):
        _ssd_pool_sample_kernel(*[r.at[s] for r in seq_in], *consts, *[r.at[s] for r in seq_out], pos0=pos0)


def ssd_pool_sample(z, xbc, dt, u, st_conv, st_ssm, st_pool, prm, pos0):
    b = z.shape[0]
    grp = SAMPLE_SEQS_PER_STEP
    assert b % grp == 0
    per_seq = lambda *shape: pl.BlockSpec((grp,) + shape, lambda i: (i,) + (0,) * len(shape))
    consts = [prm['conv_w'], prm['conv_b'], prm['dt_bias'], prm['a'], prm['dsk'], prm['ssm_norm_g'],
              prm['pool_w'], prm['pool_scale'], prm['expand']]
    outs = pl.pallas_call(
        functools.partial(_ssd_pool_sample_group_kernel, pos0=pos0), grid=(b // grp,),
        in_specs=[per_seq(1, SSM_INNER), per_seq(1, CONV_DIM), per_seq(1, LANES), per_seq(1, POOL_DIM),
                  per_seq(CONV_W - 1, CONV_DIM), per_seq(SSM_HEADS, SSM_HEAD_DIM, SSM_STATE),
                  per_seq(POOL_HIST, POOL_DIM)] + [_full(c.shape) for c in consts],
        out_specs=[per_seq(1, SSM_INNER), per_seq(1, POOL_DIM), per_seq(CONV_W - 1, CONV_DIM),
                   per_seq(SSM_HEADS, SSM_HEAD_DIM, SSM_STATE), per_seq(POOL_HIST, POOL_DIM)],
        out_shape=[jax.ShapeDtypeStruct((b, 1, SSM_INNER), f32), jax.ShapeDtypeStruct((b, 1, POOL_DIM), f32),
                   jax.ShapeDtypeStruct((b, CONV_W - 1, CONV_DIM), f32),
                   jax.ShapeDtypeStruct((b, SSM_HEADS, SSM_HEAD_DIM, SSM_STATE), f32),
                   jax.ShapeDtypeStruct((b, POOL_HIST, POOL_DIM), f32)],
        compiler_params=_cparams(1), name="ssd_pool_sample",
    )(z.reshape(b, 1, -1), xbc.reshape(b, 1, -1), dt.reshape(b, 1, -1), u.reshape(b, 1, -1),
      st_conv, st_ssm, st_pool, *consts)
    y, po, cn, sn, pn = outs
    return y.reshape(b, -1), po.reshape(b, -1), cn, sn, pn


def _topk_blocks(gate, n_past, axis):
    blk = lax.broadcasted_iota(jnp.int32, gate.shape, axis).astype(f32)
    g = jnp.where(blk < jnp.asarray(n_past, f32), gate, -jnp.inf)
    picks = []
    for _ in range(MOBA_TOP_K):
        m = jnp.max(g, axis=axis, keepdims=True)
        idx = jnp.min(jnp.where(g == m, blk, float(LANES)), axis=axis, keepdims=True)
        picks.append((idx, jnp.abs(m) < jnp.inf))
        g = jnp.where(blk == idx, -jnp.inf, g)
    return picks


def _moba_prompt_kernel(q_ref, k_ref, v_ref, o_ref,
                        km_ref, kb_ref, vt_ref, qst_ref, nmt_ref, acc_ref, sa_ref, sb_ref):
    i = pl.program_id(0)
    tq = MOBA_BLOCK
    scale = ATT_HEAD_DIM ** -0.5

    @pl.when(i == 0)
    def _():
        km_ref[...] = jnp.zeros(km_ref.shape, f32)

    k = k_ref[...]
    kb = k.astype(bf16)
    head = lambda h: slice(h * ATT_HEAD_DIM, (h + 1) * ATT_HEAD_DIM)
    for h in range(ATT_HEADS):
        kb_ref[i, h] = kb[:, head(h)]
    vt = v_ref[...].T.astype(bf16)
    ones_rows = jnp.ones((V_AUG_ROWS - ATT_HEAD_DIM, tq), bf16)
    for h in range(ATT_HEADS):
        vt_ref[i, h] = jnp.concatenate([vt[head(h), :], ones_rows], axis=0)
    qt = q_ref[...].T
    n_slots = km_ref.shape[0]
    blk_row = lax.broadcasted_iota(jnp.int32, (n_slots, tq), 0).astype(f32)
    key_io = lax.broadcasted_iota(jnp.int32, (tq, tq), 0)
    qry_io = lax.broadcasted_iota(jnp.int32, (tq, tq), 1)
    km = km_ref[...]
    m_own = []
    for h in range(ATT_HEADS):
        qth = qt[head(h), :]
        gate = _dot(km[:, head(h)], qth, precision=HIGHEST)
        sel = jnp.zeros((n_slots, tq), jnp.bool_)
        for idx, ok in _topk_blocks(gate, i, axis=0):
            sel = sel | ((blk_row == idx) & ok)
        nmt_ref[h] = jnp.where(sel, 0.0, MASKED)
        qst = (qth * (scale * LOG2E)).astype(bf16)
        qst_ref[h] = qst
        st = jnp.where(key_io <= qry_io, _dot(kb[:, head(h)], qst), MASKED)
        m = jnp.max(st, axis=0, keepdims=True)
        m_own.append(m)
        acc_ref[h] = _dot(vt_ref[i, h], jnp.exp2(st - m).astype(bf16))

    km_ref[pl.ds(i, 1), :] = jnp.mean(k, axis=0, keepdims=True)

    def scores(j, dst):
        for h in range(ATT_HEADS):
            dst[h] = _dot(kb_ref[j, h], qst_ref[h])

    def absorb(j, src, carry):
        new = []
        for h in range(ATT_HEADS):
            m_prev = carry[h]
            raw = src[h]
            bias = nmt_ref[h, pl.ds(j, 1), :]
            m_new = jnp.maximum(m_prev, jnp.max(raw, axis=0, keepdims=True) + bias)
            alpha = jnp.exp2(m_prev - m_new)
            p = jnp.exp2(raw - (m_new - bias))
            new.append(m_new)
            acc_ref[h] = acc_ref[h] * alpha + _dot(vt_ref[j, h], p.astype(bf16))
        return tuple(new)

    last = jnp.maximum(i - 1, 0)
    scores(0, sa_ref)

    def pair(t, carry):
        scores(jnp.minimum(2 * t + 1, last), sb_ref)
        carry = absorb(2 * t, sa_ref, carry)
        scores(jnp.minimum(2 * t + 2, last), sa_ref)
        return absorb(2 * t + 1, sb_ref, carry)

    def pairs(first, count, c):
        for t in range(count):
            c = pair(first + t, c)
        return c

    per_trip = MOBA_PAIRS_PER_TRIP
    carry = lax.fori_loop(0, i // (2 * per_trip), lambda u, c: pairs(per_trip * u, per_trip, c), tuple(m_own))
    n = per_trip // 2
    while n >= 1:
        first = i // (4 * n) * (2 * n)
        carry = lax.cond(i % (4 * n) >= 2 * n, functools.partial(pairs, first, n), lambda c: c, carry)
        n //= 2
    carry = lax.cond(i % 2 == 1, lambda c: absorb(i - 1, sa_ref, c), lambda c: c, carry)
    out_t = jnp.concatenate(
        [acc_ref[h, 0:ATT_HEAD_DIM, :] / acc_ref[h, ATT_HEAD_DIM:ATT_HEAD_DIM + 1, :] for h in range(ATT_HEADS)],
        axis=0)
    o_ref[...] = out_t.T.astype(o_ref.dtype)


def moba_prompt(q, k, v):
    t = q.shape[0]
    tq = MOBA_BLOCK
    n_blk = t // tq
    n_slots = -(-n_blk // 8) * 8
    assert n_slots <= LANES
    tile = pl.BlockSpec((tq, ATT_DIM), lambda i: (i, 0))
    return pl.pallas_call(
        _moba_prompt_kernel, grid=(n_blk,),
        in_specs=[tile, tile, tile], out_specs=tile,
        out_shape=jax.ShapeDtypeStruct((t, ATT_DIM), bf16),
        scratch_shapes=[pltpu.VMEM((n_slots, ATT_DIM), f32),
                        pltpu.VMEM((n_blk, ATT_HEADS, tq, ATT_HEAD_DIM), bf16),
                        pltpu.VMEM((n_blk, ATT_HEADS, V_AUG_ROWS, tq), bf16),
                        pltpu.VMEM((ATT_HEADS, ATT_HEAD_DIM, tq), bf16),
                        pltpu.VMEM((ATT_HEADS, n_slots, tq), f32),
                        pltpu.VMEM((ATT_HEADS, V_AUG_ROWS, tq), f32),
                        pltpu.VMEM((ATT_HEADS, tq, tq), f32),
                        pltpu.VMEM((ATT_HEADS, tq, tq), f32)],
        compiler_params=_cparams(1), name="moba_prompt",
    )(q, k, v)


K_CHUNK_PAGES = 16
K_SLOTS = 8
PAGES_PER_BLOCK = MOBA_BLOCK // PAGE_SIZE
CHUNKS_BEFORE_FINISH = 2


def _moba_decode_kernel(pt_ref, q_ref, q8_ref, kn8_ref, vn8_ref, kc_ref, vc_ref, o_ref,
                        kbuf, vbuf, s_ref, p_ref, gate_ref, stash_ref, ids_ref, ksem, vsem, *, layer, n_pages):
    b = pl.program_id(0)
    n_seq = pl.num_programs(0) - 1
    n_chunks = n_pages // K_CHUNK_PAGES
    blocks_per_chunk = K_CHUNK_PAGES // PAGES_PER_BLOCK
    n_blocks = n_pages // PAGES_PER_BLOCK
    scale = ATT_HEAD_DIM ** -0.5

    def k_copy(seq, page_slot, slot, p):
        page = pt_ref[seq, page_slot]
        return pltpu.make_async_copy(kc_ref.at[layer, page], kbuf.at[slot, p], ksem.at[slot])

    def start_chunk(seq, c, slot):
        for p in range(K_CHUNK_PAGES):
            k_copy(seq, c * K_CHUNK_PAGES + p, slot, p).start()

    def wait_chunk(seq, c, slot):
        for p in range(K_CHUNK_PAGES):
            k_copy(seq, c * K_CHUNK_PAGES + p, slot, p).wait()

    n_picks = ATT_HEADS * MOBA_TOP_K

    def v_copy(h, r, half, page):
        return pltpu.make_async_copy(vc_ref.at[layer, page, pl.ds(h * ATT_HEAD_DIM, ATT_HEAD_DIM), :],
                                     vbuf.at[h * MOBA_TOP_K + r, half], vsem.at[0])

    def stream(chunks):
        lane8 = lax.broadcasted_iota(jnp.int32, (8, ATT_DIM), 1)
        row8 = lax.broadcasted_iota(jnp.int32, (8, ATT_DIM), 0)
        qblk_b = jnp.where(lane8 // ATT_HEAD_DIM == row8, q_ref[...], 0.0).astype(bf16)
        blk_lane = lax.broadcasted_iota(jnp.int32, (8, LANES), 1)
        gate = gate_ref[...]
        for c in chunks:
            slot = c % K_SLOTS
            ahead = c + K_SLOTS - 1
            if ahead < n_chunks:
                start_chunk(b, ahead, ahead % K_SLOTS)
            else:
                @pl.when(b + 1 < n_seq)
                def _():
                    start_chunk(b + 1, ahead - n_chunks, ahead % K_SLOTS)
            wait_chunk(b, c, slot)
            for t in range(blocks_per_chunk):
                blk = c * blocks_per_chunk + t
                halves = [_dot(qblk_b, kbuf[slot, t * PAGES_PER_BLOCK + half].astype(bf16))
                          for half in range(PAGES_PER_BLOCK)]
                for half in range(PAGES_PER_BLOCK):
                    s_ref[blk, :, half * PAGE_SIZE:(half + 1) * PAGE_SIZE] = halves[half]
                tot = jnp.sum(sum(halves), axis=-1, keepdims=True) * (1.0 / MOBA_BLOCK)
                gate = jnp.where(blk_lane == blk, tot, gate)
        gate_ref[...] = gate

    def choose_and_weigh():
        picks = _topk_blocks(gate_ref[...], n_blocks, axis=1)
        for h in range(ATT_HEADS):
            for r in range(MOBA_TOP_K):
                blk = picks[r][0][h, 0].astype(jnp.int32)
                ids_ref[h * MOBA_TOP_K + r] = blk
                for half in range(PAGES_PER_BLOCK):
                    page = pt_ref[b, blk * PAGES_PER_BLOCK + half]
                    ids_ref[n_picks + (h * MOBA_TOP_K + r) * PAGES_PER_BLOCK + half] = page
                    v_copy(h, r, half, page).start()
        blk_io = lax.broadcasted_iota(jnp.int32, (n_blocks, 8, MOBA_BLOCK), 0).astype(f32)
        seen = jnp.zeros((n_blocks, 8, MOBA_BLOCK), jnp.bool_)
        for idx, ok in picks:
            seen = seen | ((blk_io == idx[None]) & ok[None])
        s_all = jnp.where(seen, s_ref[...] * scale, -jnp.inf)
        s_own = jnp.sum(q8_ref[...] * kn8_ref[...], axis=-1, keepdims=True) * scale
        m = jnp.maximum(jnp.max(jnp.max(s_all, axis=0), axis=-1, keepdims=True), s_own)
        p_all = jnp.exp(s_all - m[None])
        p_own = jnp.exp(s_own - m)
        l = jnp.sum(jnp.sum(p_all, axis=0), axis=-1, keepdims=True) + p_own
        p_ref[...] = p_all
        stash_ref[:, 0:ATT_HEAD_DIM] = p_own * vn8_ref[...]
        stash_ref[:, ATT_HEAD_DIM:] = jnp.broadcast_to(l, (8, LANES - ATT_HEAD_DIM))

    def finish_previous():
        for h in range(ATT_HEADS):
            for r in range(MOBA_TOP_K):
                for half in range(PAGES_PER_BLOCK):
                    page = ids_ref[n_picks + (h * MOBA_TOP_K + r) * PAGES_PER_BLOCK + half]
                    v_copy(h, r, half, page).wait()
        row_hd = lax.broadcasted_iota(jnp.int32, (8, ATT_HEAD_DIM), 0)
        o = stash_ref[:, 0:ATT_HEAD_DIM]
        for h in range(ATT_HEADS):
            oh = jnp.zeros((8, ATT_HEAD_DIM), f32)
            for r in range(MOBA_TOP_K):
                pb = p_ref[ids_ref[h * MOBA_TOP_K + r]].astype(bf16)
                for half in range(PAGES_PER_BLOCK):
                    oh = oh + _dot_nt(pb[:, half * PAGE_SIZE:(half + 1) * PAGE_SIZE],
                                      vbuf[h * MOBA_TOP_K + r, half].astype(bf16))
            o = o + jnp.where(row_hd == h, oh, 0.0)
        o_ref[...] = o / stash_ref[:, ATT_HEAD_DIM:ATT_HEAD_DIM + 1]

    @pl.when(b == 0)
    def _():
        for c in range(K_SLOTS - 1):
            start_chunk(0, c, c)

    @pl.when(b < n_seq)
    def _():
        gate_ref[...] = jnp.zeros((8, LANES), f32)
        stream(range(0, CHUNKS_BEFORE_FINISH))

    @pl.when(b > 0)
    def _():
        finish_previous()

    @pl.when(b < n_seq)
    def _():
        stream(range(CHUNKS_BEFORE_FINISH, n_chunks))
        choose_and_weigh()


def moba_decode(page_table, q, k_new, v_new, cache_kt, cache_vt, layer):
    b = q.shape[0]
    n_pages = page_table.shape[1]
    n_blocks = n_pages // PAGES_PER_BLOCK
    assert n_blocks <= LANES and n_blocks >= MOBA_TOP_K
    heads8 = lambda a: jnp.pad(a.reshape(b, ATT_HEADS, ATT_HEAD_DIM), ((0, 0), (0, 8 - ATT_HEADS), (0, 0)))
    n_picks = ATT_HEADS * MOBA_TOP_K
    cur = lambda i, pt: (jnp.minimum(i, b - 1), 0, 0)
    prev = lambda i, pt: (jnp.maximum(i - 1, 0), 0, 0)
    per_head = pl.BlockSpec((None, 8, ATT_HEAD_DIM), cur)
    grid_spec = pltpu.PrefetchScalarGridSpec(
        num_scalar_prefetch=1, grid=(b + 1,),
        in_specs=[pl.BlockSpec((None, 1, ATT_DIM), cur), per_head, per_head, per_head,
                  pl.BlockSpec(memory_space=pl.ANY), pl.BlockSpec(memory_space=pl.ANY)],
        out_specs=pl.BlockSpec((None, 8, ATT_HEAD_DIM), prev),
        scratch_shapes=[pltpu.VMEM((K_SLOTS, K_CHUNK_PAGES, ATT_DIM, PAGE_SIZE), f32),
                        pltpu.VMEM((n_picks, PAGES_PER_BLOCK, ATT_HEAD_DIM, PAGE_SIZE), f32),
                        pltpu.VMEM((n_blocks, 8, MOBA_BLOCK), f32),
                        pltpu.VMEM((n_blocks, 8, MOBA_BLOCK), f32),
                        pltpu.VMEM((8, LANES), f32),
                        pltpu.VMEM((8, LANES), f32),
                        pltpu.SMEM((n_picks * (1 + PAGES_PER_BLOCK),), jnp.int32),
                        pltpu.SemaphoreType.DMA((K_SLOTS,)),
                        pltpu.SemaphoreType.DMA((1,))])
    out = pl.pallas_call(
        functools.partial(_moba_decode_kernel, layer=layer, n_pages=n_pages),
        grid_spec=grid_spec,
        out_shape=jax.ShapeDtypeStruct((b, 8, ATT_HEAD_DIM), f32),
        compiler_params=_cparams(1), name="moba_decode",
    )(page_table, q.reshape(b, 1, ATT_DIM), heads8(q), heads8(k_new), heads8(v_new), cache_kt, cache_vt)
    return out[:, :ATT_HEADS].reshape(b, ATT_DIM)


def _layer_params(l, w_in, conv_w, conv_b, dt_bias, a_log, d_skip, ssm_norm_g, pool_w, pool_scale, w_out):
    o_xbc = SSM_INNER
    o_dt = o_xbc + CONV_DIM
    o_pool = o_dt + SSM_HEADS
    o_q = o_pool + POOL_DIM
    o_k = o_q + ATT_DIM
    o_v = o_k + ATT_DIM
    wl = w_in[l]
    pad_heads = lambda v: jnp.pad(v.astype(f32), (0, LANES - SSM_HEADS)).reshape(1, LANES)
    w_cat = jnp.concatenate([wl[:, :o_pool], jnp.zeros((wl.shape[0], LANES - SSM_HEADS), wl.dtype),
                             wl[:, o_pool:]], axis=1).astype(bf16)
    pw = jnp.zeros((POOL_DIM, POOL_DIM), f32)
    for g in range(len(POOL_WINDOWS)):
        pw = pw.at[g * POOL_GROUP:(g + 1) * POOL_GROUP, g * POOL_GROUP:(g + 1) * POOL_GROUP].set(pool_w[l, g])
    expand = (jnp.arange(LANES)[:, None] == (jnp.arange(SSM_INNER)[None, :] // SSM_HEAD_DIM)).astype(f32)
    wo = w_out[l].astype(bf16)
    return {
        'w_in': w_cat,
        'conv_w': conv_w[l], 'conv_b': conv_b[l].reshape(1, CONV_DIM),
        'dt_bias': pad_heads(dt_bias[l]), 'a': pad_heads(-jnp.exp(a_log[l].astype(f32))),
        'dsk': jnp.repeat(d_skip[l].astype(f32), SSM_HEAD_DIM).reshape(1, SSM_INNER),
        'ssm_norm_g': ssm_norm_g[l].reshape(1, SSM_INNER),
        'pool_w': pw.astype(bf16), 'pool_scale': pool_scale[l].reshape(1, POOL_DIM),
        'expand': expand,
        'w_out': [wo[:SSM_INNER], wo[SSM_INNER:SSM_INNER + POOL_DIM], wo[SSM_INNER + POOL_DIM:]],
    }


def _in_proj_outs():
    sizes = [SSM_INNER, CONV_DIM, LANES, POOL_DIM, ATT_DIM, ATT_DIM, ATT_DIM]
    starts = [sum(sizes[:i]) for i in range(len(sizes))]
    return [(s, n, f32) for s, n in zip(starts, sizes)]


IN_PROJ_OUTS = _in_proj_outs()


def _pages_channel_major(cache):
    d, n_phys = cache.shape[:2]
    return jnp.transpose(cache, (0, 1, 3, 4, 2)).reshape(d, n_phys, ATT_DIM, PAGE_SIZE)


def kernel(x_prompt, x_sample, cache_moba_k, cache_moba_v, state_ssm, state_conv, state_pool, cache_mem_k, cache_mem_v, page_table, mem_prompt, norm_mix_g, w_in, conv_w, conv_b, dt_bias, a_log, d_skip, ssm_norm_g, pool_w, pool_scale, w_out, norm_cross_g, norm_mem_g, w_mem_q, w_mem_kv, w_mem_o, norm_ffn_g, w_gate_up, w_down, final_norm_g):
    depth = w_in.shape[0]
    bp, t, d = x_prompt.shape
    bs = x_sample.shape[0]
    assert bp == 1 and x_sample.shape[1] == 1
    past_len = page_table.shape[1] * PAGE_SIZE
    cache_kt = _pages_channel_major(cache_moba_k)
    cache_vt = _pages_channel_major(cache_moba_v)
    mem_k_split = _mem_split_view(cache_mem_k)
    mem_v_split = _mem_split_view(cache_mem_v)
    mem_len = mem_prompt.shape[1]
    xp = x_prompt.reshape(t, d)
    xs = x_sample.reshape(bs, d)
    mem = mem_prompt.reshape(mem_len, d)
    tm_p, tm_s = 512, bs
    outs = {n: [] for n in ('kp', 'vp', 'ks', 'vs', 'sp', 'ss', 'cp', 'cs', 'pp', 'ps', 'mk', 'mv')}
    for l in range(depth):
        prm = _layer_params(l, w_in, conv_w, conv_b, dt_bias, a_log, d_skip, ssm_norm_g, pool_w, pool_scale, w_out)
        wq = w_mem_q[l].astype(bf16)
        wkv = w_mem_kv[l].astype(bf16)
        wo_mem = w_mem_o[l].astype(bf16)
        wgu = w_gate_up[l].astype(bf16)
        wdn = w_down[l].astype(bf16)
        half = MEM_HEADS * MEM_HEAD_DIM
        last_layer = l == depth - 1

        mk, mv, mkb, mvb = norm_matmul(mem, norm_mem_g[l], wkv,
                                       [(0, half, f32), (half, half, f32), (0, half, bf16), (half, half, bf16)],
                                       tm=mem_len)
        z, xbc, dtr, u, q, k, v = norm_matmul(xp, norm_mix_g[l], prm['w_in'], IN_PROJ_OUTS, tm=tm_p)
        y, po, s_new = ssd_pool_prompt(z, xbc, dtr, u, prm, tm=tm_p)
        att = moba_prompt(q, k, v)
        xp = prompt_tail(xp, [y, po, att], prm['w_out'], norm_cross_g[l], wq, mkb, mvb, wo_mem,
                         norm_ffn_g[l], wgu, wdn, final_norm_g, norm_out=last_layer, tm=tm_p)
        outs['kp'].append(k.reshape(1, t, ATT_HEADS, ATT_HEAD_DIM))
        outs['vp'].append(v.reshape(1, t, ATT_HEADS, ATT_HEAD_DIM))
        outs['sp'].append(s_new[None])
        outs['cp'].append(xbc[t - (CONV_W - 1):][None])
        outs['pp'].append(u[t - POOL_HIST:][None])
        outs['mk'].append(mk.reshape(1, mem_len, MEM_HEADS, MEM_HEAD_DIM))
        outs['mv'].append(mv.reshape(1, mem_len, MEM_HEADS, MEM_HEAD_DIM))

        z, xbc, dtr, u, q, k, v = norm_matmul(xs, norm_mix_g[l], prm['w_in'], IN_PROJ_OUTS, tm=tm_s)
        y, po, c_new, s_new, p_new = ssd_pool_sample(z, xbc, dtr, u, state_conv[l], state_ssm[l],
                                                     state_pool[l], prm, pos0=past_len)
        att = moba_decode(page_table, q, k, v, cache_kt, cache_vt, layer=l)
        xs = matmul_residual(xs, [y, po, att], prm['w_out'], tm=tm_s)
        (qc,) = norm_matmul(xs, norm_cross_g[l], _split_channel_order(wq, 1), [(0, d, f32)], tm=tm_s)
        oc = cross_sample(qc, mem_k_split, mem_v_split, layer=l)
        xs = matmul_residual(xs, [oc], [_split_channel_order(wo_mem, 0)], tm=tm_s)
        xs = swiglu_block(xs, norm_ffn_g[l], wgu, wdn, final_norm_g, norm_out=last_layer, tm=tm_s, tf=D_FF)
        outs['ks'].append(k.reshape(bs, 1, ATT_HEADS, ATT_HEAD_DIM))
        outs['vs'].append(v.reshape(bs, 1, ATT_HEADS, ATT_HEAD_DIM))
        outs['ss'].append(s_new)
        outs['cs'].append(c_new)
        outs['ps'].append(p_new)

    y_prompt = xp.reshape(1, t, d)
    y_sample = xs.reshape(bs, 1, d)
    st = lambda n: jnp.stack(outs[n])
    return (y_prompt, y_sample, st('kp'), st('vp'), st('ks'), st('vs'), st('sp'), st('ss'),
            st('cp'), st('cs'), st('pp'), st('ps'), st('mk'), st('mv'))
```

```python
import functools

import jax
import jax.numpy as jnp
from jax import lax
from jax.experimental import pallas as pl
from jax.experimental.pallas import tpu as pltpu

f32 = jnp.float32
bf16 = jnp.bfloat16
HIGHEST = lax.Precision.HIGHEST

D_MODEL = 1024
SSM_INNER = 512
SSM_HEAD_DIM = 64
SSM_HEADS = 8
SSM_GROUPS = 2
SSM_STATE = 128
CONV_W = 4
CONV_DIM = SSM_INNER + 2 * SSM_GROUPS * SSM_STATE
SSD_CHUNK = 128
POOL_DIM = 256
POOL_WINDOWS = (2, 4, 8, 16)
POOL_GROUP = 64
POOL_HIST = 15
ATT_DIM = 256
ATT_HEAD_DIM = 64
ATT_HEADS = 4
MOBA_BLOCK = 256
MOBA_TOP_K = 3
PAGE_SIZE = 128
MEM_HEADS = 4
MEM_HEAD_DIM = 256
D_FF = 2816
RMS_EPS = 1e-6
LANES = 128
CONV_HALO = 8
POOL_HALO = 16
MASKED = -1e30
LOG2E = 1.4426950408889634
BF16_SUBLANES = 16
V_AUG_ROWS = ATT_HEAD_DIM + BF16_SUBLANES
MOBA_PAIRS_PER_TRIP = 4
VMEM_LIMIT = 56 * 1024 * 1024


def _cparams(n_axes):
    return pltpu.CompilerParams(dimension_semantics=("arbitrary",) * n_axes,
                                vmem_limit_bytes=VMEM_LIMIT)


def _rms(x, g):
    ms = jnp.mean(x * x, axis=-1, keepdims=True)
    return x * lax.rsqrt(ms + RMS_EPS) * g


def _dot(a, b, **kw):
    return jnp.dot(a, b, preferred_element_type=f32, **kw)


def _dot_nt(a, b, **kw):
    return lax.dot_general(a, b, (((1,), (1,)), ((), ())), preferred_element_type=f32, **kw)


def _silu(x):
    return x * jax.nn.sigmoid(x)


def _softplus(x):
    return jnp.maximum(x, 0.0) + jnp.log1p(jnp.exp(-jnp.abs(x)))


def _full(shape):
    return pl.BlockSpec(shape, lambda *_: (0,) * len(shape))


def _norm_mm_kernel(x_ref, g_ref, w_ref, *o_refs, groups):
    hb = _rms(x_ref[...], g_ref[...]).astype(bf16)
    res = {}
    for o_ref, (start, size) in zip(o_refs, groups):
        if (start, size) not in res:
            res[start, size] = _dot(hb, w_ref[:, start:start + size])
        o_ref[...] = res[start, size].astype(o_ref.dtype)


def norm_matmul(x, g, w, outs, tm):
    m, d = x.shape
    assert all(start % LANES == 0 and size % LANES == 0 for start, size, _ in outs)
    return pl.pallas_call(
        functools.partial(_norm_mm_kernel, groups=tuple((s, n) for s, n, _ in outs)),
        grid=(m // tm,),
        in_specs=[pl.BlockSpec((tm, d), lambda i: (i, 0)), _full((1, d)), _full(w.shape)],
        out_specs=[pl.BlockSpec((tm, n), lambda i: (i, 0)) for _, n, _ in outs],
        out_shape=[jax.ShapeDtypeStruct((m, n), dt) for _, n, dt in outs],
        compiler_params=_cparams(1), name="norm_matmul",
    )(x, g.reshape(1, d), w)


def _mm_res_kernel(*refs, n_a):
    res_ref = refs[0]
    a_refs = refs[1:1 + n_a]
    w_refs = refs[1 + n_a:1 + 2 * n_a]
    o_ref = refs[-1]
    acc = res_ref[...]
    for a, w in zip(a_refs, w_refs):
        acc = acc + _dot(a[...].astype(bf16), w[...])
    o_ref[...] = acc


def matmul_residual(res, a_list, w_list, tm):
    m, d = res.shape
    in_specs = [pl.BlockSpec((tm, d), lambda i: (i, 0))]
    in_specs += [pl.BlockSpec((tm, a.shape[1]), lambda i: (i, 0)) for a in a_list]
    in_specs += [_full(w.shape) for w in w_list]
    return pl.pallas_call(
        functools.partial(_mm_res_kernel, n_a=len(a_list)),
        grid=(m // tm,), in_specs=in_specs,
        out_specs=pl.BlockSpec((tm, d), lambda i: (i, 0)),
        out_shape=jax.ShapeDtypeStruct((m, d), f32),
        compiler_params=_cparams(1), name="matmul_residual",
    )(res, *a_list, *w_list)


def _prompt_tail_kernel(*refs, n_mix, norm_out):
    x_ref = refs[0]
    a_refs = refs[1:1 + n_mix]
    w_refs = refs[1 + n_mix:1 + 2 * n_mix]
    (g_ref, wq_ref, mk_ref, mv_ref, wo_ref,
     gf_ref, wg_ref, wu_ref, wd_ref, gout_ref, o_ref) = refs[1 + 2 * n_mix:]
    x = x_ref[...]
    for a, w in zip(a_refs, w_refs):
        x = x + _dot(a[...].astype(bf16), w[...])
    if n_mix:
        o_ref[...] = x
        x = o_ref[...]
    hb = _rms(x, g_ref[...]).astype(bf16)
    q = _dot(hb, wq_ref[...])
    acc = x
    for h in range(MEM_HEADS):
        sl = slice(h * MEM_HEAD_DIM, (h + 1) * MEM_HEAD_DIM)
        s = _dot_nt(q[:, sl].astype(bf16), mk_ref[:, sl]) * (MEM_HEAD_DIM ** -0.5)
        p = jnp.exp(s - jnp.max(s, axis=-1, keepdims=True))
        p = p / jnp.sum(p, axis=-1, keepdims=True)
        oh = _dot(p.astype(bf16), mv_ref[:, sl])
        acc = acc + _dot(oh.astype(bf16), wo_ref[sl, :])
    o_ref[...] = acc
    x2 = o_ref[...]
    hb = _rms(x2, gf_ref[...]).astype(bf16)
    act = _silu(_dot(hb, wg_ref[...])) * _dot(hb, wu_ref[...])
    y = x2 + _dot(act.astype(bf16), wd_ref[...])
    o_ref[...] = _rms(y, gout_ref[...]) if norm_out else y


def prompt_tail(x, mix_list, w_out_list, g, wq, mkb, mvb, wo, g_ffn, w_gu, w_d, g_out, norm_out, tm):
    m, d = x.shape
    row = lambda c: pl.BlockSpec((tm, c), lambda i: (i, 0))
    once = lambda shape, idx: pl.BlockSpec(shape, idx, pipeline_mode=pl.Buffered(1))
    resident = lambda a: once(a.shape, lambda i: (0,) * a.ndim)
    consts = [g.reshape(1, d), wq, mkb, mvb, wo, g_ffn.reshape(1, d)]
    tail = [g_out.reshape(1, d)]
    return pl.pallas_call(
        functools.partial(_prompt_tail_kernel, n_mix=len(mix_list), norm_out=norm_out), grid=(m // tm,),
        in_specs=[row(d)] + [row(a.shape[1]) for a in mix_list] + [resident(w) for w in w_out_list]
                 + [resident(c) for c in consts]
                 + [once((d, D_FF), lambda i: (0, 0)), once((d, D_FF), lambda i: (0, 1)), resident(w_d)]
                 + [resident(c) for c in tail],
        out_specs=row(d),
        out_shape=jax.ShapeDtypeStruct((m, d), f32),
        compiler_params=_cparams(1), name="prompt_tail",
    )(x, *mix_list, *w_out_list, *consts, w_gu, w_gu, w_d, *tail)


MEM_SPLIT = MEM_HEAD_DIM // LANES
MEM_ROWS = MEM_SPLIT * MEM_HEADS


def _split_channel_order(w, axis):
    shape = w.shape
    w = w.reshape(shape[:axis] + (MEM_HEADS, MEM_SPLIT, LANES) + shape[axis + 1:])
    return jnp.swapaxes(w, axis, axis + 1).reshape(shape)


def _mem_split_view(cache):
    d, b, m = cache.shape[:3]
    x = cache.reshape(d, b, m, MEM_HEADS, MEM_SPLIT, LANES)
    return jnp.transpose(x, (0, 1, 2, 4, 3, 5)).reshape(d, b, m, MEM_ROWS, LANES)


def _cross_sample_kernel(q_ref, mk_ref, mv_ref, o_ref):
    part = jnp.sum(mk_ref[...] * q_ref[...][None], axis=-1, keepdims=True)
    s = part
    for piece in range(1, MEM_SPLIT):
        s = s + jnp.roll(part, piece * MEM_HEADS, axis=1)
    s = s * (MEM_HEAD_DIM ** -0.5)
    p = jnp.exp(s - jnp.max(s, axis=0, keepdims=True))
    l = jnp.sum(p, axis=0)
    o_ref[...] = jnp.sum(p * mv_ref[...], axis=0) / l


def cross_sample(q_split, mem_k_split, mem_v_split, layer):
    b = q_split.shape[0]
    mlen = mem_k_split.shape[2]
    mem_spec = pl.BlockSpec((None, None, mlen, MEM_ROWS, LANES), lambda i: (layer, i, 0, 0, 0))
    row_spec = pl.BlockSpec((None, MEM_ROWS, LANES), lambda i: (i, 0, 0))
    out = pl.pallas_call(
        _cross_sample_kernel, grid=(b,),
        in_specs=[row_spec, mem_spec, mem_spec], out_specs=row_spec,
        out_shape=jax.ShapeDtypeStruct((b, MEM_ROWS, LANES), f32),
        compiler_params=_cparams(1), name="cross_sample",
    )(q_split.reshape(b, MEM_ROWS, LANES), mem_k_split, mem_v_split)
    return out.reshape(b, MEM_ROWS * LANES)


def _swiglu_kernel(x_ref, g_ref, wg_ref, wu_ref, wd_ref, gout_ref, o_ref, h_ref, acc_ref, *, norm_out):
    f = pl.program_id(1)

    @pl.when(f == 0)
    def _():
        h_ref[...] = _rms(x_ref[...], g_ref[...]).astype(bf16)
        acc_ref[...] = x_ref[...]

    hb = h_ref[...]
    a = _silu(_dot(hb, wg_ref[...])) * _dot(hb, wu_ref[...])
    acc_ref[...] += _dot(a.astype(bf16), wd_ref[...])

    @pl.when(f == pl.num_programs(1) - 1)
    def _():
        y = acc_ref[...]
        o_ref[...] = _rms(y, gout_ref[...]) if norm_out else y


def swiglu_block(x, g, w_gu, w_d, g_out, norm_out, tm, tf):
    m, d = x.shape
    nf = D_FF // tf
    mode = dict(pipeline_mode=pl.Buffered(1)) if nf == 1 else {}
    return pl.pallas_call(
        functools.partial(_swiglu_kernel, norm_out=norm_out), grid=(m // tm, nf),
        in_specs=[pl.BlockSpec((tm, d), lambda i, f: (i, 0)), _full((1, d)),
                  pl.BlockSpec((d, tf), lambda i, f: (0, f), **mode),
                  pl.BlockSpec((d, tf), lambda i, f: (0, f + nf), **mode),
                  pl.BlockSpec((tf, d), lambda i, f: (f, 0), **mode), _full((1, d))],
        out_specs=pl.BlockSpec((tm, d), lambda i, f: (i, 0)),
        out_shape=jax.ShapeDtypeStruct((m, d), f32),
        scratch_shapes=[pltpu.VMEM((tm, d), bf16), pltpu.VMEM((tm, d), f32)],
        compiler_params=_cparams(2), name="swiglu_block",
    )(x, g.reshape(1, d), w_gu, w_gu, w_d, g_out.reshape(1, d))


def _pool_select(w2, w4, w8, w16, cnt_of, lane):
    out = w16 / cnt_of(16)
    for win, acc in ((8, w8), (4, w4), (2, w2)):
        g = POOL_WINDOWS.index(win)
        out = jnp.where(lane < (g + 1) * POOL_GROUP, acc / cnt_of(win), out)
    return out


def _group_rmsnorm(y, g):
    half = SSM_INNER // SSM_GROUPS
    parts = [_rms(y[:, i * half:(i + 1) * half], g[:, i * half:(i + 1) * half])
             for i in range(SSM_GROUPS)]
    return jnp.concatenate(parts, axis=-1)


def _ssd_pool_prompt_kernel(z_ref, xbc_ref, dt_ref, u_ref, cw_ref, cb_ref, dtb_ref, a_ref, dsk_ref,
                            ng_ref, pw_ref, ps_ref,
                            y_ref, po_ref, st_ref,
                            ext_ref, pext_ref, xc_ref, s_ref, *, tm):
    i = pl.program_id(0)
    halo = CONV_HALO
    phalo = POOL_HALO

    @pl.when(i == 0)
    def _():
        ext_ref[0:halo, :] = jnp.zeros((halo, CONV_DIM), f32)
        pext_ref[0:phalo, :] = jnp.zeros((phalo, POOL_DIM), f32)
        s_ref[...] = jnp.zeros(s_ref.shape, f32)

    @pl.when(i > 0)
    def _():
        ext_ref[0:halo, :] = ext_ref[tm:tm + halo, :]
        pext_ref[0:phalo, :] = pext_ref[tm:tm + phalo, :]

    ext_ref[halo:halo + tm, :] = xbc_ref[...]
    pext_ref[phalo:phalo + tm, :] = u_ref[...]

    rp, lp = 64, 256
    for r0 in range(0, tm, rp):
        for c0 in range(0, CONV_DIM, lp):
            acc = jnp.broadcast_to(cb_ref[:, c0:c0 + lp], (rp, lp))
            for j in range(CONV_W):
                off = halo - (CONV_W - 1) + j + r0
                acc = acc + ext_ref[off:off + rp, c0:c0 + lp] * cw_ref[j:j + 1, c0:c0 + lp]
            xc_ref[r0:r0 + rp, c0:c0 + lp] = _silu(acc)

    lane_p = lax.broadcasted_iota(jnp.int32, (rp, POOL_DIM), 1)
    row_p = lax.broadcasted_iota(jnp.int32, (rp, POOL_DIM), 0)
    for r0 in range(0, tm, rp):
        run = pext_ref[r0:r0 + phalo + rp, :]
        u = run[phalo:]
        sums, win = {}, 1
        while win < max(POOL_WINDOWS):
            run = run + pltpu.roll(run, win, 0)
            win *= 2
            sums[win] = run[phalo:]
        pos1 = (row_p + (i * tm + r0 + 1)).astype(f32)
        pooled = _pool_select(sums[2], sums[4], sums[8], sums[16],
                              lambda win: jnp.minimum(float(win), pos1), lane_p) - u
        po_ref[r0:r0 + rp, :] = (_dot(pooled.astype(bf16), pw_ref[...]) * ps_ref[...]).astype(po_ref.dtype)

    cl = SSD_CHUNK
    r_io = lax.broadcasted_iota(jnp.int32, (cl, cl), 0)
    c_io = lax.broadcasted_iota(jnp.int32, (cl, cl), 1)
    ltri = (c_io <= r_io).astype(f32)
    causal = c_io <= r_io
    first_half = c_io < SSM_HEAD_DIM
    n_pairs = SSM_HEADS // 2
    heads_per_group = SSM_HEADS // SSM_GROUPS

    def chunk(c, carry):
        r0 = pl.multiple_of(c * cl, cl)
        xc = xc_ref[pl.ds(r0, cl), :]
        dt = _softplus(dt_ref[pl.ds(r0, cl), :] + dtb_ref[...])
        a_cum = _dot(ltri, dt * a_ref[...], precision=HIGHEST)
        a_cum_t = a_cum.T
        dt_t = dt.T
        to_end_t = jnp.exp(a_cum_t[:, cl - 1:cl] - a_cum_t) * dt_t
        b_t = [xc[:, SSM_INNER + g * SSM_STATE:SSM_INNER + (g + 1) * SSM_STATE].T
               for g in range(SSM_GROUPS)]
        cm = [xc[:, SSM_INNER + SSM_GROUPS * SSM_STATE + g * SSM_STATE:
                 SSM_INNER + SSM_GROUPS * SSM_STATE + (g + 1) * SSM_STATE].astype(bf16)
              for g in range(SSM_GROUPS)]
        scores = [_dot(cm[g], b_t[g].astype(bf16)) for g in range(SSM_GROUPS)]
        y_pairs = []
        for k in range(n_pairs):
            g = (2 * k) // heads_per_group
            xs_pair = xc[:, k * LANES:(k + 1) * LANES]
            xs_b = xs_pair.astype(bf16)
            yd, cs, eb = [], [], []
            for h in (2 * k, 2 * k + 1):
                colb = jnp.broadcast_to(a_cum[:, h:h + 1], (cl, cl))
                seg = colb - a_cum_t[h:h + 1, :]
                decay = jnp.exp(jnp.where(causal, seg, -jnp.inf))
                mh = scores[g] * decay * dt_t[h:h + 1, :]
                yd.append(_dot(mh.astype(bf16), xs_b))
                cs.append(_dot((b_t[g] * to_end_t[h:h + 1, :]).astype(bf16), xs_b))
                eb.append(jnp.exp(colb))
            e_pair = jnp.where(first_half, eb[0], eb[1])
            s_old = s_ref[k]
            y_off = _dot(cm[g], s_old.astype(bf16)) * e_pair
            y_pairs.append(jnp.where(first_half, yd[0], yd[1]) + y_off
                           + dsk_ref[:, k * LANES:(k + 1) * LANES] * xs_pair)
            s_ref[k] = s_old * e_pair[cl - 1:cl, :] + jnp.where(first_half, cs[0], cs[1])
        y = jnp.concatenate(y_pairs, axis=-1) * _silu(z_ref[pl.ds(r0, cl), :])
        y_ref[pl.ds(r0, cl), :] = _group_rmsnorm(y, ng_ref[...]).astype(y_ref.dtype)
        return carry

    lax.fori_loop(0, tm // cl, chunk, 0, unroll=True)

    @pl.when(i == pl.num_programs(0) - 1)
    def _():
        for k in range(n_pairs):
            st = s_ref[k].T
            st_ref[2 * k] = st[0:SSM_HEAD_DIM, :]
            st_ref[2 * k + 1] = st[SSM_HEAD_DIM:2 * SSM_HEAD_DIM, :]


def ssd_pool_prompt(z, xbc, dt, u, prm, tm):
    t = z.shape[0]
    assert POOL_WINDOWS == tuple(2 ** (n + 1) for n in range(len(POOL_WINDOWS)))
    row = lambda c: pl.BlockSpec((tm, c), lambda i: (i, 0))
    consts = [prm['conv_w'], prm['conv_b'], prm['dt_bias'], prm['a'], prm['dsk'], prm['ssm_norm_g'],
              prm['pool_w'], prm['pool_scale']]
    return pl.pallas_call(
        functools.partial(_ssd_pool_prompt_kernel, tm=tm), grid=(t // tm,),
        in_specs=[row(SSM_INNER), row(CONV_DIM), row(LANES), row(POOL_DIM)] + [_full(c.shape) for c in consts],
        out_specs=[row(SSM_INNER), row(POOL_DIM), _full((SSM_HEADS, SSM_HEAD_DIM, SSM_STATE))],
        out_shape=[jax.ShapeDtypeStruct((t, SSM_INNER), bf16), jax.ShapeDtypeStruct((t, POOL_DIM), bf16),
                   jax.ShapeDtypeStruct((SSM_HEADS, SSM_HEAD_DIM, SSM_STATE), f32)],
        scratch_shapes=[pltpu.VMEM((tm + CONV_HALO, CONV_DIM), f32), pltpu.VMEM((tm + POOL_HALO, POOL_DIM), f32),
                        pltpu.VMEM((tm, CONV_DIM), f32),
                        pltpu.VMEM((SSM_HEADS // 2, SSM_STATE, 2 * SSM_HEAD_DIM), f32)],
        compiler_params=_cparams(1), name="ssd_pool_prompt",
    )(z, xbc, dt, u, *consts)


def _col_tile(row):
    return jnp.broadcast_to(row, (LANES, LANES)).T


def _ssd_pool_sample_kernel(z_ref, xbc_ref, dt_ref, u_ref, sc_ref, ss_ref, sp_ref,
                            cw_ref, cb_ref, dtb_ref, a_ref, dsk_ref, ng_ref, pw_ref, ps_ref, ex_ref,
                            y_ref, po_ref, cn_ref, sn_ref, pn_ref, *, pos0):
    xrow = xbc_ref[...]
    acc = cb_ref[...] + xrow * cw_ref[CONV_W - 1:CONV_W, :]
    for j in range(CONV_W - 1):
        acc = acc + sc_ref[j:j + 1, :] * cw_ref[j:j + 1, :]
        if j > 0:
            cn_ref[j - 1:j, :] = sc_ref[j:j + 1, :]
    cn_ref[CONV_W - 2:CONV_W - 1, :] = xrow
    xc = _silu(acc)

    dt = _softplus(dt_ref[...] + dtb_ref[...])
    dta = dt * a_ref[...]
    both = jnp.concatenate([jnp.broadcast_to(dt, (8, LANES)), jnp.broadcast_to(dta, (8, LANES))], axis=0)
    both_x = _dot(both, ex_ref[...], precision=HIGHEST)
    dtx = both_x[0:1, :]
    dec_x = jnp.exp(both_x[8:9, :])
    xs = xc[:, 0:SSM_INNER]
    xdt = xs * dtx
    heads_per_group = SSM_HEADS // SSM_GROUPS
    y_pairs = []
    for k in range(SSM_HEADS // 2):
        g = (2 * k) // heads_per_group
        b_row = xc[:, SSM_INNER + g * SSM_STATE:SSM_INNER + (g + 1) * SSM_STATE]
        c_row = xc[:, SSM_INNER + SSM_GROUPS * SSM_STATE + g * SSM_STATE:
                   SSM_INNER + SSM_GROUPS * SSM_STATE + (g + 1) * SSM_STATE]
        sl = slice(k * LANES, (k + 1) * LANES)
        s_old = jnp.concatenate([ss_ref[2 * k], ss_ref[2 * k + 1]], axis=0)
        s_new = s_old * _col_tile(dec_x[:, sl]) + _col_tile(xdt[:, sl]) * b_row
        sn_ref[2 * k] = s_new[0:SSM_HEAD_DIM, :]
        sn_ref[2 * k + 1] = s_new[SSM_HEAD_DIM:, :]
        y_k = _dot_nt(jnp.broadcast_to(c_row, (8, SSM_STATE)), s_new, precision=HIGHEST)[0:1, :]
        y_pairs.append(y_k + dsk_ref[:, sl] * xs[:, sl])
    y = jnp.concatenate(y_pairs, axis=-1) * _silu(z_ref[...])
    y_ref[...] = _group_rmsnorm(y, ng_ref[...]).astype(y_ref.dtype)

    u = u_ref[...]
    prev = sp_ref[...]
    rowi = lax.broadcasted_iota(jnp.int32, prev.shape, 0)
    tail = lambda win: u + jnp.sum(jnp.where(rowi >= POOL_HIST - (win - 1), prev, 0.0), axis=0, keepdims=True)
    lane_p = lax.broadcasted_iota(jnp.int32, (1, POOL_DIM), 1)
    pooled = _pool_select(tail(2), tail(4), tail(8), tail(16),
                          lambda win: float(min(win, pos0 + 1)), lane_p) - u
    po = _dot(jnp.broadcast_to(pooled, (8, POOL_DIM)).astype(bf16), pw_ref[...])[0:1, :] * ps_ref[...]
    po_ref[...] = po.astype(po_ref.dtype)
    pn_ref[0:POOL_HIST - 1, :] = sp_ref[1:POOL_HIST, :]
    pn_ref[POOL_HIST - 1:POOL_HIST, :] = u


SAMPLE_SEQS_PER_STEP = 2
N_SAMPLE_SEQ_INPUTS, N_SAMPLE_SEQ_OUTPUTS = 7, 5


def _ssd_pool_sample_group_kernel(*refs, pos0):
    seq_in = refs[:N_SAMPLE_SEQ_INPUTS]
    consts = refs[N_SAMPLE_SEQ_INPUTS:len(refs) - N_SAMPLE_SEQ_OUTPUTS]
    seq_out = refs[len(refs) - N_SAMPLE_SEQ_OUTPUTS:]
    for s in range(SAMPLE_SEQS_PER_STEP):
        _ssd_pool_sample_kernel(*[r.at[s] for r in seq_in], *consts, *[r.at[s] for r in seq_out], pos0=pos0)


def ssd_pool_sample(z, xbc, dt, u, st_conv, st_ssm, st_pool, prm, pos0):
    b = z.shape[0]
    grp = SAMPLE_SEQS_PER_STEP
    assert b % grp == 0
    per_seq = lambda *shape: pl.BlockSpec((grp,) + shape, lambda i: (i,) + (0,) * len(shape))
    consts = [prm['conv_w'], prm['conv_b'], prm['dt_bias'], prm['a'], prm['dsk'], prm['ssm_norm_g'],
              prm['pool_w'], prm['pool_scale'], prm['expand']]
    outs = pl.pallas_call(
        functools.partial(_ssd_pool_sample_group_kernel, pos0=pos0), grid=(b // grp,),
        in_specs=[per_seq(1, SSM_INNER), per_seq(1, CONV_DIM), per_seq(1, LANES), per_seq(1, POOL_DIM),
                  per_seq(CONV_W - 1, CONV_DIM), per_seq(SSM_HEADS, SSM_HEAD_DIM, SSM_STATE),
                  per_seq(POOL_HIST, POOL_DIM)] + [_full(c.shape) for c in consts],
        out_specs=[per_seq(1, SSM_INNER), per_seq(1, POOL_DIM), per_seq(CONV_W - 1, CONV_DIM),
                   per_seq(SSM_HEADS, SSM_HEAD_DIM, SSM_STATE), per_seq(POOL_HIST, POOL_DIM)],
        out_shape=[jax.ShapeDtypeStruct((b, 1, SSM_INNER), f32), jax.ShapeDtypeStruct((b, 1, POOL_DIM), f32),
                   jax.ShapeDtypeStruct((b, CONV_W - 1, CONV_DIM), f32),
                   jax.ShapeDtypeStruct((b, SSM_HEADS, SSM_HEAD_DIM, SSM_STATE), f32),
                   jax.ShapeDtypeStruct((b, POOL_HIST, POOL_DIM), f32)],
        compiler_params=_cparams(1), name="ssd_pool_sample",
    )(z.reshape(b, 1, -1), xbc.reshape(b, 1, -1), dt.reshape(b, 1, -1), u.reshape(b, 1, -1),
      st_conv, st_ssm, st_pool, *consts)
    y, po, cn, sn, pn = outs
    return y.reshape(b, -1), po.reshape(b, -1), cn, sn, pn


def _topk_blocks(gate, n_past, axis):
    blk = lax.broadcasted_iota(jnp.int32, gate.shape, axis).astype(f32)
    g = jnp.where(blk < jnp.asarray(n_past, f32), gate, -jnp.inf)
    picks = []
    for _ in range(MOBA_TOP_K):
        m = jnp.max(g, axis=axis, keepdims=True)
        idx = jnp.min(jnp.where(g == m, blk, float(LANES)), axis=axis, keepdims=True)
        picks.append((idx, jnp.abs(m) < jnp.inf))
        g = jnp.where(blk == idx, -jnp.inf, g)
    return picks


def _moba_prompt_kernel(*refs, n_prev):
    q_ref, k_ref, v_ref = refs[:3]
    (o_ref, kt_out_ref, vt_out_ref,
     km_ref, kb_ref, vt_ref, qst_ref, nmt_ref, acc_ref, sa_ref, sb_ref) = refs[3 + n_prev:]
    i = pl.program_id(0)
    tq = MOBA_BLOCK
    scale = ATT_HEAD_DIM ** -0.5

    @pl.when(i == 0)
    def _():
        km_ref[...] = jnp.zeros(km_ref.shape, f32)

    k = k_ref[...]
    kb = k.astype(bf16)
    head = lambda h: slice(h * ATT_HEAD_DIM, (h + 1) * ATT_HEAD_DIM)
    for h in range(ATT_HEADS):
        kb_ref[i, h] = kb[:, head(h)]
    vt_f32 = v_ref[...].T
    kt_f32 = k.T
    if n_prev:
        kt_out_ref[...] = kt_f32
        vt_out_ref[...] = vt_f32
    else:
        for slab in range(kt_out_ref.shape[0]):
            kt_out_ref[slab] = kt_f32
            vt_out_ref[slab] = vt_f32
    vt = vt_f32.astype(bf16)
    ones_rows = jnp.ones((V_AUG_ROWS - ATT_HEAD_DIM, tq), bf16)
    for h in range(ATT_HEADS):
        vt_ref[i, h] = jnp.concatenate([vt[head(h), :], ones_rows], axis=0)
    qt = q_ref[...].T
    n_slots = km_ref.shape[0]
    blk_row = lax.broadcasted_iota(jnp.int32, (n_slots, tq), 0).astype(f32)
    key_io = lax.broadcasted_iota(jnp.int32, (tq, tq), 0)
    qry_io = lax.broadcasted_iota(jnp.int32, (tq, tq), 1)
    km = km_ref[...]
    m_own = []
    for h in range(ATT_HEADS):
        qth = qt[head(h), :]
        gate = _dot(km[:, head(h)], qth, precision=HIGHEST)
        sel = jnp.zeros((n_slots, tq), jnp.bool_)
        for idx, ok in _topk_blocks(gate, i, axis=0):
            sel = sel | ((blk_row == idx) & ok)
        nmt_ref[h] = jnp.where(sel, 0.0, MASKED)
        qst = (qth * (scale * LOG2E)).astype(bf16)
        qst_ref[h] = qst
        st = jnp.where(key_io <= qry_io, _dot(kb[:, head(h)], qst), MASKED)
        m = jnp.max(st, axis=0, keepdims=True)
        m_own.append(m)
        acc_ref[h] = _dot(vt_ref[i, h], jnp.exp2(st - m).astype(bf16))

    km_ref[pl.ds(i, 1), :] = jnp.mean(k, axis=0, keepdims=True)

    def scores(j, dst):
        for h in range(ATT_HEADS):
            dst[h] = _dot(kb_ref[j, h], qst_ref[h])

    def absorb(j, src, carry):
        new = []
        for h in range(ATT_HEADS):
            m_prev = carry[h]
            raw = src[h]
            bias = nmt_ref[h, pl.ds(j, 1), :]
            m_new = jnp.maximum(m_prev, jnp.max(raw, axis=0, keepdims=True) + bias)
            alpha = jnp.exp2(m_prev - m_new)
            p = jnp.exp2(raw - (m_new - bias))
            new.append(m_new)
            acc_ref[h] = acc_ref[h] * alpha + _dot(vt_ref[j, h], p.astype(bf16))
        return tuple(new)

    last = jnp.maximum(i - 1, 0)
    scores(0, sa_ref)

    def pair(t, carry):
        scores(jnp.minimum(2 * t + 1, last), sb_ref)
        carry = absorb(2 * t, sa_ref, carry)
        scores(jnp.minimum(2 * t + 2, last), sa_ref)
        return absorb(2 * t + 1, sb_ref, carry)

    def pairs(first, count, c):
        for t in range(count):
            c = pair(first + t, c)
        return c

    per_trip = MOBA_PAIRS_PER_TRIP
    carry = lax.fori_loop(0, i // (2 * per_trip), lambda u, c: pairs(per_trip * u, per_trip, c), tuple(m_own))
    n = per_trip // 2
    while n >= 1:
        first = i // (4 * n) * (2 * n)
        carry = lax.cond(i % (4 * n) >= 2 * n, functools.partial(pairs, first, n), lambda c: c, carry)
        n //= 2
    carry = lax.cond(i % 2 == 1, lambda c: absorb(i - 1, sa_ref, c), lambda c: c, carry)
    out_t = jnp.concatenate(
        [acc_ref[h, 0:ATT_HEAD_DIM, :] / acc_ref[h, ATT_HEAD_DIM:ATT_HEAD_DIM + 1, :] for h in range(ATT_HEADS)],
        axis=0)
    o_ref[...] = out_t.T.astype(o_ref.dtype)


def moba_prompt(q, k, v, layer, depth, kv_t_prev=None):
    t = q.shape[0]
    tq = MOBA_BLOCK
    n_blk = t // tq
    n_slots = -(-n_blk // 8) * 8
    assert n_slots <= LANES
    tile = pl.BlockSpec((tq, ATT_DIM), lambda i: (i, 0))
    stack = jax.ShapeDtypeStruct((depth, ATT_DIM, t), f32)
    prev = () if kv_t_prev is None else tuple(kv_t_prev)
    assert bool(prev) == (layer > 0)
    slab = (pl.BlockSpec((None, ATT_DIM, tq), lambda i: (layer, 0, i)) if prev
            else pl.BlockSpec((depth, ATT_DIM, tq), lambda i: (0, 0, i)))
    return pl.pallas_call(
        functools.partial(_moba_prompt_kernel, n_prev=len(prev)), grid=(n_blk,),
        in_specs=[tile, tile, tile] + [pl.BlockSpec(memory_space=pl.ANY)] * len(prev),
        out_specs=[tile, slab, slab],
        out_shape=[jax.ShapeDtypeStruct((t, ATT_DIM), bf16), stack, stack],
        input_output_aliases={3 + n: 1 + n for n in range(len(prev))},
        scratch_shapes=[pltpu.VMEM((n_slots, ATT_DIM), f32),
                        pltpu.VMEM((n_blk, ATT_HEADS, tq, ATT_HEAD_DIM), bf16),
                        pltpu.VMEM((n_blk, ATT_HEADS, V_AUG_ROWS, tq), bf16),
                        pltpu.VMEM((ATT_HEADS, ATT_HEAD_DIM, tq), bf16),
                        pltpu.VMEM((ATT_HEADS, n_slots, tq), f32),
                        pltpu.VMEM((ATT_HEADS, V_AUG_ROWS, tq), f32),
                        pltpu.VMEM((ATT_HEADS, tq, tq), f32),
                        pltpu.VMEM((ATT_HEADS, tq, tq), f32)],
        compiler_params=_cparams(1), name="moba_prompt",
    )(q, k, v, *prev)


K_CHUNK_PAGES = 16
K_SLOTS = 8
PAGES_PER_BLOCK = MOBA_BLOCK // PAGE_SIZE
CHUNKS_BEFORE_FINISH = 2


def _moba_decode_kernel(pt_ref, q_ref, q8_ref, kn8_ref, vn8_ref, kc_ref, vc_ref, o_ref,
                        kbuf, vbuf, s_ref, p_ref, gate_ref, stash_ref, ids_ref, ksem, vsem, *, layer, n_pages):
    b = pl.program_id(0)
    n_seq = pl.num_programs(0) - 1
    n_chunks = n_pages // K_CHUNK_PAGES
    blocks_per_chunk = K_CHUNK_PAGES // PAGES_PER_BLOCK
    n_blocks = n_pages // PAGES_PER_BLOCK
    scale = ATT_HEAD_DIM ** -0.5

    def k_copy(seq, page_slot, slot, p):
        page = pt_ref[seq, page_slot]
        return pltpu.make_async_copy(kc_ref.at[layer, page], kbuf.at[slot, p], ksem.at[slot])

    def start_chunk(seq, c, slot):
        for p in range(K_CHUNK_PAGES):
            k_copy(seq, c * K_CHUNK_PAGES + p, slot, p).start()

    def wait_chunk(seq, c, slot):
        for p in range(K_CHUNK_PAGES):
            k_copy(seq, c * K_CHUNK_PAGES + p, slot, p).wait()

    n_picks = ATT_HEADS * MOBA_TOP_K

    def v_copy(h, r, half, page):
        return pltpu.make_async_copy(vc_ref.at[layer, page, pl.ds(h * ATT_HEAD_DIM, ATT_HEAD_DIM), :],
                                     vbuf.at[h * MOBA_TOP_K + r, half], vsem.at[0])

    def stream(chunks):
        lane8 = lax.broadcasted_iota(jnp.int32, (8, ATT_DIM), 1)
        row8 = lax.broadcasted_iota(jnp.int32, (8, ATT_DIM), 0)
        qblk_b = jnp.where(lane8 // ATT_HEAD_DIM == row8, q_ref[...], 0.0).astype(bf16)
        blk_lane = lax.broadcasted_iota(jnp.int32, (8, LANES), 1)
        gate = gate_ref[...]
        for c in chunks:
            slot = c % K_SLOTS
            ahead = c + K_SLOTS - 1
            if ahead < n_chunks:
                start_chunk(b, ahead, ahead % K_SLOTS)
            else:
                @pl.when(b + 1 < n_seq)
                def _():
                    start_chunk(b + 1, ahead - n_chunks, ahead % K_SLOTS)
            wait_chunk(b, c, slot)
            for t in range(blocks_per_chunk):
                blk = c * blocks_per_chunk + t
                halves = [_dot(qblk_b, kbuf[slot, t * PAGES_PER_BLOCK + half].astype(bf16))
                          for half in range(PAGES_PER_BLOCK)]
                for half in range(PAGES_PER_BLOCK):
                    s_ref[blk, :, half * PAGE_SIZE:(half + 1) * PAGE_SIZE] = halves[half]
                tot = jnp.sum(sum(halves), axis=-1, keepdims=True) * (1.0 / MOBA_BLOCK)
                gate = jnp.where(blk_lane == blk, tot, gate)
        gate_ref[...] = gate

    def choose_and_weigh():
        picks = _topk_blocks(gate_ref[...], n_blocks, axis=1)
        for h in range(ATT_HEADS):
            for r in range(MOBA_TOP_K):
                blk = picks[r][0][h, 0].astype(jnp.int32)
                ids_ref[h * MOBA_TOP_K + r] = blk
                for half in range(PAGES_PER_BLOCK):
                    page = pt_ref[b, blk * PAGES_PER_BLOCK + half]
                    ids_ref[n_picks + (h * MOBA_TOP_K + r) * PAGES_PER_BLOCK + half] = page
                    v_copy(h, r, half, page).start()
        blk_io = lax.broadcasted_iota(jnp.int32, (n_blocks, 8, MOBA_BLOCK), 0).astype(f32)
        seen = jnp.zeros((n_blocks, 8, MOBA_BLOCK), jnp.bool_)
        for idx, ok in picks:
            seen = seen | ((blk_io == idx[None]) & ok[None])
        s_all = jnp.where(seen, s_ref[...] * scale, -jnp.inf)
        s_own = jnp.sum(q8_ref[...] * kn8_ref[...], axis=-1, keepdims=True) * scale
        m = jnp.maximum(jnp.max(jnp.max(s_all, axis=0), axis=-1, keepdims=True), s_own)
        p_all = jnp.exp(s_all - m[None])
        p_own = jnp.exp(s_own - m)
        l = jnp.sum(jnp.sum(p_all, axis=0), axis=-1, keepdims=True) + p_own
        p_ref[...] = p_all
        stash_ref[:, 0:ATT_HEAD_DIM] = p_own * vn8_ref[...]
        stash_ref[:, ATT_HEAD_DIM:] = jnp.broadcast_to(l, (8, LANES - ATT_HEAD_DIM))

    def finish_previous():
        for h in range(ATT_HEADS):
            for r in range(MOBA_TOP_K):
                for half in range(PAGES_PER_BLOCK):
                    page = ids_ref[n_picks + (h * MOBA_TOP_K + r) * PAGES_PER_BLOCK + half]
                    v_copy(h, r, half, page).wait()
        row_hd = lax.broadcasted_iota(jnp.int32, (8, ATT_HEAD_DIM), 0)
        o = stash_ref[:, 0:ATT_HEAD_DIM]
        for h in range(ATT_HEADS):
            oh = jnp.zeros((8, ATT_HEAD_DIM), f32)
            for r in range(MOBA_TOP_K):
                pb = p_ref[ids_ref[h * MOBA_TOP_K + r]].astype(bf16)
                for half in range(PAGES_PER_BLOCK):
                    oh = oh + _dot_nt(pb[:, half * PAGE_SIZE:(half + 1) * PAGE_SIZE],
                                      vbuf[h * MOBA_TOP_K + r, half].astype(bf16))
            o = o + jnp.where(row_hd == h, oh, 0.0)
        o_ref[...] = o / stash_ref[:, ATT_HEAD_DIM:ATT_HEAD_DIM + 1]

    @pl.when(b == 0)
    def _():
        for c in range(K_SLOTS - 1):
            start_chunk(0, c, c)

    @pl.when(b < n_seq)
    def _():
        gate_ref[...] = jnp.zeros((8, LANES), f32)
        stream(range(0, CHUNKS_BEFORE_FINISH))

    @pl.when(b > 0)
    def _():
        finish_previous()

    @pl.when(b < n_seq)
    def _():
        stream(range(CHUNKS_BEFORE_FINISH, n_chunks))
        choose_and_weigh()


def moba_decode(page_table, q, k_new, v_new, cache_kt, cache_vt, layer):
    b = q.shape[0]
    n_pages = page_table.shape[1]
    n_blocks = n_pages // PAGES_PER_BLOCK
    assert n_blocks <= LANES and n_blocks >= MOBA_TOP_K
    heads8 = lambda a: jnp.pad(a.reshape(b, ATT_HEADS, ATT_HEAD_DIM), ((0, 0), (0, 8 - ATT_HEADS), (0, 0)))
    n_picks = ATT_HEADS * MOBA_TOP_K
    cur = lambda i, pt: (jnp.minimum(i, b - 1), 0, 0)
    prev = lambda i, pt: (jnp.maximum(i - 1, 0), 0, 0)
    per_head = pl.BlockSpec((None, 8, ATT_HEAD_DIM), cur)
    grid_spec = pltpu.PrefetchScalarGridSpec(
        num_scalar_prefetch=1, grid=(b + 1,),
        in_specs=[pl.BlockSpec((None, 1, ATT_DIM), cur), per_head, per_head, per_head,
                  pl.BlockSpec(memory_space=pl.ANY), pl.BlockSpec(memory_space=pl.ANY)],
        out_specs=pl.BlockSpec((None, 8, ATT_HEAD_DIM), prev),
        scratch_shapes=[pltpu.VMEM((K_SLOTS, K_CHUNK_PAGES, ATT_DIM, PAGE_SIZE), f32),
                        pltpu.VMEM((n_picks, PAGES_PER_BLOCK, ATT_HEAD_DIM, PAGE_SIZE), f32),
                        pltpu.VMEM((n_blocks, 8, MOBA_BLOCK), f32),
                        pltpu.VMEM((n_blocks, 8, MOBA_BLOCK), f32),
                        pltpu.VMEM((8, LANES), f32),
                        pltpu.VMEM((8, LANES), f32),
                        pltpu.SMEM((n_picks * (1 + PAGES_PER_BLOCK),), jnp.int32),
                        pltpu.SemaphoreType.DMA((K_SLOTS,)),
                        pltpu.SemaphoreType.DMA((1,))])
    out = pl.pallas_call(
        functools.partial(_moba_decode_kernel, layer=layer, n_pages=n_pages),
        grid_spec=grid_spec,
        out_shape=jax.ShapeDtypeStruct((b, 8, ATT_HEAD_DIM), f32),
        compiler_params=_cparams(1), name="moba_decode",
    )(page_table, q.reshape(b, 1, ATT_DIM), heads8(q), heads8(k_new), heads8(v_new), cache_kt, cache_vt)
    return out[:, :ATT_HEADS].reshape(b, ATT_DIM)


def _layer_params(l, w_in, conv_w, conv_b, dt_bias, a_log, d_skip, ssm_norm_g, pool_w, pool_scale, w_out):
    o_xbc = SSM_INNER
    o_dt = o_xbc + CONV_DIM
    o_pool = o_dt + SSM_HEADS
    o_q = o_pool + POOL_DIM
    o_k = o_q + ATT_DIM
    o_v = o_k + ATT_DIM
    wl = w_in[l]
    pad_heads = lambda v: jnp.pad(v.astype(f32), (0, LANES - SSM_HEADS)).reshape(1, LANES)
    w_cat = jnp.concatenate([wl[:, :o_pool], jnp.zeros((wl.shape[0], LANES - SSM_HEADS), wl.dtype),
                             wl[:, o_pool:]], axis=1).astype(bf16)
    pw = jnp.zeros((POOL_DIM, POOL_DIM), f32)
    for g in range(len(POOL_WINDOWS)):
        pw = pw.at[g * POOL_GROUP:(g + 1) * POOL_GROUP, g * POOL_GROUP:(g + 1) * POOL_GROUP].set(pool_w[l, g])
    expand = (jnp.arange(LANES)[:, None] == (jnp.arange(SSM_INNER)[None, :] // SSM_HEAD_DIM)).astype(f32)
    wo = w_out[l].astype(bf16)
    return {
        'w_in': w_cat,
        'conv_w': conv_w[l], 'conv_b': conv_b[l].reshape(1, CONV_DIM),
        'dt_bias': pad_heads(dt_bias[l]), 'a': pad_heads(-jnp.exp(a_log[l].astype(f32))),
        'dsk': jnp.repeat(d_skip[l].astype(f32), SSM_HEAD_DIM).reshape(1, SSM_INNER),
        'ssm_norm_g': ssm_norm_g[l].reshape(1, SSM_INNER),
        'pool_w': pw.astype(bf16), 'pool_scale': pool_scale[l].reshape(1, POOL_DIM),
        'expand': expand,
        'w_out': [wo[:SSM_INNER], wo[SSM_INNER:SSM_INNER + POOL_DIM], wo[SSM_INNER + POOL_DIM:]],
    }


def _in_proj_outs():
    sizes = [SSM_INNER, CONV_DIM, LANES, POOL_DIM, ATT_DIM, ATT_DIM, ATT_DIM]
    starts = [sum(sizes[:i]) for i in range(len(sizes))]
    return [(s, n, f32) for s, n in zip(starts, sizes)]


IN_PROJ_OUTS = _in_proj_outs()


def _pages_channel_major(cache):
    d, n_phys = cache.shape[:2]
    return jnp.transpose(cache, (0, 1, 3, 4, 2)).reshape(d, n_phys, ATT_DIM, PAGE_SIZE)


def kernel(x_prompt, x_sample, cache_moba_k, cache_moba_v, state_ssm, state_conv, state_pool, cache_mem_k, cache_mem_v, page_table, mem_prompt, norm_mix_g, w_in, conv_w, conv_b, dt_bias, a_log, d_skip, ssm_norm_g, pool_w, pool_scale, w_out, norm_cross_g, norm_mem_g, w_mem_q, w_mem_kv, w_mem_o, norm_ffn_g, w_gate_up, w_down, final_norm_g):
    depth = w_in.shape[0]
    bp, t, d = x_prompt.shape
    bs = x_sample.shape[0]
    assert bp == 1 and x_sample.shape[1] == 1
    past_len = page_table.shape[1] * PAGE_SIZE
    cache_kt = _pages_channel_major(cache_moba_k)
    cache_vt = _pages_channel_major(cache_moba_v)
    mem_k_split = _mem_split_view(cache_mem_k)
    mem_v_split = _mem_split_view(cache_mem_v)
    mem_len = mem_prompt.shape[1]
    xp = x_prompt.reshape(t, d)
    xs = x_sample.reshape(bs, d)
    mem = mem_prompt.reshape(mem_len, d)
    tm_p, tm_s = 512, bs
    outs = {n: [] for n in ('ks', 'vs', 'sp', 'ss', 'cp', 'cs', 'pp', 'ps', 'mk', 'mv')}
    kv_t = None
    for l in range(depth):
        prm = _layer_params(l, w_in, conv_w, conv_b, dt_bias, a_log, d_skip, ssm_norm_g, pool_w, pool_scale, w_out)
        wq = w_mem_q[l].astype(bf16)
        wkv = w_mem_kv[l].astype(bf16)
        wo_mem = w_mem_o[l].astype(bf16)
        wgu = w_gate_up[l].astype(bf16)
        wdn = w_down[l].astype(bf16)
        half = MEM_HEADS * MEM_HEAD_DIM
        last_layer = l == depth - 1

        mk, mv, mkb, mvb = norm_matmul(mem, norm_mem_g[l], wkv,
                                       [(0, half, f32), (half, half, f32), (0, half, bf16), (half, half, bf16)],
                                       tm=mem_len)
        z, xbc, dtr, u, q, k, v = norm_matmul(xp, norm_mix_g[l], prm['w_in'], IN_PROJ_OUTS, tm=tm_p)
        y, po, s_new = ssd_pool_prompt(z, xbc, dtr, u, prm, tm=tm_p)
        att, *kv_t = moba_prompt(q, k, v, layer=l, depth=depth, kv_t_prev=kv_t)
        xp = prompt_tail(xp, [y, po, att], prm['w_out'], norm_cross_g[l], wq, mkb, mvb, wo_mem,
                         norm_ffn_g[l], wgu, wdn, final_norm_g, norm_out=last_layer, tm=tm_p)
        outs['sp'].append(s_new[None])
        outs['cp'].append(xbc[t - (CONV_W - 1):][None])
        outs['pp'].append(u[t - POOL_HIST:][None])
        outs['mk'].append(mk.reshape(1, mem_len, MEM_HEADS, MEM_HEAD_DIM))
        outs['mv'].append(mv.reshape(1, mem_len, MEM_HEADS, MEM_HEAD_DIM))

        z, xbc, dtr, u, q, k, v = norm_matmul(xs, norm_mix_g[l], prm['w_in'], IN_PROJ_OUTS, tm=tm_s)
        y, po, c_new, s_new, p_new = ssd_pool_sample(z, xbc, dtr, u, state_conv[l], state_ssm[l],
                                                     state_pool[l], prm, pos0=past_len)
        att = moba_decode(page_table, q, k, v, cache_kt, cache_vt, layer=l)
        xs = matmul_residual(xs, [y, po, att], prm['w_out'], tm=tm_s)
        (qc,) = norm_matmul(xs, norm_cross_g[l], _split_channel_order(wq, 1), [(0, d, f32)], tm=tm_s)
        oc = cross_sample(qc, mem_k_split, mem_v_split, layer=l)
        xs = matmul_residual(xs, [oc], [_split_channel_order(wo_mem, 0)], tm=tm_s)
        xs = swiglu_block(xs, norm_ffn_g[l], wgu, wdn, final_norm_g, norm_out=last_layer, tm=tm_s, tf=D_FF)
        outs['ks'].append(k.reshape(bs, 1, ATT_HEADS, ATT_HEAD_DIM))
        outs['vs'].append(v.reshape(bs, 1, ATT_HEADS, ATT_HEAD_DIM))
        outs['ss'].append(s_new)
        outs['cs'].append(c_new)
        outs['ps'].append(p_new)

    y_prompt = xp.reshape(1, t, d)
    y_sample = xs.reshape(bs, 1, d)
    st = lambda n: jnp.stack(outs[n])
    token_major = lambda a: jnp.transpose(a.reshape(depth, 1, ATT_HEADS, ATT_HEAD_DIM, t), (0, 1, 4, 2, 3))
    return (y_prompt, y_sample, token_major(kv_t[0]), token_major(kv_t[1]), st('ks'), st('vs'), st('sp'), st('ss'),
            st('cp'), st('cs'), st('pp'), st('ps'), st('mk'), st('mv'))
```

```python
import functools

import jax
import jax.numpy as jnp
from jax import lax
from jax.experimental import pallas as pl
from jax.experimental.pallas import tpu as pltpu

f32 = jnp.float32
bf16 = jnp.bfloat16
HIGHEST = lax.Precision.HIGHEST

D_MODEL = 1024
SSM_INNER = 512
SSM_HEAD_DIM = 64
SSM_HEADS = 8
SSM_GROUPS = 2
SSM_STATE = 128
CONV_W = 4
CONV_DIM = SSM_INNER + 2 * SSM_GROUPS * SSM_STATE
SSD_CHUNK = 128
POOL_DIM = 256
POOL_WINDOWS = (2, 4, 8, 16)
POOL_GROUP = 64
POOL_HIST = 15
ATT_DIM = 256
ATT_HEAD_DIM = 64
ATT_HEADS = 4
MOBA_BLOCK = 256
MOBA_TOP_K = 3
PAGE_SIZE = 128
MEM_HEADS = 4
MEM_HEAD_DIM = 256
D_FF = 2816
RMS_EPS = 1e-6
LANES = 128
CONV_HALO = 8
POOL_HALO = 16
MASKED = -1e30
LOG2E = 1.4426950408889634
BF16_SUBLANES = 16
V_AUG_ROWS = ATT_HEAD_DIM + BF16_SUBLANES
MOBA_PAIRS_PER_TRIP = 4
VMEM_LIMIT = 56 * 1024 * 1024


def _cparams(n_axes):
    return pltpu.CompilerParams(dimension_semantics=("arbitrary",) * n_axes,
                                vmem_limit_bytes=VMEM_LIMIT)


def _rms(x, g):
    ms = jnp.mean(x * x, axis=-1, keepdims=True)
    return x * lax.rsqrt(ms + RMS_EPS) * g


def _dot(a, b, **kw):
    return jnp.dot(a, b, preferred_element_type=f32, **kw)


def _dot_nt(a, b, **kw):
    return lax.dot_general(a, b, (((1,), (1,)), ((), ())), preferred_element_type=f32, **kw)


def _silu(x):
    return x * jax.nn.sigmoid(x)


def _softplus(x):
    return jnp.maximum(x, 0.0) + jnp.log1p(jnp.exp(-jnp.abs(x)))


def _full(shape):
    return pl.BlockSpec(shape, lambda *_: (0,) * len(shape))


def _norm_mm_kernel(x_ref, g_ref, w_ref, *o_refs, groups):
    hb = _rms(x_ref[...], g_ref[...]).astype(bf16)
    res = {}
    for o_ref, (start, size) in zip(o_refs, groups):
        if (start, size) not in res:
            res[start, size] = _dot(hb, w_ref[:, start:start + size])
        o_ref[...] = res[start, size].astype(o_ref.dtype)


def norm_matmul(x, g, w, outs, tm):
    m, d = x.shape
    assert all(start % LANES == 0 and size % LANES == 0 for start, size, _ in outs)
    return pl.pallas_call(
        functools.partial(_norm_mm_kernel, groups=tuple((s, n) for s, n, _ in outs)),
        grid=(m // tm,),
        in_specs=[pl.BlockSpec((tm, d), lambda i: (i, 0)), _full((1, d)), _full(w.shape)],
        out_specs=[pl.BlockSpec((tm, n), lambda i: (i, 0)) for _, n, _ in outs],
        out_shape=[jax.ShapeDtypeStruct((m, n), dt) for _, n, dt in outs],
        compiler_params=_cparams(1), name="norm_matmul",
    )(x, g.reshape(1, d), w)


def _mm_res_kernel(*refs, n_a):
    res_ref = refs[0]
    a_refs = refs[1:1 + n_a]
    w_refs = refs[1 + n_a:1 + 2 * n_a]
    o_ref = refs[-1]
    acc = res_ref[...]
    for a, w in zip(a_refs, w_refs):
        acc = acc + _dot(a[...].astype(bf16), w[...])
    o_ref[...] = acc


def matmul_residual(res, a_list, w, tm):
    m, d = res.shape
    in_specs = [pl.BlockSpec((tm, d), lambda i: (i, 0))]
    in_specs += [pl.BlockSpec((tm, a.shape[1]), lambda i: (i, 0)) for a in a_list]
    in_specs += [pl.BlockSpec((n, d), lambda i, b=b: (b, 0)) for n, b in _row_groups([a.shape[1] for a in a_list])]
    return pl.pallas_call(
        functools.partial(_mm_res_kernel, n_a=len(a_list)),
        grid=(m // tm,), in_specs=in_specs,
        out_specs=pl.BlockSpec((tm, d), lambda i: (i, 0)),
        out_shape=jax.ShapeDtypeStruct((m, d), f32),
        compiler_params=_cparams(1), name="matmul_residual",
    )(res, *a_list, *[w] * len(a_list))


def _prompt_tail_kernel(*refs, n_mix, norm_out):
    x_ref = refs[0]
    a_refs = refs[1:1 + n_mix]
    w_refs = refs[1 + n_mix:1 + 2 * n_mix]
    (g_ref, wq_ref, mk_ref, mv_ref, wo_ref,
     gf_ref, wg_ref, wu_ref, wd_ref, gout_ref, o_ref) = refs[1 + 2 * n_mix:]
    x = x_ref[...]
    for a, w in zip(a_refs, w_refs):
        x = x + _dot(a[...].astype(bf16), w[...])
    if n_mix:
        o_ref[...] = x
        x = o_ref[...]
    hb = _rms(x, g_ref[...]).astype(bf16)
    q = _dot(hb, wq_ref[...])
    acc = x
    for h in range(MEM_HEADS):
        sl = slice(h * MEM_HEAD_DIM, (h + 1) * MEM_HEAD_DIM)
        s = _dot_nt(q[:, sl].astype(bf16), mk_ref[:, sl]) * (MEM_HEAD_DIM ** -0.5)
        p = jnp.exp(s - jnp.max(s, axis=-1, keepdims=True))
        p = p / jnp.sum(p, axis=-1, keepdims=True)
        oh = _dot(p.astype(bf16), mv_ref[:, sl])
        acc = acc + _dot(oh.astype(bf16), wo_ref[sl, :])
    o_ref[...] = acc
    x2 = o_ref[...]
    hb = _rms(x2, gf_ref[...]).astype(bf16)
    act = _silu(_dot(hb, wg_ref[...])) * _dot(hb, wu_ref[...])
    y = x2 + _dot(act.astype(bf16), wd_ref[...])
    o_ref[...] = _rms(y, gout_ref[...]) if norm_out else y


def _row_groups(widths):
    out, start = [], 0
    for n in widths:
        assert start % n == 0
        out.append((n, start // n))
        start += n
    return out


def prompt_tail(x, mix_list, w_out, g, wq, mkb, mvb, wo, g_ffn, w_gu, w_d, g_out, norm_out, tm):
    m, d = x.shape
    row = lambda c: pl.BlockSpec((tm, c), lambda i: (i, 0))
    once = lambda shape, idx: pl.BlockSpec(shape, idx, pipeline_mode=pl.Buffered(1))
    resident = lambda a: once(a.shape, lambda i: (0,) * a.ndim)
    w_out_specs = [once((n, d), lambda i, b=b: (b, 0)) for n, b in _row_groups([a.shape[1] for a in mix_list])]
    consts = [g.reshape(1, d), wq, mkb, mvb, wo, g_ffn.reshape(1, d)]
    tail = [g_out.reshape(1, d)]
    return pl.pallas_call(
        functools.partial(_prompt_tail_kernel, n_mix=len(mix_list), norm_out=norm_out), grid=(m // tm,),
        in_specs=[row(d)] + [row(a.shape[1]) for a in mix_list] + w_out_specs
                 + [resident(c) for c in consts]
                 + [once((d, D_FF), lambda i: (0, 0)), once((d, D_FF), lambda i: (0, 1)), resident(w_d)]
                 + [resident(c) for c in tail],
        out_specs=row(d),
        out_shape=jax.ShapeDtypeStruct((m, d), f32),
        compiler_params=_cparams(1), name="prompt_tail",
    )(x, *mix_list, *[w_out] * len(mix_list), *consts, w_gu, w_gu, w_d, *tail)


MEM_SPLIT = MEM_HEAD_DIM // LANES
MEM_ROWS = MEM_SPLIT * MEM_HEADS


def _split_channel_order(w, axis):
    shape = w.shape
    w = w.reshape(shape[:axis] + (MEM_HEADS, MEM_SPLIT, LANES) + shape[axis + 1:])
    return jnp.swapaxes(w, axis, axis + 1).reshape(shape)


def _mem_split_view(cache):
    d, b, m = cache.shape[:3]
    x = cache.reshape(d, b, m, MEM_HEADS, MEM_SPLIT, LANES)
    return jnp.transpose(x, (0, 1, 2, 4, 3, 5)).reshape(d, b, m, MEM_ROWS, LANES)


def _cross_sample_kernel(q_ref, mk_ref, mv_ref, o_ref):
    part = jnp.sum(mk_ref[...] * q_ref[...][None], axis=-1, keepdims=True)
    s = part
    for piece in range(1, MEM_SPLIT):
        s = s + jnp.roll(part, piece * MEM_HEADS, axis=1)
    s = s * (MEM_HEAD_DIM ** -0.5)
    p = jnp.exp(s - jnp.max(s, axis=0, keepdims=True))
    l = jnp.sum(p, axis=0)
    o_ref[...] = jnp.sum(p * mv_ref[...], axis=0) / l


def cross_sample(q_split, mem_k_split, mem_v_split, layer):
    b = q_split.shape[0]
    mlen = mem_k_split.shape[2]
    mem_spec = pl.BlockSpec((None, None, mlen, MEM_ROWS, LANES), lambda i: (layer, i, 0, 0, 0))
    row_spec = pl.BlockSpec((None, MEM_ROWS, LANES), lambda i: (i, 0, 0))
    out = pl.pallas_call(
        _cross_sample_kernel, grid=(b,),
        in_specs=[row_spec, mem_spec, mem_spec], out_specs=row_spec,
        out_shape=jax.ShapeDtypeStruct((b, MEM_ROWS, LANES), f32),
        compiler_params=_cparams(1), name="cross_sample",
    )(q_split.reshape(b, MEM_ROWS, LANES), mem_k_split, mem_v_split)
    return out.reshape(b, MEM_ROWS * LANES)


def _swiglu_kernel(x_ref, g_ref, wg_ref, wu_ref, wd_ref, gout_ref, o_ref, h_ref, acc_ref, *, norm_out):
    f = pl.program_id(1)

    @pl.when(f == 0)
    def _():
        h_ref[...] = _rms(x_ref[...], g_ref[...]).astype(bf16)
        acc_ref[...] = x_ref[...]

    hb = h_ref[...]
    a = _silu(_dot(hb, wg_ref[...])) * _dot(hb, wu_ref[...])
    acc_ref[...] += _dot(a.astype(bf16), wd_ref[...])

    @pl.when(f == pl.num_programs(1) - 1)
    def _():
        y = acc_ref[...]
        o_ref[...] = _rms(y, gout_ref[...]) if norm_out else y


def swiglu_block(x, g, w_gu, w_d, g_out, norm_out, tm, tf):
    m, d = x.shape
    nf = D_FF // tf
    mode = dict(pipeline_mode=pl.Buffered(1)) if nf == 1 else {}
    return pl.pallas_call(
        functools.partial(_swiglu_kernel, norm_out=norm_out), grid=(m // tm, nf),
        in_specs=[pl.BlockSpec((tm, d), lambda i, f: (i, 0)), _full((1, d)),
                  pl.BlockSpec((d, tf), lambda i, f: (0, f), **mode),
                  pl.BlockSpec((d, tf), lambda i, f: (0, f + nf), **mode),
                  pl.BlockSpec((tf, d), lambda i, f: (f, 0), **mode), _full((1, d))],
        out_specs=pl.BlockSpec((tm, d), lambda i, f: (i, 0)),
        out_shape=jax.ShapeDtypeStruct((m, d), f32),
        scratch_shapes=[pltpu.VMEM((tm, d), bf16), pltpu.VMEM((tm, d), f32)],
        compiler_params=_cparams(2), name="swiglu_block",
    )(x, g.reshape(1, d), w_gu, w_gu, w_d, g_out.reshape(1, d))


def _pool_select(w2, w4, w8, w16, cnt_of, lane):
    out = w16 / cnt_of(16)
    for win, acc in ((8, w8), (4, w4), (2, w2)):
        g = POOL_WINDOWS.index(win)
        out = jnp.where(lane < (g + 1) * POOL_GROUP, acc / cnt_of(win), out)
    return out


def _group_rmsnorm(y, g):
    half = SSM_INNER // SSM_GROUPS
    parts = [_rms(y[:, i * half:(i + 1) * half], g[:, i * half:(i + 1) * half])
             for i in range(SSM_GROUPS)]
    return jnp.concatenate(parts, axis=-1)


def _ssd_pool_prompt_kernel(z_ref, xbc_ref, dt_ref, u_ref, cw_ref, cb_ref, dtb_ref, a_ref, dsk_ref,
                            ng_ref, pw_ref, ps_ref,
                            y_ref, po_ref, st_ref,
                            ext_ref, pext_ref, xc_ref, s_ref, *, tm):
    i = pl.program_id(0)
    halo = CONV_HALO
    phalo = POOL_HALO

    @pl.when(i == 0)
    def _():
        ext_ref[0:halo, :] = jnp.zeros((halo, CONV_DIM), f32)
        pext_ref[0:phalo, :] = jnp.zeros((phalo, POOL_DIM), f32)
        s_ref[...] = jnp.zeros(s_ref.shape, f32)

    @pl.when(i > 0)
    def _():
        ext_ref[0:halo, :] = ext_ref[tm:tm + halo, :]
        pext_ref[0:phalo, :] = pext_ref[tm:tm + phalo, :]

    ext_ref[halo:halo + tm, :] = xbc_ref[...]
    pext_ref[phalo:phalo + tm, :] = u_ref[...]

    rp, lp = 64, 256
    for r0 in range(0, tm, rp):
        for c0 in range(0, CONV_DIM, lp):
            acc = jnp.broadcast_to(cb_ref[:, c0:c0 + lp], (rp, lp))
            for j in range(CONV_W):
                off = halo - (CONV_W - 1) + j + r0
                acc = acc + ext_ref[off:off + rp, c0:c0 + lp] * cw_ref[j:j + 1, c0:c0 + lp]
            xc_ref[r0:r0 + rp, c0:c0 + lp] = _silu(acc)

    lane_p = lax.broadcasted_iota(jnp.int32, (rp, POOL_DIM), 1)
    row_p = lax.broadcasted_iota(jnp.int32, (rp, POOL_DIM), 0)
    for r0 in range(0, tm, rp):
        run = pext_ref[r0:r0 + phalo + rp, :]
        u = run[phalo:]
        sums, win = {}, 1
        while win < max(POOL_WINDOWS):
            run = run + pltpu.roll(run, win, 0)
            win *= 2
            sums[win] = run[phalo:]
        pos1 = (row_p + (i * tm + r0 + 1)).astype(f32)
        pooled = _pool_select(sums[2], sums[4], sums[8], sums[16],
                              lambda win: jnp.minimum(float(win), pos1), lane_p) - u
        po_ref[r0:r0 + rp, :] = (_dot(pooled.astype(bf16), pw_ref[...]) * ps_ref[...]).astype(po_ref.dtype)

    cl = SSD_CHUNK
    r_io = lax.broadcasted_iota(jnp.int32, (cl, cl), 0)
    c_io = lax.broadcasted_iota(jnp.int32, (cl, cl), 1)
    ltri = (c_io <= r_io).astype(f32)
    causal = c_io <= r_io
    first_half = c_io < SSM_HEAD_DIM
    n_pairs = SSM_HEADS // 2
    heads_per_group = SSM_HEADS // SSM_GROUPS

    def chunk(c, carry):
        r0 = pl.multiple_of(c * cl, cl)
        xc = xc_ref[pl.ds(r0, cl), :]
        dt = _softplus(dt_ref[pl.ds(r0, cl), :] + dtb_ref[...])
        a_cum = _dot(ltri, dt * a_ref[...], precision=HIGHEST)
        a_cum_t = a_cum.T
        dt_t = dt.T
        to_end_t = jnp.exp(a_cum_t[:, cl - 1:cl] - a_cum_t) * dt_t
        b_t = [xc[:, SSM_INNER + g * SSM_STATE:SSM_INNER + (g + 1) * SSM_STATE].T
               for g in range(SSM_GROUPS)]
        cm = [xc[:, SSM_INNER + SSM_GROUPS * SSM_STATE + g * SSM_STATE:
                 SSM_INNER + SSM_GROUPS * SSM_STATE + (g + 1) * SSM_STATE].astype(bf16)
              for g in range(SSM_GROUPS)]
        scores = [_dot(cm[g], b_t[g].astype(bf16)) for g in range(SSM_GROUPS)]
        y_pairs = []
        for k in range(n_pairs):
            g = (2 * k) // heads_per_group
            xs_pair = xc[:, k * LANES:(k + 1) * LANES]
            xs_b = xs_pair.astype(bf16)
            yd, cs, eb = [], [], []
            for h in (2 * k, 2 * k + 1):
                colb = jnp.broadcast_to(a_cum[:, h:h + 1], (cl, cl))
                seg = colb - a_cum_t[h:h + 1, :]
                decay = jnp.exp(jnp.where(causal, seg, -jnp.inf))
                mh = scores[g] * decay * dt_t[h:h + 1, :]
                yd.append(_dot(mh.astype(bf16), xs_b))
                cs.append(_dot((b_t[g] * to_end_t[h:h + 1, :]).astype(bf16), xs_b))
                eb.append(jnp.exp(colb))
            e_pair = jnp.where(first_half, eb[0], eb[1])
            s_old = s_ref[k]
            y_off = _dot(cm[g], s_old.astype(bf16)) * e_pair
            y_pairs.append(jnp.where(first_half, yd[0], yd[1]) + y_off
                           + dsk_ref[:, k * LANES:(k + 1) * LANES] * xs_pair)
            s_ref[k] = s_old * e_pair[cl - 1:cl, :] + jnp.where(first_half, cs[0], cs[1])
        y = jnp.concatenate(y_pairs, axis=-1) * _silu(z_ref[pl.ds(r0, cl), :])
        y_ref[pl.ds(r0, cl), :] = _group_rmsnorm(y, ng_ref[...]).astype(y_ref.dtype)
        return carry

    lax.fori_loop(0, tm // cl, chunk, 0, unroll=True)

    @pl.when(i == pl.num_programs(0) - 1)
    def _():
        for k in range(n_pairs):
            st = s_ref[k].T
            st_ref[2 * k] = st[0:SSM_HEAD_DIM, :]
            st_ref[2 * k + 1] = st[SSM_HEAD_DIM:2 * SSM_HEAD_DIM, :]


def ssd_pool_prompt(z, xbc, dt, u, prm, tm):
    t = z.shape[0]
    assert POOL_WINDOWS == tuple(2 ** (n + 1) for n in range(len(POOL_WINDOWS)))
    row = lambda c: pl.BlockSpec((tm, c), lambda i: (i, 0))
    consts = [prm['conv_w'], prm['conv_b'], prm['dt_bias'], prm['a'], prm['dsk'], prm['ssm_norm_g'],
              prm['pool_w'], prm['pool_scale']]
    return pl.pallas_call(
        functools.partial(_ssd_pool_prompt_kernel, tm=tm), grid=(t // tm,),
        in_specs=[row(SSM_INNER), row(CONV_DIM), row(LANES), row(POOL_DIM)] + [_full(c.shape) for c in consts],
        out_specs=[row(SSM_INNER), row(POOL_DIM), _full((SSM_HEADS, SSM_HEAD_DIM, SSM_STATE))],
        out_shape=[jax.ShapeDtypeStruct((t, SSM_INNER), bf16), jax.ShapeDtypeStruct((t, POOL_DIM), bf16),
                   jax.ShapeDtypeStruct((SSM_HEADS, SSM_HEAD_DIM, SSM_STATE), f32)],
        scratch_shapes=[pltpu.VMEM((tm + CONV_HALO, CONV_DIM), f32), pltpu.VMEM((tm + POOL_HALO, POOL_DIM), f32),
                        pltpu.VMEM((tm, CONV_DIM), f32),
                        pltpu.VMEM((SSM_HEADS // 2, SSM_STATE, 2 * SSM_HEAD_DIM), f32)],
        compiler_params=_cparams(1), name="ssd_pool_prompt",
    )(z, xbc, dt, u, *consts)


def _col_tile(row):
    return jnp.broadcast_to(row, (LANES, LANES)).T


def _ssd_pool_sample_kernel(z_ref, xbc_ref, dt_ref, u_ref, sc_ref, ss_ref, sp_ref,
                            cw_ref, cb_ref, dtb_ref, a_ref, dsk_ref, ng_ref, pw_ref, ps_ref, ex_ref,
                            y_ref, po_ref, cn_ref, sn_ref, pn_ref, *, pos0):
    xrow = xbc_ref[...]
    acc = cb_ref[...] + xrow * cw_ref[CONV_W - 1:CONV_W, :]
    for j in range(CONV_W - 1):
        acc = acc + sc_ref[j:j + 1, :] * cw_ref[j:j + 1, :]
        if j > 0:
            cn_ref[j - 1:j, :] = sc_ref[j:j + 1, :]
    cn_ref[CONV_W - 2:CONV_W - 1, :] = xrow
    xc = _silu(acc)

    dt = _softplus(dt_ref[...] + dtb_ref[...])
    dta = dt * a_ref[...]
    both = jnp.concatenate([jnp.broadcast_to(dt, (8, LANES)), jnp.broadcast_to(dta, (8, LANES))], axis=0)
    both_x = _dot(both, ex_ref[...], precision=HIGHEST)
    dtx = both_x[0:1, :]
    dec_x = jnp.exp(both_x[8:9, :])
    xs = xc[:, 0:SSM_INNER]
    xdt = xs * dtx
    heads_per_group = SSM_HEADS // SSM_GROUPS
    y_pairs = []
    for k in range(SSM_HEADS // 2):
        g = (2 * k) // heads_per_group
        b_row = xc[:, SSM_INNER + g * SSM_STATE:SSM_INNER + (g + 1) * SSM_STATE]
        c_row = xc[:, SSM_INNER + SSM_GROUPS * SSM_STATE + g * SSM_STATE:
                   SSM_INNER + SSM_GROUPS * SSM_STATE + (g + 1) * SSM_STATE]
        sl = slice(k * LANES, (k + 1) * LANES)
        s_old = jnp.concatenate([ss_ref[2 * k], ss_ref[2 * k + 1]], axis=0)
        s_new = s_old * _col_tile(dec_x[:, sl]) + _col_tile(xdt[:, sl]) * b_row
        sn_ref[2 * k] = s_new[0:SSM_HEAD_DIM, :]
        sn_ref[2 * k + 1] = s_new[SSM_HEAD_DIM:, :]
        y_k = _dot_nt(jnp.broadcast_to(c_row, (8, SSM_STATE)), s_new, precision=HIGHEST)[0:1, :]
        y_pairs.append(y_k + dsk_ref[:, sl] * xs[:, sl])
    y = jnp.concatenate(y_pairs, axis=-1) * _silu(z_ref[...])
    y_ref[...] = _group_rmsnorm(y, ng_ref[...]).astype(y_ref.dtype)

    u = u_ref[...]
    prev = sp_ref[...]
    rowi = lax.broadcasted_iota(jnp.int32, prev.shape, 0)
    tail = lambda win: u + jnp.sum(jnp.where(rowi >= POOL_HIST - (win - 1), prev, 0.0), axis=0, keepdims=True)
    lane_p = lax.broadcasted_iota(jnp.int32, (1, POOL_DIM), 1)
    pooled = _pool_select(tail(2), tail(4), tail(8), tail(16),
                          lambda win: float(min(win, pos0 + 1)), lane_p) - u
    po = _dot(jnp.broadcast_to(pooled, (8, POOL_DIM)).astype(bf16), pw_ref[...])[0:1, :] * ps_ref[...]
    po_ref[...] = po.astype(po_ref.dtype)
    pn_ref[0:POOL_HIST - 1, :] = sp_ref[1:POOL_HIST, :]
    pn_ref[POOL_HIST - 1:POOL_HIST, :] = u


SAMPLE_SEQS_PER_STEP = 2
N_SAMPLE_SEQ_INPUTS, N_SAMPLE_SEQ_OUTPUTS = 7, 5


def _ssd_pool_sample_group_kernel(*refs, pos0):
    seq_in = refs[:N_SAMPLE_SEQ_INPUTS]
    consts = refs[N_SAMPLE_SEQ_INPUTS:len(refs) - N_SAMPLE_SEQ_OUTPUTS]
    seq_out = refs[len(refs) - N_SAMPLE_SEQ_OUTPUTS:]
    for s in range(SAMPLE_SEQS_PER_STEP):
        _ssd_pool_sample_kernel(*[r.at[s] for r in seq_in], *consts, *[r.at[s] for r in seq_out], pos0=pos0)


def ssd_pool_sample(z, xbc, dt, u, st_conv, st_ssm, st_pool, prm, pos0):
    b = z.shape[0]
    grp = SAMPLE_SEQS_PER_STEP
    assert b % grp == 0
    per_seq = lambda *shape: pl.BlockSpec((grp,) + shape, lambda i: (i,) + (0,) * len(shape))
    consts = [prm['conv_w'], prm['conv_b'], prm['dt_bias'], prm['a'], prm['dsk'], prm['ssm_norm_g'],
              prm['pool_w'], prm['pool_scale'], prm['expand']]
    outs = pl.pallas_call(
        functools.partial(_ssd_pool_sample_group_kernel, pos0=pos0), grid=(b // grp,),
        in_specs=[per_seq(1, SSM_INNER), per_seq(1, CONV_DIM), per_seq(1, LANES), per_seq(1, POOL_DIM),
                  per_seq(CONV_W - 1, CONV_DIM), per_seq(SSM_HEADS, SSM_HEAD_DIM, SSM_STATE),
                  per_seq(POOL_HIST, POOL_DIM)] + [_full(c.shape) for c in consts],
        out_specs=[per_seq(1, SSM_INNER), per_seq(1, POOL_DIM), per_seq(CONV_W - 1, CONV_DIM),
                   per_seq(SSM_HEADS, SSM_HEAD_DIM, SSM_STATE), per_seq(POOL_HIST, POOL_DIM)],
        out_shape=[jax.ShapeDtypeStruct((b, 1, SSM_INNER), f32), jax.ShapeDtypeStruct((b, 1, POOL_DIM), f32),
                   jax.ShapeDtypeStruct((b, CONV_W - 1, CONV_DIM), f32),
                   jax.ShapeDtypeStruct((b, SSM_HEADS, SSM_HEAD_DIM, SSM_STATE), f32),
                   jax.ShapeDtypeStruct((b, POOL_HIST, POOL_DIM), f32)],
        compiler_params=_cparams(1), name="ssd_pool_sample",
    )(z.reshape(b, 1, -1), xbc.reshape(b, 1, -1), dt.reshape(b, 1, -1), u.reshape(b, 1, -1),
      st_conv, st_ssm, st_pool, *consts)
    y, po, cn, sn, pn = outs
    return y.reshape(b, -1), po.reshape(b, -1), cn, sn, pn


def _topk_blocks(gate, n_past, axis):
    blk = lax.broadcasted_iota(jnp.int32, gate.shape, axis).astype(f32)
    g = jnp.where(blk < jnp.asarray(n_past, f32), gate, -jnp.inf)
    picks = []
    for _ in range(MOBA_TOP_K):
        m = jnp.max(g, axis=axis, keepdims=True)
        idx = jnp.min(jnp.where(g == m, blk, float(LANES)), axis=axis, keepdims=True)
        picks.append((idx, jnp.abs(m) < jnp.inf))
        g = jnp.where(blk == idx, -jnp.inf, g)
    return picks


def _moba_prompt_kernel(*refs, n_prev):
    q_ref, k_ref, v_ref = refs[:3]
    (o_ref, kt_out_ref, vt_out_ref,
     km_ref, kb_ref, vt_ref, qst_ref, nmt_ref, acc_ref, sa_ref, sb_ref) = refs[3 + n_prev:]
    i = pl.program_id(0)
    tq = MOBA_BLOCK
    scale = ATT_HEAD_DIM ** -0.5

    @pl.when(i == 0)
    def _():
        km_ref[...] = jnp.zeros(km_ref.shape, f32)

    k = k_ref[...]
    kb = k.astype(bf16)
    head = lambda h: slice(h * ATT_HEAD_DIM, (h + 1) * ATT_HEAD_DIM)
    for h in range(ATT_HEADS):
        kb_ref[i, h] = kb[:, head(h)]
    vt_f32 = v_ref[...].T
    kt_f32 = k.T
    if n_prev:
        kt_out_ref[...] = kt_f32
        vt_out_ref[...] = vt_f32
    else:
        for slab in range(kt_out_ref.shape[0]):
            kt_out_ref[slab] = kt_f32
            vt_out_ref[slab] = vt_f32
    vt = vt_f32.astype(bf16)
    ones_rows = jnp.ones((V_AUG_ROWS - ATT_HEAD_DIM, tq), bf16)
    for h in range(ATT_HEADS):
        vt_ref[i, h] = jnp.concatenate([vt[head(h), :], ones_rows], axis=0)
    qt = q_ref[...].T
    n_slots = km_ref.shape[0]
    blk_row = lax.broadcasted_iota(jnp.int32, (n_slots, tq), 0).astype(f32)
    key_io = lax.broadcasted_iota(jnp.int32, (tq, tq), 0)
    qry_io = lax.broadcasted_iota(jnp.int32, (tq, tq), 1)
    km = km_ref[...]
    m_own = []
    for h in range(ATT_HEADS):
        qth = qt[head(h), :]
        gate = _dot(km[:, head(h)], qth, precision=HIGHEST)
        sel = jnp.zeros((n_slots, tq), jnp.bool_)
        for idx, ok in _topk_blocks(gate, i, axis=0):
            sel = sel | ((blk_row == idx) & ok)
        nmt_ref[h] = jnp.where(sel, 0.0, MASKED)
        qst = (qth * (scale * LOG2E)).astype(bf16)
        qst_ref[h] = qst
        st = jnp.where(key_io <= qry_io, _dot(kb[:, head(h)], qst), MASKED)
        m = jnp.max(st, axis=0, keepdims=True)
        m_own.append(m)
        acc_ref[h] = _dot(vt_ref[i, h], jnp.exp2(st - m).astype(bf16))

    km_ref[pl.ds(i, 1), :] = jnp.mean(k, axis=0, keepdims=True)

    def scores(j, dst):
        for h in range(ATT_HEADS):
            dst[h] = _dot(kb_ref[j, h], qst_ref[h])

    def absorb(j, src, carry):
        new = []
        for h in range(ATT_HEADS):
            m_prev = carry[h]
            raw = src[h]
            bias = nmt_ref[h, pl.ds(j, 1), :]
            m_new = jnp.maximum(m_prev, jnp.max(raw, axis=0, keepdims=True) + bias)
            alpha = jnp.exp2(m_prev - m_new)
            p = jnp.exp2(raw - (m_new - bias))
            new.append(m_new)
            acc_ref[h] = acc_ref[h] * alpha + _dot(vt_ref[j, h], p.astype(bf16))
        return tuple(new)

    last = jnp.maximum(i - 1, 0)
    scores(0, sa_ref)

    def pair(t, carry):
        scores(jnp.minimum(2 * t + 1, last), sb_ref)
        carry = absorb(2 * t, sa_ref, carry)
        scores(jnp.minimum(2 * t + 2, last), sa_ref)
        return absorb(2 * t + 1, sb_ref, carry)

    def pairs(first, count, c):
        for t in range(count):
            c = pair(first + t, c)
        return c

    per_trip = MOBA_PAIRS_PER_TRIP
    carry = lax.fori_loop(0, i // (2 * per_trip), lambda u, c: pairs(per_trip * u, per_trip, c), tuple(m_own))
    n = per_trip // 2
    while n >= 1:
        first = i // (4 * n) * (2 * n)
        carry = lax.cond(i % (4 * n) >= 2 * n, functools.partial(pairs, first, n), lambda c: c, carry)
        n //= 2
    carry = lax.cond(i % 2 == 1, lambda c: absorb(i - 1, sa_ref, c), lambda c: c, carry)
    out_t = jnp.concatenate(
        [acc_ref[h, 0:ATT_HEAD_DIM, :] / acc_ref[h, ATT_HEAD_DIM:ATT_HEAD_DIM + 1, :] for h in range(ATT_HEADS)],
        axis=0)
    o_ref[...] = out_t.T.astype(o_ref.dtype)


def moba_prompt(q, k, v, layer, depth, kv_t_prev=None):
    t = q.shape[0]
    tq = MOBA_BLOCK
    n_blk = t // tq
    n_slots = -(-n_blk // 8) * 8
    assert n_slots <= LANES
    tile = pl.BlockSpec((tq, ATT_DIM), lambda i: (i, 0))
    stack = jax.ShapeDtypeStruct((depth, ATT_DIM, t), f32)
    prev = () if kv_t_prev is None else tuple(kv_t_prev)
    assert bool(prev) == (layer > 0)
    slab = (pl.BlockSpec((None, ATT_DIM, tq), lambda i: (layer, 0, i)) if prev
            else pl.BlockSpec((depth, ATT_DIM, tq), lambda i: (0, 0, i)))
    return pl.pallas_call(
        functools.partial(_moba_prompt_kernel, n_prev=len(prev)), grid=(n_blk,),
        in_specs=[tile, tile, tile] + [pl.BlockSpec(memory_space=pl.ANY)] * len(prev),
        out_specs=[tile, slab, slab],
        out_shape=[jax.ShapeDtypeStruct((t, ATT_DIM), bf16), stack, stack],
        input_output_aliases={3 + n: 1 + n for n in range(len(prev))},
        scratch_shapes=[pltpu.VMEM((n_slots, ATT_DIM), f32),
                        pltpu.VMEM((n_blk, ATT_HEADS, tq, ATT_HEAD_DIM), bf16),
                        pltpu.VMEM((n_blk, ATT_HEADS, V_AUG_ROWS, tq), bf16),
                        pltpu.VMEM((ATT_HEADS, ATT_HEAD_DIM, tq), bf16),
                        pltpu.VMEM((ATT_HEADS, n_slots, tq), f32),
                        pltpu.VMEM((ATT_HEADS, V_AUG_ROWS, tq), f32),
                        pltpu.VMEM((ATT_HEADS, tq, tq), f32),
                        pltpu.VMEM((ATT_HEADS, tq, tq), f32)],
        compiler_params=_cparams(1), name="moba_prompt",
    )(q, k, v, *prev)


K_CHUNK_PAGES = 16
K_SLOTS = 8
PAGES_PER_BLOCK = MOBA_BLOCK // PAGE_SIZE
CHUNKS_BEFORE_FINISH = 2


def _moba_decode_kernel(pt_ref, q_ref, q8_ref, kn8_ref, vn8_ref, kc_ref, vc_ref, o_ref,
                        kbuf, vbuf, s_ref, p_ref, gate_ref, stash_ref, ids_ref, ksem, vsem, *, layer, n_pages):
    b = pl.program_id(0)
    n_seq = pl.num_programs(0) - 1
    n_chunks = n_pages // K_CHUNK_PAGES
    blocks_per_chunk = K_CHUNK_PAGES // PAGES_PER_BLOCK
    n_blocks = n_pages // PAGES_PER_BLOCK
    scale = ATT_HEAD_DIM ** -0.5

    def k_copy(seq, page_slot, slot, p):
        page = pt_ref[seq, page_slot]
        return pltpu.make_async_copy(kc_ref.at[layer, page], kbuf.at[slot, p], ksem.at[slot])

    def start_chunk(seq, c, slot):
        for p in range(K_CHUNK_PAGES):
            k_copy(seq, c * K_CHUNK_PAGES + p, slot, p).start()

    def wait_chunk(seq, c, slot):
        for p in range(K_CHUNK_PAGES):
            k_copy(seq, c * K_CHUNK_PAGES + p, slot, p).wait()

    n_picks = ATT_HEADS * MOBA_TOP_K

    def v_copy(h, r, half, page):
        return pltpu.make_async_copy(vc_ref.at[layer, page, pl.ds(h * ATT_HEAD_DIM, ATT_HEAD_DIM), :],
                                     vbuf.at[h * MOBA_TOP_K + r, half], vsem.at[0])

    def stream(chunks):
        lane8 = lax.broadcasted_iota(jnp.int32, (8, ATT_DIM), 1)
        row8 = lax.broadcasted_iota(jnp.int32, (8, ATT_DIM), 0)
        qblk_b = jnp.where(lane8 // ATT_HEAD_DIM == row8, q_ref[...], 0.0).astype(bf16)
        blk_lane = lax.broadcasted_iota(jnp.int32, (8, LANES), 1)
        gate = gate_ref[...]
        for c in chunks:
            slot = c % K_SLOTS
            ahead = c + K_SLOTS - 1
            if ahead < n_chunks:
                start_chunk(b, ahead, ahead % K_SLOTS)
            else:
                @pl.when(b + 1 < n_seq)
                def _():
                    start_chunk(b + 1, ahead - n_chunks, ahead % K_SLOTS)
            wait_chunk(b, c, slot)
            for t in range(blocks_per_chunk):
                blk = c * blocks_per_chunk + t
                halves = [_dot(qblk_b, kbuf[slot, t * PAGES_PER_BLOCK + half].astype(bf16))
                          for half in range(PAGES_PER_BLOCK)]
                for half in range(PAGES_PER_BLOCK):
                    s_ref[blk, :, half * PAGE_SIZE:(half + 1) * PAGE_SIZE] = halves[half]
                tot = jnp.sum(sum(halves), axis=-1, keepdims=True) * (1.0 / MOBA_BLOCK)
                gate = jnp.where(blk_lane == blk, tot, gate)
        gate_ref[...] = gate

    def choose_and_weigh():
        picks = _topk_blocks(gate_ref[...], n_blocks, axis=1)
        for h in range(ATT_HEADS):
            for r in range(MOBA_TOP_K):
                blk = picks[r][0][h, 0].astype(jnp.int32)
                ids_ref[h * MOBA_TOP_K + r] = blk
                for half in range(PAGES_PER_BLOCK):
                    page = pt_ref[b, blk * PAGES_PER_BLOCK + half]
                    ids_ref[n_picks + (h * MOBA_TOP_K + r) * PAGES_PER_BLOCK + half] = page
                    v_copy(h, r, half, page).start()
        blk_io = lax.broadcasted_iota(jnp.int32, (n_blocks, 8, MOBA_BLOCK), 0).astype(f32)
        seen = jnp.zeros((n_blocks, 8, MOBA_BLOCK), jnp.bool_)
        for idx, ok in picks:
            seen = seen | ((blk_io == idx[None]) & ok[None])
        s_all = jnp.where(seen, s_ref[...] * scale, -jnp.inf)
        s_own = jnp.sum(q8_ref[...] * kn8_ref[...], axis=-1, keepdims=True) * scale
        m = jnp.maximum(jnp.max(jnp.max(s_all, axis=0), axis=-1, keepdims=True), s_own)
        p_all = jnp.exp(s_all - m[None])
        p_own = jnp.exp(s_own - m)
        l = jnp.sum(jnp.sum(p_all, axis=0), axis=-1, keepdims=True) + p_own
        p_ref[...] = p_all
        stash_ref[:, 0:ATT_HEAD_DIM] = p_own * vn8_ref[...]
        stash_ref[:, ATT_HEAD_DIM:] = jnp.broadcast_to(l, (8, LANES - ATT_HEAD_DIM))

    def finish_previous():
        for h in range(ATT_HEADS):
            for r in range(MOBA_TOP_K):
                for half in range(PAGES_PER_BLOCK):
                    page = ids_ref[n_picks + (h * MOBA_TOP_K + r) * PAGES_PER_BLOCK + half]
                    v_copy(h, r, half, page).wait()
        row_hd = lax.broadcasted_iota(jnp.int32, (8, ATT_HEAD_DIM), 0)
        o = stash_ref[:, 0:ATT_HEAD_DIM]
        for h in range(ATT_HEADS):
            oh = jnp.zeros((8, ATT_HEAD_DIM), f32)
            for r in range(MOBA_TOP_K):
                pb = p_ref[ids_ref[h * MOBA_TOP_K + r]].astype(bf16)
                for half in range(PAGES_PER_BLOCK):
                    oh = oh + _dot_nt(pb[:, half * PAGE_SIZE:(half + 1) * PAGE_SIZE],
                                      vbuf[h * MOBA_TOP_K + r, half].astype(bf16))
            o = o + jnp.where(row_hd == h, oh, 0.0)
        o_ref[...] = o / stash_ref[:, ATT_HEAD_DIM:ATT_HEAD_DIM + 1]

    @pl.when(b == 0)
    def _():
        for c in range(K_SLOTS - 1):
            start_chunk(0, c, c)

    @pl.when(b < n_seq)
    def _():
        gate_ref[...] = jnp.zeros((8, LANES), f32)
        stream(range(0, CHUNKS_BEFORE_FINISH))

    @pl.when(b > 0)
    def _():
        finish_previous()

    @pl.when(b < n_seq)
    def _():
        stream(range(CHUNKS_BEFORE_FINISH, n_chunks))
        choose_and_weigh()


def moba_decode(page_table, q, k_new, v_new, cache_kt, cache_vt, layer):
    b = q.shape[0]
    n_pages = page_table.shape[1]
    n_blocks = n_pages // PAGES_PER_BLOCK
    assert n_blocks <= LANES and n_blocks >= MOBA_TOP_K
    heads8 = lambda a: jnp.pad(a.reshape(b, ATT_HEADS, ATT_HEAD_DIM), ((0, 0), (0, 8 - ATT_HEADS), (0, 0)))
    n_picks = ATT_HEADS * MOBA_TOP_K
    cur = lambda i, pt: (jnp.minimum(i, b - 1), 0, 0)
    prev = lambda i, pt: (jnp.maximum(i - 1, 0), 0, 0)
    per_head = pl.BlockSpec((None, 8, ATT_HEAD_DIM), cur)
    grid_spec = pltpu.PrefetchScalarGridSpec(
        num_scalar_prefetch=1, grid=(b + 1,),
        in_specs=[pl.BlockSpec((None, 1, ATT_DIM), cur), per_head, per_head, per_head,
                  pl.BlockSpec(memory_space=pl.ANY), pl.BlockSpec(memory_space=pl.ANY)],
        out_specs=pl.BlockSpec((None, 8, ATT_HEAD_DIM), prev),
        scratch_shapes=[pltpu.VMEM((K_SLOTS, K_CHUNK_PAGES, ATT_DIM, PAGE_SIZE), f32),
                        pltpu.VMEM((n_picks, PAGES_PER_BLOCK, ATT_HEAD_DIM, PAGE_SIZE), f32),
                        pltpu.VMEM((n_blocks, 8, MOBA_BLOCK), f32),
                        pltpu.VMEM((n_blocks, 8, MOBA_BLOCK), f32),
                        pltpu.VMEM((8, LANES), f32),
                        pltpu.VMEM((8, LANES), f32),
                        pltpu.SMEM((n_picks * (1 + PAGES_PER_BLOCK),), jnp.int32),
                        pltpu.SemaphoreType.DMA((K_SLOTS,)),
                        pltpu.SemaphoreType.DMA((1,))])
    out = pl.pallas_call(
        functools.partial(_moba_decode_kernel, layer=layer, n_pages=n_pages),
        grid_spec=grid_spec,
        out_shape=jax.ShapeDtypeStruct((b, 8, ATT_HEAD_DIM), f32),
        compiler_params=_cparams(1), name="moba_decode",
    )(page_table, q.reshape(b, 1, ATT_DIM), heads8(q), heads8(k_new), heads8(v_new), cache_kt, cache_vt)
    return out[:, :ATT_HEADS].reshape(b, ATT_DIM)


def _layer_params(l, w_in, conv_w, conv_b, dt_bias, a_log, d_skip, ssm_norm_g, pool_w, pool_scale, w_out):
    o_xbc = SSM_INNER
    o_dt = o_xbc + CONV_DIM
    o_pool = o_dt + SSM_HEADS
    o_q = o_pool + POOL_DIM
    o_k = o_q + ATT_DIM
    o_v = o_k + ATT_DIM
    wl = w_in[l]
    pad_heads = lambda v: jnp.pad(v.astype(f32), (0, LANES - SSM_HEADS)).reshape(1, LANES)
    w_cat = jnp.concatenate([wl[:, :o_pool], jnp.zeros((wl.shape[0], LANES - SSM_HEADS), wl.dtype),
                             wl[:, o_pool:]], axis=1).astype(bf16)
    pw = jnp.zeros((POOL_DIM, POOL_DIM), f32)
    for g in range(len(POOL_WINDOWS)):
        pw = pw.at[g * POOL_GROUP:(g + 1) * POOL_GROUP, g * POOL_GROUP:(g + 1) * POOL_GROUP].set(pool_w[l, g])
    expand = (jnp.arange(LANES)[:, None] == (jnp.arange(SSM_INNER)[None, :] // SSM_HEAD_DIM)).astype(f32)
    wo = w_out[l].astype(bf16)
    return {
        'w_in': w_cat,
        'conv_w': conv_w[l], 'conv_b': conv_b[l].reshape(1, CONV_DIM),
        'dt_bias': pad_heads(dt_bias[l]), 'a': pad_heads(-jnp.exp(a_log[l].astype(f32))),
        'dsk': jnp.repeat(d_skip[l].astype(f32), SSM_HEAD_DIM).reshape(1, SSM_INNER),
        'ssm_norm_g': ssm_norm_g[l].reshape(1, SSM_INNER),
        'pool_w': pw.astype(bf16), 'pool_scale': pool_scale[l].reshape(1, POOL_DIM),
        'expand': expand,
        'w_out': wo,
    }


def _in_proj_outs():
    sizes = [SSM_INNER, CONV_DIM, LANES, POOL_DIM, ATT_DIM, ATT_DIM, ATT_DIM]
    starts = [sum(sizes[:i]) for i in range(len(sizes))]
    return [(s, n, f32) for s, n in zip(starts, sizes)]


IN_PROJ_OUTS = _in_proj_outs()


def _pages_channel_major(cache):
    d, n_phys = cache.shape[:2]
    return jnp.transpose(cache, (0, 1, 3, 4, 2)).reshape(d, n_phys, ATT_DIM, PAGE_SIZE)


def kernel(x_prompt, x_sample, cache_moba_k, cache_moba_v, state_ssm, state_conv, state_pool, cache_mem_k, cache_mem_v, page_table, mem_prompt, norm_mix_g, w_in, conv_w, conv_b, dt_bias, a_log, d_skip, ssm_norm_g, pool_w, pool_scale, w_out, norm_cross_g, norm_mem_g, w_mem_q, w_mem_kv, w_mem_o, norm_ffn_g, w_gate_up, w_down, final_norm_g):
    depth = w_in.shape[0]
    bp, t, d = x_prompt.shape
    bs = x_sample.shape[0]
    assert bp == 1 and x_sample.shape[1] == 1
    past_len = page_table.shape[1] * PAGE_SIZE
    cache_kt = _pages_channel_major(cache_moba_k)
    cache_vt = _pages_channel_major(cache_moba_v)
    mem_k_split = _mem_split_view(cache_mem_k)
    mem_v_split = _mem_split_view(cache_mem_v)
    mem_len = mem_prompt.shape[1]
    xp = x_prompt.reshape(t, d)
    xs = x_sample.reshape(bs, d)
    mem = mem_prompt.reshape(mem_len, d)
    tm_p, tm_s = 512, bs
    outs = {n: [] for n in ('ks', 'vs', 'sp', 'ss', 'cp', 'cs', 'pp', 'ps', 'mk', 'mv')}
    kv_t = None
    for l in range(depth):
        prm = _layer_params(l, w_in, conv_w, conv_b, dt_bias, a_log, d_skip, ssm_norm_g, pool_w, pool_scale, w_out)
        wq = w_mem_q[l].astype(bf16)
        wkv = w_mem_kv[l].astype(bf16)
        wo_mem = w_mem_o[l].astype(bf16)
        wgu = w_gate_up[l].astype(bf16)
        wdn = w_down[l].astype(bf16)
        half = MEM_HEADS * MEM_HEAD_DIM
        last_layer = l == depth - 1

        mk, mv, mkb, mvb = norm_matmul(mem, norm_mem_g[l], wkv,
                                       [(0, half, f32), (half, half, f32), (0, half, bf16), (half, half, bf16)],
                                       tm=mem_len)
        z, xbc, dtr, u, q, k, v = norm_matmul(xp, norm_mix_g[l], prm['w_in'], IN_PROJ_OUTS, tm=tm_p)
        y, po, s_new = ssd_pool_prompt(z, xbc, dtr, u, prm, tm=tm_p)
        att, *kv_t = moba_prompt(q, k, v, layer=l, depth=depth, kv_t_prev=kv_t)
        xp = prompt_tail(xp, [y, po, att], prm['w_out'], norm_cross_g[l], wq, mkb, mvb, wo_mem,
                         norm_ffn_g[l], wgu, wdn, final_norm_g, norm_out=last_layer, tm=tm_p)
        outs['sp'].append(s_new[None])
        outs['cp'].append(xbc[t - (CONV_W - 1):][None])
        outs['pp'].append(u[t - POOL_HIST:][None])
        outs['mk'].append(mk.reshape(1, mem_len, MEM_HEADS, MEM_HEAD_DIM))
        outs['mv'].append(mv.reshape(1, mem_len, MEM_HEADS, MEM_HEAD_DIM))

        z, xbc, dtr, u, q, k, v = norm_matmul(xs, norm_mix_g[l], prm['w_in'], IN_PROJ_OUTS, tm=tm_s)
        y, po, c_new, s_new, p_new = ssd_pool_sample(z, xbc, dtr, u, state_conv[l], state_ssm[l],
                                                     state_pool[l], prm, pos0=past_len)
        att = moba_decode(page_table, q, k, v, cache_kt, cache_vt, layer=l)
        xs = matmul_residual(xs, [y, po, att], prm['w_out'], tm=tm_s)
        (qc,) = norm_matmul(xs, norm_cross_g[l], _split_channel_order(wq, 1), [(0, d, f32)], tm=tm_s)
        oc = cross_sample(qc, mem_k_split, mem_v_split, layer=l)
        xs = matmul_residual(xs, [oc], _split_channel_order(wo_mem, 0), tm=tm_s)
        xs = swiglu_block(xs, norm_ffn_g[l], wgu, wdn, final_norm_g, norm_out=last_layer, tm=tm_s, tf=D_FF)
        outs['ks'].append(k.reshape(bs, 1, ATT_HEADS, ATT_HEAD_DIM))
        outs['vs'].append(v.reshape(bs, 1, ATT_HEADS, ATT_HEAD_DIM))
        outs['ss'].append(s_new)
        outs['cs'].append(c_new)
        outs['ps'].append(p_new)

    y_prompt = xp.reshape(1, t, d)
    y_sample = xs.reshape(bs, 1, d)
    st = lambda n: jnp.stack(outs[n])
    token_major = lambda a: jnp.transpose(a.reshape(depth, 1, ATT_HEADS, ATT_HEAD_DIM, t), (0, 1, 4, 2, 3))
    return (y_prompt, y_sample, token_major(kv_t[0]), token_major(kv_t[1]), st('ks'), st('vs'), st('sp'), st('ss'),
            st('cp'), st('cs'), st('pp'), st('ps'), st('mk'), st('mv'))
```

```python
import functools

import jax
import jax.numpy as jnp
from jax import lax
from jax.experimental import pallas as pl
from jax.experimental.pallas import tpu as pltpu

f32 = jnp.float32
bf16 = jnp.bfloat16
HIGHEST = lax.Precision.HIGHEST

D_MODEL = 1024
SSM_INNER = 512
SSM_HEAD_DIM = 64
SSM_HEADS = 8
SSM_GROUPS = 2
SSM_STATE = 128
CONV_W = 4
CONV_DIM = SSM_INNER + 2 * SSM_GROUPS * SSM_STATE
SSD_CHUNK = 128
POOL_DIM = 256
POOL_WINDOWS = (2, 4, 8, 16)
POOL_GROUP = 64
POOL_HIST = 15
ATT_DIM = 256
ATT_HEAD_DIM = 64
ATT_HEADS = 4
MOBA_BLOCK = 256
MOBA_TOP_K = 3
PAGE_SIZE = 128
MEM_HEADS = 4
MEM_HEAD_DIM = 256
D_FF = 2816
RMS_EPS = 1e-6
LANES = 128
CONV_HALO = 8
POOL_HALO = 16
MASKED = -1e30
LOG2E = 1.4426950408889634
BF16_SUBLANES = 16
V_AUG_ROWS = ATT_HEAD_DIM + BF16_SUBLANES
MOBA_PAIRS_PER_TRIP = 8
VMEM_LIMIT = 56 * 1024 * 1024


def _cparams(n_axes):
    return pltpu.CompilerParams(dimension_semantics=("arbitrary",) * n_axes,
                                vmem_limit_bytes=VMEM_LIMIT)


def _rms(x, g):
    ms = jnp.mean(x * x, axis=-1, keepdims=True)
    return x * lax.rsqrt(ms + RMS_EPS) * g


def _dot(a, b, **kw):
    return jnp.dot(a, b, preferred_element_type=f32, **kw)


def _dot_nt(a, b, **kw):
    return lax.dot_general(a, b, (((1,), (1,)), ((), ())), preferred_element_type=f32, **kw)


def _silu(x):
    return x * jax.nn.sigmoid(x)


def _softplus(x):
    return jnp.maximum(x, 0.0) + jnp.log1p(jnp.exp(-jnp.abs(x)))


def _full(shape):
    return pl.BlockSpec(shape, lambda *_: (0,) * len(shape))


def _norm_mm_kernel(x_ref, g_ref, w_ref, *o_refs, groups):
    hb = _rms(x_ref[...], g_ref[...]).astype(bf16)
    res = {}
    for o_ref, (start, size) in zip(o_refs, groups):
        if (start, size) not in res:
            res[start, size] = _dot(hb, w_ref[:, start:start + size])
        o_ref[...] = res[start, size].astype(o_ref.dtype)


def norm_matmul(x, g, w, outs, tm):
    m, d = x.shape
    assert all(start % LANES == 0 and size % LANES == 0 for start, size, _ in outs)
    return pl.pallas_call(
        functools.partial(_norm_mm_kernel, groups=tuple((s, n) for s, n, _ in outs)),
        grid=(m // tm,),
        in_specs=[pl.BlockSpec((tm, d), lambda i: (i, 0)), _full((1, d)), _full(w.shape)],
        out_specs=[pl.BlockSpec((tm, n), lambda i: (i, 0)) for _, n, _ in outs],
        out_shape=[jax.ShapeDtypeStruct((m, n), dt) for _, n, dt in outs],
        compiler_params=_cparams(1), name="norm_matmul",
    )(x, g.reshape(1, d), w)


def _mm_res_kernel(*refs, n_a):
    res_ref = refs[0]
    a_refs = refs[1:1 + n_a]
    w_refs = refs[1 + n_a:1 + 2 * n_a]
    o_ref = refs[-1]
    acc = res_ref[...]
    for a, w in zip(a_refs, w_refs):
        acc = acc + _dot(a[...].astype(bf16), w[...])
    o_ref[...] = acc


def matmul_residual(res, a_list, w, tm):
    m, d = res.shape
    in_specs = [pl.BlockSpec((tm, d), lambda i: (i, 0))]
    in_specs += [pl.BlockSpec((tm, a.shape[1]), lambda i: (i, 0)) for a in a_list]
    in_specs += [pl.BlockSpec((n, d), lambda i, b=b: (b, 0)) for n, b in _row_groups([a.shape[1] for a in a_list])]
    return pl.pallas_call(
        functools.partial(_mm_res_kernel, n_a=len(a_list)),
        grid=(m // tm,), in_specs=in_specs,
        out_specs=pl.BlockSpec((tm, d), lambda i: (i, 0)),
        out_shape=jax.ShapeDtypeStruct((m, d), f32),
        compiler_params=_cparams(1), name="matmul_residual",
    )(res, *a_list, *[w] * len(a_list))


def _prompt_tail_kernel(*refs, n_mix, norm_out):
    x_ref = refs[0]
    a_refs = refs[1:1 + n_mix]
    w_refs = refs[1 + n_mix:1 + 2 * n_mix]
    (g_ref, wq_ref, mk_ref, mv_ref, wo_ref,
     gf_ref, wg_ref, wu_ref, wd_ref, gout_ref, o_ref) = refs[1 + 2 * n_mix:]
    x = x_ref[...]
    for a, w in zip(a_refs, w_refs):
        x = x + _dot(a[...].astype(bf16), w[...])
    if n_mix:
        o_ref[...] = x
        x = o_ref[...]
    hb = _rms(x, g_ref[...]).astype(bf16)
    q = _dot(hb, wq_ref[...])
    acc = x
    for h in range(MEM_HEADS):
        sl = slice(h * MEM_HEAD_DIM, (h + 1) * MEM_HEAD_DIM)
        s = _dot_nt(q[:, sl].astype(bf16), mk_ref[:, sl]) * (MEM_HEAD_DIM ** -0.5)
        p = jnp.exp(s - jnp.max(s, axis=-1, keepdims=True))
        p = p / jnp.sum(p, axis=-1, keepdims=True)
        oh = _dot(p.astype(bf16), mv_ref[:, sl])
        acc = acc + _dot(oh.astype(bf16), wo_ref[sl, :])
    o_ref[...] = acc
    x2 = o_ref[...]
    hb = _rms(x2, gf_ref[...]).astype(bf16)
    act = _silu(_dot(hb, wg_ref[...])) * _dot(hb, wu_ref[...])
    y = x2 + _dot(act.astype(bf16), wd_ref[...])
    o_ref[...] = _rms(y, gout_ref[...]) if norm_out else y


def _row_groups(widths):
    out, start = [], 0
    for n in widths:
        assert start % n == 0
        out.append((n, start // n))
        start += n
    return out


def prompt_tail(x, mix_list, w_out, g, wq, mkb, mvb, wo, g_ffn, w_gu, w_d, g_out, norm_out, tm):
    m, d = x.shape
    row = lambda c: pl.BlockSpec((tm, c), lambda i: (i, 0))
    once = lambda shape, idx: pl.BlockSpec(shape, idx, pipeline_mode=pl.Buffered(1))
    resident = lambda a: once(a.shape, lambda i: (0,) * a.ndim)
    w_out_specs = [once((n, d), lambda i, b=b: (b, 0)) for n, b in _row_groups([a.shape[1] for a in mix_list])]
    consts = [g.reshape(1, d), wq, mkb, mvb, wo, g_ffn.reshape(1, d)]
    tail = [g_out.reshape(1, d)]
    return pl.pallas_call(
        functools.partial(_prompt_tail_kernel, n_mix=len(mix_list), norm_out=norm_out), grid=(m // tm,),
        in_specs=[row(d)] + [row(a.shape[1]) for a in mix_list] + w_out_specs
                 + [resident(c) for c in consts]
                 + [once((d, D_FF), lambda i: (0, 0)), once((d, D_FF), lambda i: (0, 1)), resident(w_d)]
                 + [resident(c) for c in tail],
        out_specs=row(d),
        out_shape=jax.ShapeDtypeStruct((m, d), f32),
        compiler_params=_cparams(1), name="prompt_tail",
    )(x, *mix_list, *[w_out] * len(mix_list), *consts, w_gu, w_gu, w_d, *tail)


MEM_SPLIT = MEM_HEAD_DIM // LANES
MEM_ROWS = MEM_SPLIT * MEM_HEADS


def _split_channel_order(w, axis):
    shape = w.shape
    w = w.reshape(shape[:axis] + (MEM_HEADS, MEM_SPLIT, LANES) + shape[axis + 1:])
    return jnp.swapaxes(w, axis, axis + 1).reshape(shape)


def _mem_split_view(cache):
    d, b, m = cache.shape[:3]
    x = cache.reshape(d, b, m, MEM_HEADS, MEM_SPLIT, LANES)
    return jnp.transpose(x, (0, 1, 2, 4, 3, 5)).reshape(d, b, m, MEM_ROWS, LANES)


def _cross_sample_kernel(q_ref, mk_ref, mv_ref, o_ref):
    part = jnp.sum(mk_ref[...] * q_ref[...][None], axis=-1, keepdims=True)
    s = part
    for piece in range(1, MEM_SPLIT):
        s = s + jnp.roll(part, piece * MEM_HEADS, axis=1)
    s = s * (MEM_HEAD_DIM ** -0.5)
    p = jnp.exp(s - jnp.max(s, axis=0, keepdims=True))
    l = jnp.sum(p, axis=0)
    o_ref[...] = jnp.sum(p * mv_ref[...], axis=0) / l


def cross_sample(q_split, mem_k_split, mem_v_split, layer):
    b = q_split.shape[0]
    mlen = mem_k_split.shape[2]
    mem_spec = pl.BlockSpec((None, None, mlen, MEM_ROWS, LANES), lambda i: (layer, i, 0, 0, 0))
    row_spec = pl.BlockSpec((None, MEM_ROWS, LANES), lambda i: (i, 0, 0))
    out = pl.pallas_call(
        _cross_sample_kernel, grid=(b,),
        in_specs=[row_spec, mem_spec, mem_spec], out_specs=row_spec,
        out_shape=jax.ShapeDtypeStruct((b, MEM_ROWS, LANES), f32),
        compiler_params=_cparams(1), name="cross_sample",
    )(q_split.reshape(b, MEM_ROWS, LANES), mem_k_split, mem_v_split)
    return out.reshape(b, MEM_ROWS * LANES)


def _swiglu_kernel(x_ref, g_ref, wg_ref, wu_ref, wd_ref, gout_ref, o_ref, h_ref, acc_ref, *, norm_out):
    f = pl.program_id(1)

    @pl.when(f == 0)
    def _():
        h_ref[...] = _rms(x_ref[...], g_ref[...]).astype(bf16)
        acc_ref[...] = x_ref[...]

    hb = h_ref[...]
    a = _silu(_dot(hb, wg_ref[...])) * _dot(hb, wu_ref[...])
    acc_ref[...] += _dot(a.astype(bf16), wd_ref[...])

    @pl.when(f == pl.num_programs(1) - 1)
    def _():
        y = acc_ref[...]
        o_ref[...] = _rms(y, gout_ref[...]) if norm_out else y


def swiglu_block(x, g, w_gu, w_d, g_out, norm_out, tm, tf):
    m, d = x.shape
    nf = D_FF // tf
    mode = dict(pipeline_mode=pl.Buffered(1)) if nf == 1 else {}
    return pl.pallas_call(
        functools.partial(_swiglu_kernel, norm_out=norm_out), grid=(m // tm, nf),
        in_specs=[pl.BlockSpec((tm, d), lambda i, f: (i, 0)), _full((1, d)),
                  pl.BlockSpec((d, tf), lambda i, f: (0, f), **mode),
                  pl.BlockSpec((d, tf), lambda i, f: (0, f + nf), **mode),
                  pl.BlockSpec((tf, d), lambda i, f: (f, 0), **mode), _full((1, d))],
        out_specs=pl.BlockSpec((tm, d), lambda i, f: (i, 0)),
        out_shape=jax.ShapeDtypeStruct((m, d), f32),
        scratch_shapes=[pltpu.VMEM((tm, d), bf16), pltpu.VMEM((tm, d), f32)],
        compiler_params=_cparams(2), name="swiglu_block",
    )(x, g.reshape(1, d), w_gu, w_gu, w_d, g_out.reshape(1, d))


def _pool_select(w2, w4, w8, w16, cnt_of, lane):
    out = w16 / cnt_of(16)
    for win, acc in ((8, w8), (4, w4), (2, w2)):
        g = POOL_WINDOWS.index(win)
        out = jnp.where(lane < (g + 1) * POOL_GROUP, acc / cnt_of(win), out)
    return out


def _group_rmsnorm(y, g):
    half = SSM_INNER // SSM_GROUPS
    parts = [_rms(y[:, i * half:(i + 1) * half], g[:, i * half:(i + 1) * half])
             for i in range(SSM_GROUPS)]
    return jnp.concatenate(parts, axis=-1)


def _ssd_pool_prompt_kernel(z_ref, xbc_ref, dt_ref, u_ref, cw_ref, cb_ref, dtb_ref, a_ref, dsk_ref,
                            ng_ref, pw_ref, ps_ref,
                            y_ref, po_ref, st_ref,
                            ext_ref, pext_ref, xc_ref, s_ref, *, tm):
    i = pl.program_id(0)
    halo = CONV_HALO
    phalo = POOL_HALO

    @pl.when(i == 0)
    def _():
        ext_ref[0:halo, :] = jnp.zeros((halo, CONV_DIM), f32)
        pext_ref[0:phalo, :] = jnp.zeros((phalo, POOL_DIM), f32)
        s_ref[...] = jnp.zeros(s_ref.shape, f32)

    @pl.when(i > 0)
    def _():
        ext_ref[0:halo, :] = ext_ref[tm:tm + halo, :]
        pext_ref[0:phalo, :] = pext_ref[tm:tm + phalo, :]

    ext_ref[halo:halo + tm, :] = xbc_ref[...]
    pext_ref[phalo:phalo + tm, :] = u_ref[...]

    rp, lp = 64, 256
    for r0 in range(0, tm, rp):
        for c0 in range(0, CONV_DIM, lp):
            acc = jnp.broadcast_to(cb_ref[:, c0:c0 + lp], (rp, lp))
            for j in range(CONV_W):
                off = halo - (CONV_W - 1) + j + r0
                acc = acc + ext_ref[off:off + rp, c0:c0 + lp] * cw_ref[j:j + 1, c0:c0 + lp]
            xc_ref[r0:r0 + rp, c0:c0 + lp] = _silu(acc)

    lane_p = lax.broadcasted_iota(jnp.int32, (rp, POOL_DIM), 1)
    row_p = lax.broadcasted_iota(jnp.int32, (rp, POOL_DIM), 0)
    for r0 in range(0, tm, rp):
        run = pext_ref[r0:r0 + phalo + rp, :]
        u = run[phalo:]
        sums, win = {}, 1
        while win < max(POOL_WINDOWS):
            run = run + pltpu.roll(run, win, 0)
            win *= 2
            sums[win] = run[phalo:]
        pos1 = (row_p + (i * tm + r0 + 1)).astype(f32)
        pooled = _pool_select(sums[2], sums[4], sums[8], sums[16],
                              lambda win: jnp.minimum(float(win), pos1), lane_p) - u
        po_ref[r0:r0 + rp, :] = (_dot(pooled.astype(bf16), pw_ref[...]) * ps_ref[...]).astype(po_ref.dtype)

    cl = SSD_CHUNK
    r_io = lax.broadcasted_iota(jnp.int32, (cl, cl), 0)
    c_io = lax.broadcasted_iota(jnp.int32, (cl, cl), 1)
    ltri = (c_io <= r_io).astype(f32)
    causal = c_io <= r_io
    first_half = c_io < SSM_HEAD_DIM
    n_pairs = SSM_HEADS // 2
    heads_per_group = SSM_HEADS // SSM_GROUPS

    def chunk(c, carry):
        r0 = pl.multiple_of(c * cl, cl)
        xc = xc_ref[pl.ds(r0, cl), :]
        dt = _softplus(dt_ref[pl.ds(r0, cl), :] + dtb_ref[...])
        a_cum = _dot(ltri, dt * a_ref[...], precision=HIGHEST)
        a_cum_t = a_cum.T
        dt_t = dt.T
        to_end_t = jnp.exp(a_cum_t[:, cl - 1:cl] - a_cum_t) * dt_t
        b_t = [xc[:, SSM_INNER + g * SSM_STATE:SSM_INNER + (g + 1) * SSM_STATE].T
               for g in range(SSM_GROUPS)]
        cm = [xc[:, SSM_INNER + SSM_GROUPS * SSM_STATE + g * SSM_STATE:
                 SSM_INNER + SSM_GROUPS * SSM_STATE + (g + 1) * SSM_STATE].astype(bf16)
              for g in range(SSM_GROUPS)]
        scores = [_dot(cm[g], b_t[g].astype(bf16)) for g in range(SSM_GROUPS)]
        y_pairs = []
        for k in range(n_pairs):
            g = (2 * k) // heads_per_group
            xs_pair = xc[:, k * LANES:(k + 1) * LANES]
            xs_b = xs_pair.astype(bf16)
            yd, cs, eb = [], [], []
            for h in (2 * k, 2 * k + 1):
                colb = jnp.broadcast_to(a_cum[:, h:h + 1], (cl, cl))
                seg = colb - a_cum_t[h:h + 1, :]
                decay = jnp.exp(jnp.where(causal, seg, -jnp.inf))
                mh = scores[g] * decay * dt_t[h:h + 1, :]
                yd.append(_dot(mh.astype(bf16), xs_b))
                cs.append(_dot((b_t[g] * to_end_t[h:h + 1, :]).astype(bf16), xs_b))
                eb.append(jnp.exp(colb))
            e_pair = jnp.where(first_half, eb[0], eb[1])
            s_old = s_ref[k]
            y_off = _dot(cm[g], s_old.astype(bf16)) * e_pair
            y_pairs.append(jnp.where(first_half, yd[0], yd[1]) + y_off
                           + dsk_ref[:, k * LANES:(k + 1) * LANES] * xs_pair)
            s_ref[k] = s_old * e_pair[cl - 1:cl, :] + jnp.where(first_half, cs[0], cs[1])
        y = jnp.concatenate(y_pairs, axis=-1) * _silu(z_ref[pl.ds(r0, cl), :])
        y_ref[pl.ds(r0, cl), :] = _group_rmsnorm(y, ng_ref[...]).astype(y_ref.dtype)
        return carry

    lax.fori_loop(0, tm // cl, chunk, 0, unroll=True)

    @pl.when(i == pl.num_programs(0) - 1)
    def _():
        for k in range(n_pairs):
            st = s_ref[k].T
            st_ref[2 * k] = st[0:SSM_HEAD_DIM, :]
            st_ref[2 * k + 1] = st[SSM_HEAD_DIM:2 * SSM_HEAD_DIM, :]


def ssd_pool_prompt(z, xbc, dt, u, prm, tm):
    t = z.shape[0]
    assert POOL_WINDOWS == tuple(2 ** (n + 1) for n in range(len(POOL_WINDOWS)))
    row = lambda c: pl.BlockSpec((tm, c), lambda i: (i, 0))
    consts = [prm['conv_w'], prm['conv_b'], prm['dt_bias'], prm['a'], prm['dsk'], prm['ssm_norm_g'],
              prm['pool_w'], prm['pool_scale']]
    return pl.pallas_call(
        functools.partial(_ssd_pool_prompt_kernel, tm=tm), grid=(t // tm,),
        in_specs=[row(SSM_INNER), row(CONV_DIM), row(LANES), row(POOL_DIM)] + [_full(c.shape) for c in consts],
        out_specs=[row(SSM_INNER), row(POOL_DIM), _full((SSM_HEADS, SSM_HEAD_DIM, SSM_STATE))],
        out_shape=[jax.ShapeDtypeStruct((t, SSM_INNER), bf16), jax.ShapeDtypeStruct((t, POOL_DIM), bf16),
                   jax.ShapeDtypeStruct((SSM_HEADS, SSM_HEAD_DIM, SSM_STATE), f32)],
        scratch_shapes=[pltpu.VMEM((tm + CONV_HALO, CONV_DIM), f32), pltpu.VMEM((tm + POOL_HALO, POOL_DIM), f32),
                        pltpu.VMEM((tm, CONV_DIM), f32),
                        pltpu.VMEM((SSM_HEADS // 2, SSM_STATE, 2 * SSM_HEAD_DIM), f32)],
        compiler_params=_cparams(1), name="ssd_pool_prompt",
    )(z, xbc, dt, u, *consts)


def _col_tile(row):
    return jnp.broadcast_to(row, (LANES, LANES)).T


def _ssd_pool_sample_kernel(z_ref, xbc_ref, dt_ref, u_ref, sc_ref, ss_ref, sp_ref,
                            cw_ref, cb_ref, dtb_ref, a_ref, dsk_ref, ng_ref, pw_ref, ps_ref, ex_ref,
                            y_ref, po_ref, cn_ref, sn_ref, pn_ref, *, pos0):
    xrow = xbc_ref[...]
    acc = cb_ref[...] + xrow * cw_ref[CONV_W - 1:CONV_W, :]
    for j in range(CONV_W - 1):
        acc = acc + sc_ref[j:j + 1, :] * cw_ref[j:j + 1, :]
        if j > 0:
            cn_ref[j - 1:j, :] = sc_ref[j:j + 1, :]
    cn_ref[CONV_W - 2:CONV_W - 1, :] = xrow
    xc = _silu(acc)

    dt = _softplus(dt_ref[...] + dtb_ref[...])
    dta = dt * a_ref[...]
    both = jnp.concatenate([jnp.broadcast_to(dt, (8, LANES)), jnp.broadcast_to(dta, (8, LANES))], axis=0)
    both_x = _dot(both, ex_ref[...], precision=HIGHEST)
    dtx = both_x[0:1, :]
    dec_x = jnp.exp(both_x[8:9, :])
    xs = xc[:, 0:SSM_INNER]
    xdt = xs * dtx
    heads_per_group = SSM_HEADS // SSM_GROUPS
    y_pairs = []
    for k in range(SSM_HEADS // 2):
        g = (2 * k) // heads_per_group
        b_row = xc[:, SSM_INNER + g * SSM_STATE:SSM_INNER + (g + 1) * SSM_STATE]
        c_row = xc[:, SSM_INNER + SSM_GROUPS * SSM_STATE + g * SSM_STATE:
                   SSM_INNER + SSM_GROUPS * SSM_STATE + (g + 1) * SSM_STATE]
        sl = slice(k * LANES, (k + 1) * LANES)
        s_old = jnp.concatenate([ss_ref[2 * k], ss_ref[2 * k + 1]], axis=0)
        s_new = s_old * _col_tile(dec_x[:, sl]) + _col_tile(xdt[:, sl]) * b_row
        sn_ref[2 * k] = s_new[0:SSM_HEAD_DIM, :]
        sn_ref[2 * k + 1] = s_new[SSM_HEAD_DIM:, :]
        y_k = _dot_nt(jnp.broadcast_to(c_row, (8, SSM_STATE)), s_new, precision=HIGHEST)[0:1, :]
        y_pairs.append(y_k + dsk_ref[:, sl] * xs[:, sl])
    y = jnp.concatenate(y_pairs, axis=-1) * _silu(z_ref[...])
    y_ref[...] = _group_rmsnorm(y, ng_ref[...]).astype(y_ref.dtype)

    u = u_ref[...]
    prev = sp_ref[...]
    rowi = lax.broadcasted_iota(jnp.int32, prev.shape, 0)
    tail = lambda win: u + jnp.sum(jnp.where(rowi >= POOL_HIST - (win - 1), prev, 0.0), axis=0, keepdims=True)
    lane_p = lax.broadcasted_iota(jnp.int32, (1, POOL_DIM), 1)
    pooled = _pool_select(tail(2), tail(4), tail(8), tail(16),
                          lambda win: float(min(win, pos0 + 1)), lane_p) - u
    po = _dot(jnp.broadcast_to(pooled, (8, POOL_DIM)).astype(bf16), pw_ref[...])[0:1, :] * ps_ref[...]
    po_ref[...] = po.astype(po_ref.dtype)
    pn_ref[0:POOL_HIST - 1, :] = sp_ref[1:POOL_HIST, :]
    pn_ref[POOL_HIST - 1:POOL_HIST, :] = u


SAMPLE_SEQS_PER_STEP = 2
N_SAMPLE_SEQ_INPUTS, N_SAMPLE_SEQ_OUTPUTS = 7, 5


def _ssd_pool_sample_group_kernel(*refs, pos0):
    seq_in = refs[:N_SAMPLE_SEQ_INPUTS]
    consts = refs[N_SAMPLE_SEQ_INPUTS:len(refs) - N_SAMPLE_SEQ_OUTPUTS]
    seq_out = refs[len(refs) - N_SAMPLE_SEQ_OUTPUTS:]
    for s in range(SAMPLE_SEQS_PER_STEP):
        _ssd_pool_sample_kernel(*[r.at[s] for r in seq_in], *consts, *[r.at[s] for r in seq_out], pos0=pos0)


def ssd_pool_sample(z, xbc, dt, u, st_conv, st_ssm, st_pool, prm, pos0):
    b = z.shape[0]
    grp = SAMPLE_SEQS_PER_STEP
    assert b % grp == 0
    per_seq = lambda *shape: pl.BlockSpec((grp,) + shape, lambda i: (i,) + (0,) * len(shape))
    consts = [prm['conv_w'], prm['conv_b'], prm['dt_bias'], prm['a'], prm['dsk'], prm['ssm_norm_g'],
              prm['pool_w'], prm['pool_scale'], prm['expand']]
    outs = pl.pallas_call(
        functools.partial(_ssd_pool_sample_group_kernel, pos0=pos0), grid=(b // grp,),
        in_specs=[per_seq(1, SSM_INNER), per_seq(1, CONV_DIM), per_seq(1, LANES), per_seq(1, POOL_DIM),
                  per_seq(CONV_W - 1, CONV_DIM), per_seq(SSM_HEADS, SSM_HEAD_DIM, SSM_STATE),
                  per_seq(POOL_HIST, POOL_DIM)] + [_full(c.shape) for c in consts],
        out_specs=[per_seq(1, SSM_INNER), per_seq(1, POOL_DIM), per_seq(CONV_W - 1, CONV_DIM),
                   per_seq(SSM_HEADS, SSM_HEAD_DIM, SSM_STATE), per_seq(POOL_HIST, POOL_DIM)],
        out_shape=[jax.ShapeDtypeStruct((b, 1, SSM_INNER), f32), jax.ShapeDtypeStruct((b, 1, POOL_DIM), f32),
                   jax.ShapeDtypeStruct((b, CONV_W - 1, CONV_DIM), f32),
                   jax.ShapeDtypeStruct((b, SSM_HEADS, SSM_HEAD_DIM, SSM_STATE), f32),
                   jax.ShapeDtypeStruct((b, POOL_HIST, POOL_DIM), f32)],
        compiler_params=_cparams(1), name="ssd_pool_sample",
    )(z.reshape(b, 1, -1), xbc.reshape(b, 1, -1), dt.reshape(b, 1, -1), u.reshape(b, 1, -1),
      st_conv, st_ssm, st_pool, *consts)
    y, po, cn, sn, pn = outs
    return y.reshape(b, -1), po.reshape(b, -1), cn, sn, pn


def _topk_blocks(gate, n_past, axis):
    blk = lax.broadcasted_iota(jnp.int32, gate.shape, axis).astype(f32)
    g = jnp.where(blk < jnp.asarray(n_past, f32), gate, -jnp.inf)
    picks = []
    for _ in range(MOBA_TOP_K):
        m = jnp.max(g, axis=axis, keepdims=True)
        idx = jnp.min(jnp.where(g == m, blk, float(LANES)), axis=axis, keepdims=True)
        picks.append((idx, jnp.abs(m) < jnp.inf))
        g = jnp.where(blk == idx, -jnp.inf, g)
    return picks


def _moba_prompt_kernel(*refs, n_prev):
    q_ref, k_ref, v_ref = refs[:3]
    (o_ref, kt_out_ref, vt_out_ref,
     km_ref, kb_ref, vt_ref, qst_ref, nmt_ref, acc_ref, sa_ref, sb_ref) = refs[3 + n_prev:]
    i = pl.program_id(0)
    tq = MOBA_BLOCK
    scale = ATT_HEAD_DIM ** -0.5

    @pl.when(i == 0)
    def _():
        km_ref[...] = jnp.zeros(km_ref.shape, f32)

    k = k_ref[...]
    kb = k.astype(bf16)
    head = lambda h: slice(h * ATT_HEAD_DIM, (h + 1) * ATT_HEAD_DIM)
    for h in range(ATT_HEADS):
        kb_ref[i, h] = kb[:, head(h)]
    vt_f32 = v_ref[...].T
    kt_f32 = k.T
    if n_prev:
        kt_out_ref[...] = kt_f32
        vt_out_ref[...] = vt_f32
    else:
        for slab in range(kt_out_ref.shape[0]):
            kt_out_ref[slab] = kt_f32
            vt_out_ref[slab] = vt_f32
    vt = vt_f32.astype(bf16)
    ones_rows = jnp.ones((V_AUG_ROWS - ATT_HEAD_DIM, tq), bf16)
    for h in range(ATT_HEADS):
        vt_ref[i, h] = jnp.concatenate([vt[head(h), :], ones_rows], axis=0)
    qt = q_ref[...].T
    n_slots = km_ref.shape[0]
    blk_row = lax.broadcasted_iota(jnp.int32, (n_slots, tq), 0).astype(f32)
    key_io = lax.broadcasted_iota(jnp.int32, (tq, tq), 0)
    qry_io = lax.broadcasted_iota(jnp.int32, (tq, tq), 1)
    km = km_ref[...]
    m_own = []
    for h in range(ATT_HEADS):
        qth = qt[head(h), :]
        gate = _dot(km[:, head(h)], qth, precision=HIGHEST)
        bias = jnp.full((n_slots, tq), MASKED, f32)
        for idx, ok in _topk_blocks(gate, i, axis=0):
            bias = jnp.where(blk_row == idx, jnp.where(ok, 0.0, bias), bias)
        nmt_ref[h] = bias
        qst = (qth * (scale * LOG2E)).astype(bf16)
        qst_ref[h] = qst
        st = jnp.where(key_io <= qry_io, _dot(kb[:, head(h)], qst), MASKED)
        m = jnp.max(st, axis=0, keepdims=True)
        m_own.append(m)
        acc_ref[h] = _dot(vt_ref[i, h], jnp.exp2(st - m).astype(bf16))

    km_ref[pl.ds(i, 1), :] = jnp.mean(k, axis=0, keepdims=True)

    def scores(j, dst):
        for h in range(ATT_HEADS):
            dst[h] = _dot(kb_ref[j, h], qst_ref[h])

    def absorb(j, src, carry):
        new = []
        for h in range(ATT_HEADS):
            m_prev = carry[h]
            raw = src[h]
            bias = nmt_ref[h, pl.ds(j, 1), :]
            m_new = jnp.maximum(m_prev, jnp.max(raw, axis=0, keepdims=True) + bias)
            alpha = jnp.exp2(m_prev - m_new)
            p = jnp.exp2(raw - (m_new - bias))
            new.append(m_new)
            acc_ref[h] = acc_ref[h] * alpha + _dot(vt_ref[j, h], p.astype(bf16))
        return tuple(new)

    last = jnp.maximum(i - 1, 0)
    scores(0, sa_ref)

    def pair(t, carry):
        scores(jnp.minimum(2 * t + 1, last), sb_ref)
        carry = absorb(2 * t, sa_ref, carry)
        scores(jnp.minimum(2 * t + 2, last), sa_ref)
        return absorb(2 * t + 1, sb_ref, carry)

    def pairs(first, count, c):
        for t in range(count):
            c = pair(first + t, c)
        return c

    per_trip = MOBA_PAIRS_PER_TRIP
    carry = lax.fori_loop(0, i // (2 * per_trip), lambda u, c: pairs(per_trip * u, per_trip, c), tuple(m_own))
    n = per_trip // 2
    while n >= 1:
        first = i // (4 * n) * (2 * n)
        carry = lax.cond(i % (4 * n) >= 2 * n, functools.partial(pairs, first, n), lambda c: c, carry)
        n //= 2
    carry = lax.cond(i % 2 == 1, lambda c: absorb(i - 1, sa_ref, c), lambda c: c, carry)
    out_t = jnp.concatenate(
        [acc_ref[h, 0:ATT_HEAD_DIM, :] / acc_ref[h, ATT_HEAD_DIM:ATT_HEAD_DIM + 1, :] for h in range(ATT_HEADS)],
        axis=0)
    o_ref[...] = out_t.T.astype(o_ref.dtype)


def moba_prompt(q, k, v, layer, depth, kv_t_prev=None):
    t = q.shape[0]
    tq = MOBA_BLOCK
    n_blk = t // tq
    n_slots = -(-n_blk // 8) * 8
    assert n_slots <= LANES
    tile = pl.BlockSpec((tq, ATT_DIM), lambda i: (i, 0))
    stack = jax.ShapeDtypeStruct((depth, ATT_DIM, t), f32)
    prev = () if kv_t_prev is None else tuple(kv_t_prev)
    assert bool(prev) == (layer > 0)
    slab = (pl.BlockSpec((None, ATT_DIM, tq), lambda i: (layer, 0, i)) if prev
            else pl.BlockSpec((depth, ATT_DIM, tq), lambda i: (0, 0, i)))
    return pl.pallas_call(
        functools.partial(_moba_prompt_kernel, n_prev=len(prev)), grid=(n_blk,),
        in_specs=[tile, tile, tile] + [pl.BlockSpec(memory_space=pl.ANY)] * len(prev),
        out_specs=[tile, slab, slab],
        out_shape=[jax.ShapeDtypeStruct((t, ATT_DIM), bf16), stack, stack],
        input_output_aliases={3 + n: 1 + n for n in range(len(prev))},
        scratch_shapes=[pltpu.VMEM((n_slots, ATT_DIM), f32),
                        pltpu.VMEM((n_blk, ATT_HEADS, tq, ATT_HEAD_DIM), bf16),
                        pltpu.VMEM((n_blk, ATT_HEADS, V_AUG_ROWS, tq), bf16),
                        pltpu.VMEM((ATT_HEADS, ATT_HEAD_DIM, tq), bf16),
                        pltpu.VMEM((ATT_HEADS, n_slots, tq), f32),
                        pltpu.VMEM((ATT_HEADS, V_AUG_ROWS, tq), f32),
                        pltpu.VMEM((ATT_HEADS, tq, tq), f32),
                        pltpu.VMEM((ATT_HEADS, tq, tq), f32)],
        compiler_params=_cparams(1), name="moba_prompt",
    )(q, k, v, *prev)


K_CHUNK_PAGES = 16
K_SLOTS = 8
PAGES_PER_BLOCK = MOBA_BLOCK // PAGE_SIZE
CHUNKS_BEFORE_FINISH = 2


def _moba_decode_kernel(pt_ref, q_ref, q8_ref, kn8_ref, vn8_ref, kc_ref, vc_ref, o_ref,
                        kbuf, vbuf, s_ref, p_ref, gate_ref, stash_ref, ids_ref, ksem, vsem, *, layer, n_pages):
    b = pl.program_id(0)
    n_seq = pl.num_programs(0) - 1
    n_chunks = n_pages // K_CHUNK_PAGES
    blocks_per_chunk = K_CHUNK_PAGES // PAGES_PER_BLOCK
    n_blocks = n_pages // PAGES_PER_BLOCK
    scale = ATT_HEAD_DIM ** -0.5

    def k_copy(seq, page_slot, slot, p):
        page = pt_ref[seq, page_slot]
        return pltpu.make_async_copy(kc_ref.at[layer, page], kbuf.at[slot, p], ksem.at[slot])

    def start_chunk(seq, c, slot):
        for p in range(K_CHUNK_PAGES):
            k_copy(seq, c * K_CHUNK_PAGES + p, slot, p).start()

    def wait_chunk(seq, c, slot):
        for p in range(K_CHUNK_PAGES):
            k_copy(seq, c * K_CHUNK_PAGES + p, slot, p).wait()

    n_picks = ATT_HEADS * MOBA_TOP_K

    def v_copy(h, r, half, page):
        return pltpu.make_async_copy(vc_ref.at[layer, page, pl.ds(h * ATT_HEAD_DIM, ATT_HEAD_DIM), :],
                                     vbuf.at[h * MOBA_TOP_K + r, half], vsem.at[0])

    def stream(chunks):
        lane8 = lax.broadcasted_iota(jnp.int32, (8, ATT_DIM), 1)
        row8 = lax.broadcasted_iota(jnp.int32, (8, ATT_DIM), 0)
        qblk_b = jnp.where(lane8 // ATT_HEAD_DIM == row8, q_ref[...], 0.0).astype(bf16)
        blk_lane = lax.broadcasted_iota(jnp.int32, (8, LANES), 1)
        gate = gate_ref[...]
        for c in chunks:
            slot = c % K_SLOTS
            ahead = c + K_SLOTS - 1
            if ahead < n_chunks:
                start_chunk(b, ahead, ahead % K_SLOTS)
            else:
                @pl.when(b + 1 < n_seq)
                def _():
                    start_chunk(b + 1, ahead - n_chunks, ahead % K_SLOTS)
            wait_chunk(b, c, slot)
            for t in range(blocks_per_chunk):
                blk = c * blocks_per_chunk + t
                halves = [_dot(qblk_b, kbuf[slot, t * PAGES_PER_BLOCK + half].astype(bf16))
                          for half in range(PAGES_PER_BLOCK)]
                for half in range(PAGES_PER_BLOCK):
                    s_ref[blk, :, half * PAGE_SIZE:(half + 1) * PAGE_SIZE] = halves[half]
                tot = jnp.sum(sum(halves), axis=-1, keepdims=True) * (1.0 / MOBA_BLOCK)
                gate = jnp.where(blk_lane == blk, tot, gate)
        gate_ref[...] = gate

    def choose_and_weigh():
        picks = _topk_blocks(gate_ref[...], n_blocks, axis=1)
        for h in range(ATT_HEADS):
            for r in range(MOBA_TOP_K):
                blk = picks[r][0][h, 0].astype(jnp.int32)
                ids_ref[h * MOBA_TOP_K + r] = blk
                for half in range(PAGES_PER_BLOCK):
                    page = pt_ref[b, blk * PAGES_PER_BLOCK + half]
                    ids_ref[n_picks + (h * MOBA_TOP_K + r) * PAGES_PER_BLOCK + half] = page
                    v_copy(h, r, half, page).start()
        blk_io = lax.broadcasted_iota(jnp.int32, (n_blocks, 8, MOBA_BLOCK), 0).astype(f32)
        seen = jnp.zeros((n_blocks, 8, MOBA_BLOCK), jnp.bool_)
        for idx, ok in picks:
            seen = seen | ((blk_io == idx[None]) & ok[None])
        s_all = jnp.where(seen, s_ref[...] * scale, -jnp.inf)
        s_own = jnp.sum(q8_ref[...] * kn8_ref[...], axis=-1, keepdims=True) * scale
        m = jnp.maximum(jnp.max(jnp.max(s_all, axis=0), axis=-1, keepdims=True), s_own)
        p_all = jnp.exp(s_all - m[None])
        p_own = jnp.exp(s_own - m)
        l = jnp.sum(jnp.sum(p_all, axis=0), axis=-1, keepdims=True) + p_own
        p_ref[...] = p_all
        stash_ref[:, 0:ATT_HEAD_DIM] = p_own * vn8_ref[...]
        stash_ref[:, ATT_HEAD_DIM:] = jnp.broadcast_to(l, (8, LANES - ATT_HEAD_DIM))

    def finish_previous():
        for h in range(ATT_HEADS):
            for r in range(MOBA_TOP_K):
                for half in range(PAGES_PER_BLOCK):
                    page = ids_ref[n_picks + (h * MOBA_TOP_K + r) * PAGES_PER_BLOCK + half]
                    v_copy(h, r, half, page).wait()
        row_hd = lax.broadcasted_iota(jnp.int32, (8, ATT_HEAD_DIM), 0)
        o = stash_ref[:, 0:ATT_HEAD_DIM]
        for h in range(ATT_HEADS):
            oh = jnp.zeros((8, ATT_HEAD_DIM), f32)
            for r in range(MOBA_TOP_K):
                pb = p_ref[ids_ref[h * MOBA_TOP_K + r]].astype(bf16)
                for half in range(PAGES_PER_BLOCK):
                    oh = oh + _dot_nt(pb[:, half * PAGE_SIZE:(half + 1) * PAGE_SIZE],
                                      vbuf[h * MOBA_TOP_K + r, half].astype(bf16))
            o = o + jnp.where(row_hd == h, oh, 0.0)
        o_ref[...] = o / stash_ref[:, ATT_HEAD_DIM:ATT_HEAD_DIM + 1]

    @pl.when(b == 0)
    def _():
        for c in range(K_SLOTS - 1):
            start_chunk(0, c, c)

    @pl.when(b < n_seq)
    def _():
        gate_ref[...] = jnp.zeros((8, LANES), f32)
        stream(range(0, CHUNKS_BEFORE_FINISH))

    @pl.when(b > 0)
    def _():
        finish_previous()

    @pl.when(b < n_seq)
    def _():
        stream(range(CHUNKS_BEFORE_FINISH, n_chunks))
        choose_and_weigh()


def moba_decode(page_table, q, k_new, v_new, cache_kt, cache_vt, layer):
    b = q.shape[0]
    n_pages = page_table.shape[1]
    n_blocks = n_pages // PAGES_PER_BLOCK
    assert n_blocks <= LANES and n_blocks >= MOBA_TOP_K
    heads8 = lambda a: jnp.pad(a.reshape(b, ATT_HEADS, ATT_HEAD_DIM), ((0, 0), (0, 8 - ATT_HEADS), (0, 0)))
    n_picks = ATT_HEADS * MOBA_TOP_K
    cur = lambda i, pt: (jnp.minimum(i, b - 1), 0, 0)
    prev = lambda i, pt: (jnp.maximum(i - 1, 0), 0, 0)
    per_head = pl.BlockSpec((None, 8, ATT_HEAD_DIM), cur)
    grid_spec = pltpu.PrefetchScalarGridSpec(
        num_scalar_prefetch=1, grid=(b + 1,),
        in_specs=[pl.BlockSpec((None, 1, ATT_DIM), cur), per_head, per_head, per_head,
                  pl.BlockSpec(memory_space=pl.ANY), pl.BlockSpec(memory_space=pl.ANY)],
        out_specs=pl.BlockSpec((None, 8, ATT_HEAD_DIM), prev),
        scratch_shapes=[pltpu.VMEM((K_SLOTS, K_CHUNK_PAGES, ATT_DIM, PAGE_SIZE), f32),
                        pltpu.VMEM((n_picks, PAGES_PER_BLOCK, ATT_HEAD_DIM, PAGE_SIZE), f32),
                        pltpu.VMEM((n_blocks, 8, MOBA_BLOCK), f32),
                        pltpu.VMEM((n_blocks, 8, MOBA_BLOCK), f32),
                        pltpu.VMEM((8, LANES), f32),
                        pltpu.VMEM((8, LANES), f32),
                        pltpu.SMEM((n_picks * (1 + PAGES_PER_BLOCK),), jnp.int32),
                        pltpu.SemaphoreType.DMA((K_SLOTS,)),
                        pltpu.SemaphoreType.DMA((1,))])
    out = pl.pallas_call(
        functools.partial(_moba_decode_kernel, layer=layer, n_pages=n_pages),
        grid_spec=grid_spec,
        out_shape=jax.ShapeDtypeStruct((b, 8, ATT_HEAD_DIM), f32),
        compiler_params=_cparams(1), name="moba_decode",
    )(page_table, q.reshape(b, 1, ATT_DIM), heads8(q), heads8(k_new), heads8(v_new), cache_kt, cache_vt)
    return out[:, :ATT_HEADS].reshape(b, ATT_DIM)


def _layer_params(l, w_in, conv_w, conv_b, dt_bias, a_log, d_skip, ssm_norm_g, pool_w, pool_scale, w_out):
    o_xbc = SSM_INNER
    o_dt = o_xbc + CONV_DIM
    o_pool = o_dt + SSM_HEADS
    o_q = o_pool + POOL_DIM
    o_k = o_q + ATT_DIM
    o_v = o_k + ATT_DIM
    wl = w_in[l]
    pad_heads = lambda v: jnp.pad(v.astype(f32), (0, LANES - SSM_HEADS)).reshape(1, LANES)
    w_cat = jnp.concatenate([wl[:, :o_pool], jnp.zeros((wl.shape[0], LANES - SSM_HEADS), wl.dtype),
                             wl[:, o_pool:]], axis=1).astype(bf16)
    pw = jnp.zeros((POOL_DIM, POOL_DIM), f32)
    for g in range(len(POOL_WINDOWS)):
        pw = pw.at[g * POOL_GROUP:(g + 1) * POOL_GROUP, g * POOL_GROUP:(g + 1) * POOL_GROUP].set(pool_w[l, g])
    expand = (jnp.arange(LANES)[:, None] == (jnp.arange(SSM_INNER)[None, :] // SSM_HEAD_DIM)).astype(f32)
    wo = w_out[l].astype(bf16)
    return {
        'w_in': w_cat,
        'conv_w': conv_w[l], 'conv_b': conv_b[l].reshape(1, CONV_DIM),
        'dt_bias': pad_heads(dt_bias[l]), 'a': pad_heads(-jnp.exp(a_log[l].astype(f32))),
        'dsk': jnp.repeat(d_skip[l].astype(f32), SSM_HEAD_DIM).reshape(1, SSM_INNER),
        'ssm_norm_g': ssm_norm_g[l].reshape(1, SSM_INNER),
        'pool_w': pw.astype(bf16), 'pool_scale': pool_scale[l].reshape(1, POOL_DIM),
        'expand': expand,
        'w_out': wo,
    }


def _in_proj_outs():
    sizes = [SSM_INNER, CONV_DIM, LANES, POOL_DIM, ATT_DIM, ATT_DIM, ATT_DIM]
    starts = [sum(sizes[:i]) for i in range(len(sizes))]
    return [(s, n, f32) for s, n in zip(starts, sizes)]


IN_PROJ_OUTS = _in_proj_outs()


def _pages_channel_major(cache):
    d, n_phys = cache.shape[:2]
    return jnp.transpose(cache, (0, 1, 3, 4, 2)).reshape(d, n_phys, ATT_DIM, PAGE_SIZE)


def kernel(x_prompt, x_sample, cache_moba_k, cache_moba_v, state_ssm, state_conv, state_pool, cache_mem_k, cache_mem_v, page_table, mem_prompt, norm_mix_g, w_in, conv_w, conv_b, dt_bias, a_log, d_skip, ssm_norm_g, pool_w, pool_scale, w_out, norm_cross_g, norm_mem_g, w_mem_q, w_mem_kv, w_mem_o, norm_ffn_g, w_gate_up, w_down, final_norm_g):
    depth = w_in.shape[0]
    bp, t, d = x_prompt.shape
    bs = x_sample.shape[0]
    assert bp == 1 and x_sample.shape[1] == 1
    past_len = page_table.shape[1] * PAGE_SIZE
    cache_kt = _pages_channel_major(cache_moba_k)
    cache_vt = _pages_channel_major(cache_moba_v)
    mem_k_split = _mem_split_view(cache_mem_k)
    mem_v_split = _mem_split_view(cache_mem_v)
    mem_len = mem_prompt.shape[1]
    xp = x_prompt.reshape(t, d)
    xs = x_sample.reshape(bs, d)
    mem = mem_prompt.reshape(mem_len, d)
    tm_p, tm_s = 512, bs
    outs = {n: [] for n in ('ks', 'vs', 'sp', 'ss', 'cp', 'cs', 'pp', 'ps', 'mk', 'mv')}
    kv_t = None
    for l in range(depth):
        prm = _layer_params(l, w_in, conv_w, conv_b, dt_bias, a_log, d_skip, ssm_norm_g, pool_w, pool_scale, w_out)
        wq = w_mem_q[l].astype(bf16)
        wkv = w_mem_kv[l].astype(bf16)
        wo_mem = w_mem_o[l].astype(bf16)
        wgu = w_gate_up[l].astype(bf16)
        wdn = w_down[l].astype(bf16)
        half = MEM_HEADS * MEM_HEAD_DIM
        last_layer = l == depth - 1

        mk, mv, mkb, mvb = norm_matmul(mem, norm_mem_g[l], wkv,
                                       [(0, half, f32), (half, half, f32), (0, half, bf16), (half, half, bf16)],
                                       tm=mem_len)
        z, xbc, dtr, u, q, k, v = norm_matmul(xp, norm_mix_g[l], prm['w_in'], IN_PROJ_OUTS, tm=tm_p)
        y, po, s_new = ssd_pool_prompt(z, xbc, dtr, u, prm, tm=tm_p)
        att, *kv_t = moba_prompt(q, k, v, layer=l, depth=depth, kv_t_prev=kv_t)
        xp = prompt_tail(xp, [y, po, att], prm['w_out'], norm_cross_g[l], wq, mkb, mvb, wo_mem,
                         norm_ffn_g[l], wgu, wdn, final_norm_g, norm_out=last_layer, tm=tm_p)
        outs['sp'].append(s_new[None])
        outs['cp'].append(xbc[t - (CONV_W - 1):][None])
        outs['pp'].append(u[t - POOL_HIST:][None])
        outs['mk'].append(mk.reshape(1, mem_len, MEM_HEADS, MEM_HEAD_DIM))
        outs['mv'].append(mv.reshape(1, mem_len, MEM_HEADS, MEM_HEAD_DIM))

        z, xbc, dtr, u, q, k, v = norm_matmul(xs, norm_mix_g[l], prm['w_in'], IN_PROJ_OUTS, tm=tm_s)
        y, po, c_new, s_new, p_new = ssd_pool_sample(z, xbc, dtr, u, state_conv[l], state_ssm[l],
                                                     state_pool[l], prm, pos0=past_len)
        att = moba_decode(page_table, q, k, v, cache_kt, cache_vt, layer=l)
        xs = matmul_residual(xs, [y, po, att], prm['w_out'], tm=tm_s)
        (qc,) = norm_matmul(xs, norm_cross_g[l], _split_channel_order(wq, 1), [(0, d, f32)], tm=tm_s)
        oc = cross_sample(qc, mem_k_split, mem_v_split, layer=l)
        xs = matmul_residual(xs, [oc], _split_channel_order(wo_mem, 0), tm=tm_s)
        xs = swiglu_block(xs, norm_ffn_g[l], wgu, wdn, final_norm_g, norm_out=last_layer, tm=tm_s, tf=D_FF)
        outs['ks'].append(k.reshape(bs, 1, ATT_HEADS, ATT_HEAD_DIM))
        outs['vs'].append(v.reshape(bs, 1, ATT_HEADS, ATT_HEAD_DIM))
        outs['ss'].append(s_new)
        outs['cs'].append(c_new)
        outs['ps'].append(p_new)

    y_prompt = xp.reshape(1, t, d)
    y_sample = xs.reshape(bs, 1, d)
    st = lambda n: jnp.stack(outs[n])
    token_major = lambda a: jnp.transpose(a.reshape(depth, 1, ATT_HEADS, ATT_HEAD_DIM, t), (0, 1, 4, 2, 3))
    return (y_prompt, y_sample, token_major(kv_t[0]), token_major(kv_t[1]), st('ks'), st('vs'), st('sp'), st('ss'),
            st('cp'), st('cs'), st('pp'), st('ps'), st('mk'), st('mv'))
```

```python
import functools

import jax
import jax.numpy as jnp
from jax import lax
from jax.experimental import pallas as pl
from jax.experimental.pallas import tpu as pltpu

f32 = jnp.float32
bf16 = jnp.bfloat16
HIGHEST = lax.Precision.HIGHEST

D_MODEL = 1024
SSM_INNER = 512
SSM_HEAD_DIM = 64
SSM_HEADS = 8
SSM_GROUPS = 2
SSM_STATE = 128
CONV_W = 4
CONV_DIM = SSM_INNER + 2 * SSM_GROUPS * SSM_STATE
SSD_CHUNK = 128
POOL_DIM = 256
POOL_WINDOWS = (2, 4, 8, 16)
POOL_GROUP = 64
POOL_HIST = 15
ATT_DIM = 256
ATT_HEAD_DIM = 64
ATT_HEADS = 4
MOBA_BLOCK = 256
MOBA_TOP_K = 3
PAGE_SIZE = 128
MEM_HEADS = 4
MEM_HEAD_DIM = 256
D_FF = 2816
RMS_EPS = 1e-6
LANES = 128
CONV_HALO = 8
POOL_HALO = 16
MASKED = -1e30
LOG2E = 1.4426950408889634
BF16_SUBLANES = 16
V_AUG_ROWS = ATT_HEAD_DIM + BF16_SUBLANES
MOBA_PAIRS_PER_TRIP = 8
VMEM_LIMIT = 56 * 1024 * 1024


def _cparams(n_axes):
    return pltpu.CompilerParams(dimension_semantics=("arbitrary",) * n_axes,
                                vmem_limit_bytes=VMEM_LIMIT)


def _rms(x, g):
    ms = jnp.mean(x * x, axis=-1, keepdims=True)
    return x * lax.rsqrt(ms + RMS_EPS) * g


def _dot(a, b, **kw):
    return jnp.dot(a, b, preferred_element_type=f32, **kw)


def _dot_nt(a, b, **kw):
    return lax.dot_general(a, b, (((1,), (1,)), ((), ())), preferred_element_type=f32, **kw)


def _silu(x):
    return x * jax.nn.sigmoid(x)


def _softplus(x):
    return jnp.maximum(x, 0.0) + jnp.log1p(jnp.exp(-jnp.abs(x)))


def _full(shape):
    return pl.BlockSpec(shape, lambda *_: (0,) * len(shape))


def _norm_mm_kernel(x_ref, g_ref, w_ref, *o_refs, groups):
    hb = _rms(x_ref[...], g_ref[...]).astype(bf16)
    res = {}
    for o_ref, (start, size) in zip(o_refs, groups):
        if (start, size) not in res:
            res[start, size] = _dot(hb, w_ref[:, start:start + size])
        o_ref[...] = res[start, size].astype(o_ref.dtype)


def norm_matmul(x, g, w, outs, tm):
    m, d = x.shape
    assert all(start % LANES == 0 and size % LANES == 0 for start, size, _ in outs)
    return pl.pallas_call(
        functools.partial(_norm_mm_kernel, groups=tuple((s, n) for s, n, _ in outs)),
        grid=(m // tm,),
        in_specs=[pl.BlockSpec((tm, d), lambda i: (i, 0)), _full((1, d)), _full(w.shape)],
        out_specs=[pl.BlockSpec((tm, n), lambda i: (i, 0)) for _, n, _ in outs],
        out_shape=[jax.ShapeDtypeStruct((m, n), dt) for _, n, dt in outs],
        compiler_params=_cparams(1), name="norm_matmul",
    )(x, g.reshape(1, d), w)


def _mm_res_kernel(*refs, n_a):
    res_ref = refs[0]
    a_refs = refs[1:1 + n_a]
    w_refs = refs[1 + n_a:1 + 2 * n_a]
    o_ref = refs[-1]
    acc = res_ref[...]
    for a, w in zip(a_refs, w_refs):
        acc = acc + _dot(a[...].astype(bf16), w[...])
    o_ref[...] = acc


def matmul_residual(res, a_list, w, tm):
    m, d = res.shape
    in_specs = [pl.BlockSpec((tm, d), lambda i: (i, 0))]
    in_specs += [pl.BlockSpec((tm, a.shape[1]), lambda i: (i, 0)) for a in a_list]
    in_specs += [pl.BlockSpec((n, d), lambda i, b=b: (b, 0)) for n, b in _row_groups([a.shape[1] for a in a_list])]
    return pl.pallas_call(
        functools.partial(_mm_res_kernel, n_a=len(a_list)),
        grid=(m // tm,), in_specs=in_specs,
        out_specs=pl.BlockSpec((tm, d), lambda i: (i, 0)),
        out_shape=jax.ShapeDtypeStruct((m, d), f32),
        compiler_params=_cparams(1), name="matmul_residual",
    )(res, *a_list, *[w] * len(a_list))


def _prompt_tail_kernel(*refs, n_mix, norm_out):
    x_ref = refs[0]
    a_refs = refs[1:1 + n_mix]
    w_refs = refs[1 + n_mix:1 + 2 * n_mix]
    (g_ref, wq_ref, mk_ref, mv_ref, wo_ref,
     gf_ref, wg_ref, wu_ref, wd_ref, gout_ref, o_ref) = refs[1 + 2 * n_mix:]
    x = x_ref[...]
    for a, w in zip(a_refs, w_refs):
        x = x + _dot(a[...].astype(bf16), w[...])
    if n_mix:
        o_ref[...] = x
        x = o_ref[...]
    hb = _rms(x, g_ref[...]).astype(bf16)
    q = _dot(hb, wq_ref[...])
    acc = x
    for h in range(MEM_HEADS):
        sl = slice(h * MEM_HEAD_DIM, (h + 1) * MEM_HEAD_DIM)
        s = _dot_nt(q[:, sl].astype(bf16), mk_ref[:, sl]) * (MEM_HEAD_DIM ** -0.5)
        p = jnp.exp(s - jnp.max(s, axis=-1, keepdims=True))
        p = p / jnp.sum(p, axis=-1, keepdims=True)
        oh = _dot(p.astype(bf16), mv_ref[:, sl])
        acc = acc + _dot(oh.astype(bf16), wo_ref[sl, :])
    o_ref[...] = acc
    x2 = o_ref[...]
    hb = _rms(x2, gf_ref[...]).astype(bf16)
    act = _silu(_dot(hb, wg_ref[...])) * _dot(hb, wu_ref[...])
    y = x2 + _dot(act.astype(bf16), wd_ref[...])
    o_ref[...] = _rms(y, gout_ref[...]) if norm_out else y


def _row_groups(widths):
    out, start = [], 0
    for n in widths:
        assert start % n == 0
        out.append((n, start // n))
        start += n
    return out


def prompt_tail(x, mix_list, w_out, g, wq, mkb, mvb, wo, g_ffn, w_gu, w_d, g_out, norm_out, tm):
    m, d = x.shape
    row = lambda c: pl.BlockSpec((tm, c), lambda i: (i, 0))
    once = lambda shape, idx: pl.BlockSpec(shape, idx, pipeline_mode=pl.Buffered(1))
    resident = lambda a: once(a.shape, lambda i: (0,) * a.ndim)
    w_out_specs = [once((n, d), lambda i, b=b: (b, 0)) for n, b in _row_groups([a.shape[1] for a in mix_list])]
    consts = [g.reshape(1, d), wq, mkb, mvb, wo, g_ffn.reshape(1, d)]
    tail = [g_out.reshape(1, d)]
    return pl.pallas_call(
        functools.partial(_prompt_tail_kernel, n_mix=len(mix_list), norm_out=norm_out), grid=(m // tm,),
        in_specs=[row(d)] + [row(a.shape[1]) for a in mix_list] + w_out_specs
                 + [resident(c) for c in consts]
                 + [once((d, D_FF), lambda i: (0, 0)), once((d, D_FF), lambda i: (0, 1)), resident(w_d)]
                 + [resident(c) for c in tail],
        out_specs=row(d),
        out_shape=jax.ShapeDtypeStruct((m, d), f32),
        compiler_params=_cparams(1), name="prompt_tail",
    )(x, *mix_list, *[w_out] * len(mix_list), *consts, w_gu, w_gu, w_d, *tail)


MEM_SPLIT = MEM_HEAD_DIM // LANES
MEM_ROWS = MEM_SPLIT * MEM_HEADS


def _split_channel_order(w, axis):
    shape = w.shape
    w = w.reshape(shape[:axis] + (MEM_HEADS, MEM_SPLIT, LANES) + shape[axis + 1:])
    return jnp.swapaxes(w, axis, axis + 1).reshape(shape)


def _mem_split_view(cache):
    d, b, m = cache.shape[:3]
    x = cache.reshape(d, b, m, MEM_HEADS, MEM_SPLIT, LANES)
    return jnp.transpose(x, (0, 1, 2, 4, 3, 5)).reshape(d, b, m, MEM_ROWS, LANES)


def _cross_sample_kernel(q_ref, mk_ref, mv_ref, o_ref):
    part = jnp.sum(mk_ref[...] * q_ref[...][None], axis=-1, keepdims=True)
    s = part
    for piece in range(1, MEM_SPLIT):
        s = s + jnp.roll(part, piece * MEM_HEADS, axis=1)
    s = s * (MEM_HEAD_DIM ** -0.5)
    p = jnp.exp(s - jnp.max(s, axis=0, keepdims=True))
    l = jnp.sum(p, axis=0)
    o_ref[...] = jnp.sum(p * mv_ref[...], axis=0) / l


def cross_sample(q_split, mem_k_split, mem_v_split, layer):
    b = q_split.shape[0]
    mlen = mem_k_split.shape[2]
    mem_spec = pl.BlockSpec((None, None, mlen, MEM_ROWS, LANES), lambda i: (layer, i, 0, 0, 0))
    row_spec = pl.BlockSpec((None, MEM_ROWS, LANES), lambda i: (i, 0, 0))
    out = pl.pallas_call(
        _cross_sample_kernel, grid=(b,),
        in_specs=[row_spec, mem_spec, mem_spec], out_specs=row_spec,
        out_shape=jax.ShapeDtypeStruct((b, MEM_ROWS, LANES), f32),
        compiler_params=_cparams(1), name="cross_sample",
    )(q_split.reshape(b, MEM_ROWS, LANES), mem_k_split, mem_v_split)
    return out.reshape(b, MEM_ROWS * LANES)


def _swiglu_kernel(x_ref, g_ref, wg_ref, wu_ref, wd_ref, gout_ref, o_ref, h_ref, acc_ref, *, norm_out):
    f = pl.program_id(1)

    @pl.when(f == 0)
    def _():
        h_ref[...] = _rms(x_ref[...], g_ref[...]).astype(bf16)
        acc_ref[...] = x_ref[...]

    hb = h_ref[...]
    a = _silu(_dot(hb, wg_ref[...])) * _dot(hb, wu_ref[...])
    acc_ref[...] += _dot(a.astype(bf16), wd_ref[...])

    @pl.when(f == pl.num_programs(1) - 1)
    def _():
        y = acc_ref[...]
        o_ref[...] = _rms(y, gout_ref[...]) if norm_out else y


def swiglu_block(x, g, w_gu, w_d, g_out, norm_out, tm, tf):
    m, d = x.shape
    nf = D_FF // tf
    mode = dict(pipeline_mode=pl.Buffered(1)) if nf == 1 else {}
    return pl.pallas_call(
        functools.partial(_swiglu_kernel, norm_out=norm_out), grid=(m // tm, nf),
        in_specs=[pl.BlockSpec((tm, d), lambda i, f: (i, 0)), _full((1, d)),
                  pl.BlockSpec((d, tf), lambda i, f: (0, f), **mode),
                  pl.BlockSpec((d, tf), lambda i, f: (0, f + nf), **mode),
                  pl.BlockSpec((tf, d), lambda i, f: (f, 0), **mode), _full((1, d))],
        out_specs=pl.BlockSpec((tm, d), lambda i, f: (i, 0)),
        out_shape=jax.ShapeDtypeStruct((m, d), f32),
        scratch_shapes=[pltpu.VMEM((tm, d), bf16), pltpu.VMEM((tm, d), f32)],
        compiler_params=_cparams(2), name="swiglu_block",
    )(x, g.reshape(1, d), w_gu, w_gu, w_d, g_out.reshape(1, d))


def _pool_select(w2, w4, w8, w16, cnt_of, lane):
    out = w16 / cnt_of(16)
    for win, acc in ((8, w8), (4, w4), (2, w2)):
        g = POOL_WINDOWS.index(win)
        out = jnp.where(lane < (g + 1) * POOL_GROUP, acc / cnt_of(win), out)
    return out


def _group_rmsnorm(y, g):
    half = SSM_INNER // SSM_GROUPS
    parts = [_rms(y[:, i * half:(i + 1) * half], g[:, i * half:(i + 1) * half])
             for i in range(SSM_GROUPS)]
    return jnp.concatenate(parts, axis=-1)


def _ssd_pool_prompt_kernel(z_ref, xbc_ref, dt_ref, u_ref, cw_ref, cb_ref, dtb_ref, a_ref, dsk_ref,
                            ng_ref, pw_ref, ps_ref,
                            y_ref, po_ref, st_ref,
                            ext_ref, pext_ref, xc_ref, s_ref, *, tm):
    i = pl.program_id(0)
    halo = CONV_HALO
    phalo = POOL_HALO

    @pl.when(i == 0)
    def _():
        ext_ref[0:halo, :] = jnp.zeros((halo, CONV_DIM), f32)
        pext_ref[0:phalo, :] = jnp.zeros((phalo, POOL_DIM), f32)
        s_ref[...] = jnp.zeros(s_ref.shape, f32)

    @pl.when(i > 0)
    def _():
        ext_ref[0:halo, :] = ext_ref[tm:tm + halo, :]
        pext_ref[0:phalo, :] = pext_ref[tm:tm + phalo, :]

    ext_ref[halo:halo + tm, :] = xbc_ref[...]
    pext_ref[phalo:phalo + tm, :] = u_ref[...]

    rp, lp = 64, 256
    for r0 in range(0, tm, rp):
        for c0 in range(0, CONV_DIM, lp):
            acc = jnp.broadcast_to(cb_ref[:, c0:c0 + lp], (rp, lp))
            for j in range(CONV_W):
                off = halo - (CONV_W - 1) + j + r0
                acc = acc + ext_ref[off:off + rp, c0:c0 + lp] * cw_ref[j:j + 1, c0:c0 + lp]
            xc_ref[r0:r0 + rp, c0:c0 + lp] = _silu(acc)

    lane_p = lax.broadcasted_iota(jnp.int32, (rp, POOL_DIM), 1)
    row_p = lax.broadcasted_iota(jnp.int32, (rp, POOL_DIM), 0)
    for r0 in range(0, tm, rp):
        run = pext_ref[r0:r0 + phalo + rp, :]
        u = run[phalo:]
        sums, win = {}, 1
        while win < max(POOL_WINDOWS):
            run = run + pltpu.roll(run, win, 0)
            win *= 2
            sums[win] = run[phalo:]
        pos1 = (row_p + (i * tm + r0 + 1)).astype(f32)
        pooled = _pool_select(sums[2], sums[4], sums[8], sums[16],
                              lambda win: jnp.minimum(float(win), pos1), lane_p) - u
        po_ref[r0:r0 + rp, :] = (_dot(pooled.astype(bf16), pw_ref[...]) * ps_ref[...]).astype(po_ref.dtype)

    cl = SSD_CHUNK
    r_io = lax.broadcasted_iota(jnp.int32, (cl, cl), 0)
    c_io = lax.broadcasted_iota(jnp.int32, (cl, cl), 1)
    ltri = (c_io <= r_io).astype(f32)
    causal = c_io <= r_io
    first_half = c_io < SSM_HEAD_DIM
    n_pairs = SSM_HEADS // 2
    heads_per_group = SSM_HEADS // SSM_GROUPS

    def chunk(c, carry):
        r0 = pl.multiple_of(c * cl, cl)
        xc = xc_ref[pl.ds(r0, cl), :]
        dt = _softplus(dt_ref[pl.ds(r0, cl), :] + dtb_ref[...])
        a_cum = _dot(ltri, dt * a_ref[...], precision=HIGHEST)
        a_cum_t = a_cum.T
        dt_t = dt.T
        to_end_t = jnp.exp(a_cum_t[:, cl - 1:cl] - a_cum_t) * dt_t
        b_t = [xc[:, SSM_INNER + g * SSM_STATE:SSM_INNER + (g + 1) * SSM_STATE].T
               for g in range(SSM_GROUPS)]
        cm = [xc[:, SSM_INNER + SSM_GROUPS * SSM_STATE + g * SSM_STATE:
                 SSM_INNER + SSM_GROUPS * SSM_STATE + (g + 1) * SSM_STATE].astype(bf16)
              for g in range(SSM_GROUPS)]
        scores = [_dot(cm[g], b_t[g].astype(bf16)) for g in range(SSM_GROUPS)]
        y_pairs = []
        for k in range(n_pairs):
            g = (2 * k) // heads_per_group
            xs_pair = xc[:, k * LANES:(k + 1) * LANES]
            xs_b = xs_pair.astype(bf16)
            yd, cs, eb = [], [], []
            for h in (2 * k, 2 * k + 1):
                colb = jnp.broadcast_to(a_cum[:, h:h + 1], (cl, cl))
                seg = colb - a_cum_t[h:h + 1, :]
                decay = jnp.exp(jnp.where(causal, seg, -jnp.inf))
                mh = scores[g] * decay * dt_t[h:h + 1, :]
                yd.append(_dot(mh.astype(bf16), xs_b))
                cs.append(_dot((b_t[g] * to_end_t[h:h + 1, :]).astype(bf16), xs_b))
                eb.append(jnp.exp(colb))
            e_pair = jnp.where(first_half, eb[0], eb[1])
            s_old = s_ref[k]
            y_off = _dot(cm[g], s_old.astype(bf16)) * e_pair
            y_pairs.append(jnp.where(first_half, yd[0], yd[1]) + y_off
                           + dsk_ref[:, k * LANES:(k + 1) * LANES] * xs_pair)
            s_ref[k] = s_old * e_pair[cl - 1:cl, :] + jnp.where(first_half, cs[0], cs[1])
        y = jnp.concatenate(y_pairs, axis=-1) * _silu(z_ref[pl.ds(r0, cl), :])
        y_ref[pl.ds(r0, cl), :] = _group_rmsnorm(y, ng_ref[...]).astype(y_ref.dtype)
        return carry

    lax.fori_loop(0, tm // cl, chunk, 0, unroll=True)

    @pl.when(i == pl.num_programs(0) - 1)
    def _():
        for k in range(n_pairs):
            st = s_ref[k].T
            st_ref[2 * k] = st[0:SSM_HEAD_DIM, :]
            st_ref[2 * k + 1] = st[SSM_HEAD_DIM:2 * SSM_HEAD_DIM, :]


def ssd_pool_prompt(z, xbc, dt, u, prm, tm):
    t = z.shape[0]
    assert POOL_WINDOWS == tuple(2 ** (n + 1) for n in range(len(POOL_WINDOWS)))
    row = lambda c: pl.BlockSpec((tm, c), lambda i: (i, 0))
    consts = [prm['conv_w'], prm['conv_b'], prm['dt_bias'], prm['a'], prm['dsk'], prm['ssm_norm_g'],
              prm['pool_w'], prm['pool_scale']]
    return pl.pallas_call(
        functools.partial(_ssd_pool_prompt_kernel, tm=tm), grid=(t // tm,),
        in_specs=[row(SSM_INNER), row(CONV_DIM), row(LANES), row(POOL_DIM)] + [_full(c.shape) for c in consts],
        out_specs=[row(SSM_INNER), row(POOL_DIM), _full((SSM_HEADS, SSM_HEAD_DIM, SSM_STATE))],
        out_shape=[jax.ShapeDtypeStruct((t, SSM_INNER), bf16), jax.ShapeDtypeStruct((t, POOL_DIM), bf16),
                   jax.ShapeDtypeStruct((SSM_HEADS, SSM_HEAD_DIM, SSM_STATE), f32)],
        scratch_shapes=[pltpu.VMEM((tm + CONV_HALO, CONV_DIM), f32), pltpu.VMEM((tm + POOL_HALO, POOL_DIM), f32),
                        pltpu.VMEM((tm, CONV_DIM), f32),
                        pltpu.VMEM((SSM_HEADS // 2, SSM_STATE, 2 * SSM_HEAD_DIM), f32)],
        compiler_params=_cparams(1), name="ssd_pool_prompt",
    )(z, xbc, dt, u, *consts)


def _col_tile(row):
    return jnp.broadcast_to(row, (LANES, LANES)).T


def _ssd_pool_sample_kernel(z_ref, xbc_ref, dt_ref, u_ref, sc_ref, ss_ref, sp_ref,
                            cw_ref, cb_ref, dtb_ref, a_ref, dsk_ref, ng_ref, pw_ref, ps_ref, ex_ref,
                            y_ref, po_ref, cn_ref, sn_ref, pn_ref, *, pos0):
    xrow = xbc_ref[...]
    acc = cb_ref[...] + xrow * cw_ref[CONV_W - 1:CONV_W, :]
    for j in range(CONV_W - 1):
        acc = acc + sc_ref[j:j + 1, :] * cw_ref[j:j + 1, :]
        if j > 0:
            cn_ref[j - 1:j, :] = sc_ref[j:j + 1, :]
    cn_ref[CONV_W - 2:CONV_W - 1, :] = xrow
    xc = _silu(acc)

    dt = _softplus(dt_ref[...] + dtb_ref[...])
    dta = dt * a_ref[...]
    both = jnp.concatenate([jnp.broadcast_to(dt, (8, LANES)), jnp.broadcast_to(dta, (8, LANES))], axis=0)
    both_x = _dot(both, ex_ref[...], precision=HIGHEST)
    dtx = both_x[0:1, :]
    dec_x = jnp.exp(both_x[8:9, :])
    xs = xc[:, 0:SSM_INNER]
    xdt = xs * dtx
    heads_per_group = SSM_HEADS // SSM_GROUPS
    y_pairs = []
    for k in range(SSM_HEADS // 2):
        g = (2 * k) // heads_per_group
        b_row = xc[:, SSM_INNER + g * SSM_STATE:SSM_INNER + (g + 1) * SSM_STATE]
        c_row = xc[:, SSM_INNER + SSM_GROUPS * SSM_STATE + g * SSM_STATE:
                   SSM_INNER + SSM_GROUPS * SSM_STATE + (g + 1) * SSM_STATE]
        sl = slice(k * LANES, (k + 1) * LANES)
        s_old = jnp.concatenate([ss_ref[2 * k], ss_ref[2 * k + 1]], axis=0)
        s_new = s_old * _col_tile(dec_x[:, sl]) + _col_tile(xdt[:, sl]) * b_row
        sn_ref[2 * k] = s_new[0:SSM_HEAD_DIM, :]
        sn_ref[2 * k + 1] = s_new[SSM_HEAD_DIM:, :]
        y_k = _dot_nt(jnp.broadcast_to(c_row, (8, SSM_STATE)), s_new, precision=HIGHEST)[0:1, :]
        y_pairs.append(y_k + dsk_ref[:, sl] * xs[:, sl])
    y = jnp.concatenate(y_pairs, axis=-1) * _silu(z_ref[...])
    y_ref[...] = _group_rmsnorm(y, ng_ref[...]).astype(y_ref.dtype)

    u = u_ref[...]
    prev = sp_ref[...]
    rowi = lax.broadcasted_iota(jnp.int32, prev.shape, 0)
    tail = lambda win: u + jnp.sum(jnp.where(rowi >= POOL_HIST - (win - 1), prev, 0.0), axis=0, keepdims=True)
    lane_p = lax.broadcasted_iota(jnp.int32, (1, POOL_DIM), 1)
    pooled = _pool_select(tail(2), tail(4), tail(8), tail(16),
                          lambda win: float(min(win, pos0 + 1)), lane_p) - u
    po = _dot(jnp.broadcast_to(pooled, (8, POOL_DIM)).astype(bf16), pw_ref[...])[0:1, :] * ps_ref[...]
    po_ref[...] = po.astype(po_ref.dtype)
    pn_ref[0:POOL_HIST - 1, :] = sp_ref[1:POOL_HIST, :]
    pn_ref[POOL_HIST - 1:POOL_HIST, :] = u


SAMPLE_SEQS_PER_STEP = 2
N_SAMPLE_SEQ_INPUTS, N_SAMPLE_SEQ_OUTPUTS = 7, 5


def _ssd_pool_sample_group_kernel(*refs, pos0):
    seq_in = refs[:N_SAMPLE_SEQ_INPUTS]
    consts = refs[N_SAMPLE_SEQ_INPUTS:len(refs) - N_SAMPLE_SEQ_OUTPUTS]
    seq_out = refs[len(refs) - N_SAMPLE_SEQ_OUTPUTS:]
    for s in range(SAMPLE_SEQS_PER_STEP):
        _ssd_pool_sample_kernel(*[r.at[s] for r in seq_in], *consts, *[r.at[s] for r in seq_out], pos0=pos0)


def ssd_pool_sample(z, xbc, dt, u, st_conv, st_ssm, st_pool, prm, pos0):
    b = z.shape[0]
    grp = SAMPLE_SEQS_PER_STEP
    assert b % grp == 0
    per_seq = lambda *shape: pl.BlockSpec((grp,) + shape, lambda i: (i,) + (0,) * len(shape))
    consts = [prm['conv_w'], prm['conv_b'], prm['dt_bias'], prm['a'], prm['dsk'], prm['ssm_norm_g'],
              prm['pool_w'], prm['pool_scale'], prm['expand']]
    outs = pl.pallas_call(
        functools.partial(_ssd_pool_sample_group_kernel, pos0=pos0), grid=(b // grp,),
        in_specs=[per_seq(1, SSM_INNER), per_seq(1, CONV_DIM), per_seq(1, LANES), per_seq(1, POOL_DIM),
                  per_seq(CONV_W - 1, CONV_DIM), per_seq(SSM_HEADS, SSM_HEAD_DIM, SSM_STATE),
                  per_seq(POOL_HIST, POOL_DIM)] + [_full(c.shape) for c in consts],
        out_specs=[per_seq(1, SSM_INNER), per_seq(1, POOL_DIM), per_seq(CONV_W - 1, CONV_DIM),
                   per_seq(SSM_HEADS, SSM_HEAD_DIM, SSM_STATE), per_seq(POOL_HIST, POOL_DIM)],
        out_shape=[jax.ShapeDtypeStruct((b, 1, SSM_INNER), f32), jax.ShapeDtypeStruct((b, 1, POOL_DIM), f32),
                   jax.ShapeDtypeStruct((b, CONV_W - 1, CONV_DIM), f32),
                   jax.ShapeDtypeStruct((b, SSM_HEADS, SSM_HEAD_DIM, SSM_STATE), f32),
                   jax.ShapeDtypeStruct((b, POOL_HIST, POOL_DIM), f32)],
        compiler_params=_cparams(1), name="ssd_pool_sample",
    )(z.reshape(b, 1, -1), xbc.reshape(b, 1, -1), dt.reshape(b, 1, -1), u.reshape(b, 1, -1),
      st_conv, st_ssm, st_pool, *consts)
    y, po, cn, sn, pn = outs
    return y.reshape(b, -1), po.reshape(b, -1), cn, sn, pn


def _topk_blocks(gate, n_past, axis):
    blk = lax.broadcasted_iota(jnp.int32, gate.shape, axis).astype(f32)
    g = jnp.where(blk < jnp.asarray(n_past, f32), gate, -jnp.inf)
    picks = []
    for _ in range(MOBA_TOP_K):
        m = jnp.max(g, axis=axis, keepdims=True)
        idx = jnp.min(jnp.where(g == m, blk, float(LANES)), axis=axis, keepdims=True)
        picks.append((idx, jnp.abs(m) < jnp.inf))
        g = jnp.where(blk == idx, -jnp.inf, g)
    return picks


def _moba_prompt_kernel(*refs, n_prev):
    q_ref, k_ref, v_ref = refs[:3]
    (o_ref, kt_out_ref, vt_out_ref,
     km_ref, kb_ref, vt_ref, qst_ref, nmt_ref, acc_ref, sa_ref, sb_ref) = refs[3 + n_prev:]
    i = pl.program_id(0)
    tq = MOBA_BLOCK
    scale = ATT_HEAD_DIM ** -0.5

    @pl.when(i == 0)
    def _():
        km_ref[...] = jnp.zeros(km_ref.shape, f32)

    k = k_ref[...]
    kb = k.astype(bf16)
    head = lambda h: slice(h * ATT_HEAD_DIM, (h + 1) * ATT_HEAD_DIM)
    for h in range(ATT_HEADS):
        kb_ref[i, h] = kb[:, head(h)]
    vt_f32 = v_ref[...].T
    kt_f32 = k.T
    if n_prev:
        kt_out_ref[...] = kt_f32
        vt_out_ref[...] = vt_f32
    else:
        for slab in range(kt_out_ref.shape[0]):
            kt_out_ref[slab] = kt_f32
            vt_out_ref[slab] = vt_f32
    vt = vt_f32.astype(bf16)
    ones_rows = jnp.ones((V_AUG_ROWS - ATT_HEAD_DIM, tq), bf16)
    for h in range(ATT_HEADS):
        vt_ref[i, h] = jnp.concatenate([vt[head(h), :], ones_rows], axis=0)
    qt = q_ref[...].T
    n_slots = km_ref.shape[0]
    blk_row = lax.broadcasted_iota(jnp.int32, (n_slots, tq), 0).astype(f32)
    key_io = lax.broadcasted_iota(jnp.int32, (tq, tq), 0)
    qry_io = lax.broadcasted_iota(jnp.int32, (tq, tq), 1)
    km = km_ref[...]
    m_own = []
    for h in range(ATT_HEADS):
        qth = qt[head(h), :]
        gate = _dot(km[:, head(h)], qth, precision=HIGHEST)
        bias = jnp.full((n_slots, tq), MASKED, f32)
        for idx, ok in _topk_blocks(gate, i, axis=0):
            bias = jnp.where(blk_row == idx, jnp.where(ok, 0.0, bias), bias)
        nmt_ref[h] = bias
        qst = (qth * (scale * LOG2E)).astype(bf16)
        qst_ref[h] = qst
        st = jnp.where(key_io <= qry_io, _dot(kb[:, head(h)], qst), MASKED)
        m = jnp.max(st, axis=0, keepdims=True)
        m_own.append(m)
        acc_ref[h] = _dot(vt_ref[i, h], jnp.exp2(st - m).astype(bf16))

    km_ref[pl.ds(i, 1), :] = jnp.mean(k, axis=0, keepdims=True)

    def scores(j, dst):
        for h in range(ATT_HEADS):
            dst[h] = _dot(kb_ref[j, h], qst_ref[h])

    def absorb(j, src, carry):
        new = []
        for h in range(ATT_HEADS):
            m_prev = carry[h]
            raw = src[h]
            bias = nmt_ref[h, pl.ds(j, 1), :]
            m_new = jnp.maximum(m_prev, jnp.max(raw, axis=0, keepdims=True) + bias)
            alpha = jnp.exp2(m_prev - m_new)
            p = jnp.exp2(raw - (m_new - bias))
            new.append(m_new)
            acc_ref[h] = acc_ref[h] * alpha + _dot(vt_ref[j, h], p.astype(bf16))
        return tuple(new)

    last = jnp.maximum(i - 1, 0)
    scores(0, sa_ref)

    def pair(t, carry):
        scores(jnp.minimum(2 * t + 1, last), sb_ref)
        carry = absorb(2 * t, sa_ref, carry)
        scores(jnp.minimum(2 * t + 2, last), sa_ref)
        return absorb(2 * t + 1, sb_ref, carry)

    def pairs(first, count, c):
        for t in range(count):
            c = pair(first + t, c)
        return c

    per_trip = MOBA_PAIRS_PER_TRIP
    carry = lax.fori_loop(0, i // (2 * per_trip), lambda u, c: pairs(per_trip * u, per_trip, c), tuple(m_own))
    n = per_trip // 2
    while n >= 1:
        first = i // (4 * n) * (2 * n)
        carry = lax.cond(i % (4 * n) >= 2 * n, functools.partial(pairs, first, n), lambda c: c, carry)
        n //= 2
    carry = lax.cond(i % 2 == 1, lambda c: absorb(i - 1, sa_ref, c), lambda c: c, carry)
    out_t = jnp.concatenate(
        [acc_ref[h, 0:ATT_HEAD_DIM, :] / acc_ref[h, ATT_HEAD_DIM:ATT_HEAD_DIM + 1, :] for h in range(ATT_HEADS)],
        axis=0)
    o_ref[...] = out_t.T.astype(o_ref.dtype)


def moba_prompt(q, k, v, layer, depth, kv_t_prev=None):
    t = q.shape[0]
    tq = MOBA_BLOCK
    n_blk = t // tq
    n_slots = -(-n_blk // 8) * 8
    assert n_slots <= LANES
    tile = pl.BlockSpec((tq, ATT_DIM), lambda i: (i, 0))
    stack = jax.ShapeDtypeStruct((depth, ATT_DIM, t), f32)
    prev = () if kv_t_prev is None else tuple(kv_t_prev)
    assert bool(prev) == (layer > 0)
    slab = (pl.BlockSpec((None, ATT_DIM, tq), lambda i: (layer, 0, i)) if prev
            else pl.BlockSpec((depth, ATT_DIM, tq), lambda i: (0, 0, i)))
    return pl.pallas_call(
        functools.partial(_moba_prompt_kernel, n_prev=len(prev)), grid=(n_blk,),
        in_specs=[tile, tile, tile] + [pl.BlockSpec(memory_space=pl.ANY)] * len(prev),
        out_specs=[tile, slab, slab],
        out_shape=[jax.ShapeDtypeStruct((t, ATT_DIM), bf16), stack, stack],
        input_output_aliases={3 + n: 1 + n for n in range(len(prev))},
        scratch_shapes=[pltpu.VMEM((n_slots, ATT_DIM), f32),
                        pltpu.VMEM((n_blk, ATT_HEADS, tq, ATT_HEAD_DIM), bf16),
                        pltpu.VMEM((n_blk, ATT_HEADS, V_AUG_ROWS, tq), bf16),
                        pltpu.VMEM((ATT_HEADS, ATT_HEAD_DIM, tq), bf16),
                        pltpu.VMEM((ATT_HEADS, n_slots, tq), f32),
                        pltpu.VMEM((ATT_HEADS, V_AUG_ROWS, tq), f32),
                        pltpu.VMEM((ATT_HEADS, tq, tq), f32),
                        pltpu.VMEM((ATT_HEADS, tq, tq), f32)],
        compiler_params=_cparams(1), name="moba_prompt",
    )(q, k, v, *prev)


K_CHUNK_PAGES = 16
K_SLOTS = 8
PAGES_PER_BLOCK = MOBA_BLOCK // PAGE_SIZE
CHUNKS_BEFORE_FINISH = 2


def _moba_decode_kernel(pt_ref, q_ref, q8_ref, kn8_ref, vn8_ref, kc_ref, vc_ref, o_ref,
                        kbuf, vbuf, s_ref, p_ref, gate_ref, stash_ref, ids_ref, ksem, vsem, *, layer, n_pages):
    b = pl.program_id(0)
    n_seq = pl.num_programs(0) - 1
    n_chunks = n_pages // K_CHUNK_PAGES
    blocks_per_chunk = K_CHUNK_PAGES // PAGES_PER_BLOCK
    n_blocks = n_pages // PAGES_PER_BLOCK
    scale = ATT_HEAD_DIM ** -0.5

    def k_copy(seq, page_slot, slot, p):
        page = pt_ref[seq, page_slot]
        return pltpu.make_async_copy(kc_ref.at[layer, page], kbuf.at[slot, p], ksem.at[slot])

    def start_chunk(seq, c, slot):
        for p in range(K_CHUNK_PAGES):
            k_copy(seq, c * K_CHUNK_PAGES + p, slot, p).start(priority=p % 2)

    def wait_chunk(seq, c, slot):
        for p in range(K_CHUNK_PAGES):
            k_copy(seq, c * K_CHUNK_PAGES + p, slot, p).wait()

    n_picks = ATT_HEADS * MOBA_TOP_K

    def v_copy(h, r, half, page):
        return pltpu.make_async_copy(vc_ref.at[layer, page, pl.ds(h * ATT_HEAD_DIM, ATT_HEAD_DIM), :],
                                     vbuf.at[h * MOBA_TOP_K + r, half], vsem.at[0])

    def stream(chunks):
        lane8 = lax.broadcasted_iota(jnp.int32, (8, ATT_DIM), 1)
        row8 = lax.broadcasted_iota(jnp.int32, (8, ATT_DIM), 0)
        qblk_b = jnp.where(lane8 // ATT_HEAD_DIM == row8, q_ref[...], 0.0).astype(bf16)
        blk_lane = lax.broadcasted_iota(jnp.int32, (8, LANES), 1)
        gate = gate_ref[...]
        for c in chunks:
            slot = c % K_SLOTS
            ahead = c + K_SLOTS - 1
            if ahead < n_chunks:
                start_chunk(b, ahead, ahead % K_SLOTS)
            else:
                @pl.when(b + 1 < n_seq)
                def _():
                    start_chunk(b + 1, ahead - n_chunks, ahead % K_SLOTS)
            wait_chunk(b, c, slot)
            for t in range(blocks_per_chunk):
                blk = c * blocks_per_chunk + t
                halves = [_dot(qblk_b, kbuf[slot, t * PAGES_PER_BLOCK + half].astype(bf16))
                          for half in range(PAGES_PER_BLOCK)]
                for half in range(PAGES_PER_BLOCK):
                    s_ref[blk, :, half * PAGE_SIZE:(half + 1) * PAGE_SIZE] = halves[half]
                tot = jnp.sum(sum(halves), axis=-1, keepdims=True) * (1.0 / MOBA_BLOCK)
                gate = jnp.where(blk_lane == blk, tot, gate)
        gate_ref[...] = gate

    def choose_and_weigh():
        picks = _topk_blocks(gate_ref[...], n_blocks, axis=1)
        for h in range(ATT_HEADS):
            for r in range(MOBA_TOP_K):
                blk = picks[r][0][h, 0].astype(jnp.int32)
                ids_ref[h * MOBA_TOP_K + r] = blk
                for half in range(PAGES_PER_BLOCK):
                    page = pt_ref[b, blk * PAGES_PER_BLOCK + half]
                    ids_ref[n_picks + (h * MOBA_TOP_K + r) * PAGES_PER_BLOCK + half] = page
                    v_copy(h, r, half, page).start()
        blk_io = lax.broadcasted_iota(jnp.int32, (n_blocks, 8, MOBA_BLOCK), 0).astype(f32)
        seen = jnp.zeros((n_blocks, 8, MOBA_BLOCK), jnp.bool_)
        for idx, ok in picks:
            seen = seen | ((blk_io == idx[None]) & ok[None])
        s_all = jnp.where(seen, s_ref[...] * scale, -jnp.inf)
        s_own = jnp.sum(q8_ref[...] * kn8_ref[...], axis=-1, keepdims=True) * scale
        m = jnp.maximum(jnp.max(jnp.max(s_all, axis=0), axis=-1, keepdims=True), s_own)
        p_all = jnp.exp(s_all - m[None])
        p_own = jnp.exp(s_own - m)
        l = jnp.sum(jnp.sum(p_all, axis=0), axis=-1, keepdims=True) + p_own
        p_ref[...] = p_all
        stash_ref[:, 0:ATT_HEAD_DIM] = p_own * vn8_ref[...]
        stash_ref[:, ATT_HEAD_DIM:] = jnp.broadcast_to(l, (8, LANES - ATT_HEAD_DIM))

    def finish_previous():
        for h in range(ATT_HEADS):
            for r in range(MOBA_TOP_K):
                for half in range(PAGES_PER_BLOCK):
                    page = ids_ref[n_picks + (h * MOBA_TOP_K + r) * PAGES_PER_BLOCK + half]
                    v_copy(h, r, half, page).wait()
        row_hd = lax.broadcasted_iota(jnp.int32, (8, ATT_HEAD_DIM), 0)
        o = stash_ref[:, 0:ATT_HEAD_DIM]
        for h in range(ATT_HEADS):
            oh = jnp.zeros((8, ATT_HEAD_DIM), f32)
            for r in range(MOBA_TOP_K):
                pb = p_ref[ids_ref[h * MOBA_TOP_K + r]].astype(bf16)
                for half in range(PAGES_PER_BLOCK):
                    oh = oh + _dot_nt(pb[:, half * PAGE_SIZE:(half + 1) * PAGE_SIZE],
                                      vbuf[h * MOBA_TOP_K + r, half].astype(bf16))
            o = o + jnp.where(row_hd == h, oh, 0.0)
        o_ref[...] = o / stash_ref[:, ATT_HEAD_DIM:ATT_HEAD_DIM + 1]

    @pl.when(b == 0)
    def _():
        for c in range(K_SLOTS - 1):
            start_chunk(0, c, c)

    @pl.when(b < n_seq)
    def _():
        gate_ref[...] = jnp.zeros((8, LANES), f32)
        stream(range(0, CHUNKS_BEFORE_FINISH))

    @pl.when(b > 0)
    def _():
        finish_previous()

    @pl.when(b < n_seq)
    def _():
        stream(range(CHUNKS_BEFORE_FINISH, n_chunks))
        choose_and_weigh()


def moba_decode(page_table, q, k_new, v_new, cache_kt, cache_vt, layer):
    b = q.shape[0]
    n_pages = page_table.shape[1]
    n_blocks = n_pages // PAGES_PER_BLOCK
    assert n_blocks <= LANES and n_blocks >= MOBA_TOP_K
    heads8 = lambda a: jnp.pad(a.reshape(b, ATT_HEADS, ATT_HEAD_DIM), ((0, 0), (0, 8 - ATT_HEADS), (0, 0)))
    n_picks = ATT_HEADS * MOBA_TOP_K
    cur = lambda i, pt: (jnp.minimum(i, b - 1), 0, 0)
    prev = lambda i, pt: (jnp.maximum(i - 1, 0), 0, 0)
    per_head = pl.BlockSpec((None, 8, ATT_HEAD_DIM), cur)
    grid_spec = pltpu.PrefetchScalarGridSpec(
        num_scalar_prefetch=1, grid=(b + 1,),
        in_specs=[pl.BlockSpec((None, 1, ATT_DIM), cur), per_head, per_head, per_head,
                  pl.BlockSpec(memory_space=pl.ANY), pl.BlockSpec(memory_space=pl.ANY)],
        out_specs=pl.BlockSpec((None, 8, ATT_HEAD_DIM), prev),
        scratch_shapes=[pltpu.VMEM((K_SLOTS, K_CHUNK_PAGES, ATT_DIM, PAGE_SIZE), f32),
                        pltpu.VMEM((n_picks, PAGES_PER_BLOCK, ATT_HEAD_DIM, PAGE_SIZE), f32),
                        pltpu.VMEM((n_blocks, 8, MOBA_BLOCK), f32),
                        pltpu.VMEM((n_blocks, 8, MOBA_BLOCK), f32),
                        pltpu.VMEM((8, LANES), f32),
                        pltpu.VMEM((8, LANES), f32),
                        pltpu.SMEM((n_picks * (1 + PAGES_PER_BLOCK),), jnp.int32),
                        pltpu.SemaphoreType.DMA((K_SLOTS,)),
                        pltpu.SemaphoreType.DMA((1,))])
    out = pl.pallas_call(
        functools.partial(_moba_decode_kernel, layer=layer, n_pages=n_pages),
        grid_spec=grid_spec,
        out_shape=jax.ShapeDtypeStruct((b, 8, ATT_HEAD_DIM), f32),
        compiler_params=_cparams(1), name="moba_decode",
    )(page_table, q.reshape(b, 1, ATT_DIM), heads8(q), heads8(k_new), heads8(v_new), cache_kt, cache_vt)
    return out[:, :ATT_HEADS].reshape(b, ATT_DIM)


def _layer_params(l, w_in, conv_w, conv_b, dt_bias, a_log, d_skip, ssm_norm_g, pool_w, pool_scale, w_out):
    o_xbc = SSM_INNER
    o_dt = o_xbc + CONV_DIM
    o_pool = o_dt + SSM_HEADS
    o_q = o_pool + POOL_DIM
    o_k = o_q + ATT_DIM
    o_v = o_k + ATT_DIM
    wl = w_in[l]
    pad_heads = lambda v: jnp.pad(v.astype(f32), (0, LANES - SSM_HEADS)).reshape(1, LANES)
    w_cat = jnp.concatenate([wl[:, :o_pool], jnp.zeros((wl.shape[0], LANES - SSM_HEADS), wl.dtype),
                             wl[:, o_pool:]], axis=1).astype(bf16)
    pw = jnp.zeros((POOL_DIM, POOL_DIM), f32)
    for g in range(len(POOL_WINDOWS)):
        pw = pw.at[g * POOL_GROUP:(g + 1) * POOL_GROUP, g * POOL_GROUP:(g + 1) * POOL_GROUP].set(pool_w[l, g])
    expand = (jnp.arange(LANES)[:, None] == (jnp.arange(SSM_INNER)[None, :] // SSM_HEAD_DIM)).astype(f32)
    wo = w_out[l].astype(bf16)
    return {
        'w_in': w_cat,
        'conv_w': conv_w[l], 'conv_b': conv_b[l].reshape(1, CONV_DIM),
        'dt_bias': pad_heads(dt_bias[l]), 'a': pad_heads(-jnp.exp(a_log[l].astype(f32))),
        'dsk': jnp.repeat(d_skip[l].astype(f32), SSM_HEAD_DIM).reshape(1, SSM_INNER),
        'ssm_norm_g': ssm_norm_g[l].reshape(1, SSM_INNER),
        'pool_w': pw.astype(bf16), 'pool_scale': pool_scale[l].reshape(1, POOL_DIM),
        'expand': expand,
        'w_out': wo,
    }


def _in_proj_outs():
    sizes = [SSM_INNER, CONV_DIM, LANES, POOL_DIM, ATT_DIM, ATT_DIM, ATT_DIM]
    starts = [sum(sizes[:i]) for i in range(len(sizes))]
    return [(s, n, f32) for s, n in zip(starts, sizes)]


IN_PROJ_OUTS = _in_proj_outs()


def _pages_channel_major(cache):
    d, n_phys = cache.shape[:2]
    return jnp.transpose(cache, (0, 1, 3, 4, 2)).reshape(d, n_phys, ATT_DIM, PAGE_SIZE)


def kernel(x_prompt, x_sample, cache_moba_k, cache_moba_v, state_ssm, state_conv, state_pool, cache_mem_k, cache_mem_v, page_table, mem_prompt, norm_mix_g, w_in, conv_w, conv_b, dt_bias, a_log, d_skip, ssm_norm_g, pool_w, pool_scale, w_out, norm_cross_g, norm_mem_g, w_mem_q, w_mem_kv, w_mem_o, norm_ffn_g, w_gate_up, w_down, final_norm_g):
    depth = w_in.shape[0]
    bp, t, d = x_prompt.shape
    bs = x_sample.shape[0]
    assert bp == 1 and x_sample.shape[1] == 1
    past_len = page_table.shape[1] * PAGE_SIZE
    cache_kt = _pages_channel_major(cache_moba_k)
    cache_vt = _pages_channel_major(cache_moba_v)
    mem_k_split = _mem_split_view(cache_mem_k)
    mem_v_split = _mem_split_view(cache_mem_v)
    mem_len = mem_prompt.shape[1]
    xp = x_prompt.reshape(t, d)
    xs = x_sample.reshape(bs, d)
    mem = mem_prompt.reshape(mem_len, d)
    tm_p, tm_s = 512, bs
    outs = {n: [] for n in ('ks', 'vs', 'sp', 'ss', 'cp', 'cs', 'pp', 'ps', 'mk', 'mv')}
    kv_t = None
    for l in range(depth):
        prm = _layer_params(l, w_in, conv_w, conv_b, dt_bias, a_log, d_skip, ssm_norm_g, pool_w, pool_scale, w_out)
        wq = w_mem_q[l].astype(bf16)
        wkv = w_mem_kv[l].astype(bf16)
        wo_mem = w_mem_o[l].astype(bf16)
        wgu = w_gate_up[l].astype(bf16)
        wdn = w_down[l].astype(bf16)
        half = MEM_HEADS * MEM_HEAD_DIM
        last_layer = l == depth - 1

        mk, mv, mkb, mvb = norm_matmul(mem, norm_mem_g[l], wkv,
                                       [(0, half, f32), (half, half, f32), (0, half, bf16), (half, half, bf16)],
                                       tm=mem_len)
        z, xbc, dtr, u, q, k, v = norm_matmul(xp, norm_mix_g[l], prm['w_in'], IN_PROJ_OUTS, tm=tm_p)
        y, po, s_new = ssd_pool_prompt(z, xbc, dtr, u, prm, tm=tm_p)
        att, *kv_t = moba_prompt(q, k, v, layer=l, depth=depth, kv_t_prev=kv_t)
        xp = prompt_tail(xp, [y, po, att], prm['w_out'], norm_cross_g[l], wq, mkb, mvb, wo_mem,
                         norm_ffn_g[l], wgu, wdn, final_norm_g, norm_out=last_layer, tm=tm_p)
        outs['sp'].append(s_new[None])
        outs['cp'].append(xbc[t - (CONV_W - 1):][None])
        outs['pp'].append(u[t - POOL_HIST:][None])
        outs['mk'].append(mk.reshape(1, mem_len, MEM_HEADS, MEM_HEAD_DIM))
        outs['mv'].append(mv.reshape(1, mem_len, MEM_HEADS, MEM_HEAD_DIM))

        z, xbc, dtr, u, q, k, v = norm_matmul(xs, norm_mix_g[l], prm['w_in'], IN_PROJ_OUTS, tm=tm_s)
        y, po, c_new, s_new, p_new = ssd_pool_sample(z, xbc, dtr, u, state_conv[l], state_ssm[l],
                                                     state_pool[l], prm, pos0=past_len)
        att = moba_decode(page_table, q, k, v, cache_kt, cache_vt, layer=l)
        xs = matmul_residual(xs, [y, po, att], prm['w_out'], tm=tm_s)
        (qc,) = norm_matmul(xs, norm_cross_g[l], _split_channel_order(wq, 1), [(0, d, f32)], tm=tm_s)
        oc = cross_sample(qc, mem_k_split, mem_v_split, layer=l)
        xs = matmul_residual(xs, [oc], _split_channel_order(wo_mem, 0), tm=tm_s)
        xs = swiglu_block(xs, norm_ffn_g[l], wgu, wdn, final_norm_g, norm_out=last_layer, tm=tm_s, tf=D_FF)
        outs['ks'].append(k.reshape(bs, 1, ATT_HEADS, ATT_HEAD_DIM))
        outs['vs'].append(v.reshape(bs, 1, ATT_HEADS, ATT_HEAD_DIM))
        outs['ss'].append(s_new)
        outs['cs'].append(c_new)
        outs['ps'].append(p_new)

    y_prompt = xp.reshape(1, t, d)
    y_sample = xs.reshape(bs, 1, d)
    st = lambda n: jnp.stack(outs[n])
    token_major = lambda a: jnp.transpose(a.reshape(depth, 1, ATT_HEADS, ATT_HEAD_DIM, t), (0, 1, 4, 2, 3))
    return (y_prompt, y_sample, token_major(kv_t[0]), token_major(kv_t[1]), st('ks'), st('vs'), st('sp'), st('ss'),
            st('cp'), st('cs'), st('pp'), st('ps'), st('mk'), st('mv'))
```
